```python
import math
import jax, jax.numpy as jnp
from jax import lax
import numpy as np

D_MODEL = 2048
BATCH = 4
SEQ = 2048
DEPTH = 1

ATT_HEADS = 8
ATT_HALF_DIM = 64
ATT_V_DIM = 2 * ATT_HALF_DIM
ATT_WIDTH = ATT_HEADS * ATT_V_DIM
CONV_CH = D_MODEL - ATT_WIDTH
CONV_KERNEL = 31
MIX_WIDTH = ATT_WIDTH + CONV_CH
QK_COLS = ATT_HEADS * 2 * ATT_HALF_DIM
IN_COLS = 2 * QK_COLS + ATT_WIDTH + 2 * CONV_CH
N_BUCKETS = 32
MAX_DISTANCE = 128
Q_BLOCK = 128
N_EXPERTS = 32
TOP_K = 4
D_FF = D_MODEL
SWIGLU_LIMIT = 7.0
SWIGLU_ALPHA = 1.702
EPS = 1e-6

kernel_name = "hymba_diffattn_conformer_moe_encoder"


def rmsnorm(x, g):
    xf = x.astype(jnp.float32)
    y = xf * lax.rsqrt(jnp.mean(xf * xf, axis=-1, keepdims=True) + EPS)
    return (y * g.astype(jnp.float32)).astype(x.dtype)


def layernorm(x, g, b):
    xf = x.astype(jnp.float32)
    mu = jnp.mean(xf, axis=-1, keepdims=True)
    var = jnp.mean(jnp.square(xf - mu), axis=-1, keepdims=True)
    y = (xf - mu) * lax.rsqrt(var + EPS)
    return (y * g.astype(jnp.float32) + b.astype(jnp.float32)).astype(x.dtype)


def t5_bucket(rel):
    half = N_BUCKETS // 2
    max_exact = half // 2
    ret = jnp.where(rel > 0, half, 0)
    n = jnp.abs(rel)
    nf = jnp.maximum(n, 1).astype(jnp.float32)
    large = max_exact + (jnp.log(nf / max_exact) / math.log(MAX_DISTANCE / max_exact)
                         * (half - max_exact)).astype(jnp.int32)
    large = jnp.minimum(large, half - 1)
    return ret + jnp.where(n < max_exact, n, large)


def diff_attention(q, k, v, lam, rel_bias):
    B, S = q.shape[0], q.shape[1]
    nb = S // Q_BLOCK
    scale = ATT_HALF_DIM ** -0.5
    qb = q.reshape(B, nb, Q_BLOCK, ATT_HEADS, 2, ATT_HALF_DIM).transpose(1, 0, 2, 3, 4, 5)
    k1, k2 = k[..., 0, :], k[..., 1, :]
    kpos = jnp.arange(S)

    def block(args):
        q_blk, start = args
        qpos = start + jnp.arange(Q_BLOCK)
        bucket = t5_bucket(kpos[None, :] - qpos[:, None])
        bias = rel_bias[bucket].astype(jnp.float32).transpose(2, 0, 1)
        s1 = jnp.einsum('bqhd,bkhd->bhqk', q_blk[..., 0, :], k1).astype(jnp.float32) * scale + bias
        s2 = jnp.einsum('bqhd,bkhd->bhqk', q_blk[..., 1, :], k2).astype(jnp.float32) * scale + bias
        p = jax.nn.softmax(s1, axis=-1) - lam * jax.nn.softmax(s2, axis=-1)
        return jnp.einsum('bhqk,bkhe->bqhe', p.astype(v.dtype), v)

    out = lax.map(block, (qb, jnp.arange(nb) * Q_BLOCK))
    return out.transpose(1, 0, 2, 3, 4).reshape(B, S, ATT_HEADS, ATT_V_DIM)


def conformer_conv(u, conv_w, conv_b, ln_g, ln_b):
    a, g = jnp.split(u, 2, axis=-1)
    z = a * jax.nn.sigmoid(g)
    z = lax.conv_general_dilated(
        z, conv_w.reshape(CONV_KERNEL, 1, CONV_CH).astype(z.dtype),
        window_strides=(1,), padding=[(CONV_KERNEL // 2, CONV_KERNEL // 2)],
        dimension_numbers=('NWC', 'WIO', 'NWC'), feature_group_count=CONV_CH) + conv_b
    z = layernorm(z, ln_g, ln_b)
    return jax.nn.silu(z)


def moe_ffn(h, w_router, b_router, w_gate_up, b_gate_up, w_down, b_down):
    B, S, D = h.shape
    T = B * S
    hf = h.reshape(T, D)
    logits = (hf @ w_router + b_router).astype(jnp.float32)
    top_v, top_i = lax.top_k(logits, TOP_K)
    top_w = jax.nn.softmax(top_v, axis=-1)
    flat_ids = top_i.reshape(-1)
    flat_w = top_w.reshape(-1)
    order = jnp.argsort(flat_ids)
    sorted_ids = flat_ids[order]
    tok = order // TOP_K
    xs = hf[tok]
    group_sizes = jnp.bincount(flat_ids, length=N_EXPERTS).astype(jnp.int32)
    gu = lax.ragged_dot(xs, w_gate_up, group_sizes) + b_gate_up[sorted_ids]
    gate, up = jnp.split(gu, 2, axis=-1)
    gate = jnp.minimum(gate, SWIGLU_LIMIT)
    up = jnp.clip(up, -SWIGLU_LIMIT, SWIGLU_LIMIT)
    act = (up + 1.0) * (gate * jax.nn.sigmoid(SWIGLU_ALPHA * gate))
    y = lax.ragged_dot(act, w_down, group_sizes) + b_down[sorted_ids]
    y = y * flat_w[order][:, None].astype(y.dtype)
    out = jax.ops.segment_sum(y, tok, num_segments=T)
    return out.reshape(B, S, D)


def setup_inputs(seed: int = 0) -> dict:
    key = jax.random.key(seed)
    ks = jax.random.split(key, 26)
    f32 = jnp.float32
    nrm = lambda k, shape, s: jax.random.normal(k, shape, f32) * s
    L = DEPTH
    return {
        "x": nrm(ks[0], (BATCH, SEQ, D_MODEL), 1.0),
        "c": nrm(ks[1], (BATCH, D_MODEL), 1.0),
        "w_ada": nrm(ks[2], (L, D_MODEL, 6 * D_MODEL), 0.5 * D_MODEL ** -0.5),
        "b_ada": nrm(ks[3], (L, 6 * D_MODEL), 0.02),
        "norm1_g": 1.0 + nrm(ks[4], (L, D_MODEL), 0.02),
        "w_in": nrm(ks[5], (L, D_MODEL, IN_COLS), D_MODEL ** -0.5),
        "lambda_q1": nrm(ks[6], (L, ATT_HALF_DIM), 0.1),
        "lambda_k1": nrm(ks[7], (L, ATT_HALF_DIM), 0.1),
        "lambda_q2": nrm(ks[8], (L, ATT_HALF_DIM), 0.1),
        "lambda_k2": nrm(ks[9], (L, ATT_HALF_DIM), 0.1),
        "subln_g": 1.0 + nrm(ks[10], (L, ATT_V_DIM), 0.02),
        "rel_bias": nrm(ks[11], (N_BUCKETS, ATT_HEADS), 0.5),
        "conv_w": nrm(ks[12], (L, CONV_KERNEL, CONV_CH), CONV_KERNEL ** -0.5),
        "conv_b": nrm(ks[13], (L, CONV_CH), 0.02),
        "conv_ln_g": 1.0 + nrm(ks[14], (L, CONV_CH), 0.02),
        "conv_ln_b": nrm(ks[15], (L, CONV_CH), 0.02),
        "w_out": nrm(ks[16], (L, MIX_WIDTH, D_MODEL), MIX_WIDTH ** -0.5),
        "norm2_g": 1.0 + nrm(ks[17], (L, D_MODEL), 0.02),
        "w_router": nrm(ks[18], (L, D_MODEL, N_EXPERTS), D_MODEL ** -0.5),
        "b_router": nrm(ks[19], (L, N_EXPERTS), 0.01),
        "w_gate_up": nrm(ks[20], (L, N_EXPERTS, D_MODEL, 2 * D_FF), D_MODEL ** -0.5),
        "b_gate_up": nrm(ks[21], (L, N_EXPERTS, 2 * D_FF), 0.02),
        "w_down": nrm(ks[22], (L, N_EXPERTS, D_FF, D_MODEL), D_FF ** -0.5),
        "b_down": nrm(ks[23], (L, N_EXPERTS, D_MODEL), 0.02),
        "final_g": 1.0 + nrm(ks[24], (D_MODEL,), 0.02),
    }


def reference(x, c, w_ada, b_ada, norm1_g, w_in, lambda_q1, lambda_k1, lambda_q2, lambda_k2,
              subln_g, rel_bias, conv_w, conv_b, conv_ln_g, conv_ln_b, w_out, norm2_g,
              w_router, b_router, w_gate_up, b_gate_up, w_down, b_down, final_g):
    B, S, D = x.shape
    cs = jax.nn.silu(c)
    for l in range(DEPTH):
        mod = cs @ w_ada[l] + b_ada[l]
        shift1, scale1, gate1, shift2, scale2, gate2 = [m[:, None, :] for m in jnp.split(mod, 6, axis=-1)]

        h = rmsnorm(x, norm1_g[l]) * (1.0 + scale1) + shift1
        proj = h @ w_in[l]
        q, k, v, u = jnp.split(proj, [QK_COLS, 2 * QK_COLS, 2 * QK_COLS + ATT_WIDTH], axis=-1)
        q = q.reshape(B, S, ATT_HEADS, 2, ATT_HALF_DIM)
        k = k.reshape(B, S, ATT_HEADS, 2, ATT_HALF_DIM)
        v = v.reshape(B, S, ATT_HEADS, ATT_V_DIM)

        lambda_init = 0.8 - 0.6 * math.exp(-0.3 * l)
        lam = (jnp.exp(jnp.sum(lambda_q1[l].astype(jnp.float32) * lambda_k1[l].astype(jnp.float32)))
               - jnp.exp(jnp.sum(lambda_q2[l].astype(jnp.float32) * lambda_k2[l].astype(jnp.float32)))
               + lambda_init)
        att = diff_attention(q, k, v, lam, rel_bias)
        att = rmsnorm(att, subln_g[l]) * (1.0 - lambda_init)
        att = att.reshape(B, S, ATT_WIDTH)

        cv = conformer_conv(u, conv_w[l], conv_b[l], conv_ln_g[l], conv_ln_b[l])

        mixed = jnp.concatenate([att, cv], axis=-1) @ w_out[l]
        x = x + gate1 * mixed

        h2 = rmsnorm(x, norm2_g[l]) * (1.0 + scale2) + shift2
        x = x + gate2 * moe_ffn(h2, w_router[l], b_router[l], w_gate_up[l], b_gate_up[l],
                                w_down[l], b_down[l])
    return rmsnorm(x, final_g)
```

```python
import functools
import math

import jax
import jax.numpy as jnp
from jax import lax
from jax.experimental import pallas as pl
from jax.experimental.pallas import tpu as pltpu

f32 = jnp.float32
bf16 = jnp.bfloat16
i32 = jnp.int32

D_MODEL = 2048
BATCH = 4
SEQ = 2048
TOKENS = BATCH * SEQ
ATT_HEADS = 8
ATT_HALF_DIM = 64
ATT_V_DIM = 128
ATT_WIDTH = ATT_HEADS * ATT_V_DIM
CONV_CH = D_MODEL - ATT_WIDTH
CONV_KERNEL = 31
CONV_HALF = CONV_KERNEL // 2
QK_COLS = ATT_HEADS * 2 * ATT_HALF_DIM
IN_COLS = 2 * QK_COLS + ATT_WIDTH + 2 * CONV_CH
N_BUCKETS = 32
MAX_DISTANCE = 128
N_EXPERTS = 32
TOP_K = 4
D_FF = D_MODEL
SWIGLU_LIMIT = 7.0
SWIGLU_ALPHA = 1.702
EPS = 1e-6
LAMBDA_INIT = 0.8 - 0.6 * math.exp(-0.3 * 0)

ADA_TN = 1024
INP_TM = 1024
INP_TN = 1024
ATT_TQ = 512
ATT_KC = 256
ATT_EXT_CHUNKS = 6
CONV_TS = 512
CONV_HALO = 16
CONV_RB = 32
OUT_TM = 512
MOE_TM = 256
MOE_NT = TOKENS * TOP_K // MOE_TM + N_EXPERTS
MOE_ROWS = MOE_NT * MOE_TM
GMM1_TN = 512
GMM2_TN = 1024
CMB_TB = 128
ROW_CHUNK = 64

_MIB = 1024 * 1024


def _cparams(sem, vmem_mib):
    return pltpu.CompilerParams(dimension_semantics=sem, vmem_limit_bytes=vmem_mib * _MIB)


def _row_chunks(n_rows, chunk, body):
    def step(i, carry):
        body(pl.multiple_of(i * chunk, chunk))
        return carry
    lax.fori_loop(0, n_rows // chunk, step, 0)


def _adaln_kernel(c_ref, w_ref, b_ref, o_ref):
    c = c_ref[...]
    cs = c * jax.nn.sigmoid(c)
    o_ref[...] = jnp.dot(cs.astype(bf16), w_ref[...].astype(bf16),
                         preferred_element_type=f32) + b_ref[...]


def _adaln(c, w_ada, b_ada):
    n = w_ada.shape[1]
    return pl.pallas_call(
        _adaln_kernel,
        grid=(n // ADA_TN,),
        in_specs=[pl.BlockSpec((BATCH, D_MODEL), lambda j: (0, 0)),
                  pl.BlockSpec((D_MODEL, ADA_TN), lambda j: (0, j)),
                  pl.BlockSpec((1, ADA_TN), lambda j: (0, j))],
        out_specs=pl.BlockSpec((BATCH, ADA_TN), lambda j: (0, j)),
        out_shape=jax.ShapeDtypeStruct((BATCH, n), f32),
        compiler_params=_cparams(("arbitrary",), 40),
        name="adaln",
    )(c, w_ada, b_ada.reshape(1, n))


def _inproj_kernel(x_ref, g_ref, sc_ref, sh_ref, w_ref, o_ref, h_scr):
    @pl.when(pl.program_id(1) == 0)
    def _():
        g = g_ref[...]
        one_plus_scale = 1.0 + sc_ref[0]
        shift = sh_ref[0]

        def chunk(r0):
            x = x_ref[pl.ds(r0, ROW_CHUNK), :]
            ms = jnp.mean(x * x, axis=-1, keepdims=True)
            y = x * lax.rsqrt(ms + EPS) * g
            h_scr[pl.ds(r0, ROW_CHUNK), :] = (y * one_plus_scale + shift).astype(bf16)
        _row_chunks(INP_TM, ROW_CHUNK, chunk)

    o_ref[...] = jnp.dot(h_scr[...], w_ref[...], preferred_element_type=f32).astype(bf16)


def _inproj(x2d, norm_g, scale, shift, w_bf):
    tiles_per_batch = SEQ // INP_TM
    return pl.pallas_call(
        _inproj_kernel,
        grid=(TOKENS // INP_TM, IN_COLS // INP_TN),
        in_specs=[pl.BlockSpec((INP_TM, D_MODEL), lambda i, j: (i, 0)),
                  pl.BlockSpec((1, D_MODEL), lambda i, j: (0, 0)),
                  pl.BlockSpec((1, 1, D_MODEL), lambda i, j: (i // tiles_per_batch, 0, 0)),
                  pl.BlockSpec((1, 1, D_MODEL), lambda i, j: (i // tiles_per_batch, 0, 0)),
                  pl.BlockSpec((D_MODEL, INP_TN), lambda i, j: (0, j))],
        out_specs=pl.BlockSpec((INP_TM, INP_TN), lambda i, j: (i, j)),
        out_shape=jax.ShapeDtypeStruct((TOKENS, IN_COLS), bf16),
        scratch_shapes=[pltpu.VMEM((INP_TM, D_MODEL), bf16)],
        compiler_params=_cparams(("parallel", "arbitrary"), 48),
        name="inproj",
    )(x2d, norm_g.reshape(1, D_MODEL), scale.reshape(BATCH, 1, D_MODEL),
      shift.reshape(BATCH, 1, D_MODEL), w_bf)


def _t5_bucket(rel):
    half = N_BUCKETS // 2
    max_exact = half // 2
    ret = jnp.where(rel > 0, half, 0)
    n = jnp.abs(rel)
    nf = jnp.maximum(n, 1).astype(f32)
    large = max_exact + (jnp.log(nf / max_exact) / math.log(MAX_DISTANCE / max_exact)
                         * (half - max_exact)).astype(i32)
    large = jnp.minimum(large, half - 1)
    return ret + jnp.where(n < max_exact, n, large)


def _bias_band(rel_bias):
    r = jnp.arange(ATT_TQ)[:, None]
    j = jnp.arange(ATT_EXT_CHUNKS * ATT_KC)[None, :]
    rel = j - 2 * ATT_KC - r
    return rel_bias[_t5_bucket(rel)].astype(f32).transpose(2, 0, 1)


def _attn_kernel(q_ref, k_ref, v_ref, band_ref, lq1_ref, lk1_ref, lq2_ref, lk2_ref, sg_ref,
                 o_ref, s_scr):
    t = pl.program_id(2)
    n_chunks = SEQ // ATT_KC
    lam = (jnp.exp(jnp.sum(lq1_ref[...] * lk1_ref[...], axis=-1, keepdims=True))
           - jnp.exp(jnp.sum(lq2_ref[...] * lk2_ref[...], axis=-1, keepdims=True))
           + LAMBDA_INIT)
    q = q_ref[...] * jnp.asarray(ATT_HALF_DIM ** -0.5, bf16)
    lane = lax.broadcasted_iota(i32, q.shape, 1)
    zero = jnp.zeros_like(q)

    def softmax_pv(qm):
        m = jnp.full((ATT_TQ, 1), -jnp.inf, f32)
        for c in range(n_chunks):
            d = jnp.clip(c - t * (ATT_TQ // ATT_KC) + 2, 0, ATT_EXT_CHUNKS - 1)
            off = pl.multiple_of(d * ATT_KC, ATT_KC)
            s = lax.dot_general(qm, k_ref[c * ATT_KC:(c + 1) * ATT_KC, :],
                                (((1,), (1,)), ((), ())), preferred_element_type=f32)
            s = s + band_ref[0, :, pl.ds(off, ATT_KC)]
            s_scr[:, c * ATT_KC:(c + 1) * ATT_KC] = s
            m = jnp.maximum(m, jnp.max(s, axis=-1, keepdims=True))
        l = jnp.zeros((ATT_TQ, 1), f32)
        o = jnp.zeros((ATT_TQ, ATT_V_DIM), f32)
        for c in range(n_chunks):
            p = jnp.exp(s_scr[:, c * ATT_KC:(c + 1) * ATT_KC] - m)
            l = l + jnp.sum(p, axis=-1, keepdims=True)
            o = o + jnp.dot(p.astype(bf16), v_ref[c * ATT_KC:(c + 1) * ATT_KC, :],
                            preferred_element_type=f32)
        return o / l

    o1 = softmax_pv(jnp.where(lane < ATT_HALF_DIM, q, zero))
    o2 = softmax_pv(jnp.where(lane >= ATT_HALF_DIM, q, zero))
    out = o1 - lam * o2
    ms = jnp.mean(out * out, axis=-1, keepdims=True)
    y = out * lax.rsqrt(ms + EPS) * sg_ref[...]
    o_ref[...] = (y * (1.0 - LAMBDA_INIT)).astype(bf16)


def _attention(proj, band, lq1, lk1, lq2, lk2, subln_g):
    qt = SEQ // ATT_TQ
    k_col0 = QK_COLS // ATT_V_DIM
    v_col0 = 2 * QK_COLS // ATT_V_DIM
    vec = lambda n: pl.BlockSpec((1, n), lambda b, h, t: (0, 0))
    return pl.pallas_call(
        _attn_kernel,
        grid=(BATCH, ATT_HEADS, qt),
        in_specs=[pl.BlockSpec((ATT_TQ, ATT_V_DIM), lambda b, h, t: (b * qt + t, h)),
                  pl.BlockSpec((SEQ, ATT_V_DIM), lambda b, h, t: (b, k_col0 + h)),
                  pl.BlockSpec((SEQ, ATT_V_DIM), lambda b, h, t: (b, v_col0 + h)),
                  pl.BlockSpec((1, ATT_TQ, ATT_EXT_CHUNKS * ATT_KC), lambda b, h, t: (h, 0, 0)),
                  vec(ATT_HALF_DIM), vec(ATT_HALF_DIM), vec(ATT_HALF_DIM), vec(ATT_HALF_DIM),
                  vec(ATT_V_DIM)],
        out_specs=pl.BlockSpec((ATT_TQ, ATT_V_DIM), lambda b, h, t: (b * qt + t, h)),
        out_shape=jax.ShapeDtypeStruct((TOKENS, ATT_WIDTH), bf16),
        scratch_shapes=[pltpu.VMEM((ATT_TQ, SEQ), f32)],
        compiler_params=_cparams(("parallel", "parallel", "arbitrary"), 40),
        name="attn",
    )(proj, proj, proj, band, lq1.reshape(1, -1), lk1.reshape(1, -1), lq2.reshape(1, -1),
      lk2.reshape(1, -1), subln_g.reshape(1, -1))


def _conv_kernel(ap_ref, ac_ref, an_ref, gp_ref, gc_ref, gn_ref, w_ref, cb_ref, lg_ref, lb_ref,
                 o_ref, z_scr, c_scr):
    s = pl.program_id(1)
    last = pl.num_programs(1) - 1

    def glu(a_ref, g_ref):
        return a_ref[...].astype(f32) * jax.nn.sigmoid(g_ref[...].astype(f32))

    z_scr[0:CONV_HALO, :] = jnp.where(s > 0, glu(ap_ref, gp_ref), 0.0)
    z_scr[CONV_HALO + CONV_TS:, :] = jnp.where(s < last, glu(an_ref, gn_ref), 0.0)

    def glu_chunk(r0):
        a = ac_ref[pl.ds(r0, ROW_CHUNK), :].astype(f32)
        g = gc_ref[pl.ds(r0, ROW_CHUNK), :].astype(f32)
        z_scr[pl.ds(CONV_HALO + r0, ROW_CHUNK), :] = a * jax.nn.sigmoid(g)
    _row_chunks(CONV_TS, ROW_CHUNK, glu_chunk)

    lanes = 128
    tap0 = CONV_HALO - CONV_HALF

    def lane_chunk(lc, carry):
        cols = pl.ds(pl.multiple_of(lc * lanes, lanes), lanes)
        taps = [w_ref[j:j + 1, cols] for j in range(CONV_KERNEL)]
        bias = cb_ref[:, cols]
        for r0 in range(0, CONV_TS, CONV_RB):
            acc = jnp.zeros((CONV_RB, lanes), f32)
            for j in range(CONV_KERNEL):
                acc = acc + z_scr[r0 + tap0 + j:r0 + tap0 + j + CONV_RB, cols] * taps[j]
            c_scr[r0:r0 + CONV_RB, cols] = acc + bias
        return carry
    lax.fori_loop(0, CONV_CH // lanes, lane_chunk, 0)

    lg = lg_ref[...]
    lb = lb_ref[...]

    def ln_chunk(r0):
        c = c_scr[pl.ds(r0, ROW_CHUNK), :]
        mu = jnp.mean(c, axis=-1, keepdims=True)
        cc = c - mu
        var = jnp.mean(cc * cc, axis=-1, keepdims=True)
        y = cc * lax.rsqrt(var + EPS) * lg + lb
        o_ref[pl.ds(r0, ROW_CHUNK), :] = (y * jax.nn.sigmoid(y)).astype(bf16)
    _row_chunks(CONV_TS, ROW_CHUNK, ln_chunk)


def _conv(proj, conv_w, conv_b, ln_g, ln_b):
    st = SEQ // CONV_TS
    halo_per_tile = CONV_TS // CONV_HALO
    halo_per_seq = SEQ // CONV_HALO
    a_col = (2 * QK_COLS + ATT_WIDTH) // CONV_CH
    g_col = a_col + 1

    def cur(col):
        return pl.BlockSpec((CONV_TS, CONV_CH), lambda b, s: (b * st + s, col))

    def prev(col):
        return pl.BlockSpec(
            (CONV_HALO, CONV_CH),
            lambda b, s: (jnp.maximum(b * halo_per_seq + s * halo_per_tile - 1, 0), col))

    def nxt(col):
        return pl.BlockSpec(
            (CONV_HALO, CONV_CH),
            lambda b, s: (jnp.minimum(b * halo_per_seq + (s + 1) * halo_per_tile,
                                      TOKENS // CONV_HALO - 1), col))

    vec = pl.BlockSpec((1, CONV_CH), lambda b, s: (0, 0))
    return pl.pallas_call(
        _conv_kernel,
        grid=(BATCH, st),
        in_specs=[prev(a_col), cur(a_col), nxt(a_col), prev(g_col), cur(g_col), nxt(g_col),
                  pl.BlockSpec((CONV_KERNEL, CONV_CH), lambda b, s: (0, 0)), vec, vec, vec],
        out_specs=pl.BlockSpec((CONV_TS, CONV_CH), lambda b, s: (b * st + s, 0)),
        out_shape=jax.ShapeDtypeStruct((TOKENS, CONV_CH), bf16),
        scratch_shapes=[pltpu.VMEM((CONV_TS + 2 * CONV_HALO, CONV_CH), f32),
                        pltpu.VMEM((CONV_TS, CONV_CH), f32)],
        compiler_params=_cparams(("parallel", "arbitrary"), 32),
        name="conv",
    )(proj, proj, proj, proj, proj, proj, conv_w, conv_b.reshape(1, -1), ln_g.reshape(1, -1),
      ln_b.reshape(1, -1))


def _outproj_kernel(att_ref, cv_ref, wo_ref, x_ref, g1_ref, sc2_ref, sh2_ref, n2_ref, wr_ref,
                    br_ref, x1_ref, h2_ref, ti_ref, tw_ref, rk_ref, cnt_ref, carry_scr):
    @pl.when(pl.program_id(0) == 0)
    def _():
        carry_scr[...] = jnp.zeros_like(carry_scr)

    mixed = (jnp.dot(att_ref[...], wo_ref[0:ATT_WIDTH, :], preferred_element_type=f32)
             + jnp.dot(cv_ref[...], wo_ref[ATT_WIDTH:, :], preferred_element_type=f32))
    x1_ref[...] = x_ref[...] + g1_ref[0] * mixed

    n2 = n2_ref[...]
    one_plus_scale = 1.0 + sc2_ref[0]
    shift = sh2_ref[0]

    def chunk(r0):
        x1 = x1_ref[pl.ds(r0, ROW_CHUNK), :]
        ms = jnp.mean(x1 * x1, axis=-1, keepdims=True)
        y = x1 * lax.rsqrt(ms + EPS) * n2
        h2_ref[pl.ds(r0, ROW_CHUNK), :] = y * one_plus_scale + shift
    _row_chunks(OUT_TM, ROW_CHUNK, chunk)

    logits = lax.dot_general(wr_ref[...], h2_ref[...], (((1,), (1,)), ((), ())),
                             precision=lax.Precision.HIGHEST,
                             preferred_element_type=f32) + br_ref[...]
    eidx = lax.broadcasted_iota(i32, logits.shape, 0)
    vals = logits
    sels, tops = [], []
    for k in range(TOP_K):
        m = jnp.max(vals, axis=0, keepdims=True)
        idx = jnp.min(jnp.where(vals == m, eidx, N_EXPERTS), axis=0, keepdims=True)
        sel = eidx == idx
        ti_ref[k:k + 1, :] = idx
        sels.append(sel)
        tops.append(m)
        vals = jnp.where(sel, -jnp.inf, vals)
    exps = [jnp.exp(v - tops[0]) for v in tops]
    denom = exps[0] + exps[1] + exps[2] + exps[3]
    for k in range(TOP_K):
        tw_ref[k:k + 1, :] = exps[k] / denom

    onehot = jnp.where(sels[0] | sels[1] | sels[2] | sels[3], 1.0, 0.0)
    r_i = lax.broadcasted_iota(i32, (OUT_TM, OUT_TM), 0)
    c_i = lax.broadcasted_iota(i32, (OUT_TM, OUT_TM), 1)
    earlier = jnp.where(r_i < c_i, 1.0, 0.0).astype(bf16)
    before = jnp.dot(onehot.astype(bf16), earlier, preferred_element_type=f32) + carry_scr[...]
    for k in range(TOP_K):
        rk_ref[k:k + 1, :] = jnp.sum(jnp.where(sels[k], before, 0.0), axis=0,
                                     keepdims=True).astype(i32)
    total = carry_scr[...] + jnp.sum(onehot, axis=1, keepdims=True)
    carry_scr[...] = total
    cnt_ref[...] = jnp.broadcast_to(total, cnt_ref.shape).astype(i32)


def _outproj(att, cv, wo_bf, x2d, gate1, scale2, shift2, norm2_g, w_router, b_router):
    tiles_per_batch = SEQ // OUT_TM
    row = lambda w: pl.BlockSpec((OUT_TM, w), lambda i: (i, 0))
    mod = pl.BlockSpec((1, 1, D_MODEL), lambda i: (i // tiles_per_batch, 0, 0))
    slot = pl.BlockSpec((TOP_K, OUT_TM), lambda i: (0, i))
    return pl.pallas_call(
        _outproj_kernel,
        grid=(TOKENS // OUT_TM,),
        in_specs=[row(ATT_WIDTH), row(CONV_CH),
                  pl.BlockSpec((D_MODEL, D_MODEL), lambda i: (0, 0), pipeline_mode=pl.Buffered(1)),
                  row(D_MODEL), mod, mod, mod,
                  pl.BlockSpec((1, D_MODEL), lambda i: (0, 0)),
                  pl.BlockSpec((N_EXPERTS, D_MODEL), lambda i: (0, 0)),
                  pl.BlockSpec((N_EXPERTS, 1), lambda i: (0, 0))],
        out_specs=[row(D_MODEL), row(D_MODEL), slot, slot, slot,
                   pl.BlockSpec((N_EXPERTS, 128), lambda i: (0, 0))],
        out_shape=[jax.ShapeDtypeStruct((TOKENS, D_MODEL), f32),
                   jax.ShapeDtypeStruct((TOKENS, D_MODEL), f32),
                   jax.ShapeDtypeStruct((TOP_K, TOKENS), i32),
                   jax.ShapeDtypeStruct((TOP_K, TOKENS), f32),
                   jax.ShapeDtypeStruct((TOP_K, TOKENS), i32),
                   jax.ShapeDtypeStruct((N_EXPERTS, 128), i32)],
        scratch_shapes=[pltpu.VMEM((N_EXPERTS, 1), f32)],
        compiler_params=_cparams(("arbitrary",), 52),
        name="outproj",
    )(att, cv, wo_bf, x2d, gate1.reshape(BATCH, 1, D_MODEL), scale2.reshape(BATCH, 1, D_MODEL),
      shift2.reshape(BATCH, 1, D_MODEL), norm2_g.reshape(1, D_MODEL), w_router.T,
      b_router.reshape(N_EXPERTS, 1))


def _row_copy(src_hbm, row, dst, sem):
    return pltpu.make_async_copy(src_hbm.at[pl.ds(row, 1), :], dst, sem)


def _dispatch_kernel(src_cur, src_nxt, h2_hbm, o_ref, buf, sem):
    m = pl.program_id(0)
    n = pl.num_programs(0)
    slot = m % 2

    def start_tile(idx_ref, s):
        def body(r, carry):
            _row_copy(h2_hbm, idx_ref[0, 0, r], buf.at[s, pl.ds(r, 1), :], sem.at[s]).start()
            return carry
        lax.fori_loop(0, MOE_TM, body, 0)

    @pl.when(m == 0)
    def _():
        start_tile(src_cur, 0)

    @pl.when(m + 1 < n)
    def _():
        start_tile(src_nxt, 1 - slot)

    def wait_row(r, carry):
        _row_copy(h2_hbm, 0, buf.at[slot, pl.ds(r, 1), :], sem.at[slot]).wait()
        return carry
    lax.fori_loop(0, MOE_TM, wait_row, 0)
    o_ref[...] = buf[slot].astype(bf16)


def _dispatch(src_rows, h2):
    src3 = src_rows.reshape(MOE_NT, 1, MOE_TM)
    return pl.pallas_call(
        _dispatch_kernel,
        grid=(MOE_NT,),
        in_specs=[pl.BlockSpec((1, 1, MOE_TM), lambda m: (m, 0, 0), memory_space=pltpu.SMEM),
                  pl.BlockSpec((1, 1, MOE_TM), lambda m: (jnp.minimum(m + 1, MOE_NT - 1), 0, 0),
                               memory_space=pltpu.SMEM),
                  pl.BlockSpec(memory_space=pl.ANY)],
        out_specs=pl.BlockSpec((MOE_TM, D_MODEL), lambda m: (m, 0)),
        out_shape=jax.ShapeDtypeStruct((MOE_ROWS, D_MODEL), bf16),
        scratch_shapes=[pltpu.VMEM((2, MOE_TM, D_MODEL), f32), pltpu.SemaphoreType.DMA((2,))],
        compiler_params=_cparams(("arbitrary",), 24),
        name="dispatch",
    )(src3, src3, h2)


def _weights_changed(te_ref, m):
    prev = te_ref[jnp.maximum(m - 1, 0)]
    return (m == 0) | (te_ref[m] != prev)


def _gmm1_kernel(te_ref, nv_ref, xs_ref, wg_ref, wu_ref, bg_ref, bu_ref, o_ref, wg_bf, wu_bf):
    m = pl.program_id(1)

    @pl.when(_weights_changed(te_ref, m))
    def _():
        wg_bf[...] = wg_ref[0].astype(bf16)
        wu_bf[...] = wu_ref[0].astype(bf16)

    @pl.when(m < nv_ref[0])
    def _():
        x = xs_ref[...]
        gate = jnp.dot(x, wg_bf[...], preferred_element_type=f32) + bg_ref[0]
        up = jnp.dot(x, wu_bf[...], preferred_element_type=f32) + bu_ref[0]
        gate = jnp.minimum(gate, SWIGLU_LIMIT)
        up = jnp.clip(up, -SWIGLU_LIMIT, SWIGLU_LIMIT)
        act = (up + 1.0) * (gate * jax.nn.sigmoid(SWIGLU_ALPHA * gate))
        o_ref[...] = act.astype(bf16)

    @pl.when(m >= nv_ref[0])
    def _():
        o_ref[...] = jnp.zeros_like(o_ref)


def _gmm1(tile_expert, n_valid, xs, w_gate_up, b_gate_up):
    up0 = D_FF // GMM1_TN
    grid_spec = pltpu.PrefetchScalarGridSpec(
        num_scalar_prefetch=2,
        grid=(D_FF // GMM1_TN, MOE_NT),
        in_specs=[pl.BlockSpec((MOE_TM, D_MODEL), lambda n, m, te, nv: (m, 0)),
                  pl.BlockSpec((1, D_MODEL, GMM1_TN), lambda n, m, te, nv: (te[m], 0, n)),
                  pl.BlockSpec((1, D_MODEL, GMM1_TN), lambda n, m, te, nv: (te[m], 0, up0 + n)),
                  pl.BlockSpec((1, 1, GMM1_TN), lambda n, m, te, nv: (te[m], 0, n)),
                  pl.BlockSpec((1, 1, GMM1_TN), lambda n, m, te, nv: (te[m], 0, up0 + n))],
        out_specs=pl.BlockSpec((MOE_TM, GMM1_TN), lambda n, m, te, nv: (m, n)),
        scratch_shapes=[pltpu.VMEM((D_MODEL, GMM1_TN), bf16), pltpu.VMEM((D_MODEL, GMM1_TN), bf16)])
    b3 = b_gate_up.reshape(N_EXPERTS, 1, 2 * D_FF)
    return pl.pallas_call(
        _gmm1_kernel,
        grid_spec=grid_spec,
        out_shape=jax.ShapeDtypeStruct((MOE_ROWS, D_FF), bf16),
        compiler_params=_cparams(("arbitrary", "arbitrary"), 40),
        name="gmm1",
    )(tile_expert, n_valid, xs, w_gate_up, w_gate_up, b3, b3)


def _gmm2_kernel(te_ref, nv_ref, a_ref, wd_ref, bd_ref, o_ref, wd_bf):
    m = pl.program_id(1)

    @pl.when(_weights_changed(te_ref, m))
    def _():
        wd_bf[...] = wd_ref[0].astype(bf16)

    @pl.when(m < nv_ref[0])
    def _():
        o_ref[...] = jnp.dot(a_ref[...], wd_bf[...], preferred_element_type=f32) + bd_ref[0]

    @pl.when(m >= nv_ref[0])
    def _():
        o_ref[...] = jnp.zeros_like(o_ref)


def _gmm2(tile_expert, n_valid, act, w_down, b_down):
    grid_spec = pltpu.PrefetchScalarGridSpec(
        num_scalar_prefetch=2,
        grid=(D_MODEL // GMM2_TN, MOE_NT),
        in_specs=[pl.BlockSpec((MOE_TM, D_FF), lambda n, m, te, nv: (m, 0)),
                  pl.BlockSpec((1, D_FF, GMM2_TN), lambda n, m, te, nv: (te[m], 0, n)),
                  pl.BlockSpec((1, 1, GMM2_TN), lambda n, m, te, nv: (te[m], 0, n))],
        out_specs=pl.BlockSpec((MOE_TM, GMM2_TN), lambda n, m, te, nv: (m, n)),
        scratch_shapes=[pltpu.VMEM((D_FF, GMM2_TN), bf16)])
    return pl.pallas_call(
        _gmm2_kernel,
        grid_spec=grid_spec,
        out_shape=jax.ShapeDtypeStruct((MOE_ROWS, D_MODEL), f32),
        compiler_params=_cparams(("arbitrary", "arbitrary"), 40),
        name="gmm2",
    )(tile_expert, n_valid, act, w_down, b_down.reshape(N_EXPERTS, 1, D_MODEL))


def _combine_kernel(pos_cur, pos_nxt, y_hbm, x1_ref, g2_ref, tw_ref, fg_ref, o_ref, buf, sem):
    i = pl.program_id(0)
    n = pl.num_programs(0)
    slot = i % 2

    def start_tile(pos_ref, s):
        def body(r, carry):
            for k in range(TOP_K):
                _row_copy(y_hbm, pos_ref[0, k, r], buf.at[s, k, pl.ds(r, 1), :], sem.at[s]).start()
            return carry
        lax.fori_loop(0, CMB_TB, body, 0)

    @pl.when(i == 0)
    def _():
        start_tile(pos_cur, 0)

    @pl.when(i + 1 < n)
    def _():
        start_tile(pos_nxt, 1 - slot)

    def wait_row(r, carry):
        for k in range(TOP_K):
            _row_copy(y_hbm, 0, buf.at[slot, k, pl.ds(r, 1), :], sem.at[slot]).wait()
        return carry
    lax.fori_loop(0, CMB_TB, wait_row, 0)

    gate2 = g2_ref[0]
    fg = fg_ref[...]

    def chunk(r0):
        rows = pl.ds(r0, ROW_CHUNK)
        w = tw_ref[rows, :]
        moe = buf[slot, 0, rows, :] * w[:, 0:1]
        for k in range(1, TOP_K):
            moe = moe + buf[slot, k, rows, :] * w[:, k:k + 1]
        x2 = x1_ref[rows, :] + gate2 * moe
        ms = jnp.mean(x2 * x2, axis=-1, keepdims=True)
        o_ref[rows, :] = x2 * lax.rsqrt(ms + EPS) * fg
    _row_chunks(CMB_TB, ROW_CHUNK, chunk)


def _combine(pos, y, x1, gate2, top_w_rows, final_g):
    nt = TOKENS // CMB_TB
    tiles_per_batch = SEQ // CMB_TB
    pos3 = pos.reshape(TOP_K, nt, CMB_TB).transpose(1, 0, 2)
    return pl.pallas_call(
        _combine_kernel,
        grid=(nt,),
        in_specs=[pl.BlockSpec((1, TOP_K, CMB_TB), lambda i: (i, 0, 0), memory_space=pltpu.SMEM),
                  pl.BlockSpec((1, TOP_K, CMB_TB), lambda i: (jnp.minimum(i + 1, nt - 1), 0, 0),
                               memory_space=pltpu.SMEM),
                  pl.BlockSpec(memory_space=pl.ANY),
                  pl.BlockSpec((CMB_TB, D_MODEL), lambda i: (i, 0)),
                  pl.BlockSpec((1, 1, D_MODEL), lambda i: (i // tiles_per_batch, 0, 0)),
                  pl.BlockSpec((CMB_TB, TOP_K), lambda i: (i, 0)),
                  pl.BlockSpec((1, D_MODEL), lambda i: (0, 0))],
        out_specs=pl.BlockSpec((CMB_TB, D_MODEL), lambda i: (i, 0)),
        out_shape=jax.ShapeDtypeStruct((TOKENS, D_MODEL), f32),
        scratch_shapes=[pltpu.VMEM((2, TOP_K, CMB_TB, D_MODEL), f32),
                        pltpu.SemaphoreType.DMA((2,))],
        compiler_params=_cparams(("arbitrary",), 32),
        name="combine",
    )(pos3, pos3, y, x1, gate2.reshape(BATCH, 1, D_MODEL), top_w_rows, final_g.reshape(1, D_MODEL))


def _routing_tables(counts, top_i, rank):
    padded = (counts + MOE_TM - 1) // MOE_TM * MOE_TM
    ends = jnp.cumsum(padded)
    starts = ends - padded
    n_valid = ends[-1] // MOE_TM
    tile_ids = jnp.minimum(jnp.arange(MOE_NT, dtype=i32), n_valid - 1)
    tile_expert = jnp.searchsorted(ends // MOE_TM, tile_ids, side="right").astype(i32)
    pos = starts[top_i] + rank
    token = jnp.broadcast_to(jnp.arange(TOKENS, dtype=i32)[None, :], pos.shape)
    src_rows = jnp.zeros((MOE_ROWS,), i32).at[pos.reshape(-1)].set(token.reshape(-1))
    return tile_expert, n_valid.reshape(1).astype(i32), pos.astype(i32), src_rows


def kernel(x, c, w_ada, b_ada, norm1_g, w_in, lambda_q1, lambda_k1, lambda_q2, lambda_k2, subln_g,
           rel_bias, conv_w, conv_b, conv_ln_g, conv_ln_b, w_out, norm2_g, w_router, b_router,
           w_gate_up, b_gate_up, w_down, b_down, final_g):
    assert x.shape == (BATCH, SEQ, D_MODEL) and w_ada.shape[0] == 1
    l = 0
    x2d = x.reshape(TOKENS, D_MODEL)
    mod = _adaln(c, w_ada[l], b_ada[l])
    shift1, scale1, gate1, shift2, scale2, gate2 = [
        mod[:, i * D_MODEL:(i + 1) * D_MODEL] for i in range(6)]

    proj = _inproj(x2d, norm1_g[l], scale1, shift1, w_in[l].astype(bf16))
    att = _attention(proj, _bias_band(rel_bias), lambda_q1[l], lambda_k1[l], lambda_q2[l],
                     lambda_k2[l], subln_g[l])
    cv = _conv(proj, conv_w[l], conv_b[l], conv_ln_g[l], conv_ln_b[l])

    x1, h2, top_i, top_w, rank, counts = _outproj(
        att, cv, w_out[l].astype(bf16), x2d, gate1, scale2, shift2, norm2_g[l], w_router[l],
        b_router[l])
    tile_expert, n_valid, pos, src_rows = _routing_tables(counts[:, 0], top_i, rank)

    xs = _dispatch(src_rows, h2)
    act = _gmm1(tile_expert, n_valid, xs, w_gate_up[l], b_gate_up[l])
    y = _gmm2(tile_expert, n_valid, act, w_down[l], b_down[l])
    out = _combine(pos, y, x1, gate2, top_w.T, final_g)
    return out.reshape(BATCH, SEQ, D_MODEL)
```

```python
import functools
import math

import jax
import jax.numpy as jnp
from jax import lax
from jax.experimental import pallas as pl
from jax.experimental.pallas import tpu as pltpu

f32 = jnp.float32
bf16 = jnp.bfloat16
i32 = jnp.int32

D_MODEL = 2048
BATCH = 4
SEQ = 2048
TOKENS = BATCH * SEQ
ATT_HEADS = 8
ATT_HALF_DIM = 64
ATT_V_DIM = 128
ATT_WIDTH = ATT_HEADS * ATT_V_DIM
CONV_CH = D_MODEL - ATT_WIDTH
CONV_KERNEL = 31
CONV_HALF = CONV_KERNEL // 2
QK_COLS = ATT_HEADS * 2 * ATT_HALF_DIM
IN_COLS = 2 * QK_COLS + ATT_WIDTH + 2 * CONV_CH
N_BUCKETS = 32
MAX_DISTANCE = 128
N_EXPERTS = 32
TOP_K = 4
D_FF = D_MODEL
SWIGLU_LIMIT = 7.0
SWIGLU_ALPHA = 1.702
EPS = 1e-6
LAMBDA_INIT = 0.8 - 0.6 * math.exp(-0.3 * 0)

ADA_TN = 1024
INP_TM = 1024
INP_TN = 1024
ATT_TQ = 512
ATT_KC = 256
ATT_EXT_CHUNKS = 6
CONV_TS = 512
CONV_HALO = 16
CONV_RB = 32
OUT_TM = 512
MOE_TM = 256
MOE_NT = TOKENS * TOP_K // MOE_TM + N_EXPERTS
MOE_ROWS = MOE_NT * MOE_TM
GMM1_TN = 1024
GMM2_TN = 2048
CMB_TB = 128
ROW_CHUNK = 64
DMA_UNROLL = 8

_MIB = 1024 * 1024


def _cparams(sem, vmem_mib):
    return pltpu.CompilerParams(dimension_semantics=sem, vmem_limit_bytes=vmem_mib * _MIB)


def _row_chunks(n_rows, chunk, body):
    def step(i, carry):
        body(pl.multiple_of(i * chunk, chunk))
        return carry
    lax.fori_loop(0, n_rows // chunk, step, 0)


def _adaln_kernel(c_ref, w_ref, b_ref, o_ref):
    c = c_ref[...]
    cs = c * jax.nn.sigmoid(c)
    o_ref[0] = jnp.dot(cs.astype(bf16), w_ref[...].astype(bf16),
                       preferred_element_type=f32) + b_ref[...]


def _adaln(c, w_ada, b_ada):
    n = w_ada.shape[1]
    per_vec = D_MODEL // ADA_TN
    return pl.pallas_call(
        _adaln_kernel,
        grid=(n // ADA_TN,),
        in_specs=[pl.BlockSpec((BATCH, D_MODEL), lambda j: (0, 0)),
                  pl.BlockSpec((D_MODEL, ADA_TN), lambda j: (0, j)),
                  pl.BlockSpec((1, ADA_TN), lambda j: (0, j))],
        out_specs=pl.BlockSpec((1, BATCH, ADA_TN), lambda j: (j // per_vec, 0, j % per_vec)),
        out_shape=jax.ShapeDtypeStruct((n // D_MODEL, BATCH, D_MODEL), f32),
        compiler_params=_cparams(("arbitrary",), 40),
        name="adaln",
    )(c, w_ada, b_ada.reshape(1, n))


def _inproj_kernel(x_ref, g_ref, sc_ref, sh_ref, w_ref, o_ref, h_scr):
    @pl.when(pl.program_id(1) == 0)
    def _():
        g = g_ref[...]
        one_plus_scale = 1.0 + sc_ref[0]
        shift = sh_ref[0]

        def chunk(r0):
            x = x_ref[pl.ds(r0, ROW_CHUNK), :]
            ms = jnp.mean(x * x, axis=-1, keepdims=True)
            y = x * lax.rsqrt(ms + EPS) * g
            h_scr[pl.ds(r0, ROW_CHUNK), :] = (y * one_plus_scale + shift).astype(bf16)
        _row_chunks(INP_TM, ROW_CHUNK, chunk)

    o_ref[...] = jnp.dot(h_scr[...], w_ref[...], preferred_element_type=f32).astype(bf16)


def _inproj(x2d, norm_g, scale, shift, w_bf):
    tiles_per_batch = SEQ // INP_TM
    return pl.pallas_call(
        _inproj_kernel,
        grid=(TOKENS // INP_TM, IN_COLS // INP_TN),
        in_specs=[pl.BlockSpec((INP_TM, D_MODEL), lambda i, j: (i, 0)),
                  pl.BlockSpec((1, D_MODEL), lambda i, j: (0, 0)),
                  pl.BlockSpec((1, 1, D_MODEL), lambda i, j: (i // tiles_per_batch, 0, 0)),
                  pl.BlockSpec((1, 1, D_MODEL), lambda i, j: (i // tiles_per_batch, 0, 0)),
                  pl.BlockSpec((D_MODEL, INP_TN), lambda i, j: (0, j))],
        out_specs=pl.BlockSpec((INP_TM, INP_TN), lambda i, j: (i, j)),
        out_shape=jax.ShapeDtypeStruct((TOKENS, IN_COLS), bf16),
        scratch_shapes=[pltpu.VMEM((INP_TM, D_MODEL), bf16)],
        compiler_params=_cparams(("parallel", "arbitrary"), 48),
        name="inproj",
    )(x2d, norm_g.reshape(1, D_MODEL), scale.reshape(BATCH, 1, D_MODEL),
      shift.reshape(BATCH, 1, D_MODEL), w_bf)


def _t5_bucket(rel):
    half = N_BUCKETS // 2
    max_exact = half // 2
    ret = jnp.where(rel > 0, half, 0)
    n = jnp.abs(rel)
    nf = jnp.maximum(n, 1).astype(f32)
    large = max_exact + (jnp.log(nf / max_exact) / math.log(MAX_DISTANCE / max_exact)
                         * (half - max_exact)).astype(i32)
    large = jnp.minimum(large, half - 1)
    return ret + jnp.where(n < max_exact, n, large)


def _bias_band(rel_bias):
    r = jnp.arange(ATT_TQ)[:, None]
    j = jnp.arange(ATT_EXT_CHUNKS * ATT_KC)[None, :]
    rel = j - 2 * ATT_KC - r
    onehot = (_t5_bucket(rel)[..., None] == jnp.arange(N_BUCKETS)).astype(f32)
    return jnp.einsum("rjn,nh->hrj", onehot, rel_bias.astype(f32),
                      precision=lax.Precision.HIGHEST)


def _attn_kernel(q_ref, k_ref, v_ref, band_ref, lq1_ref, lk1_ref, lq2_ref, lk2_ref, sg_ref,
                 o_ref, s_scr):
    t = pl.program_id(2)
    n_chunks = SEQ // ATT_KC
    lam = (jnp.exp(jnp.sum(lq1_ref[...] * lk1_ref[...], axis=-1, keepdims=True))
           - jnp.exp(jnp.sum(lq2_ref[...] * lk2_ref[...], axis=-1, keepdims=True))
           + LAMBDA_INIT)
    q = q_ref[...] * jnp.asarray(ATT_HALF_DIM ** -0.5, bf16)
    lane = lax.broadcasted_iota(i32, q.shape, 1)
    zero = jnp.zeros_like(q)

    def softmax_pv(qm):
        m = jnp.full((ATT_TQ, 1), -jnp.inf, f32)
        for c in range(n_chunks):
            d = jnp.clip(c - t * (ATT_TQ // ATT_KC) + 2, 0, ATT_EXT_CHUNKS - 1)
            off = pl.multiple_of(d * ATT_KC, ATT_KC)
            s = lax.dot_general(qm, k_ref[c * ATT_KC:(c + 1) * ATT_KC, :],
                                (((1,), (1,)), ((), ())), preferred_element_type=f32)
            s = s + band_ref[0, :, pl.ds(off, ATT_KC)]
            s_scr[:, c * ATT_KC:(c + 1) * ATT_KC] = s
            m = jnp.maximum(m, jnp.max(s, axis=-1, keepdims=True))
        l = jnp.zeros((ATT_TQ, 1), f32)
        o = jnp.zeros((ATT_TQ, ATT_V_DIM), f32)
        for c in range(n_chunks):
            p = jnp.exp(s_scr[:, c * ATT_KC:(c + 1) * ATT_KC] - m)
            l = l + jnp.sum(p, axis=-1, keepdims=True)
            o = o + jnp.dot(p.astype(bf16), v_ref[c * ATT_KC:(c + 1) * ATT_KC, :],
                            preferred_element_type=f32)
        return o / l

    o1 = softmax_pv(jnp.where(lane < ATT_HALF_DIM, q, zero))
    o2 = softmax_pv(jnp.where(lane >= ATT_HALF_DIM, q, zero))
    out = o1 - lam * o2
    ms = jnp.mean(out * out, axis=-1, keepdims=True)
    y = out * lax.rsqrt(ms + EPS) * sg_ref[...]
    o_ref[...] = (y * (1.0 - LAMBDA_INIT)).astype(bf16)


def _attention(proj, band, lq1, lk1, lq2, lk2, subln_g):
    qt = SEQ // ATT_TQ
    k_col0 = QK_COLS // ATT_V_DIM
    v_col0 = 2 * QK_COLS // ATT_V_DIM
    vec = lambda n: pl.BlockSpec((1, n), lambda b, h, t: (0, 0))
    return pl.pallas_call(
        _attn_kernel,
        grid=(BATCH, ATT_HEADS, qt),
        in_specs=[pl.BlockSpec((ATT_TQ, ATT_V_DIM), lambda b, h, t: (b * qt + t, h)),
                  pl.BlockSpec((SEQ, ATT_V_DIM), lambda b, h, t: (b, k_col0 + h)),
                  pl.BlockSpec((SEQ, ATT_V_DIM), lambda b, h, t: (b, v_col0 + h)),
                  pl.BlockSpec((1, ATT_TQ, ATT_EXT_CHUNKS * ATT_KC), lambda b, h, t: (h, 0, 0)),
                  vec(ATT_HALF_DIM), vec(ATT_HALF_DIM), vec(ATT_HALF_DIM), vec(ATT_HALF_DIM),
                  vec(ATT_V_DIM)],
        out_specs=pl.BlockSpec((ATT_TQ, ATT_V_DIM), lambda b, h, t: (b * qt + t, h)),
        out_shape=jax.ShapeDtypeStruct((TOKENS, ATT_WIDTH), bf16),
        scratch_shapes=[pltpu.VMEM((ATT_TQ, SEQ), f32)],
        compiler_params=_cparams(("parallel", "parallel", "arbitrary"), 40),
        name="attn",
    )(proj, proj, proj, band, lq1.reshape(1, -1), lk1.reshape(1, -1), lq2.reshape(1, -1),
      lk2.reshape(1, -1), subln_g.reshape(1, -1))


def _conv_kernel(ap_ref, ac_ref, an_ref, gp_ref, gc_ref, gn_ref, w_ref, cb_ref, lg_ref, lb_ref,
                 o_ref, z_scr, c_scr):
    s = pl.program_id(1)
    last = pl.num_programs(1) - 1

    def glu(a_ref, g_ref):
        return a_ref[...].astype(f32) * jax.nn.sigmoid(g_ref[...].astype(f32))

    z_scr[0:CONV_HALO, :] = jnp.where(s > 0, glu(ap_ref, gp_ref), 0.0)
    z_scr[CONV_HALO + CONV_TS:, :] = jnp.where(s < last, glu(an_ref, gn_ref), 0.0)

    def glu_chunk(r0):
        a = ac_ref[pl.ds(r0, ROW_CHUNK), :].astype(f32)
        g = gc_ref[pl.ds(r0, ROW_CHUNK), :].astype(f32)
        z_scr[pl.ds(CONV_HALO + r0, ROW_CHUNK), :] = a * jax.nn.sigmoid(g)
    _row_chunks(CONV_TS, ROW_CHUNK, glu_chunk)

    lanes = 128
    tap0 = CONV_HALO - CONV_HALF

    def lane_chunk(lc, carry):
        cols = pl.ds(pl.multiple_of(lc * lanes, lanes), lanes)
        taps = [w_ref[j:j + 1, cols] for j in range(CONV_KERNEL)]
        bias = cb_ref[:, cols]
        for r0 in range(0, CONV_TS, CONV_RB):
            acc = jnp.zeros((CONV_RB, lanes), f32)
            for j in range(CONV_KERNEL):
                acc = acc + z_scr[r0 + tap0 + j:r0 + tap0 + j + CONV_RB, cols] * taps[j]
            c_scr[r0:r0 + CONV_RB, cols] = acc + bias
        return carry
    lax.fori_loop(0, CONV_CH // lanes, lane_chunk, 0)

    lg = lg_ref[...]
    lb = lb_ref[...]

    def ln_chunk(r0):
        c = c_scr[pl.ds(r0, ROW_CHUNK), :]
        mu = jnp.mean(c, axis=-1, keepdims=True)
        cc = c - mu
        var = jnp.mean(cc * cc, axis=-1, keepdims=True)
        y = cc * lax.rsqrt(var + EPS) * lg + lb
        o_ref[pl.ds(r0, ROW_CHUNK), :] = (y * jax.nn.sigmoid(y)).astype(bf16)
    _row_chunks(CONV_TS, ROW_CHUNK, ln_chunk)


def _conv(proj, conv_w, conv_b, ln_g, ln_b):
    st = SEQ // CONV_TS
    halo_per_tile = CONV_TS // CONV_HALO
    halo_per_seq = SEQ // CONV_HALO
    a_col = (2 * QK_COLS + ATT_WIDTH) // CONV_CH
    g_col = a_col + 1

    def cur(col):
        return pl.BlockSpec((CONV_TS, CONV_CH), lambda b, s: (b * st + s, col))

    def prev(col):
        return pl.BlockSpec(
            (CONV_HALO, CONV_CH),
            lambda b, s: (jnp.maximum(b * halo_per_seq + s * halo_per_tile - 1, 0), col))

    def nxt(col):
        return pl.BlockSpec(
            (CONV_HALO, CONV_CH),
            lambda b, s: (jnp.minimum(b * halo_per_seq + (s + 1) * halo_per_tile,
                                      TOKENS // CONV_HALO - 1), col))

    vec = pl.BlockSpec((1, CONV_CH), lambda b, s: (0, 0))
    return pl.pallas_call(
        _conv_kernel,
        grid=(BATCH, st),
        in_specs=[prev(a_col), cur(a_col), nxt(a_col), prev(g_col), cur(g_col), nxt(g_col),
                  pl.BlockSpec((CONV_KERNEL, CONV_CH), lambda b, s: (0, 0)), vec, vec, vec],
        out_specs=pl.BlockSpec((CONV_TS, CONV_CH), lambda b, s: (b * st + s, 0)),
        out_shape=jax.ShapeDtypeStruct((TOKENS, CONV_CH), bf16),
        scratch_shapes=[pltpu.VMEM((CONV_TS + 2 * CONV_HALO, CONV_CH), f32),
                        pltpu.VMEM((CONV_TS, CONV_CH), f32)],
        compiler_params=_cparams(("parallel", "arbitrary"), 32),
        name="conv",
    )(proj, proj, proj, proj, proj, proj, conv_w, conv_b.reshape(1, -1), ln_g.reshape(1, -1),
      ln_b.reshape(1, -1))


def _outproj_kernel(att_ref, cv_ref, wo_ref, x_ref, g1_ref, sc2_ref, sh2_ref, n2_ref, wr_ref,
                    br_ref, x1_ref, h2_ref, ti_ref, tw_ref, rk_ref, cnt_ref, carry_scr):
    @pl.when(pl.program_id(0) == 0)
    def _():
        carry_scr[...] = jnp.zeros_like(carry_scr)

    mixed = (jnp.dot(att_ref[...], wo_ref[0:ATT_WIDTH, :], preferred_element_type=f32)
             + jnp.dot(cv_ref[...], wo_ref[ATT_WIDTH:, :], preferred_element_type=f32))
    x1_ref[...] = x_ref[...] + g1_ref[0] * mixed

    n2 = n2_ref[...]
    one_plus_scale = 1.0 + sc2_ref[0]
    shift = sh2_ref[0]

    def chunk(r0):
        x1 = x1_ref[pl.ds(r0, ROW_CHUNK), :]
        ms = jnp.mean(x1 * x1, axis=-1, keepdims=True)
        y = x1 * lax.rsqrt(ms + EPS) * n2
        h2_ref[pl.ds(r0, ROW_CHUNK), :] = y * one_plus_scale + shift
    _row_chunks(OUT_TM, ROW_CHUNK, chunk)

    logits = lax.dot_general(wr_ref[...], h2_ref[...], (((1,), (1,)), ((), ())),
                             precision=lax.Precision.HIGHEST,
                             preferred_element_type=f32) + br_ref[...]
    eidx = lax.broadcasted_iota(i32, logits.shape, 0)
    vals = logits
    sels, tops = [], []
    for k in range(TOP_K):
        m = jnp.max(vals, axis=0, keepdims=True)
        idx = jnp.min(jnp.where(vals == m, eidx, N_EXPERTS), axis=0, keepdims=True)
        sel = eidx == idx
        ti_ref[k:k + 1, :] = idx
        sels.append(sel)
        tops.append(m)
        vals = jnp.where(sel, -jnp.inf, vals)
    exps = [jnp.exp(v - tops[0]) for v in tops]
    denom = exps[0] + exps[1] + exps[2] + exps[3]
    for k in range(TOP_K):
        tw_ref[k:k + 1, :] = exps[k] / denom

    onehot = jnp.where(sels[0] | sels[1] | sels[2] | sels[3], 1.0, 0.0)
    r_i = lax.broadcasted_iota(i32, (OUT_TM, OUT_TM), 0)
    c_i = lax.broadcasted_iota(i32, (OUT_TM, OUT_TM), 1)
    earlier = jnp.where(r_i < c_i, 1.0, 0.0).astype(bf16)
    before = jnp.dot(onehot.astype(bf16), earlier, preferred_element_type=f32) + carry_scr[...]
    for k in range(TOP_K):
        rk_ref[k:k + 1, :] = jnp.sum(jnp.where(sels[k], before, 0.0), axis=0,
                                     keepdims=True).astype(i32)
    total = carry_scr[...] + jnp.sum(onehot, axis=1, keepdims=True)
    carry_scr[...] = total
    cnt_ref[...] = jnp.broadcast_to(total, cnt_ref.shape).astype(i32)


def _outproj(att, cv, wo_bf, x2d, gate1, scale2, shift2, norm2_g, w_router, b_router):
    tiles_per_batch = SEQ // OUT_TM
    row = lambda w: pl.BlockSpec((OUT_TM, w), lambda i: (i, 0))
    mod = pl.BlockSpec((1, 1, D_MODEL), lambda i: (i // tiles_per_batch, 0, 0))
    slot = pl.BlockSpec((TOP_K, OUT_TM), lambda i: (0, i))
    return pl.pallas_call(
        _outproj_kernel,
        grid=(TOKENS // OUT_TM,),
        in_specs=[row(ATT_WIDTH), row(CONV_CH),
                  pl.BlockSpec((D_MODEL, D_MODEL), lambda i: (0, 0), pipeline_mode=pl.Buffered(1)),
                  row(D_MODEL), mod, mod, mod,
                  pl.BlockSpec((1, D_MODEL), lambda i: (0, 0)),
                  pl.BlockSpec((N_EXPERTS, D_MODEL), lambda i: (0, 0)),
                  pl.BlockSpec((N_EXPERTS, 1), lambda i: (0, 0))],
        out_specs=[row(D_MODEL), row(D_MODEL), slot, slot, slot,
                   pl.BlockSpec((N_EXPERTS, 128), lambda i: (0, 0))],
        out_shape=[jax.ShapeDtypeStruct((TOKENS, D_MODEL), f32),
                   jax.ShapeDtypeStruct((TOKENS, D_MODEL), f32),
                   jax.ShapeDtypeStruct((TOP_K, TOKENS), i32),
                   jax.ShapeDtypeStruct((TOP_K, TOKENS), f32),
                   jax.ShapeDtypeStruct((TOP_K, TOKENS), i32),
                   jax.ShapeDtypeStruct((N_EXPERTS, 128), i32)],
        scratch_shapes=[pltpu.VMEM((N_EXPERTS, 1), f32)],
        compiler_params=_cparams(("arbitrary",), 52),
        name="outproj",
    )(att, cv, wo_bf, x2d, gate1.reshape(BATCH, 1, D_MODEL), scale2.reshape(BATCH, 1, D_MODEL),
      shift2.reshape(BATCH, 1, D_MODEL), norm2_g.reshape(1, D_MODEL), w_router.T,
      b_router.reshape(N_EXPERTS, 1))


def _row_copy(src_hbm, row, dst, sem):
    return pltpu.make_async_copy(src_hbm.at[pl.ds(row, 1), :], dst, sem)


def _dispatch_kernel(src_cur, src_nxt, h2_hbm, o_ref, buf, sem):
    m = pl.program_id(0)
    n = pl.num_programs(0)
    slot = m % 2

    def start_tile(idx_ref, s):
        def body(g, carry):
            for u in range(DMA_UNROLL):
                r = g * DMA_UNROLL + u
                _row_copy(h2_hbm, idx_ref[0, 0, r], buf.at[s, pl.ds(r, 1), :], sem.at[s]).start()
            return carry
        lax.fori_loop(0, MOE_TM // DMA_UNROLL, body, 0)

    @pl.when(m == 0)
    def _():
        start_tile(src_cur, 0)

    @pl.when(m + 1 < n)
    def _():
        start_tile(src_nxt, 1 - slot)

    pltpu.make_async_copy(h2_hbm.at[pl.ds(0, MOE_TM), :], buf.at[slot], sem.at[slot]).wait()
    o_ref[...] = buf[slot].astype(bf16)


def _dispatch(src_rows, h2):
    src3 = src_rows.reshape(MOE_NT, 1, MOE_TM)
    return pl.pallas_call(
        _dispatch_kernel,
        grid=(MOE_NT,),
        in_specs=[pl.BlockSpec((1, 1, MOE_TM), lambda m: (m, 0, 0), memory_space=pltpu.SMEM),
                  pl.BlockSpec((1, 1, MOE_TM), lambda m: (jnp.minimum(m + 1, MOE_NT - 1), 0, 0),
                               memory_space=pltpu.SMEM),
                  pl.BlockSpec(memory_space=pl.ANY)],
        out_specs=pl.BlockSpec((MOE_TM, D_MODEL), lambda m: (m, 0)),
        out_shape=jax.ShapeDtypeStruct((MOE_ROWS, D_MODEL), bf16),
        scratch_shapes=[pltpu.VMEM((2, MOE_TM, D_MODEL), f32), pltpu.SemaphoreType.DMA((2,))],
        compiler_params=_cparams(("arbitrary",), 24),
        name="dispatch",
    )(src3, src3, h2)


def _weights_changed(te_ref, m):
    prev = te_ref[jnp.maximum(m - 1, 0)]
    return (m == 0) | (te_ref[m] != prev)


def _gmm1_kernel(te_ref, nv_ref, xs_ref, wg_ref, wu_ref, bg_ref, bu_ref, o_ref, wg_bf, wu_bf):
    m = pl.program_id(1)

    @pl.when(_weights_changed(te_ref, m))
    def _():
        wg_bf[...] = wg_ref[0].astype(bf16)
        wu_bf[...] = wu_ref[0].astype(bf16)

    @pl.when(m < nv_ref[0])
    def _():
        x = xs_ref[...]
        gate = jnp.dot(x, wg_bf[...], preferred_element_type=f32) + bg_ref[0]
        up = jnp.dot(x, wu_bf[...], preferred_element_type=f32) + bu_ref[0]
        gate = jnp.minimum(gate, SWIGLU_LIMIT)
        up = jnp.clip(up, -SWIGLU_LIMIT, SWIGLU_LIMIT)
        act = (up + 1.0) * (gate * jax.nn.sigmoid(SWIGLU_ALPHA * gate))
        o_ref[...] = act.astype(bf16)

    @pl.when(m >= nv_ref[0])
    def _():
        o_ref[...] = jnp.zeros_like(o_ref)


def _gmm1(tile_expert, n_valid, xs, w_gate_up, b_gate_up):
    up0 = D_FF // GMM1_TN
    grid_spec = pltpu.PrefetchScalarGridSpec(
        num_scalar_prefetch=2,
        grid=(D_FF // GMM1_TN, MOE_NT),
        in_specs=[pl.BlockSpec((MOE_TM, D_MODEL), lambda n, m, te, nv: (m, 0)),
                  pl.BlockSpec((1, D_MODEL, GMM1_TN), lambda n, m, te, nv: (te[m], 0, n)),
                  pl.BlockSpec((1, D_MODEL, GMM1_TN), lambda n, m, te, nv: (te[m], 0, up0 + n)),
                  pl.BlockSpec((1, 1, GMM1_TN), lambda n, m, te, nv: (te[m], 0, n)),
                  pl.BlockSpec((1, 1, GMM1_TN), lambda n, m, te, nv: (te[m], 0, up0 + n))],
        out_specs=pl.BlockSpec((MOE_TM, GMM1_TN), lambda n, m, te, nv: (m, n)),
        scratch_shapes=[pltpu.VMEM((D_MODEL, GMM1_TN), bf16), pltpu.VMEM((D_MODEL, GMM1_TN), bf16)])
    b3 = b_gate_up.reshape(N_EXPERTS, 1, 2 * D_FF)
    return pl.pallas_call(
        _gmm1_kernel,
        grid_spec=grid_spec,
        out_shape=jax.ShapeDtypeStruct((MOE_ROWS, D_FF), bf16),
        compiler_params=_cparams(("arbitrary", "arbitrary"), 52),
        name="gmm1",
    )(tile_expert, n_valid, xs, w_gate_up, w_gate_up, b3, b3)


def _gmm2_kernel(te_ref, nv_ref, a_ref, wd_ref, bd_ref, o_ref, wd_bf):
    m = pl.program_id(1)

    @pl.when(_weights_changed(te_ref, m))
    def _():
        wd_bf[...] = wd_ref[0].astype(bf16)

    @pl.when(m < nv_ref[0])
    def _():
        o_ref[...] = jnp.dot(a_ref[...], wd_bf[...], preferred_element_type=f32) + bd_ref[0]

    @pl.when(m >= nv_ref[0])
    def _():
        o_ref[...] = jnp.zeros_like(o_ref)


def _gmm2(tile_expert, n_valid, act, w_down, b_down):
    grid_spec = pltpu.PrefetchScalarGridSpec(
        num_scalar_prefetch=2,
        grid=(D_MODEL // GMM2_TN, MOE_NT),
        in_specs=[pl.BlockSpec((MOE_TM, D_FF), lambda n, m, te, nv: (m, 0)),
                  pl.BlockSpec((1, D_FF, GMM2_TN), lambda n, m, te, nv: (te[m], 0, n)),
                  pl.BlockSpec((1, 1, GMM2_TN), lambda n, m, te, nv: (te[m], 0, n))],
        out_specs=pl.BlockSpec((MOE_TM, GMM2_TN), lambda n, m, te, nv: (m, n)),
        scratch_shapes=[pltpu.VMEM((D_FF, GMM2_TN), bf16)])
    return pl.pallas_call(
        _gmm2_kernel,
        grid_spec=grid_spec,
        out_shape=jax.ShapeDtypeStruct((MOE_ROWS, D_MODEL), f32),
        compiler_params=_cparams(("arbitrary", "arbitrary"), 56),
        name="gmm2",
    )(tile_expert, n_valid, act, w_down, b_down.reshape(N_EXPERTS, 1, D_MODEL))


def _combine_kernel(pos_cur, pos_nxt, y_hbm, x1_ref, g2_ref, tw_ref, fg_ref, o_ref, buf, sem):
    i = pl.program_id(0)
    n = pl.num_programs(0)
    slot = i % 2

    def start_tile(pos_ref, s):
        def body(g, carry):
            for u in range(DMA_UNROLL // TOP_K):
                r = g * (DMA_UNROLL // TOP_K) + u
                for k in range(TOP_K):
                    _row_copy(y_hbm, pos_ref[0, k, r], buf.at[s, k, pl.ds(r, 1), :],
                              sem.at[s]).start()
            return carry
        lax.fori_loop(0, CMB_TB * TOP_K // DMA_UNROLL, body, 0)

    @pl.when(i == 0)
    def _():
        start_tile(pos_cur, 0)

    @pl.when(i + 1 < n)
    def _():
        start_tile(pos_nxt, 1 - slot)

    for k in range(TOP_K):
        pltpu.make_async_copy(y_hbm.at[pl.ds(0, CMB_TB), :], buf.at[slot, k], sem.at[slot]).wait()

    gate2 = g2_ref[0]
    fg = fg_ref[...]

    def chunk(r0):
        rows = pl.ds(r0, ROW_CHUNK)
        w = tw_ref[rows, :]
        moe = buf[slot, 0, rows, :] * w[:, 0:1]
        for k in range(1, TOP_K):
            moe = moe + buf[slot, k, rows, :] * w[:, k:k + 1]
        x2 = x1_ref[rows, :] + gate2 * moe
        ms = jnp.mean(x2 * x2, axis=-1, keepdims=True)
        o_ref[rows, :] = x2 * lax.rsqrt(ms + EPS) * fg
    _row_chunks(CMB_TB, ROW_CHUNK, chunk)


def _combine(pos, y, x1, gate2, top_w_rows, final_g):
    nt = TOKENS // CMB_TB
    tiles_per_batch = SEQ // CMB_TB
    pos3 = pos.reshape(TOP_K, nt, CMB_TB).transpose(1, 0, 2)
    return pl.pallas_call(
        _combine_kernel,
        grid=(nt,),
        in_specs=[pl.BlockSpec((1, TOP_K, CMB_TB), lambda i: (i, 0, 0), memory_space=pltpu.SMEM),
                  pl.BlockSpec((1, TOP_K, CMB_TB), lambda i: (jnp.minimum(i + 1, nt - 1), 0, 0),
                               memory_space=pltpu.SMEM),
                  pl.BlockSpec(memory_space=pl.ANY),
                  pl.BlockSpec((CMB_TB, D_MODEL), lambda i: (i, 0)),
                  pl.BlockSpec((1, 1, D_MODEL), lambda i: (i // tiles_per_batch, 0, 0)),
                  pl.BlockSpec((CMB_TB, TOP_K), lambda i: (i, 0)),
                  pl.BlockSpec((1, D_MODEL), lambda i: (0, 0))],
        out_specs=pl.BlockSpec((CMB_TB, D_MODEL), lambda i: (i, 0)),
        out_shape=jax.ShapeDtypeStruct((TOKENS, D_MODEL), f32),
        scratch_shapes=[pltpu.VMEM((2, TOP_K, CMB_TB, D_MODEL), f32),
                        pltpu.SemaphoreType.DMA((2,))],
        compiler_params=_cparams(("arbitrary",), 32),
        name="combine",
    )(pos3, pos3, y, x1, gate2.reshape(BATCH, 1, D_MODEL), top_w_rows, final_g.reshape(1, D_MODEL))


def _routing_tables(counts, top_i, rank):
    padded = (counts + MOE_TM - 1) // MOE_TM * MOE_TM
    ends = jnp.cumsum(padded)
    starts = ends - padded
    n_valid = ends[-1] // MOE_TM
    tile_ids = jnp.minimum(jnp.arange(MOE_NT, dtype=i32), n_valid - 1)
    tile_expert = jnp.sum(tile_ids[:, None] >= (ends // MOE_TM)[None, :], axis=1).astype(i32)
    experts = jnp.arange(N_EXPERTS, dtype=i32)
    pos = rank + jnp.sum(jnp.where(top_i[..., None] == experts, starts, 0), axis=-1)
    token = jnp.broadcast_to(jnp.arange(TOKENS, dtype=i32)[None, :], pos.shape)
    src_rows = jnp.zeros((MOE_ROWS,), i32).at[pos.reshape(-1)].set(token.reshape(-1))
    return tile_expert, n_valid.reshape(1).astype(i32), pos.astype(i32), src_rows


def kernel(x, c, w_ada, b_ada, norm1_g, w_in, lambda_q1, lambda_k1, lambda_q2, lambda_k2, subln_g,
           rel_bias, conv_w, conv_b, conv_ln_g, conv_ln_b, w_out, norm2_g, w_router, b_router,
           w_gate_up, b_gate_up, w_down, b_down, final_g):
    assert x.shape == (BATCH, SEQ, D_MODEL) and w_ada.shape[0] == 1
    l = 0
    x2d = x.reshape(TOKENS, D_MODEL)
    mod = _adaln(c, w_ada[l], b_ada[l])
    shift1, scale1, gate1, shift2, scale2, gate2 = [mod[i] for i in range(6)]

    proj = _inproj(x2d, norm1_g[l], scale1, shift1, w_in[l].astype(bf16))
    att = _attention(proj, _bias_band(rel_bias), lambda_q1[l], lambda_k1[l], lambda_q2[l],
                     lambda_k2[l], subln_g[l])
    cv = _conv(proj, conv_w[l], conv_b[l], conv_ln_g[l], conv_ln_b[l])

    x1, h2, top_i, top_w, rank, counts = _outproj(
        att, cv, w_out[l].astype(bf16), x2d, gate1, scale2, shift2, norm2_g[l], w_router[l],
        b_router[l])
    tile_expert, n_valid, pos, src_rows = _routing_tables(counts[:, 0], top_i, rank)

    xs = _dispatch(src_rows, h2)
    act = _gmm1(tile_expert, n_valid, xs, w_gate_up[l], b_gate_up[l])
    y = _gmm2(tile_expert, n_valid, act, w_down[l], b_down[l])
    out = _combine(pos, y, x1, gate2, top_w.T, final_g)
    return out.reshape(BATCH, SEQ, D_MODEL)
```

```python
import functools
import math

import jax
import jax.numpy as jnp
from jax import lax
from jax.experimental import pallas as pl
from jax.experimental.pallas import tpu as pltpu

f32 = jnp.float32
bf16 = jnp.bfloat16
i32 = jnp.int32

D_MODEL = 2048
BATCH = 4
SEQ = 2048
TOKENS = BATCH * SEQ
ATT_HEADS = 8
ATT_HALF_DIM = 64
ATT_V_DIM = 128
ATT_WIDTH = ATT_HEADS * ATT_V_DIM
CONV_CH = D_MODEL - ATT_WIDTH
CONV_KERNEL = 31
CONV_HALF = CONV_KERNEL // 2
QK_COLS = ATT_HEADS * 2 * ATT_HALF_DIM
IN_COLS = 2 * QK_COLS + ATT_WIDTH + 2 * CONV_CH
N_BUCKETS = 32
MAX_DISTANCE = 128
N_EXPERTS = 32
TOP_K = 4
D_FF = D_MODEL
SWIGLU_LIMIT = 7.0
SWIGLU_ALPHA = 1.702
EPS = 1e-6
LAMBDA_INIT = 0.8 - 0.6 * math.exp(-0.3 * 0)

ADA_TN = 1024
INP_TM = 1024
INP_TN = 1024
ATT_TQ = 512
ATT_KC = 256
ATT_EXT_CHUNKS = 6
CONV_TS = 512
CONV_HALO = 16
CONV_RB = 32
OUT_TM = 512
MOE_TM = 256
MOE_NT = TOKENS * TOP_K // MOE_TM + N_EXPERTS
MOE_ROWS = MOE_NT * MOE_TM
GMM1_TN = 1024
GMM2_TN = 2048
CMB_TB = 128
ROW_CHUNK = 64
DMA_UNROLL = 8

_MIB = 1024 * 1024


def _cparams(sem, vmem_mib):
    return pltpu.CompilerParams(dimension_semantics=sem, vmem_limit_bytes=vmem_mib * _MIB)


PACK_ROWS = 8
PACK_LANES = 128
_HI_MASK = 0xFFFF0000


def _pack_pair(lo, hi):
    lo_b = lax.shift_right_logical(
        lax.bitcast_convert_type(lo.astype(bf16).astype(f32), jnp.uint32), jnp.uint32(16))
    hi_b = lax.bitcast_convert_type(hi.astype(bf16).astype(f32), jnp.uint32) & jnp.uint32(_HI_MASK)
    return lo_b | hi_b


def _unpack_pair(w):
    lo = lax.bitcast_convert_type(lax.shift_left(w, jnp.uint32(16)), f32)
    hi = lax.bitcast_convert_type(w & jnp.uint32(_HI_MASK), f32)
    return lo, hi


def _store_packed(dst_ref, token0, n_tokens, get_cols):
    for s in range(PACK_ROWS):
        lo = get_cols(slice(2 * s * PACK_LANES, (2 * s + 1) * PACK_LANES))
        hi = get_cols(slice((2 * s + 1) * PACK_LANES, (2 * s + 2) * PACK_LANES))
        dst_ref[pl.ds(token0 * PACK_ROWS + s, n_tokens, stride=PACK_ROWS), :] = _pack_pair(lo, hi)


def _load_packed(src_ref, n_tokens, s):
    return _unpack_pair(src_ref[pl.ds(s, n_tokens, stride=PACK_ROWS), :])


def _row_chunks(n_rows, chunk, body):
    def step(i, carry):
        body(pl.multiple_of(i * chunk, chunk))
        return carry
    lax.fori_loop(0, n_rows // chunk, step, 0)


def _adaln_kernel(c_ref, w_ref, b_ref, o_ref):
    c = c_ref[...]
    cs = c * jax.nn.sigmoid(c)
    o_ref[0] = jnp.dot(cs.astype(bf16), w_ref[...].astype(bf16),
                       preferred_element_type=f32) + b_ref[...]


def _adaln(c, w_ada, b_ada):
    n = w_ada.shape[1]
    per_vec = D_MODEL // ADA_TN
    return pl.pallas_call(
        _adaln_kernel,
        grid=(n // ADA_TN,),
        in_specs=[pl.BlockSpec((BATCH, D_MODEL), lambda j: (0, 0)),
                  pl.BlockSpec((D_MODEL, ADA_TN), lambda j: (0, j)),
                  pl.BlockSpec((1, ADA_TN), lambda j: (0, j))],
        out_specs=pl.BlockSpec((1, BATCH, ADA_TN), lambda j: (j // per_vec, 0, j % per_vec)),
        out_shape=jax.ShapeDtypeStruct((n // D_MODEL, BATCH, D_MODEL), f32),
        compiler_params=_cparams(("arbitrary",), 40),
        name="adaln",
    )(c, w_ada, b_ada.reshape(1, n))


def _inproj_kernel(x_ref, g_ref, sc_ref, sh_ref, w_ref, o_ref, h_scr):
    @pl.when(pl.program_id(1) == 0)
    def _():
        g = g_ref[...]
        one_plus_scale = 1.0 + sc_ref[0]
        shift = sh_ref[0]

        def chunk(r0):
            x = x_ref[pl.ds(r0, ROW_CHUNK), :]
            ms = jnp.mean(x * x, axis=-1, keepdims=True)
            y = x * lax.rsqrt(ms + EPS) * g
            h_scr[pl.ds(r0, ROW_CHUNK), :] = (y * one_plus_scale + shift).astype(bf16)
        _row_chunks(INP_TM, ROW_CHUNK, chunk)

    o_ref[...] = jnp.dot(h_scr[...], w_ref[...], preferred_element_type=f32).astype(bf16)


def _inproj(x2d, norm_g, scale, shift, w_bf):
    tiles_per_batch = SEQ // INP_TM
    return pl.pallas_call(
        _inproj_kernel,
        grid=(TOKENS // INP_TM, IN_COLS // INP_TN),
        in_specs=[pl.BlockSpec((INP_TM, D_MODEL), lambda i, j: (i, 0)),
                  pl.BlockSpec((1, D_MODEL), lambda i, j: (0, 0)),
                  pl.BlockSpec((1, 1, D_MODEL), lambda i, j: (i // tiles_per_batch, 0, 0)),
                  pl.BlockSpec((1, 1, D_MODEL), lambda i, j: (i // tiles_per_batch, 0, 0)),
                  pl.BlockSpec((D_MODEL, INP_TN), lambda i, j: (0, j))],
        out_specs=pl.BlockSpec((INP_TM, INP_TN), lambda i, j: (i, j)),
        out_shape=jax.ShapeDtypeStruct((TOKENS, IN_COLS), bf16),
        scratch_shapes=[pltpu.VMEM((INP_TM, D_MODEL), bf16)],
        compiler_params=_cparams(("parallel", "arbitrary"), 48),
        name="inproj",
    )(x2d, norm_g.reshape(1, D_MODEL), scale.reshape(BATCH, 1, D_MODEL),
      shift.reshape(BATCH, 1, D_MODEL), w_bf)


def _t5_bucket(rel):
    half = N_BUCKETS // 2
    max_exact = half // 2
    ret = jnp.where(rel > 0, half, 0)
    n = jnp.abs(rel)
    nf = jnp.maximum(n, 1).astype(f32)
    large = max_exact + (jnp.log(nf / max_exact) / math.log(MAX_DISTANCE / max_exact)
                         * (half - max_exact)).astype(i32)
    large = jnp.minimum(large, half - 1)
    return ret + jnp.where(n < max_exact, n, large)


def _bias_band(rel_bias):
    r = jnp.arange(ATT_TQ)[:, None]
    j = jnp.arange(ATT_EXT_CHUNKS * ATT_KC)[None, :]
    rel = j - 2 * ATT_KC - r
    onehot = (_t5_bucket(rel)[..., None] == jnp.arange(N_BUCKETS)).astype(f32)
    return jnp.einsum("rjn,nh->hrj", onehot, rel_bias.astype(f32),
                      precision=lax.Precision.HIGHEST)


def _attn_kernel(q_ref, k_ref, v_ref, band_ref, lq1_ref, lk1_ref, lq2_ref, lk2_ref, sg_ref,
                 o_ref, s_scr):
    t = pl.program_id(2)
    n_chunks = SEQ // ATT_KC
    lam = (jnp.exp(jnp.sum(lq1_ref[...] * lk1_ref[...], axis=-1, keepdims=True))
           - jnp.exp(jnp.sum(lq2_ref[...] * lk2_ref[...], axis=-1, keepdims=True))
           + LAMBDA_INIT)
    q = q_ref[...] * jnp.asarray(ATT_HALF_DIM ** -0.5, bf16)
    lane = lax.broadcasted_iota(i32, q.shape, 1)
    zero = jnp.zeros_like(q)

    def softmax_pv(qm):
        m = jnp.full((ATT_TQ, 1), -jnp.inf, f32)
        for c in range(n_chunks):
            d = jnp.clip(c - t * (ATT_TQ // ATT_KC) + 2, 0, ATT_EXT_CHUNKS - 1)
            off = pl.multiple_of(d * ATT_KC, ATT_KC)
            s = lax.dot_general(qm, k_ref[c * ATT_KC:(c + 1) * ATT_KC, :],
                                (((1,), (1,)), ((), ())), preferred_element_type=f32)
            s = s + band_ref[0, :, pl.ds(off, ATT_KC)]
            s_scr[:, c * ATT_KC:(c + 1) * ATT_KC] = s
            m = jnp.maximum(m, jnp.max(s, axis=-1, keepdims=True))
        l = jnp.zeros((ATT_TQ, 1), f32)
        o = jnp.zeros((ATT_TQ, ATT_V_DIM), f32)
        for c in range(n_chunks):
            p = jnp.exp(s_scr[:, c * ATT_KC:(c + 1) * ATT_KC] - m)
            l = l + jnp.sum(p, axis=-1, keepdims=True)
            o = o + jnp.dot(p.astype(bf16), v_ref[c * ATT_KC:(c + 1) * ATT_KC, :],
                            preferred_element_type=f32)
        return o / l

    o1 = softmax_pv(jnp.where(lane < ATT_HALF_DIM, q, zero))
    o2 = softmax_pv(jnp.where(lane >= ATT_HALF_DIM, q, zero))
    out = o1 - lam * o2
    ms = jnp.mean(out * out, axis=-1, keepdims=True)
    y = out * lax.rsqrt(ms + EPS) * sg_ref[...]
    o_ref[...] = (y * (1.0 - LAMBDA_INIT)).astype(bf16)


def _attention(proj, band, lq1, lk1, lq2, lk2, subln_g):
    qt = SEQ // ATT_TQ
    k_col0 = QK_COLS // ATT_V_DIM
    v_col0 = 2 * QK_COLS // ATT_V_DIM
    vec = lambda n: pl.BlockSpec((1, n), lambda b, h, t: (0, 0))
    return pl.pallas_call(
        _attn_kernel,
        grid=(BATCH, ATT_HEADS, qt),
        in_specs=[pl.BlockSpec((ATT_TQ, ATT_V_DIM), lambda b, h, t: (b * qt + t, h)),
                  pl.BlockSpec((SEQ, ATT_V_DIM), lambda b, h, t: (b, k_col0 + h)),
                  pl.BlockSpec((SEQ, ATT_V_DIM), lambda b, h, t: (b, v_col0 + h)),
                  pl.BlockSpec((1, ATT_TQ, ATT_EXT_CHUNKS * ATT_KC), lambda b, h, t: (h, 0, 0)),
                  vec(ATT_HALF_DIM), vec(ATT_HALF_DIM), vec(ATT_HALF_DIM), vec(ATT_HALF_DIM),
                  vec(ATT_V_DIM)],
        out_specs=pl.BlockSpec((ATT_TQ, ATT_V_DIM), lambda b, h, t: (b * qt + t, h)),
        out_shape=jax.ShapeDtypeStruct((TOKENS, ATT_WIDTH), bf16),
        scratch_shapes=[pltpu.VMEM((ATT_TQ, SEQ), f32)],
        compiler_params=_cparams(("parallel", "parallel", "arbitrary"), 40),
        name="attn",
    )(proj, proj, proj, band, lq1.reshape(1, -1), lk1.reshape(1, -1), lq2.reshape(1, -1),
      lk2.reshape(1, -1), subln_g.reshape(1, -1))


def _conv_kernel(ap_ref, ac_ref, an_ref, gp_ref, gc_ref, gn_ref, w_ref, cb_ref, lg_ref, lb_ref,
                 o_ref, z_scr, c_scr):
    s = pl.program_id(1)
    last = pl.num_programs(1) - 1

    def glu(a_ref, g_ref):
        return a_ref[...].astype(f32) * jax.nn.sigmoid(g_ref[...].astype(f32))

    z_scr[0:CONV_HALO, :] = jnp.where(s > 0, glu(ap_ref, gp_ref), 0.0)
    z_scr[CONV_HALO + CONV_TS:, :] = jnp.where(s < last, glu(an_ref, gn_ref), 0.0)

    def glu_chunk(r0):
        a = ac_ref[pl.ds(r0, ROW_CHUNK), :].astype(f32)
        g = gc_ref[pl.ds(r0, ROW_CHUNK), :].astype(f32)
        z_scr[pl.ds(CONV_HALO + r0, ROW_CHUNK), :] = a * jax.nn.sigmoid(g)
    _row_chunks(CONV_TS, ROW_CHUNK, glu_chunk)

    lanes = 128
    tap0 = CONV_HALO - CONV_HALF

    def lane_chunk(lc, carry):
        cols = pl.ds(pl.multiple_of(lc * lanes, lanes), lanes)
        taps = [w_ref[j:j + 1, cols] for j in range(CONV_KERNEL)]
        bias = cb_ref[:, cols]
        for r0 in range(0, CONV_TS, CONV_RB):
            acc = jnp.zeros((CONV_RB, lanes), f32)
            for j in range(CONV_KERNEL):
                acc = acc + z_scr[r0 + tap0 + j:r0 + tap0 + j + CONV_RB, cols] * taps[j]
            c_scr[r0:r0 + CONV_RB, cols] = acc + bias
        return carry
    lax.fori_loop(0, CONV_CH // lanes, lane_chunk, 0)

    lg = lg_ref[...]
    lb = lb_ref[...]

    def ln_chunk(r0):
        c = c_scr[pl.ds(r0, ROW_CHUNK), :]
        mu = jnp.mean(c, axis=-1, keepdims=True)
        cc = c - mu
        var = jnp.mean(cc * cc, axis=-1, keepdims=True)
        y = cc * lax.rsqrt(var + EPS) * lg + lb
        o_ref[pl.ds(r0, ROW_CHUNK), :] = (y * jax.nn.sigmoid(y)).astype(bf16)
    _row_chunks(CONV_TS, ROW_CHUNK, ln_chunk)


def _conv(proj, conv_w, conv_b, ln_g, ln_b):
    st = SEQ // CONV_TS
    halo_per_tile = CONV_TS // CONV_HALO
    halo_per_seq = SEQ // CONV_HALO
    a_col = (2 * QK_COLS + ATT_WIDTH) // CONV_CH
    g_col = a_col + 1

    def cur(col):
        return pl.BlockSpec((CONV_TS, CONV_CH), lambda b, s: (b * st + s, col))

    def prev(col):
        return pl.BlockSpec(
            (CONV_HALO, CONV_CH),
            lambda b, s: (jnp.maximum(b * halo_per_seq + s * halo_per_tile - 1, 0), col))

    def nxt(col):
        return pl.BlockSpec(
            (CONV_HALO, CONV_CH),
            lambda b, s: (jnp.minimum(b * halo_per_seq + (s + 1) * halo_per_tile,
                                      TOKENS // CONV_HALO - 1), col))

    vec = pl.BlockSpec((1, CONV_CH), lambda b, s: (0, 0))
    return pl.pallas_call(
        _conv_kernel,
        grid=(BATCH, st),
        in_specs=[prev(a_col), cur(a_col), nxt(a_col), prev(g_col), cur(g_col), nxt(g_col),
                  pl.BlockSpec((CONV_KERNEL, CONV_CH), lambda b, s: (0, 0)), vec, vec, vec],
        out_specs=pl.BlockSpec((CONV_TS, CONV_CH), lambda b, s: (b * st + s, 0)),
        out_shape=jax.ShapeDtypeStruct((TOKENS, CONV_CH), bf16),
        scratch_shapes=[pltpu.VMEM((CONV_TS + 2 * CONV_HALO, CONV_CH), f32),
                        pltpu.VMEM((CONV_TS, CONV_CH), f32)],
        compiler_params=_cparams(("parallel", "arbitrary"), 32),
        name="conv",
    )(proj, proj, proj, proj, proj, proj, conv_w, conv_b.reshape(1, -1), ln_g.reshape(1, -1),
      ln_b.reshape(1, -1))


def _outproj_kernel(att_ref, cv_ref, wo_ref, x_ref, g1_ref, sc2_ref, sh2_ref, n2_ref, wr_ref,
                    br_ref, x1_ref, h2p_ref, ti_ref, tw_ref, rk_ref, cnt_ref, carry_scr, h2_scr):
    @pl.when(pl.program_id(0) == 0)
    def _():
        carry_scr[...] = jnp.zeros_like(carry_scr)

    mixed = (jnp.dot(att_ref[...], wo_ref[0:ATT_WIDTH, :], preferred_element_type=f32)
             + jnp.dot(cv_ref[...], wo_ref[ATT_WIDTH:, :], preferred_element_type=f32))
    x1_ref[...] = x_ref[...] + g1_ref[0] * mixed

    n2 = n2_ref[...]
    one_plus_scale = 1.0 + sc2_ref[0]
    shift = sh2_ref[0]

    def chunk(r0):
        rows = pl.ds(r0, ROW_CHUNK)
        x1 = x1_ref[rows, :]
        ms = jnp.mean(x1 * x1, axis=-1, keepdims=True)
        y = x1 * lax.rsqrt(ms + EPS) * n2
        h2_scr[rows, :] = y * one_plus_scale + shift
        _store_packed(h2p_ref, r0, ROW_CHUNK, lambda cols: h2_scr[rows, cols])
    _row_chunks(OUT_TM, ROW_CHUNK, chunk)

    logits = lax.dot_general(wr_ref[...], h2_scr[...], (((1,), (1,)), ((), ())),
                             precision=lax.Precision.HIGHEST,
                             preferred_element_type=f32) + br_ref[...]
    eidx = lax.broadcasted_iota(i32, logits.shape, 0)
    vals = logits
    sels, tops = [], []
    for k in range(TOP_K):
        m = jnp.max(vals, axis=0, keepdims=True)
        idx = jnp.min(jnp.where(vals == m, eidx, N_EXPERTS), axis=0, keepdims=True)
        sel = eidx == idx
        ti_ref[k:k + 1, :] = idx
        sels.append(sel)
        tops.append(m)
        vals = jnp.where(sel, -jnp.inf, vals)
    exps = [jnp.exp(v - tops[0]) for v in tops]
    denom = exps[0] + exps[1] + exps[2] + exps[3]
    for k in range(TOP_K):
        tw_ref[k:k + 1, :] = exps[k] / denom

    onehot = jnp.where(sels[0] | sels[1] | sels[2] | sels[3], 1.0, 0.0)
    r_i = lax.broadcasted_iota(i32, (OUT_TM, OUT_TM), 0)
    c_i = lax.broadcasted_iota(i32, (OUT_TM, OUT_TM), 1)
    earlier = jnp.where(r_i < c_i, 1.0, 0.0).astype(bf16)
    before = jnp.dot(onehot.astype(bf16), earlier, preferred_element_type=f32) + carry_scr[...]
    for k in range(TOP_K):
        rk_ref[k:k + 1, :] = jnp.sum(jnp.where(sels[k], before, 0.0), axis=0,
                                     keepdims=True).astype(i32)
    total = carry_scr[...] + jnp.sum(onehot, axis=1, keepdims=True)
    carry_scr[...] = total
    cnt_ref[...] = jnp.broadcast_to(total, cnt_ref.shape).astype(i32)


def _outproj(att, cv, wo_bf, x2d, gate1, scale2, shift2, norm2_g, w_router, b_router):
    tiles_per_batch = SEQ // OUT_TM
    row = lambda w: pl.BlockSpec((OUT_TM, w), lambda i: (i, 0))
    mod = pl.BlockSpec((1, 1, D_MODEL), lambda i: (i // tiles_per_batch, 0, 0))
    slot = pl.BlockSpec((TOP_K, OUT_TM), lambda i: (0, i))
    return pl.pallas_call(
        _outproj_kernel,
        grid=(TOKENS // OUT_TM,),
        in_specs=[row(ATT_WIDTH), row(CONV_CH),
                  pl.BlockSpec((D_MODEL, D_MODEL), lambda i: (0, 0), pipeline_mode=pl.Buffered(1)),
                  row(D_MODEL), mod, mod, mod,
                  pl.BlockSpec((1, D_MODEL), lambda i: (0, 0)),
                  pl.BlockSpec((N_EXPERTS, D_MODEL), lambda i: (0, 0)),
                  pl.BlockSpec((N_EXPERTS, 1), lambda i: (0, 0))],
        out_specs=[row(D_MODEL),
                   pl.BlockSpec((OUT_TM * PACK_ROWS, PACK_LANES), lambda i: (i, 0)),
                   slot, slot, slot,
                   pl.BlockSpec((N_EXPERTS, 128), lambda i: (0, 0))],
        out_shape=[jax.ShapeDtypeStruct((TOKENS, D_MODEL), f32),
                   jax.ShapeDtypeStruct((TOKENS * PACK_ROWS, PACK_LANES), jnp.uint32),
                   jax.ShapeDtypeStruct((TOP_K, TOKENS), i32),
                   jax.ShapeDtypeStruct((TOP_K, TOKENS), f32),
                   jax.ShapeDtypeStruct((TOP_K, TOKENS), i32),
                   jax.ShapeDtypeStruct((N_EXPERTS, 128), i32)],
        scratch_shapes=[pltpu.VMEM((N_EXPERTS, 1), f32), pltpu.VMEM((OUT_TM, D_MODEL), f32)],
        compiler_params=_cparams(("arbitrary",), 52),
        name="outproj",
    )(att, cv, wo_bf, x2d, gate1.reshape(BATCH, 1, D_MODEL), scale2.reshape(BATCH, 1, D_MODEL),
      shift2.reshape(BATCH, 1, D_MODEL), norm2_g.reshape(1, D_MODEL), w_router.T,
      b_router.reshape(N_EXPERTS, 1))


def _token_copy(src_hbm, token, dst_buf, row, sem):
    dst = dst_buf.at[pl.ds(pl.multiple_of(row * PACK_ROWS, PACK_ROWS), PACK_ROWS), :]
    return pltpu.make_async_copy(src_hbm.at[token], dst, sem)


def _dispatch_kernel(src_cur, src_nxt, h2p_hbm, o_ref, buf, sem):
    m = pl.program_id(0)
    n = pl.num_programs(0)
    slot = m % 2

    def start_tile(idx_ref, s):
        def body(g, carry):
            for u in range(DMA_UNROLL):
                r = g * DMA_UNROLL + u
                _token_copy(h2p_hbm, idx_ref[0, 0, r], buf.at[s], r, sem.at[s]).start()
            return carry
        lax.fori_loop(0, MOE_TM // DMA_UNROLL, body, 0)

    @pl.when(m == 0)
    def _():
        start_tile(src_cur, 0)

    @pl.when(m + 1 < n)
    def _():
        start_tile(src_nxt, 1 - slot)

    pltpu.make_async_copy(buf.at[slot], buf.at[slot], sem.at[slot]).wait()
    for s in range(PACK_ROWS):
        lo, hi = _load_packed(buf.at[slot], MOE_TM, s)
        o_ref[:, 2 * s * PACK_LANES:(2 * s + 1) * PACK_LANES] = lo.astype(bf16)
        o_ref[:, (2 * s + 1) * PACK_LANES:(2 * s + 2) * PACK_LANES] = hi.astype(bf16)


def _dispatch(src_rows, h2_packed):
    src3 = src_rows.reshape(MOE_NT, 1, MOE_TM)
    return pl.pallas_call(
        _dispatch_kernel,
        grid=(MOE_NT,),
        in_specs=[pl.BlockSpec((1, 1, MOE_TM), lambda m: (m, 0, 0), memory_space=pltpu.SMEM),
                  pl.BlockSpec((1, 1, MOE_TM), lambda m: (jnp.minimum(m + 1, MOE_NT - 1), 0, 0),
                               memory_space=pltpu.SMEM),
                  pl.BlockSpec(memory_space=pl.ANY)],
        out_specs=pl.BlockSpec((MOE_TM, D_MODEL), lambda m: (m, 0)),
        out_shape=jax.ShapeDtypeStruct((MOE_ROWS, D_MODEL), bf16),
        scratch_shapes=[pltpu.VMEM((2, MOE_TM * PACK_ROWS, PACK_LANES), jnp.uint32),
                        pltpu.SemaphoreType.DMA((2,))],
        compiler_params=_cparams(("arbitrary",), 24),
        name="dispatch",
    )(src3, src3, h2_packed.reshape(TOKENS, PACK_ROWS, PACK_LANES))


def _weights_changed(te_ref, m):
    prev = te_ref[jnp.maximum(m - 1, 0)]
    return (m == 0) | (te_ref[m] != prev)


def _gmm1_kernel(te_ref, nv_ref, xs_ref, wg_ref, wu_ref, bg_ref, bu_ref, o_ref, wg_bf, wu_bf):
    m = pl.program_id(1)

    @pl.when(_weights_changed(te_ref, m))
    def _():
        wg_bf[...] = wg_ref[0].astype(bf16)
        wu_bf[...] = wu_ref[0].astype(bf16)

    @pl.when(m < nv_ref[0])
    def _():
        x = xs_ref[...]
        gate = jnp.dot(x, wg_bf[...], preferred_element_type=f32) + bg_ref[0]
        up = jnp.dot(x, wu_bf[...], preferred_element_type=f32) + bu_ref[0]
        gate = jnp.minimum(gate, SWIGLU_LIMIT)
        up = jnp.clip(up, -SWIGLU_LIMIT, SWIGLU_LIMIT)
        act = (up + 1.0) * (gate * jax.nn.sigmoid(SWIGLU_ALPHA * gate))
        o_ref[...] = act.astype(bf16)

    @pl.when(m >= nv_ref[0])
    def _():
        o_ref[...] = jnp.zeros_like(o_ref)


def _gmm1(tile_expert, n_valid, xs, w_gate_up, b_gate_up):
    up0 = D_FF // GMM1_TN
    grid_spec = pltpu.PrefetchScalarGridSpec(
        num_scalar_prefetch=2,
        grid=(D_FF // GMM1_TN, MOE_NT),
        in_specs=[pl.BlockSpec((MOE_TM, D_MODEL), lambda n, m, te, nv: (m, 0)),
                  pl.BlockSpec((1, D_MODEL, GMM1_TN), lambda n, m, te, nv: (te[m], 0, n)),
                  pl.BlockSpec((1, D_MODEL, GMM1_TN), lambda n, m, te, nv: (te[m], 0, up0 + n)),
                  pl.BlockSpec((1, 1, GMM1_TN), lambda n, m, te, nv: (te[m], 0, n)),
                  pl.BlockSpec((1, 1, GMM1_TN), lambda n, m, te, nv: (te[m], 0, up0 + n))],
        out_specs=pl.BlockSpec((MOE_TM, GMM1_TN), lambda n, m, te, nv: (m, n)),
        scratch_shapes=[pltpu.VMEM((D_MODEL, GMM1_TN), bf16), pltpu.VMEM((D_MODEL, GMM1_TN), bf16)])
    b3 = b_gate_up.reshape(N_EXPERTS, 1, 2 * D_FF)
    return pl.pallas_call(
        _gmm1_kernel,
        grid_spec=grid_spec,
        out_shape=jax.ShapeDtypeStruct((MOE_ROWS, D_FF), bf16),
        compiler_params=_cparams(("arbitrary", "arbitrary"), 52),
        name="gmm1",
    )(tile_expert, n_valid, xs, w_gate_up, w_gate_up, b3, b3)


def _gmm2_kernel(te_ref, nv_ref, a_ref, wd_ref, bd_ref, o_ref, wd_bf, y_scr):
    m = pl.program_id(1)

    @pl.when(_weights_changed(te_ref, m))
    def _():
        wd_bf[...] = wd_ref[0].astype(bf16)

    @pl.when(m < nv_ref[0])
    def _():
        y_scr[...] = jnp.dot(a_ref[...], wd_bf[...], preferred_element_type=f32) + bd_ref[0]
        _store_packed(o_ref, 0, MOE_TM, lambda cols: y_scr[:, cols])

    @pl.when(m >= nv_ref[0])
    def _():
        o_ref[...] = jnp.zeros_like(o_ref)


def _gmm2(tile_expert, n_valid, act, w_down, b_down):
    assert GMM2_TN == D_MODEL
    grid_spec = pltpu.PrefetchScalarGridSpec(
        num_scalar_prefetch=2,
        grid=(1, MOE_NT),
        in_specs=[pl.BlockSpec((MOE_TM, D_FF), lambda n, m, te, nv: (m, 0)),
                  pl.BlockSpec((1, D_FF, GMM2_TN), lambda n, m, te, nv: (te[m], 0, 0)),
                  pl.BlockSpec((1, 1, GMM2_TN), lambda n, m, te, nv: (te[m], 0, 0))],
        out_specs=pl.BlockSpec((MOE_TM * PACK_ROWS, PACK_LANES), lambda n, m, te, nv: (m, 0)),
        scratch_shapes=[pltpu.VMEM((D_FF, GMM2_TN), bf16), pltpu.VMEM((MOE_TM, D_MODEL), f32)])
    return pl.pallas_call(
        _gmm2_kernel,
        grid_spec=grid_spec,
        out_shape=jax.ShapeDtypeStruct((MOE_ROWS * PACK_ROWS, PACK_LANES), jnp.uint32),
        compiler_params=_cparams(("arbitrary", "arbitrary"), 56),
        name="gmm2",
    )(tile_expert, n_valid, act, w_down, b_down.reshape(N_EXPERTS, 1, D_MODEL))


def _combine_kernel(pos_cur, pos_nxt, y_hbm, x1_ref, g2_ref, tw_ref, fg_ref, o_ref, buf, sem):
    i = pl.program_id(0)
    n = pl.num_programs(0)
    slot = i % 2

    def start_tile(pos_ref, s):
        def body(g, carry):
            for u in range(DMA_UNROLL // TOP_K):
                r = g * (DMA_UNROLL // TOP_K) + u
                for k in range(TOP_K):
                    _token_copy(y_hbm, pos_ref[0, k, r], buf.at[s, k], r, sem.at[s]).start()
            return carry
        lax.fori_loop(0, CMB_TB * TOP_K // DMA_UNROLL, body, 0)

    @pl.when(i == 0)
    def _():
        start_tile(pos_cur, 0)

    @pl.when(i + 1 < n)
    def _():
        start_tile(pos_nxt, 1 - slot)

    pltpu.make_async_copy(buf.at[slot], buf.at[slot], sem.at[slot]).wait()

    weights = [jnp.broadcast_to(tw_ref[:, k:k + 1], (CMB_TB, PACK_LANES)) for k in range(TOP_K)]
    sumsq = jnp.zeros((CMB_TB, 1), f32)
    for s in range(PACK_ROWS):
        lo, hi = _load_packed(buf.at[slot, 0], CMB_TB, s)
        lo, hi = lo * weights[0], hi * weights[0]
        for k in range(1, TOP_K):
            lo_k, hi_k = _load_packed(buf.at[slot, k], CMB_TB, s)
            lo, hi = lo + lo_k * weights[k], hi + hi_k * weights[k]
        for c, moe in ((2 * s, lo), (2 * s + 1, hi)):
            cols = slice(c * PACK_LANES, (c + 1) * PACK_LANES)
            x2 = x1_ref[:, cols] + g2_ref[0, :, cols] * moe
            o_ref[:, cols] = x2
            sumsq = sumsq + jnp.sum(x2 * x2, axis=-1, keepdims=True)
    inv = lax.rsqrt(sumsq * (1.0 / D_MODEL) + EPS)
    o_ref[...] = o_ref[...] * inv * fg_ref[...]


def _combine(pos, y_packed, x1, gate2, top_w_rows, final_g):
    nt = TOKENS // CMB_TB
    tiles_per_batch = SEQ // CMB_TB
    pos3 = pos.reshape(TOP_K, nt, CMB_TB).transpose(1, 0, 2)
    return pl.pallas_call(
        _combine_kernel,
        grid=(nt,),
        in_specs=[pl.BlockSpec((1, TOP_K, CMB_TB), lambda i: (i, 0, 0), memory_space=pltpu.SMEM),
                  pl.BlockSpec((1, TOP_K, CMB_TB), lambda i: (jnp.minimum(i + 1, nt - 1), 0, 0),
                               memory_space=pltpu.SMEM),
                  pl.BlockSpec(memory_space=pl.ANY),
                  pl.BlockSpec((CMB_TB, D_MODEL), lambda i: (i, 0)),
                  pl.BlockSpec((1, 1, D_MODEL), lambda i: (i // tiles_per_batch, 0, 0)),
                  pl.BlockSpec((CMB_TB, TOP_K), lambda i: (i, 0)),
                  pl.BlockSpec((1, D_MODEL), lambda i: (0, 0))],
        out_specs=pl.BlockSpec((CMB_TB, D_MODEL), lambda i: (i, 0)),
        out_shape=jax.ShapeDtypeStruct((TOKENS, D_MODEL), f32),
        scratch_shapes=[pltpu.VMEM((2, TOP_K, CMB_TB * PACK_ROWS, PACK_LANES), jnp.uint32),
                        pltpu.SemaphoreType.DMA((2,))],
        compiler_params=_cparams(("arbitrary",), 32),
        name="combine",
    )(pos3, pos3, y_packed.reshape(MOE_ROWS, PACK_ROWS, PACK_LANES), x1,
      gate2.reshape(BATCH, 1, D_MODEL), top_w_rows, final_g.reshape(1, D_MODEL))


def _routing_tables(counts, top_i, rank):
    padded = (counts + MOE_TM - 1) // MOE_TM * MOE_TM
    ends = jnp.cumsum(padded)
    starts = ends - padded
    n_valid = ends[-1] // MOE_TM
    tile_ids = jnp.minimum(jnp.arange(MOE_NT, dtype=i32), n_valid - 1)
    tile_expert = jnp.sum(tile_ids[:, None] >= (ends // MOE_TM)[None, :], axis=1).astype(i32)
    experts = jnp.arange(N_EXPERTS, dtype=i32)
    pos = rank + jnp.sum(jnp.where(top_i[..., None] == experts, starts, 0), axis=-1)
    token = jnp.broadcast_to(jnp.arange(TOKENS, dtype=i32)[None, :], pos.shape)
    src_rows = jnp.zeros((MOE_ROWS,), i32).at[pos.reshape(-1)].set(token.reshape(-1))
    return tile_expert, n_valid.reshape(1).astype(i32), pos.astype(i32), src_rows


def kernel(x, c, w_ada, b_ada, norm1_g, w_in, lambda_q1, lambda_k1, lambda_q2, lambda_k2, subln_g,
           rel_bias, conv_w, conv_b, conv_ln_g, conv_ln_b, w_out, norm2_g, w_router, b_router,
           w_gate_up, b_gate_up, w_down, b_down, final_g):
    assert x.shape == (BATCH, SEQ, D_MODEL) and w_ada.shape[0] == 1
    l = 0
    x2d = x.reshape(TOKENS, D_MODEL)
    mod = _adaln(c, w_ada[l], b_ada[l])
    shift1, scale1, gate1, shift2, scale2, gate2 = [mod[i] for i in range(6)]

    proj = _inproj(x2d, norm1_g[l], scale1, shift1, w_in[l].astype(bf16))
    att = _attention(proj, _bias_band(rel_bias), lambda_q1[l], lambda_k1[l], lambda_q2[l],
                     lambda_k2[l], subln_g[l])
    cv = _conv(proj, conv_w[l], conv_b[l], conv_ln_g[l], conv_ln_b[l])

    x1, h2_packed, top_i, top_w, rank, counts = _outproj(
        att, cv, w_out[l].astype(bf16), x2d, gate1, scale2, shift2, norm2_g[l], w_router[l],
        b_router[l])
    tile_expert, n_valid, pos, src_rows = _routing_tables(counts[:, 0], top_i, rank)

    xs = _dispatch(src_rows, h2_packed)
    act = _gmm1(tile_expert, n_valid, xs, w_gate_up[l], b_gate_up[l])
    y = _gmm2(tile_expert, n_valid, act, w_down[l], b_down[l])
    out = _combine(pos, y, x1, gate2, top_w.T, final_g)
    return out.reshape(BATCH, SEQ, D_MODEL)
```

```python
import functools
import math

import jax
import jax.numpy as jnp
from jax import lax
from jax.experimental import pallas as pl
from jax.experimental.pallas import tpu as pltpu

f32 = jnp.float32
bf16 = jnp.bfloat16
i32 = jnp.int32

D_MODEL = 2048
BATCH = 4
SEQ = 2048
TOKENS = BATCH * SEQ
ATT_HEADS = 8
ATT_HALF_DIM = 64
ATT_V_DIM = 128
ATT_WIDTH = ATT_HEADS * ATT_V_DIM
CONV_CH = D_MODEL - ATT_WIDTH
CONV_KERNEL = 31
CONV_HALF = CONV_KERNEL // 2
QK_COLS = ATT_HEADS * 2 * ATT_HALF_DIM
IN_COLS = 2 * QK_COLS + ATT_WIDTH + 2 * CONV_CH
N_BUCKETS = 32
MAX_DISTANCE = 128
N_EXPERTS = 32
TOP_K = 4
D_FF = D_MODEL
SWIGLU_LIMIT = 7.0
SWIGLU_ALPHA = 1.702
EPS = 1e-6
LAMBDA_INIT = 0.8 - 0.6 * math.exp(-0.3 * 0)

ADA_TN = 1024
INP_TM = 1024
INP_TN = 1024
ATT_TQ = 512
ATT_KC = 256
ATT_EXT_CHUNKS = 6
CONV_TS = 512
CONV_HALO = 16
CONV_RB = 32
OUT_TM = 512
MOE_TM = 256
MOE_NT = TOKENS * TOP_K // MOE_TM + N_EXPERTS
MOE_ROWS = MOE_NT * MOE_TM
GMM1_TN = 1024
GMM2_TN = 2048
DSP_TB = 256
CMB_TB = 128
ROW_CHUNK = 64
DMA_UNROLL = 8

_MIB = 1024 * 1024


def _cparams(sem, vmem_mib):
    return pltpu.CompilerParams(dimension_semantics=sem, vmem_limit_bytes=vmem_mib * _MIB)


PACK_ROWS = 8
PACK_LANES = 128
_HI_MASK = 0xFFFF0000


def _pack_pair(lo, hi):
    lo_b = lax.shift_right_logical(
        lax.bitcast_convert_type(lo.astype(bf16).astype(f32), jnp.uint32), jnp.uint32(16))
    hi_b = lax.bitcast_convert_type(hi.astype(bf16).astype(f32), jnp.uint32) & jnp.uint32(_HI_MASK)
    return lo_b | hi_b


def _unpack_pair(w):
    lo = lax.bitcast_convert_type(lax.shift_left(w, jnp.uint32(16)), f32)
    hi = lax.bitcast_convert_type(w & jnp.uint32(_HI_MASK), f32)
    return lo, hi


def _store_packed(dst_ref, token0, n_tokens, get_cols):
    for s in range(PACK_ROWS):
        lo = get_cols(slice(2 * s * PACK_LANES, (2 * s + 1) * PACK_LANES))
        hi = get_cols(slice((2 * s + 1) * PACK_LANES, (2 * s + 2) * PACK_LANES))
        dst_ref[pl.ds(token0 * PACK_ROWS + s, n_tokens, stride=PACK_ROWS), :] = _pack_pair(lo, hi)


def _load_packed(src_ref, n_tokens, s):
    return _unpack_pair(src_ref[pl.ds(s, n_tokens, stride=PACK_ROWS), :])


def _row_chunks(n_rows, chunk, body):
    def step(i, carry):
        body(pl.multiple_of(i * chunk, chunk))
        return carry
    lax.fori_loop(0, n_rows // chunk, step, 0)


def _adaln_kernel(c_ref, w_ref, b_ref, o_ref):
    c = c_ref[...]
    cs = c * jax.nn.sigmoid(c)
    o_ref[0] = jnp.dot(cs.astype(bf16), w_ref[...].astype(bf16),
                       preferred_element_type=f32) + b_ref[...]


def _adaln(c, w_ada, b_ada):
    n = w_ada.shape[1]
    per_vec = D_MODEL // ADA_TN
    return pl.pallas_call(
        _adaln_kernel,
        grid=(n // ADA_TN,),
        in_specs=[pl.BlockSpec((BATCH, D_MODEL), lambda j: (0, 0)),
                  pl.BlockSpec((D_MODEL, ADA_TN), lambda j: (0, j)),
                  pl.BlockSpec((1, ADA_TN), lambda j: (0, j))],
        out_specs=pl.BlockSpec((1, BATCH, ADA_TN), lambda j: (j // per_vec, 0, j % per_vec)),
        out_shape=jax.ShapeDtypeStruct((n // D_MODEL, BATCH, D_MODEL), f32),
        compiler_params=_cparams(("arbitrary",), 40),
        name="adaln",
    )(c, w_ada, b_ada.reshape(1, n))


def _inproj_kernel(x_ref, g_ref, sc_ref, sh_ref, w_ref, o_ref, h_scr):
    @pl.when(pl.program_id(1) == 0)
    def _():
        g = g_ref[...]
        one_plus_scale = 1.0 + sc_ref[0]
        shift = sh_ref[0]

        def chunk(r0):
            x = x_ref[pl.ds(r0, ROW_CHUNK), :]
            ms = jnp.mean(x * x, axis=-1, keepdims=True)
            y = x * lax.rsqrt(ms + EPS) * g
            h_scr[pl.ds(r0, ROW_CHUNK), :] = (y * one_plus_scale + shift).astype(bf16)
        _row_chunks(INP_TM, ROW_CHUNK, chunk)

    o_ref[...] = jnp.dot(h_scr[...], w_ref[...], preferred_element_type=f32).astype(bf16)


def _inproj(x2d, norm_g, scale, shift, w_bf):
    tiles_per_batch = SEQ // INP_TM
    return pl.pallas_call(
        _inproj_kernel,
        grid=(TOKENS // INP_TM, IN_COLS // INP_TN),
        in_specs=[pl.BlockSpec((INP_TM, D_MODEL), lambda i, j: (i, 0)),
                  pl.BlockSpec((1, D_MODEL), lambda i, j: (0, 0)),
                  pl.BlockSpec((1, 1, D_MODEL), lambda i, j: (i // tiles_per_batch, 0, 0)),
                  pl.BlockSpec((1, 1, D_MODEL), lambda i, j: (i // tiles_per_batch, 0, 0)),
                  pl.BlockSpec((D_MODEL, INP_TN), lambda i, j: (0, j))],
        out_specs=pl.BlockSpec((INP_TM, INP_TN), lambda i, j: (i, j)),
        out_shape=jax.ShapeDtypeStruct((TOKENS, IN_COLS), bf16),
        scratch_shapes=[pltpu.VMEM((INP_TM, D_MODEL), bf16)],
        compiler_params=_cparams(("parallel", "arbitrary"), 48),
        name="inproj",
    )(x2d, norm_g.reshape(1, D_MODEL), scale.reshape(BATCH, 1, D_MODEL),
      shift.reshape(BATCH, 1, D_MODEL), w_bf)


def _t5_bucket(rel):
    half = N_BUCKETS // 2
    max_exact = half // 2
    ret = jnp.where(rel > 0, half, 0)
    n = jnp.abs(rel)
    nf = jnp.maximum(n, 1).astype(f32)
    large = max_exact + (jnp.log(nf / max_exact) / math.log(MAX_DISTANCE / max_exact)
                         * (half - max_exact)).astype(i32)
    large = jnp.minimum(large, half - 1)
    return ret + jnp.where(n < max_exact, n, large)


def _bias_band(rel_bias):
    r = jnp.arange(ATT_TQ)[:, None]
    j = jnp.arange(ATT_EXT_CHUNKS * ATT_KC)[None, :]
    rel = j - 2 * ATT_KC - r
    onehot = (_t5_bucket(rel)[..., None] == jnp.arange(N_BUCKETS)).astype(f32)
    return jnp.einsum("rjn,nh->hrj", onehot, rel_bias.astype(f32),
                      precision=lax.Precision.HIGHEST)


def _attn_kernel(q_ref, k_ref, v_ref, band_ref, lq1_ref, lk1_ref, lq2_ref, lk2_ref, sg_ref,
                 o_ref, s_scr):
    t = pl.program_id(2)
    n_chunks = SEQ // ATT_KC
    lam = (jnp.exp(jnp.sum(lq1_ref[...] * lk1_ref[...], axis=-1, keepdims=True))
           - jnp.exp(jnp.sum(lq2_ref[...] * lk2_ref[...], axis=-1, keepdims=True))
           + LAMBDA_INIT)
    q = q_ref[...] * jnp.asarray(ATT_HALF_DIM ** -0.5, bf16)
    lane = lax.broadcasted_iota(i32, q.shape, 1)
    zero = jnp.zeros_like(q)

    def softmax_pv(qm):
        m = jnp.full((ATT_TQ, 1), -jnp.inf, f32)
        for c in range(n_chunks):
            d = jnp.clip(c - t * (ATT_TQ // ATT_KC) + 2, 0, ATT_EXT_CHUNKS - 1)
            off = pl.multiple_of(d * ATT_KC, ATT_KC)
            s = lax.dot_general(qm, k_ref[c * ATT_KC:(c + 1) * ATT_KC, :],
                                (((1,), (1,)), ((), ())), preferred_element_type=f32)
            s = s + band_ref[0, :, pl.ds(off, ATT_KC)]
            s_scr[:, c * ATT_KC:(c + 1) * ATT_KC] = s
            m = jnp.maximum(m, jnp.max(s, axis=-1, keepdims=True))
        l = jnp.zeros((ATT_TQ, 1), f32)
        o = jnp.zeros((ATT_TQ, ATT_V_DIM), f32)
        for c in range(n_chunks):
            p = jnp.exp(s_scr[:, c * ATT_KC:(c + 1) * ATT_KC] - m)
            l = l + jnp.sum(p, axis=-1, keepdims=True)
            o = o + jnp.dot(p.astype(bf16), v_ref[c * ATT_KC:(c + 1) * ATT_KC, :],
                            preferred_element_type=f32)
        return o / l

    o1 = softmax_pv(jnp.where(lane < ATT_HALF_DIM, q, zero))
    o2 = softmax_pv(jnp.where(lane >= ATT_HALF_DIM, q, zero))
    out = o1 - lam * o2
    ms = jnp.mean(out * out, axis=-1, keepdims=True)
    y = out * lax.rsqrt(ms + EPS) * sg_ref[...]
    o_ref[...] = (y * (1.0 - LAMBDA_INIT)).astype(bf16)


def _attention(proj, band, lq1, lk1, lq2, lk2, subln_g):
    qt = SEQ // ATT_TQ
    k_col0 = QK_COLS // ATT_V_DIM
    v_col0 = 2 * QK_COLS // ATT_V_DIM
    vec = lambda n: pl.BlockSpec((1, n), lambda b, h, t: (0, 0))
    return pl.pallas_call(
        _attn_kernel,
        grid=(BATCH, ATT_HEADS, qt),
        in_specs=[pl.BlockSpec((ATT_TQ, ATT_V_DIM), lambda b, h, t: (b * qt + t, h)),
                  pl.BlockSpec((SEQ, ATT_V_DIM), lambda b, h, t: (b, k_col0 + h)),
                  pl.BlockSpec((SEQ, ATT_V_DIM), lambda b, h, t: (b, v_col0 + h)),
                  pl.BlockSpec((1, ATT_TQ, ATT_EXT_CHUNKS * ATT_KC), lambda b, h, t: (h, 0, 0)),
                  vec(ATT_HALF_DIM), vec(ATT_HALF_DIM), vec(ATT_HALF_DIM), vec(ATT_HALF_DIM),
                  vec(ATT_V_DIM)],
        out_specs=pl.BlockSpec((ATT_TQ, ATT_V_DIM), lambda b, h, t: (b * qt + t, h)),
        out_shape=jax.ShapeDtypeStruct((TOKENS, ATT_WIDTH), bf16),
        scratch_shapes=[pltpu.VMEM((ATT_TQ, SEQ), f32)],
        compiler_params=_cparams(("parallel", "parallel", "arbitrary"), 40),
        name="attn",
    )(proj, proj, proj, band, lq1.reshape(1, -1), lk1.reshape(1, -1), lq2.reshape(1, -1),
      lk2.reshape(1, -1), subln_g.reshape(1, -1))


def _conv_kernel(ap_ref, ac_ref, an_ref, gp_ref, gc_ref, gn_ref, w_ref, cb_ref, lg_ref, lb_ref,
                 o_ref, z_scr, c_scr):
    s = pl.program_id(1)
    last = pl.num_programs(1) - 1

    def glu(a_ref, g_ref):
        return a_ref[...].astype(f32) * jax.nn.sigmoid(g_ref[...].astype(f32))

    z_scr[0:CONV_HALO, :] = jnp.where(s > 0, glu(ap_ref, gp_ref), 0.0)
    z_scr[CONV_HALO + CONV_TS:, :] = jnp.where(s < last, glu(an_ref, gn_ref), 0.0)

    def glu_chunk(r0):
        a = ac_ref[pl.ds(r0, ROW_CHUNK), :].astype(f32)
        g = gc_ref[pl.ds(r0, ROW_CHUNK), :].astype(f32)
        z_scr[pl.ds(CONV_HALO + r0, ROW_CHUNK), :] = a * jax.nn.sigmoid(g)
    _row_chunks(CONV_TS, ROW_CHUNK, glu_chunk)

    lanes = 128
    tap0 = CONV_HALO - CONV_HALF

    def lane_chunk(lc, carry):
        cols = pl.ds(pl.multiple_of(lc * lanes, lanes), lanes)
        taps = [w_ref[j:j + 1, cols] for j in range(CONV_KERNEL)]
        bias = cb_ref[:, cols]
        for r0 in range(0, CONV_TS, CONV_RB):
            acc = jnp.zeros((CONV_RB, lanes), f32)
            for j in range(CONV_KERNEL):
                acc = acc + z_scr[r0 + tap0 + j:r0 + tap0 + j + CONV_RB, cols] * taps[j]
            c_scr[r0:r0 + CONV_RB, cols] = acc + bias
        return carry
    lax.fori_loop(0, CONV_CH // lanes, lane_chunk, 0)

    lg = lg_ref[...]
    lb = lb_ref[...]

    def ln_chunk(r0):
        c = c_scr[pl.ds(r0, ROW_CHUNK), :]
        mu = jnp.mean(c, axis=-1, keepdims=True)
        cc = c - mu
        var = jnp.mean(cc * cc, axis=-1, keepdims=True)
        y = cc * lax.rsqrt(var + EPS) * lg + lb
        o_ref[pl.ds(r0, ROW_CHUNK), :] = (y * jax.nn.sigmoid(y)).astype(bf16)
    _row_chunks(CONV_TS, ROW_CHUNK, ln_chunk)


def _conv(proj, conv_w, conv_b, ln_g, ln_b):
    st = SEQ // CONV_TS
    halo_per_tile = CONV_TS // CONV_HALO
    halo_per_seq = SEQ // CONV_HALO
    a_col = (2 * QK_COLS + ATT_WIDTH) // CONV_CH
    g_col = a_col + 1

    def cur(col):
        return pl.BlockSpec((CONV_TS, CONV_CH), lambda b, s: (b * st + s, col))

    def prev(col):
        return pl.BlockSpec(
            (CONV_HALO, CONV_CH),
            lambda b, s: (jnp.maximum(b * halo_per_seq + s * halo_per_tile - 1, 0), col))

    def nxt(col):
        return pl.BlockSpec(
            (CONV_HALO, CONV_CH),
            lambda b, s: (jnp.minimum(b * halo_per_seq + (s + 1) * halo_per_tile,
                                      TOKENS // CONV_HALO - 1), col))

    vec = pl.BlockSpec((1, CONV_CH), lambda b, s: (0, 0))
    return pl.pallas_call(
        _conv_kernel,
        grid=(BATCH, st),
        in_specs=[prev(a_col), cur(a_col), nxt(a_col), prev(g_col), cur(g_col), nxt(g_col),
                  pl.BlockSpec((CONV_KERNEL, CONV_CH), lambda b, s: (0, 0)), vec, vec, vec],
        out_specs=pl.BlockSpec((CONV_TS, CONV_CH), lambda b, s: (b * st + s, 0)),
        out_shape=jax.ShapeDtypeStruct((TOKENS, CONV_CH), bf16),
        scratch_shapes=[pltpu.VMEM((CONV_TS + 2 * CONV_HALO, CONV_CH), f32),
                        pltpu.VMEM((CONV_TS, CONV_CH), f32)],
        compiler_params=_cparams(("parallel", "arbitrary"), 32),
        name="conv",
    )(proj, proj, proj, proj, proj, proj, conv_w, conv_b.reshape(1, -1), ln_g.reshape(1, -1),
      ln_b.reshape(1, -1))


def _outproj_kernel(att_ref, cv_ref, wo_ref, x_ref, g1_ref, sc2_ref, sh2_ref, n2_ref, wr_ref,
                    br_ref, x1_ref, h2p_ref, ti_ref, tw_ref, rk_ref, cnt_ref, carry_scr, h2_scr):
    @pl.when(pl.program_id(0) == 0)
    def _():
        carry_scr[...] = jnp.zeros_like(carry_scr)

    mixed = (jnp.dot(att_ref[...], wo_ref[0:ATT_WIDTH, :], preferred_element_type=f32)
             + jnp.dot(cv_ref[...], wo_ref[ATT_WIDTH:, :], preferred_element_type=f32))
    x1_ref[...] = x_ref[...] + g1_ref[0] * mixed

    n2 = n2_ref[...]
    one_plus_scale = 1.0 + sc2_ref[0]
    shift = sh2_ref[0]

    def chunk(r0):
        rows = pl.ds(r0, ROW_CHUNK)
        x1 = x1_ref[rows, :]
        ms = jnp.mean(x1 * x1, axis=-1, keepdims=True)
        y = x1 * lax.rsqrt(ms + EPS) * n2
        h2_scr[rows, :] = y * one_plus_scale + shift
        _store_packed(h2p_ref, r0, ROW_CHUNK, lambda cols: h2_scr[rows, cols])
    _row_chunks(OUT_TM, ROW_CHUNK, chunk)

    logits = lax.dot_general(wr_ref[...], h2_scr[...], (((1,), (1,)), ((), ())),
                             precision=lax.Precision.HIGHEST,
                             preferred_element_type=f32) + br_ref[...]
    eidx = lax.broadcasted_iota(i32, logits.shape, 0)
    vals = logits
    sels, tops = [], []
    for k in range(TOP_K):
        m = jnp.max(vals, axis=0, keepdims=True)
        idx = jnp.min(jnp.where(vals == m, eidx, N_EXPERTS), axis=0, keepdims=True)
        sel = eidx == idx
        ti_ref[k:k + 1, :] = idx
        sels.append(sel)
        tops.append(m)
        vals = jnp.where(sel, -jnp.inf, vals)
    exps = [jnp.exp(v - tops[0]) for v in tops]
    denom = exps[0] + exps[1] + exps[2] + exps[3]
    for k in range(TOP_K):
        tw_ref[k:k + 1, :] = exps[k] / denom

    onehot = jnp.where(sels[0] | sels[1] | sels[2] | sels[3], 1.0, 0.0)
    r_i = lax.broadcasted_iota(i32, (OUT_TM, OUT_TM), 0)
    c_i = lax.broadcasted_iota(i32, (OUT_TM, OUT_TM), 1)
    earlier = jnp.where(r_i < c_i, 1.0, 0.0).astype(bf16)
    before = jnp.dot(onehot.astype(bf16), earlier, preferred_element_type=f32) + carry_scr[...]
    for k in range(TOP_K):
        rk_ref[k:k + 1, :] = jnp.sum(jnp.where(sels[k], before, 0.0), axis=0,
                                     keepdims=True).astype(i32)
    total = carry_scr[...] + jnp.sum(onehot, axis=1, keepdims=True)
    carry_scr[...] = total
    cnt_ref[...] = jnp.broadcast_to(total, cnt_ref.shape).astype(i32)


def _outproj(att, cv, wo_bf, x2d, gate1, scale2, shift2, norm2_g, w_router, b_router):
    tiles_per_batch = SEQ // OUT_TM
    row = lambda w: pl.BlockSpec((OUT_TM, w), lambda i: (i, 0))
    mod = pl.BlockSpec((1, 1, D_MODEL), lambda i: (i // tiles_per_batch, 0, 0))
    slot = pl.BlockSpec((TOP_K, OUT_TM), lambda i: (0, i))
    return pl.pallas_call(
        _outproj_kernel,
        grid=(TOKENS // OUT_TM,),
        in_specs=[row(ATT_WIDTH), row(CONV_CH),
                  pl.BlockSpec((D_MODEL, D_MODEL), lambda i: (0, 0), pipeline_mode=pl.Buffered(1)),
                  row(D_MODEL), mod, mod, mod,
                  pl.BlockSpec((1, D_MODEL), lambda i: (0, 0)),
                  pl.BlockSpec((N_EXPERTS, D_MODEL), lambda i: (0, 0)),
                  pl.BlockSpec((N_EXPERTS, 1), lambda i: (0, 0))],
        out_specs=[row(D_MODEL),
                   pl.BlockSpec((OUT_TM * PACK_ROWS, PACK_LANES), lambda i: (i, 0)),
                   slot, slot, slot,
                   pl.BlockSpec((N_EXPERTS, 128), lambda i: (0, 0))],
        out_shape=[jax.ShapeDtypeStruct((TOKENS, D_MODEL), f32),
                   jax.ShapeDtypeStruct((TOKENS * PACK_ROWS, PACK_LANES), jnp.uint32),
                   jax.ShapeDtypeStruct((TOP_K, TOKENS), i32),
                   jax.ShapeDtypeStruct((TOP_K, TOKENS), f32),
                   jax.ShapeDtypeStruct((TOP_K, TOKENS), i32),
                   jax.ShapeDtypeStruct((N_EXPERTS, 128), i32)],
        scratch_shapes=[pltpu.VMEM((N_EXPERTS, 1), f32), pltpu.VMEM((OUT_TM, D_MODEL), f32)],
        compiler_params=_cparams(("arbitrary",), 52),
        name="outproj",
    )(att, cv, wo_bf, x2d, gate1.reshape(BATCH, 1, D_MODEL), scale2.reshape(BATCH, 1, D_MODEL),
      shift2.reshape(BATCH, 1, D_MODEL), norm2_g.reshape(1, D_MODEL), w_router.T,
      b_router.reshape(N_EXPERTS, 1))


def _token_copy(src_hbm, token, dst_buf, row, sem):
    dst = dst_buf.at[pl.ds(pl.multiple_of(row * PACK_ROWS, PACK_ROWS), PACK_ROWS), :]
    return pltpu.make_async_copy(src_hbm.at[token], dst, sem)


_PAD_BITS = tuple(range(MOE_TM.bit_length() - 2, -1, -1))


def _pad_fill_copies(pad_start, pad_len, zeros, xs_hbm, sem):
    pairs = []
    for b in _PAD_BITS:
        size = 1 << b
        off = pad_start + lax.shift_left(lax.shift_right_logical(pad_len, b + 1), b + 1)
        copy = pltpu.make_async_copy(zeros.at[pl.ds(0, size)], xs_hbm.at[pl.ds(off, size)], sem)
        pairs.append((lax.shift_right_logical(pad_len, b) & 1 == 1, copy))
    return pairs


def _dispatch_kernel(pad_start_ref, pad_len_ref, nv_ref, pos_ref, h2p_ref, xs_hbm, zeros, sem, zsem):
    i = pl.program_id(0)
    last = pl.num_programs(0) - 1

    def for_each_fill(act):
        def per_expert(e, carry):
            for cond, copy in _pad_fill_copies(pad_start_ref[e], pad_len_ref[e], zeros, xs_hbm, zsem):
                @pl.when(cond)
                def _(copy=copy):
                    act(copy)
            return carry
        lax.fori_loop(0, N_EXPERTS, per_expert, 0)

        def per_tile(t, carry):
            row0 = pl.multiple_of(t * MOE_TM, MOE_TM)
            act(pltpu.make_async_copy(zeros, xs_hbm.at[pl.ds(row0, MOE_TM)], zsem))
            return carry
        lax.fori_loop(nv_ref[0], MOE_NT, per_tile, 0)

    @pl.when(i == 0)
    def _():
        zeros[...] = jnp.zeros_like(zeros)
        for_each_fill(lambda copy: copy.start())

    def body(g, carry):
        for u in range(DMA_UNROLL // TOP_K):
            r = g * (DMA_UNROLL // TOP_K) + u
            for k in range(TOP_K):
                pltpu.make_async_copy(h2p_ref.at[r], xs_hbm.at[pos_ref[0, k, r]], sem).start()
        return carry
    lax.fori_loop(0, DSP_TB * TOP_K // DMA_UNROLL, body, 0)
    pltpu.make_async_copy(xs_hbm.at[pl.ds(0, DSP_TB * TOP_K)], xs_hbm.at[pl.ds(0, DSP_TB * TOP_K)],
                          sem).wait()

    @pl.when(i == last)
    def _():
        for_each_fill(lambda copy: copy.wait())


def _dispatch(pad_start, pad_len, n_valid, pos, h2_packed):
    nt = TOKENS // DSP_TB
    pos3 = pos.reshape(TOP_K, nt, DSP_TB).transpose(1, 0, 2)
    grid_spec = pltpu.PrefetchScalarGridSpec(
        num_scalar_prefetch=3,
        grid=(nt,),
        in_specs=[pl.BlockSpec((1, TOP_K, DSP_TB), lambda i, *_: (i, 0, 0), memory_space=pltpu.SMEM),
                  pl.BlockSpec((DSP_TB, PACK_ROWS, PACK_LANES), lambda i, *_: (i, 0, 0))],
        out_specs=pl.BlockSpec(memory_space=pl.ANY),
        scratch_shapes=[pltpu.VMEM((MOE_TM, PACK_ROWS, PACK_LANES), jnp.uint32),
                        pltpu.SemaphoreType.DMA(()), pltpu.SemaphoreType.DMA(())])
    return pl.pallas_call(
        _dispatch_kernel,
        grid_spec=grid_spec,
        out_shape=jax.ShapeDtypeStruct((MOE_ROWS, PACK_ROWS, PACK_LANES), jnp.uint32),
        compiler_params=_cparams(("arbitrary",), 24),
        name="dispatch",
    )(pad_start, pad_len, n_valid, pos3, h2_packed.reshape(TOKENS, PACK_ROWS, PACK_LANES))


def _weights_changed(te_ref, m):
    prev = te_ref[jnp.maximum(m - 1, 0)]
    return (m == 0) | (te_ref[m] != prev)


def _gmm1_kernel(te_ref, nv_ref, xs_ref, wg_ref, wu_ref, bg_ref, bu_ref, o_ref, wg_bf, wu_bf,
                 x_bf):
    m = pl.program_id(1)

    @pl.when(_weights_changed(te_ref, m))
    def _():
        wg_bf[...] = wg_ref[0].astype(bf16)
        wu_bf[...] = wu_ref[0].astype(bf16)

    @pl.when(m < nv_ref[0])
    def _():
        for s in range(PACK_ROWS):
            lo, hi = _load_packed(xs_ref, MOE_TM, s)
            x_bf[:, 2 * s * PACK_LANES:(2 * s + 1) * PACK_LANES] = lo.astype(bf16)
            x_bf[:, (2 * s + 1) * PACK_LANES:(2 * s + 2) * PACK_LANES] = hi.astype(bf16)
        x = x_bf[...]
        gate = jnp.dot(x, wg_bf[...], preferred_element_type=f32) + bg_ref[0]
        up = jnp.dot(x, wu_bf[...], preferred_element_type=f32) + bu_ref[0]
        gate = jnp.minimum(gate, SWIGLU_LIMIT)
        up = jnp.clip(up, -SWIGLU_LIMIT, SWIGLU_LIMIT)
        act = (up + 1.0) * (gate * jax.nn.sigmoid(SWIGLU_ALPHA * gate))
        o_ref[...] = act.astype(bf16)

    @pl.when(m >= nv_ref[0])
    def _():
        o_ref[...] = jnp.zeros_like(o_ref)


def _gmm1(tile_expert, n_valid, xs_packed, w_gate_up, b_gate_up):
    up0 = D_FF // GMM1_TN
    used = lambda m, nv: jnp.minimum(m, nv[0] - 1)
    grid_spec = pltpu.PrefetchScalarGridSpec(
        num_scalar_prefetch=2,
        grid=(D_FF // GMM1_TN, MOE_NT),
        in_specs=[pl.BlockSpec((MOE_TM * PACK_ROWS, PACK_LANES),
                               lambda n, m, te, nv: (used(m, nv), 0)),
                  pl.BlockSpec((1, D_MODEL, GMM1_TN), lambda n, m, te, nv: (te[m], 0, n)),
                  pl.BlockSpec((1, D_MODEL, GMM1_TN), lambda n, m, te, nv: (te[m], 0, up0 + n)),
                  pl.BlockSpec((1, 1, GMM1_TN), lambda n, m, te, nv: (te[m], 0, n)),
                  pl.BlockSpec((1, 1, GMM1_TN), lambda n, m, te, nv: (te[m], 0, up0 + n))],
        out_specs=pl.BlockSpec((MOE_TM, GMM1_TN), lambda n, m, te, nv: (m, n)),
        scratch_shapes=[pltpu.VMEM((D_MODEL, GMM1_TN), bf16), pltpu.VMEM((D_MODEL, GMM1_TN), bf16),
                        pltpu.VMEM((MOE_TM, D_MODEL), bf16)])
    b3 = b_gate_up.reshape(N_EXPERTS, 1, 2 * D_FF)
    return pl.pallas_call(
        _gmm1_kernel,
        grid_spec=grid_spec,
        out_shape=jax.ShapeDtypeStruct((MOE_ROWS, D_FF), bf16),
        compiler_params=_cparams(("arbitrary", "arbitrary"), 54),
        name="gmm1",
    )(tile_expert, n_valid, xs_packed.reshape(MOE_ROWS * PACK_ROWS, PACK_LANES), w_gate_up,
      w_gate_up, b3, b3)


def _gmm2_kernel(te_ref, nv_ref, a_ref, wd_ref, bd_ref, o_ref, wd_bf, y_scr):
    m = pl.program_id(1)

    @pl.when(_weights_changed(te_ref, m))
    def _():
        wd_bf[...] = wd_ref[0].astype(bf16)

    @pl.when(m < nv_ref[0])
    def _():
        y_scr[...] = jnp.dot(a_ref[...], wd_bf[...], preferred_element_type=f32) + bd_ref[0]
        _store_packed(o_ref, 0, MOE_TM, lambda cols: y_scr[:, cols])

    @pl.when(m >= nv_ref[0])
    def _():
        o_ref[...] = jnp.zeros_like(o_ref)


def _gmm2(tile_expert, n_valid, act, w_down, b_down):
    assert GMM2_TN == D_MODEL
    grid_spec = pltpu.PrefetchScalarGridSpec(
        num_scalar_prefetch=2,
        grid=(1, MOE_NT),
        in_specs=[pl.BlockSpec((MOE_TM, D_FF), lambda n, m, te, nv: (m, 0)),
                  pl.BlockSpec((1, D_FF, GMM2_TN), lambda n, m, te, nv: (te[m], 0, 0)),
                  pl.BlockSpec((1, 1, GMM2_TN), lambda n, m, te, nv: (te[m], 0, 0))],
        out_specs=pl.BlockSpec((MOE_TM * PACK_ROWS, PACK_LANES), lambda n, m, te, nv: (m, 0)),
        scratch_shapes=[pltpu.VMEM((D_FF, GMM2_TN), bf16), pltpu.VMEM((MOE_TM, D_MODEL), f32)])
    return pl.pallas_call(
        _gmm2_kernel,
        grid_spec=grid_spec,
        out_shape=jax.ShapeDtypeStruct((MOE_ROWS * PACK_ROWS, PACK_LANES), jnp.uint32),
        compiler_params=_cparams(("arbitrary", "arbitrary"), 56),
        name="gmm2",
    )(tile_expert, n_valid, act, w_down, b_down.reshape(N_EXPERTS, 1, D_MODEL))


def _combine_kernel(pos_cur, pos_nxt, y_hbm, x1_ref, g2_ref, tw_ref, fg_ref, o_ref, buf, sem):
    i = pl.program_id(0)
    n = pl.num_programs(0)
    slot = i % 2

    def start_tile(pos_ref, s):
        def body(g, carry):
            for u in range(DMA_UNROLL // TOP_K):
                r = g * (DMA_UNROLL // TOP_K) + u
                for k in range(TOP_K):
                    _token_copy(y_hbm, pos_ref[0, k, r], buf.at[s, k], r, sem.at[s]).start()
            return carry
        lax.fori_loop(0, CMB_TB * TOP_K // DMA_UNROLL, body, 0)

    @pl.when(i == 0)
    def _():
        start_tile(pos_cur, 0)

    @pl.when(i + 1 < n)
    def _():
        start_tile(pos_nxt, 1 - slot)

    pltpu.make_async_copy(buf.at[slot], buf.at[slot], sem.at[slot]).wait()

    weights = [jnp.broadcast_to(tw_ref[:, k:k + 1], (CMB_TB, PACK_LANES)) for k in range(TOP_K)]
    sumsq = jnp.zeros((CMB_TB, 1), f32)
    for s in range(PACK_ROWS):
        lo, hi = _load_packed(buf.at[slot, 0], CMB_TB, s)
        lo, hi = lo * weights[0], hi * weights[0]
        for k in range(1, TOP_K):
            lo_k, hi_k = _load_packed(buf.at[slot, k], CMB_TB, s)
            lo, hi = lo + lo_k * weights[k], hi + hi_k * weights[k]
        for c, moe in ((2 * s, lo), (2 * s + 1, hi)):
            cols = slice(c * PACK_LANES, (c + 1) * PACK_LANES)
            x2 = x1_ref[:, cols] + g2_ref[0, :, cols] * moe
            o_ref[:, cols] = x2
            sumsq = sumsq + jnp.sum(x2 * x2, axis=-1, keepdims=True)
    inv = lax.rsqrt(sumsq * (1.0 / D_MODEL) + EPS)
    o_ref[...] = o_ref[...] * inv * fg_ref[...]


def _combine(pos, y_packed, x1, gate2, top_w_rows, final_g):
    nt = TOKENS // CMB_TB
    tiles_per_batch = SEQ // CMB_TB
    pos3 = pos.reshape(TOP_K, nt, CMB_TB).transpose(1, 0, 2)
    return pl.pallas_call(
        _combine_kernel,
        grid=(nt,),
        in_specs=[pl.BlockSpec((1, TOP_K, CMB_TB), lambda i: (i, 0, 0), memory_space=pltpu.SMEM),
                  pl.BlockSpec((1, TOP_K, CMB_TB), lambda i: (jnp.minimum(i + 1, nt - 1), 0, 0),
                               memory_space=pltpu.SMEM),
                  pl.BlockSpec(memory_space=pl.ANY),
                  pl.BlockSpec((CMB_TB, D_MODEL), lambda i: (i, 0)),
                  pl.BlockSpec((1, 1, D_MODEL), lambda i: (i // tiles_per_batch, 0, 0)),
                  pl.BlockSpec((CMB_TB, TOP_K), lambda i: (i, 0)),
                  pl.BlockSpec((1, D_MODEL), lambda i: (0, 0))],
        out_specs=pl.BlockSpec((CMB_TB, D_MODEL), lambda i: (i, 0)),
        out_shape=jax.ShapeDtypeStruct((TOKENS, D_MODEL), f32),
        scratch_shapes=[pltpu.VMEM((2, TOP_K, CMB_TB * PACK_ROWS, PACK_LANES), jnp.uint32),
                        pltpu.SemaphoreType.DMA((2,))],
        compiler_params=_cparams(("arbitrary",), 32),
        name="combine",
    )(pos3, pos3, y_packed.reshape(MOE_ROWS, PACK_ROWS, PACK_LANES), x1,
      gate2.reshape(BATCH, 1, D_MODEL), top_w_rows, final_g.reshape(1, D_MODEL))


def _routing_tables(counts, top_i, rank):
    padded = (counts + MOE_TM - 1) // MOE_TM * MOE_TM
    ends = jnp.cumsum(padded)
    starts = ends - padded
    n_valid = ends[-1] // MOE_TM
    tile_ids = jnp.minimum(jnp.arange(MOE_NT, dtype=i32), n_valid - 1)
    tile_expert = jnp.sum(tile_ids[:, None] >= (ends // MOE_TM)[None, :], axis=1).astype(i32)
    experts = jnp.arange(N_EXPERTS, dtype=i32)
    pos = rank + jnp.sum(jnp.where(top_i[..., None] == experts, starts, 0), axis=-1)
    pad_start = (starts + counts).astype(i32)
    pad_len = (padded - counts).astype(i32)
    return (tile_expert, n_valid.reshape(1).astype(i32), pos.astype(i32), pad_start, pad_len)


def kernel(x, c, w_ada, b_ada, norm1_g, w_in, lambda_q1, lambda_k1, lambda_q2, lambda_k2, subln_g,
           rel_bias, conv_w, conv_b, conv_ln_g, conv_ln_b, w_out, norm2_g, w_router, b_router,
           w_gate_up, b_gate_up, w_down, b_down, final_g):
    assert x.shape == (BATCH, SEQ, D_MODEL) and w_ada.shape[0] == 1
    l = 0
    x2d = x.reshape(TOKENS, D_MODEL)
    mod = _adaln(c, w_ada[l], b_ada[l])
    shift1, scale1, gate1, shift2, scale2, gate2 = [mod[i] for i in range(6)]

    proj = _inproj(x2d, norm1_g[l], scale1, shift1, w_in[l].astype(bf16))
    att = _attention(proj, _bias_band(rel_bias), lambda_q1[l], lambda_k1[l], lambda_q2[l],
                     lambda_k2[l], subln_g[l])
    cv = _conv(proj, conv_w[l], conv_b[l], conv_ln_g[l], conv_ln_b[l])

    x1, h2_packed, top_i, top_w, rank, counts = _outproj(
        att, cv, w_out[l].astype(bf16), x2d, gate1, scale2, shift2, norm2_g[l], w_router[l],
        b_router[l])
    tile_expert, n_valid, pos, pad_start, pad_len = _routing_tables(counts[:, 0], top_i, rank)

    xs = _dispatch(pad_start, pad_len, n_valid, pos, h2_packed)
    act = _gmm1(tile_expert, n_valid, xs, w_gate_up[l], b_gate_up[l])
    y = _gmm2(tile_expert, n_valid, act, w_down[l], b_down[l])
    out = _combine(pos, y, x1, gate2, top_w.T, final_g)
    return out.reshape(BATCH, SEQ, D_MODEL)
```

```python
import functools
import math

import jax
import jax.numpy as jnp
from jax import lax
from jax.experimental import pallas as pl
from jax.experimental.pallas import tpu as pltpu

f32 = jnp.float32
bf16 = jnp.bfloat16
i32 = jnp.int32

D_MODEL = 2048
BATCH = 4
SEQ = 2048
TOKENS = BATCH * SEQ
ATT_HEADS = 8
ATT_HALF_DIM = 64
ATT_V_DIM = 128
ATT_WIDTH = ATT_HEADS * ATT_V_DIM
CONV_CH = D_MODEL - ATT_WIDTH
CONV_KERNEL = 31
CONV_HALF = CONV_KERNEL // 2
QK_COLS = ATT_HEADS * 2 * ATT_HALF_DIM
IN_COLS = 2 * QK_COLS + ATT_WIDTH + 2 * CONV_CH
N_BUCKETS = 32
MAX_DISTANCE = 128
N_EXPERTS = 32
TOP_K = 4
D_FF = D_MODEL
SWIGLU_LIMIT = 7.0
SWIGLU_ALPHA = 1.702
EPS = 1e-6
LAMBDA_INIT = 0.8 - 0.6 * math.exp(-0.3 * 0)

ADA_TN = 1024
INP_TM = 1024
INP_TN = 1024
ATT_TQ = 512
ATT_KC = 256
ATT_EXT_CHUNKS = 6
CONV_TS = 512
CONV_HALO = 16
CONV_RB = 32
OUT_TM = 512
MOE_TM = 256
MOE_NT = TOKENS * TOP_K // MOE_TM + N_EXPERTS
MOE_ROWS = MOE_NT * MOE_TM
GMM1_TN = 1024
GMM2_TN = 2048
DSP_TB = 256
CMB_TB = 128
ROW_CHUNK = 64
DMA_UNROLL = 8

_MIB = 1024 * 1024


def _cparams(sem, vmem_mib):
    return pltpu.CompilerParams(dimension_semantics=sem, vmem_limit_bytes=vmem_mib * _MIB)


PACK_ROWS = 8
PACK_LANES = 128
_HI_MASK = 0xFFFF0000


def _pack_pair(lo, hi):
    lo_b = lax.shift_right_logical(
        lax.bitcast_convert_type(lo.astype(bf16).astype(f32), jnp.uint32), jnp.uint32(16))
    hi_b = lax.bitcast_convert_type(hi.astype(bf16).astype(f32), jnp.uint32) & jnp.uint32(_HI_MASK)
    return lo_b | hi_b


def _unpack_pair(w):
    lo = lax.bitcast_convert_type(lax.shift_left(w, jnp.uint32(16)), f32)
    hi = lax.bitcast_convert_type(w & jnp.uint32(_HI_MASK), f32)
    return lo, hi


def _store_packed(dst_ref, token0, n_tokens, get_cols):
    for s in range(PACK_ROWS):
        lo = get_cols(slice(2 * s * PACK_LANES, (2 * s + 1) * PACK_LANES))
        hi = get_cols(slice((2 * s + 1) * PACK_LANES, (2 * s + 2) * PACK_LANES))
        dst_ref[pl.ds(token0 * PACK_ROWS + s, n_tokens, stride=PACK_ROWS), :] = _pack_pair(lo, hi)


def _load_packed(src_ref, n_tokens, s):
    return _unpack_pair(src_ref[pl.ds(s, n_tokens, stride=PACK_ROWS), :])


def _row_chunks(n_rows, chunk, body):
    def step(i, carry):
        body(pl.multiple_of(i * chunk, chunk))
        return carry
    lax.fori_loop(0, n_rows // chunk, step, 0)


def _adaln_kernel(c_ref, w_ref, b_ref, o_ref):
    c = c_ref[...]
    cs = c * jax.nn.sigmoid(c)
    o_ref[0] = jnp.dot(cs.astype(bf16), w_ref[...].astype(bf16),
                       preferred_element_type=f32) + b_ref[...]


def _adaln(c, w_ada, b_ada):
    n = w_ada.shape[1]
    per_vec = D_MODEL // ADA_TN
    return pl.pallas_call(
        _adaln_kernel,
        grid=(n // ADA_TN,),
        in_specs=[pl.BlockSpec((BATCH, D_MODEL), lambda j: (0, 0)),
                  pl.BlockSpec((D_MODEL, ADA_TN), lambda j: (0, j)),
                  pl.BlockSpec((1, ADA_TN), lambda j: (0, j))],
        out_specs=pl.BlockSpec((1, BATCH, ADA_TN), lambda j: (j // per_vec, 0, j % per_vec)),
        out_shape=jax.ShapeDtypeStruct((n // D_MODEL, BATCH, D_MODEL), f32),
        compiler_params=_cparams(("arbitrary",), 40),
        name="adaln",
    )(c, w_ada, b_ada.reshape(1, n))


def _inproj_kernel(x_ref, g_ref, sc_ref, sh_ref, w_ref, o_ref, h_scr):
    @pl.when(pl.program_id(1) == 0)
    def _():
        g = g_ref[...]
        one_plus_scale = 1.0 + sc_ref[0]
        shift = sh_ref[0]

        def chunk(r0):
            x = x_ref[pl.ds(r0, ROW_CHUNK), :]
            ms = jnp.mean(x * x, axis=-1, keepdims=True)
            y = x * lax.rsqrt(ms + EPS) * g
            h_scr[pl.ds(r0, ROW_CHUNK), :] = (y * one_plus_scale + shift).astype(bf16)
        _row_chunks(INP_TM, ROW_CHUNK, chunk)

    o_ref[...] = jnp.dot(h_scr[...], w_ref[...], preferred_element_type=f32).astype(bf16)


def _inproj(x2d, norm_g, scale, shift, w_bf):
    tiles_per_batch = SEQ // INP_TM
    return pl.pallas_call(
        _inproj_kernel,
        grid=(TOKENS // INP_TM, IN_COLS // INP_TN),
        in_specs=[pl.BlockSpec((INP_TM, D_MODEL), lambda i, j: (i, 0)),
                  pl.BlockSpec((1, D_MODEL), lambda i, j: (0, 0)),
                  pl.BlockSpec((1, 1, D_MODEL), lambda i, j: (i // tiles_per_batch, 0, 0)),
                  pl.BlockSpec((1, 1, D_MODEL), lambda i, j: (i // tiles_per_batch, 0, 0)),
                  pl.BlockSpec((D_MODEL, INP_TN), lambda i, j: (0, j))],
        out_specs=pl.BlockSpec((INP_TM, INP_TN), lambda i, j: (i, j)),
        out_shape=jax.ShapeDtypeStruct((TOKENS, IN_COLS), bf16),
        scratch_shapes=[pltpu.VMEM((INP_TM, D_MODEL), bf16)],
        compiler_params=_cparams(("parallel", "arbitrary"), 48),
        name="inproj",
    )(x2d, norm_g.reshape(1, D_MODEL), scale.reshape(BATCH, 1, D_MODEL),
      shift.reshape(BATCH, 1, D_MODEL), w_bf)


def _t5_bucket(rel):
    half = N_BUCKETS // 2
    max_exact = half // 2
    ret = jnp.where(rel > 0, half, 0)
    n = jnp.abs(rel)
    nf = jnp.maximum(n, 1).astype(f32)
    large = max_exact + (jnp.log(nf / max_exact) / math.log(MAX_DISTANCE / max_exact)
                         * (half - max_exact)).astype(i32)
    large = jnp.minimum(large, half - 1)
    return ret + jnp.where(n < max_exact, n, large)


def _bias_band(rel_bias):
    r = jnp.arange(ATT_TQ)[:, None]
    j = jnp.arange(ATT_EXT_CHUNKS * ATT_KC)[None, :]
    rel = j - 2 * ATT_KC - r
    onehot = (_t5_bucket(rel)[..., None] == jnp.arange(N_BUCKETS)).astype(f32)
    return jnp.einsum("rjn,nh->hrj", onehot, rel_bias.astype(f32),
                      precision=lax.Precision.HIGHEST)


def _attn_kernel(q_ref, k_ref, v_ref, band_ref, lq1_ref, lk1_ref, lq2_ref, lk2_ref, sg_ref,
                 o_ref, s_scr):
    t = pl.program_id(2)
    n_chunks = SEQ // ATT_KC
    lam = (jnp.exp(jnp.sum(lq1_ref[...] * lk1_ref[...], axis=-1, keepdims=True))
           - jnp.exp(jnp.sum(lq2_ref[...] * lk2_ref[...], axis=-1, keepdims=True))
           + LAMBDA_INIT)
    q = q_ref[...] * jnp.asarray(ATT_HALF_DIM ** -0.5, bf16)
    lane = lax.broadcasted_iota(i32, q.shape, 1)
    zero = jnp.zeros_like(q)

    def softmax_pv(qm):
        m = jnp.full((ATT_TQ, 1), -jnp.inf, f32)
        for c in range(n_chunks):
            d = jnp.clip(c - t * (ATT_TQ // ATT_KC) + 2, 0, ATT_EXT_CHUNKS - 1)
            off = pl.multiple_of(d * ATT_KC, ATT_KC)
            s = lax.dot_general(qm, k_ref[c * ATT_KC:(c + 1) * ATT_KC, :],
                                (((1,), (1,)), ((), ())), preferred_element_type=f32)
            s = s + band_ref[0, :, pl.ds(off, ATT_KC)]
            s_scr[:, c * ATT_KC:(c + 1) * ATT_KC] = s
            m = jnp.maximum(m, jnp.max(s, axis=-1, keepdims=True))
        l = jnp.zeros((ATT_TQ, 1), f32)
        o = jnp.zeros((ATT_TQ, ATT_V_DIM), f32)
        for c in range(n_chunks):
            p = jnp.exp(s_scr[:, c * ATT_KC:(c + 1) * ATT_KC] - m)
            l = l + jnp.sum(p, axis=-1, keepdims=True)
            o = o + jnp.dot(p.astype(bf16), v_ref[c * ATT_KC:(c + 1) * ATT_KC, :],
                            preferred_element_type=f32)
        return o / l

    o1 = softmax_pv(jnp.where(lane < ATT_HALF_DIM, q, zero))
    o2 = softmax_pv(jnp.where(lane >= ATT_HALF_DIM, q, zero))
    out = o1 - lam * o2
    ms = jnp.mean(out * out, axis=-1, keepdims=True)
    y = out * lax.rsqrt(ms + EPS) * sg_ref[...]
    o_ref[...] = (y * (1.0 - LAMBDA_INIT)).astype(bf16)


def _attention(proj, band, lq1, lk1, lq2, lk2, subln_g):
    qt = SEQ // ATT_TQ
    k_col0 = QK_COLS // ATT_V_DIM
    v_col0 = 2 * QK_COLS // ATT_V_DIM
    vec = lambda n: pl.BlockSpec((1, n), lambda b, h, t: (0, 0))
    return pl.pallas_call(
        _attn_kernel,
        grid=(BATCH, ATT_HEADS, qt),
        in_specs=[pl.BlockSpec((ATT_TQ, ATT_V_DIM), lambda b, h, t: (b * qt + t, h)),
                  pl.BlockSpec((SEQ, ATT_V_DIM), lambda b, h, t: (b, k_col0 + h)),
                  pl.BlockSpec((SEQ, ATT_V_DIM), lambda b, h, t: (b, v_col0 + h)),
                  pl.BlockSpec((1, ATT_TQ, ATT_EXT_CHUNKS * ATT_KC), lambda b, h, t: (h, 0, 0)),
                  vec(ATT_HALF_DIM), vec(ATT_HALF_DIM), vec(ATT_HALF_DIM), vec(ATT_HALF_DIM),
                  vec(ATT_V_DIM)],
        out_specs=pl.BlockSpec((ATT_TQ, ATT_V_DIM), lambda b, h, t: (b * qt + t, h)),
        out_shape=jax.ShapeDtypeStruct((TOKENS, ATT_WIDTH), bf16),
        scratch_shapes=[pltpu.VMEM((ATT_TQ, SEQ), f32)],
        compiler_params=_cparams(("parallel", "parallel", "arbitrary"), 40),
        name="attn",
    )(proj, proj, proj, band, lq1.reshape(1, -1), lk1.reshape(1, -1), lq2.reshape(1, -1),
      lk2.reshape(1, -1), subln_g.reshape(1, -1))


def _conv_kernel(ap_ref, ac_ref, an_ref, gp_ref, gc_ref, gn_ref, w_ref, cb_ref, lg_ref, lb_ref,
                 o_ref, z_scr, c_scr):
    s = pl.program_id(1)
    last = pl.num_programs(1) - 1

    def glu(a_ref, g_ref):
        return a_ref[...].astype(f32) * jax.nn.sigmoid(g_ref[...].astype(f32))

    z_scr[0:CONV_HALO, :] = jnp.where(s > 0, glu(ap_ref, gp_ref), 0.0)
    z_scr[CONV_HALO + CONV_TS:, :] = jnp.where(s < last, glu(an_ref, gn_ref), 0.0)

    def glu_chunk(r0):
        a = ac_ref[pl.ds(r0, ROW_CHUNK), :].astype(f32)
        g = gc_ref[pl.ds(r0, ROW_CHUNK), :].astype(f32)
        z_scr[pl.ds(CONV_HALO + r0, ROW_CHUNK), :] = a * jax.nn.sigmoid(g)
    _row_chunks(CONV_TS, ROW_CHUNK, glu_chunk)

    lanes = 128
    tap0 = CONV_HALO - CONV_HALF

    def lane_chunk(lc, carry):
        cols = pl.ds(pl.multiple_of(lc * lanes, lanes), lanes)
        taps = [w_ref[j:j + 1, cols] for j in range(CONV_KERNEL)]
        bias = cb_ref[:, cols]
        for r0 in range(0, CONV_TS, CONV_RB):
            acc = jnp.zeros((CONV_RB, lanes), f32)
            for j in range(CONV_KERNEL):
                acc = acc + z_scr[r0 + tap0 + j:r0 + tap0 + j + CONV_RB, cols] * taps[j]
            c_scr[r0:r0 + CONV_RB, cols] = acc + bias
        return carry
    lax.fori_loop(0, CONV_CH // lanes, lane_chunk, 0)

    lg = lg_ref[...]
    lb = lb_ref[...]

    def ln_chunk(r0):
        c = c_scr[pl.ds(r0, ROW_CHUNK), :]
        mu = jnp.mean(c, axis=-1, keepdims=True)
        cc = c - mu
        var = jnp.mean(cc * cc, axis=-1, keepdims=True)
        y = cc * lax.rsqrt(var + EPS) * lg + lb
        o_ref[pl.ds(r0, ROW_CHUNK), :] = (y * jax.nn.sigmoid(y)).astype(bf16)
    _row_chunks(CONV_TS, ROW_CHUNK, ln_chunk)


def _conv(proj, conv_w, conv_b, ln_g, ln_b):
    st = SEQ // CONV_TS
    halo_per_tile = CONV_TS // CONV_HALO
    halo_per_seq = SEQ // CONV_HALO
    a_col = (2 * QK_COLS + ATT_WIDTH) // CONV_CH
    g_col = a_col + 1

    def cur(col):
        return pl.BlockSpec((CONV_TS, CONV_CH), lambda b, s: (b * st + s, col))

    def prev(col):
        return pl.BlockSpec(
            (CONV_HALO, CONV_CH),
            lambda b, s: (jnp.maximum(b * halo_per_seq + s * halo_per_tile - 1, 0), col))

    def nxt(col):
        return pl.BlockSpec(
            (CONV_HALO, CONV_CH),
            lambda b, s: (jnp.minimum(b * halo_per_seq + (s + 1) * halo_per_tile,
                                      TOKENS // CONV_HALO - 1), col))

    vec = pl.BlockSpec((1, CONV_CH), lambda b, s: (0, 0))
    return pl.pallas_call(
        _conv_kernel,
        grid=(BATCH, st),
        in_specs=[prev(a_col), cur(a_col), nxt(a_col), prev(g_col), cur(g_col), nxt(g_col),
                  pl.BlockSpec((CONV_KERNEL, CONV_CH), lambda b, s: (0, 0)), vec, vec, vec],
        out_specs=pl.BlockSpec((CONV_TS, CONV_CH), lambda b, s: (b * st + s, 0)),
        out_shape=jax.ShapeDtypeStruct((TOKENS, CONV_CH), bf16),
        scratch_shapes=[pltpu.VMEM((CONV_TS + 2 * CONV_HALO, CONV_CH), f32),
                        pltpu.VMEM((CONV_TS, CONV_CH), f32)],
        compiler_params=_cparams(("parallel", "arbitrary"), 32),
        name="conv",
    )(proj, proj, proj, proj, proj, proj, conv_w, conv_b.reshape(1, -1), ln_g.reshape(1, -1),
      ln_b.reshape(1, -1))


def _outproj_kernel(att_ref, cv_ref, wo_ref, x_ref, g1_ref, sc2_ref, sh2_ref, n2_ref, wr_ref,
                    br_ref, x1_ref, h2p_ref, ti_ref, tw_ref, rk_ref, cnt_ref, carry_scr, h2_scr):
    @pl.when(pl.program_id(0) == 0)
    def _():
        carry_scr[...] = jnp.zeros_like(carry_scr)

    mixed = (jnp.dot(att_ref[...], wo_ref[0:ATT_WIDTH, :], preferred_element_type=f32)
             + jnp.dot(cv_ref[...], wo_ref[ATT_WIDTH:, :], preferred_element_type=f32))
    x1_ref[...] = x_ref[...] + g1_ref[0] * mixed

    n2 = n2_ref[...]
    one_plus_scale = 1.0 + sc2_ref[0]
    shift = sh2_ref[0]

    def chunk(r0):
        rows = pl.ds(r0, ROW_CHUNK)
        x1 = x1_ref[rows, :]
        ms = jnp.mean(x1 * x1, axis=-1, keepdims=True)
        y = x1 * lax.rsqrt(ms + EPS) * n2
        h2_scr[rows, :] = y * one_plus_scale + shift
        _store_packed(h2p_ref, r0, ROW_CHUNK, lambda cols: h2_scr[rows, cols])
    _row_chunks(OUT_TM, ROW_CHUNK, chunk)

    logits = lax.dot_general(wr_ref[...], h2_scr[...], (((1,), (1,)), ((), ())),
                             precision=lax.Precision.HIGHEST,
                             preferred_element_type=f32) + br_ref[...]
    eidx = lax.broadcasted_iota(i32, logits.shape, 0)
    vals = logits
    sels, tops = [], []
    for k in range(TOP_K):
        m = jnp.max(vals, axis=0, keepdims=True)
        idx = jnp.min(jnp.where(vals == m, eidx, N_EXPERTS), axis=0, keepdims=True)
        sel = eidx == idx
        ti_ref[k:k + 1, :] = idx
        sels.append(sel)
        tops.append(m)
        vals = jnp.where(sel, -jnp.inf, vals)
    exps = [jnp.exp(v - tops[0]) for v in tops]
    denom = exps[0] + exps[1] + exps[2] + exps[3]
    for k in range(TOP_K):
        tw_ref[k:k + 1, :] = exps[k] / denom

    onehot = jnp.where(sels[0] | sels[1] | sels[2] | sels[3], 1.0, 0.0)
    r_i = lax.broadcasted_iota(i32, (OUT_TM, OUT_TM), 0)
    c_i = lax.broadcasted_iota(i32, (OUT_TM, OUT_TM), 1)
    earlier = jnp.where(r_i < c_i, 1.0, 0.0).astype(bf16)
    before = jnp.dot(onehot.astype(bf16), earlier, preferred_element_type=f32) + carry_scr[...]
    for k in range(TOP_K):
        rk_ref[k:k + 1, :] = jnp.sum(jnp.where(sels[k], before, 0.0), axis=0,
                                     keepdims=True).astype(i32)
    total = carry_scr[...] + jnp.sum(onehot, axis=1, keepdims=True)
    carry_scr[...] = total
    cnt_ref[...] = jnp.broadcast_to(total, cnt_ref.shape).astype(i32)


def _outproj(att, cv, wo_bf, x2d, gate1, scale2, shift2, norm2_g, w_router, b_router):
    tiles_per_batch = SEQ // OUT_TM
    row = lambda w: pl.BlockSpec((OUT_TM, w), lambda i: (i, 0))
    mod = pl.BlockSpec((1, 1, D_MODEL), lambda i: (i // tiles_per_batch, 0, 0))
    slot = pl.BlockSpec((TOP_K, OUT_TM), lambda i: (0, i))
    return pl.pallas_call(
        _outproj_kernel,
        grid=(TOKENS // OUT_TM,),
        in_specs=[row(ATT_WIDTH), row(CONV_CH),
                  pl.BlockSpec((D_MODEL, D_MODEL), lambda i: (0, 0), pipeline_mode=pl.Buffered(1)),
                  row(D_MODEL), mod, mod, mod,
                  pl.BlockSpec((1, D_MODEL), lambda i: (0, 0)),
                  pl.BlockSpec((N_EXPERTS, D_MODEL), lambda i: (0, 0)),
                  pl.BlockSpec((N_EXPERTS, 1), lambda i: (0, 0))],
        out_specs=[row(D_MODEL),
                   pl.BlockSpec((OUT_TM * PACK_ROWS, PACK_LANES), lambda i: (i, 0)),
                   slot, slot, slot,
                   pl.BlockSpec((N_EXPERTS, 128), lambda i: (0, 0))],
        out_shape=[jax.ShapeDtypeStruct((TOKENS, D_MODEL), f32),
                   jax.ShapeDtypeStruct((TOKENS * PACK_ROWS, PACK_LANES), jnp.uint32),
                   jax.ShapeDtypeStruct((TOP_K, TOKENS), i32),
                   jax.ShapeDtypeStruct((TOP_K, TOKENS), f32),
                   jax.ShapeDtypeStruct((TOP_K, TOKENS), i32),
                   jax.ShapeDtypeStruct((N_EXPERTS, 128), i32)],
        scratch_shapes=[pltpu.VMEM((N_EXPERTS, 1), f32), pltpu.VMEM((OUT_TM, D_MODEL), f32)],
        compiler_params=_cparams(("arbitrary",), 52),
        name="outproj",
    )(att, cv, wo_bf, x2d, gate1.reshape(BATCH, 1, D_MODEL), scale2.reshape(BATCH, 1, D_MODEL),
      shift2.reshape(BATCH, 1, D_MODEL), norm2_g.reshape(1, D_MODEL), w_router.T,
      b_router.reshape(N_EXPERTS, 1))


def _token_copy(src_hbm, token, dst_buf, row, sem):
    dst = dst_buf.at[pl.ds(pl.multiple_of(row * PACK_ROWS, PACK_ROWS), PACK_ROWS), :]
    return pltpu.make_async_copy(src_hbm.at[token], dst, sem)


_PAD_BITS = tuple(range(MOE_TM.bit_length() - 2, -1, -1))


def _pad_fill_copies(pad_start, pad_len, zeros, xs_hbm, sem):
    pairs = []
    for b in _PAD_BITS:
        size = 1 << b
        off = pad_start + lax.shift_left(lax.shift_right_logical(pad_len, b + 1), b + 1)
        copy = pltpu.make_async_copy(zeros.at[pl.ds(0, size)], xs_hbm.at[pl.ds(off, size)], sem)
        pairs.append((lax.shift_right_logical(pad_len, b) & 1 == 1, copy))
    return pairs


def _dispatch_kernel(pad_start_ref, pad_len_ref, nv_ref, pos_ref, h2p_ref, xs_hbm, zeros, sem, zsem):
    i = pl.program_id(0)
    last = pl.num_programs(0) - 1

    def for_each_fill(act):
        def per_expert(e, carry):
            for cond, copy in _pad_fill_copies(pad_start_ref[e], pad_len_ref[e], zeros, xs_hbm, zsem):
                @pl.when(cond)
                def _(copy=copy):
                    act(copy)
            return carry
        lax.fori_loop(0, N_EXPERTS, per_expert, 0)

        def per_tile(t, carry):
            row0 = pl.multiple_of(t * MOE_TM, MOE_TM)
            act(pltpu.make_async_copy(zeros, xs_hbm.at[pl.ds(row0, MOE_TM)], zsem))
            return carry
        lax.fori_loop(nv_ref[0], MOE_NT, per_tile, 0)

    @pl.when(i == 0)
    def _():
        zeros[...] = jnp.zeros_like(zeros)
        for_each_fill(lambda copy: copy.start())

    def body(g, carry):
        for u in range(DMA_UNROLL // TOP_K):
            r = g * (DMA_UNROLL // TOP_K) + u
            for k in range(TOP_K):
                pltpu.make_async_copy(h2p_ref.at[r], xs_hbm.at[pos_ref[0, k, r]], sem).start()
        return carry
    lax.fori_loop(0, DSP_TB * TOP_K // DMA_UNROLL, body, 0)
    pltpu.make_async_copy(xs_hbm.at[pl.ds(0, DSP_TB * TOP_K)], xs_hbm.at[pl.ds(0, DSP_TB * TOP_K)],
                          sem).wait()

    @pl.when(i == last)
    def _():
        for_each_fill(lambda copy: copy.wait())


def _dispatch(pad_start, pad_len, n_valid, pos, h2_packed):
    nt = TOKENS // DSP_TB
    pos3 = pos.reshape(TOP_K, nt, DSP_TB).transpose(1, 0, 2)
    grid_spec = pltpu.PrefetchScalarGridSpec(
        num_scalar_prefetch=3,
        grid=(nt,),
        in_specs=[pl.BlockSpec((1, TOP_K, DSP_TB), lambda i, *_: (i, 0, 0), memory_space=pltpu.SMEM),
                  pl.BlockSpec((DSP_TB, PACK_ROWS, PACK_LANES), lambda i, *_: (i, 0, 0))],
        out_specs=pl.BlockSpec(memory_space=pl.ANY),
        scratch_shapes=[pltpu.VMEM((MOE_TM, PACK_ROWS, PACK_LANES), jnp.uint32),
                        pltpu.SemaphoreType.DMA(()), pltpu.SemaphoreType.DMA(())])
    return pl.pallas_call(
        _dispatch_kernel,
        grid_spec=grid_spec,
        out_shape=jax.ShapeDtypeStruct((MOE_ROWS, PACK_ROWS, PACK_LANES), jnp.uint32),
        compiler_params=_cparams(("arbitrary",), 24),
        name="dispatch",
    )(pad_start, pad_len, n_valid, pos3, h2_packed.reshape(TOKENS, PACK_ROWS, PACK_LANES))


def _weights_changed(te_ref, m):
    prev = te_ref[jnp.maximum(m - 1, 0)]
    return (m == 0) | (te_ref[m] != prev)


def _stream_weights(te_ref, nx_ref, n, m, n_passes, copies_for, convert):
    first_expert = te_ref[0]

    def start(p, e):
        for copy in copies_for(p, e):
            copy.start()

    @pl.when((n == 0) & (m == 0))
    def _():
        start(0, first_expert)

    @pl.when(_weights_changed(te_ref, m))
    def _():
        for copy in copies_for(0, 0):
            copy.wait()
        convert()
        nxt = nx_ref[m]

        @pl.when(nxt >= 0)
        def _():
            start(n, nxt)

        @pl.when((nxt < 0) & (n + 1 < n_passes))
        def _():
            start(n + 1, first_expert)


def _gmm1_kernel(te_ref, nx_ref, nv_ref, xs_ref, w_hbm, bg_ref, bu_ref, o_ref, stage, wg_bf, wu_bf,
                 x_bf, sem):
    n = pl.program_id(0)
    m = pl.program_id(1)

    def copies_for(p, e):
        col = pl.multiple_of(p * GMM1_TN, GMM1_TN)
        return [pltpu.make_async_copy(w_hbm.at[e, :, pl.ds(col, GMM1_TN)], stage.at[0], sem),
                pltpu.make_async_copy(w_hbm.at[e, :, pl.ds(D_FF + col, GMM1_TN)], stage.at[1], sem)]

    def convert():
        wg_bf[...] = stage[0].astype(bf16)
        wu_bf[...] = stage[1].astype(bf16)

    _stream_weights(te_ref, nx_ref, n, m, pl.num_programs(0), copies_for, convert)

    @pl.when(m < nv_ref[0])
    def _():
        for s in range(PACK_ROWS):
            lo, hi = _load_packed(xs_ref, MOE_TM, s)
            x_bf[:, 2 * s * PACK_LANES:(2 * s + 1) * PACK_LANES] = lo.astype(bf16)
            x_bf[:, (2 * s + 1) * PACK_LANES:(2 * s + 2) * PACK_LANES] = hi.astype(bf16)
        x = x_bf[...]
        gate = jnp.dot(x, wg_bf[...], preferred_element_type=f32) + bg_ref[0]
        up = jnp.dot(x, wu_bf[...], preferred_element_type=f32) + bu_ref[0]
        gate = jnp.minimum(gate, SWIGLU_LIMIT)
        up = jnp.clip(up, -SWIGLU_LIMIT, SWIGLU_LIMIT)
        act = (up + 1.0) * (gate * jax.nn.sigmoid(SWIGLU_ALPHA * gate))
        o_ref[...] = act.astype(bf16)

    @pl.when(m >= nv_ref[0])
    def _():
        o_ref[...] = jnp.zeros_like(o_ref)


def _gmm1(tile_expert, next_expert, n_valid, xs_packed, w_gate_up, b_gate_up):
    up0 = D_FF // GMM1_TN
    used = lambda m, nv: jnp.minimum(m, nv[0] - 1)
    grid_spec = pltpu.PrefetchScalarGridSpec(
        num_scalar_prefetch=3,
        grid=(D_FF // GMM1_TN, MOE_NT),
        in_specs=[pl.BlockSpec((MOE_TM * PACK_ROWS, PACK_LANES),
                               lambda n, m, te, nx, nv: (used(m, nv), 0)),
                  pl.BlockSpec(memory_space=pl.ANY),
                  pl.BlockSpec((1, 1, GMM1_TN), lambda n, m, te, nx, nv: (te[m], 0, n)),
                  pl.BlockSpec((1, 1, GMM1_TN), lambda n, m, te, nx, nv: (te[m], 0, up0 + n))],
        out_specs=pl.BlockSpec((MOE_TM, GMM1_TN), lambda n, m, te, nx, nv: (m, n)),
        scratch_shapes=[pltpu.VMEM((2, D_MODEL, GMM1_TN), f32),
                        pltpu.VMEM((D_MODEL, GMM1_TN), bf16), pltpu.VMEM((D_MODEL, GMM1_TN), bf16),
                        pltpu.VMEM((MOE_TM, D_MODEL), bf16), pltpu.SemaphoreType.DMA(())])
    b3 = b_gate_up.reshape(N_EXPERTS, 1, 2 * D_FF)
    return pl.pallas_call(
        _gmm1_kernel,
        grid_spec=grid_spec,
        out_shape=jax.ShapeDtypeStruct((MOE_ROWS, D_FF), bf16),
        compiler_params=_cparams(("arbitrary", "arbitrary"), 40),
        name="gmm1",
    )(tile_expert, next_expert, n_valid, xs_packed.reshape(MOE_ROWS * PACK_ROWS, PACK_LANES),
      w_gate_up, b3, b3)


def _gmm2_kernel(te_ref, nx_ref, nv_ref, a_ref, w_hbm, bd_ref, o_ref, stage, wd_bf, y_scr, sem):
    m = pl.program_id(1)

    def copies_for(p, e):
        return [pltpu.make_async_copy(w_hbm.at[e], stage, sem)]

    def convert():
        wd_bf[...] = stage[...].astype(bf16)

    _stream_weights(te_ref, nx_ref, pl.program_id(0), m, pl.num_programs(0), copies_for, convert)

    @pl.when(m < nv_ref[0])
    def _():
        y_scr[...] = jnp.dot(a_ref[...], wd_bf[...], preferred_element_type=f32) + bd_ref[0]
        _store_packed(o_ref, 0, MOE_TM, lambda cols: y_scr[:, cols])

    @pl.when(m >= nv_ref[0])
    def _():
        o_ref[...] = jnp.zeros_like(o_ref)


def _gmm2(tile_expert, next_expert, n_valid, act, w_down, b_down):
    assert GMM2_TN == D_MODEL
    grid_spec = pltpu.PrefetchScalarGridSpec(
        num_scalar_prefetch=3,
        grid=(1, MOE_NT),
        in_specs=[pl.BlockSpec((MOE_TM, D_FF), lambda n, m, te, nx, nv: (m, 0)),
                  pl.BlockSpec(memory_space=pl.ANY),
                  pl.BlockSpec((1, 1, GMM2_TN), lambda n, m, te, nx, nv: (te[m], 0, 0))],
        out_specs=pl.BlockSpec((MOE_TM * PACK_ROWS, PACK_LANES), lambda n, m, te, nx, nv: (m, 0)),
        scratch_shapes=[pltpu.VMEM((D_FF, GMM2_TN), f32), pltpu.VMEM((D_FF, GMM2_TN), bf16),
                        pltpu.VMEM((MOE_TM, D_MODEL), f32), pltpu.SemaphoreType.DMA(())])
    return pl.pallas_call(
        _gmm2_kernel,
        grid_spec=grid_spec,
        out_shape=jax.ShapeDtypeStruct((MOE_ROWS * PACK_ROWS, PACK_LANES), jnp.uint32),
        compiler_params=_cparams(("arbitrary", "arbitrary"), 40),
        name="gmm2",
    )(tile_expert, next_expert, n_valid, act, w_down, b_down.reshape(N_EXPERTS, 1, D_MODEL))


def _combine_kernel(pos_cur, pos_nxt, y_hbm, x1_ref, g2_ref, tw_ref, fg_ref, o_ref, buf, sem):
    i = pl.program_id(0)
    n = pl.num_programs(0)
    slot = i % 2

    def start_tile(pos_ref, s):
        def body(g, carry):
            for u in range(DMA_UNROLL // TOP_K):
                r = g * (DMA_UNROLL // TOP_K) + u
                for k in range(TOP_K):
                    _token_copy(y_hbm, pos_ref[0, k, r], buf.at[s, k], r, sem.at[s]).start()
            return carry
        lax.fori_loop(0, CMB_TB * TOP_K // DMA_UNROLL, body, 0)

    @pl.when(i == 0)
    def _():
        start_tile(pos_cur, 0)

    @pl.when(i + 1 < n)
    def _():
        start_tile(pos_nxt, 1 - slot)

    pltpu.make_async_copy(buf.at[slot], buf.at[slot], sem.at[slot]).wait()

    weights = [jnp.broadcast_to(tw_ref[:, k:k + 1], (CMB_TB, PACK_LANES)) for k in range(TOP_K)]
    sumsq = jnp.zeros((CMB_TB, 1), f32)
    for s in range(PACK_ROWS):
        lo, hi = _load_packed(buf.at[slot, 0], CMB_TB, s)
        lo, hi = lo * weights[0], hi * weights[0]
        for k in range(1, TOP_K):
            lo_k, hi_k = _load_packed(buf.at[slot, k], CMB_TB, s)
            lo, hi = lo + lo_k * weights[k], hi + hi_k * weights[k]
        for c, moe in ((2 * s, lo), (2 * s + 1, hi)):
            cols = slice(c * PACK_LANES, (c + 1) * PACK_LANES)
            x2 = x1_ref[:, cols] + g2_ref[0, :, cols] * moe
            o_ref[:, cols] = x2
            sumsq = sumsq + jnp.sum(x2 * x2, axis=-1, keepdims=True)
    inv = lax.rsqrt(sumsq * (1.0 / D_MODEL) + EPS)
    o_ref[...] = o_ref[...] * inv * fg_ref[...]


def _combine(pos, y_packed, x1, gate2, top_w_rows, final_g):
    nt = TOKENS // CMB_TB
    tiles_per_batch = SEQ // CMB_TB
    pos3 = pos.reshape(TOP_K, nt, CMB_TB).transpose(1, 0, 2)
    return pl.pallas_call(
        _combine_kernel,
        grid=(nt,),
        in_specs=[pl.BlockSpec((1, TOP_K, CMB_TB), lambda i: (i, 0, 0), memory_space=pltpu.SMEM),
                  pl.BlockSpec((1, TOP_K, CMB_TB), lambda i: (jnp.minimum(i + 1, nt - 1), 0, 0),
                               memory_space=pltpu.SMEM),
                  pl.BlockSpec(memory_space=pl.ANY),
                  pl.BlockSpec((CMB_TB, D_MODEL), lambda i: (i, 0)),
                  pl.BlockSpec((1, 1, D_MODEL), lambda i: (i // tiles_per_batch, 0, 0)),
                  pl.BlockSpec((CMB_TB, TOP_K), lambda i: (i, 0)),
                  pl.BlockSpec((1, D_MODEL), lambda i: (0, 0))],
        out_specs=pl.BlockSpec((CMB_TB, D_MODEL), lambda i: (i, 0)),
        out_shape=jax.ShapeDtypeStruct((TOKENS, D_MODEL), f32),
        scratch_shapes=[pltpu.VMEM((2, TOP_K, CMB_TB * PACK_ROWS, PACK_LANES), jnp.uint32),
                        pltpu.SemaphoreType.DMA((2,))],
        compiler_params=_cparams(("arbitrary",), 32),
        name="combine",
    )(pos3, pos3, y_packed.reshape(MOE_ROWS, PACK_ROWS, PACK_LANES), x1,
      gate2.reshape(BATCH, 1, D_MODEL), top_w_rows, final_g.reshape(1, D_MODEL))


def _routing_tables(counts, top_i, rank):
    padded = (counts + MOE_TM - 1) // MOE_TM * MOE_TM
    ends = jnp.cumsum(padded)
    starts = ends - padded
    n_valid = ends[-1] // MOE_TM
    tile_ids = jnp.minimum(jnp.arange(MOE_NT, dtype=i32), n_valid - 1)
    tile_expert = jnp.sum(tile_ids[:, None] >= (ends // MOE_TM)[None, :], axis=1).astype(i32)
    experts = jnp.arange(N_EXPERTS, dtype=i32)
    pos = rank + jnp.sum(jnp.where(top_i[..., None] == experts, starts, 0), axis=-1)
    pad_start = (starts + counts).astype(i32)
    pad_len = (padded - counts).astype(i32)
    later = (experts[None, :] > tile_expert[:, None]) & (counts > 0)[None, :]
    next_expert = jnp.min(jnp.where(later, experts[None, :], N_EXPERTS), axis=1)
    next_expert = jnp.where(next_expert == N_EXPERTS, -1, next_expert).astype(i32)
    return (tile_expert, next_expert, n_valid.reshape(1).astype(i32), pos.astype(i32), pad_start,
            pad_len)


def kernel(x, c, w_ada, b_ada, norm1_g, w_in, lambda_q1, lambda_k1, lambda_q2, lambda_k2, subln_g,
           rel_bias, conv_w, conv_b, conv_ln_g, conv_ln_b, w_out, norm2_g, w_router, b_router,
           w_gate_up, b_gate_up, w_down, b_down, final_g):
    assert x.shape == (BATCH, SEQ, D_MODEL) and w_ada.shape[0] == 1
    l = 0
    x2d = x.reshape(TOKENS, D_MODEL)
    mod = _adaln(c, w_ada[l], b_ada[l])
    shift1, scale1, gate1, shift2, scale2, gate2 = [mod[i] for i in range(6)]

    proj = _inproj(x2d, norm1_g[l], scale1, shift1, w_in[l].astype(bf16))
    att = _attention(proj, _bias_band(rel_bias), lambda_q1[l], lambda_k1[l], lambda_q2[l],
                     lambda_k2[l], subln_g[l])
    cv = _conv(proj, conv_w[l], conv_b[l], conv_ln_g[l], conv_ln_b[l])

    x1, h2_packed, top_i, top_w, rank, counts = _outproj(
        att, cv, w_out[l].astype(bf16), x2d, gate1, scale2, shift2, norm2_g[l], w_router[l],
        b_router[l])
    tile_expert, next_expert, n_valid, pos, pad_start, pad_len = _routing_tables(
        counts[:, 0], top_i, rank)

    xs = _dispatch(pad_start, pad_len, n_valid, pos, h2_packed)
    act = _gmm1(tile_expert, next_expert, n_valid, xs, w_gate_up[l], b_gate_up[l])
    y = _gmm2(tile_expert, next_expert, n_valid, act, w_down[l], b_down[l])
    out = _combine(pos, y, x1, gate2, top_w.T, final_g)
    return out.reshape(BATCH, SEQ, D_MODEL)
```

```python
import functools
import math

import jax
import jax.numpy as jnp
from jax import lax
from jax.experimental import pallas as pl
from jax.experimental.pallas import tpu as pltpu

f32 = jnp.float32
bf16 = jnp.bfloat16
i32 = jnp.int32

D_MODEL = 2048
BATCH = 4
SEQ = 2048
TOKENS = BATCH * SEQ
ATT_HEADS = 8
ATT_HALF_DIM = 64
ATT_V_DIM = 128
ATT_WIDTH = ATT_HEADS * ATT_V_DIM
CONV_CH = D_MODEL - ATT_WIDTH
CONV_KERNEL = 31
CONV_HALF = CONV_KERNEL // 2
QK_COLS = ATT_HEADS * 2 * ATT_HALF_DIM
IN_COLS = 2 * QK_COLS + ATT_WIDTH + 2 * CONV_CH
N_BUCKETS = 32
MAX_DISTANCE = 128
N_EXPERTS = 32
TOP_K = 4
D_FF = D_MODEL
SWIGLU_LIMIT = 7.0
SWIGLU_ALPHA = 1.702
EPS = 1e-6
LAMBDA_INIT = 0.8 - 0.6 * math.exp(-0.3 * 0)
LOG2E = math.log2(math.e)
Q_SCALE = LOG2E * ATT_HALF_DIM ** -0.5

ADA_TN = 1024
INP_TM = 1024
INP_TN = 1024
ATT_TQ = 512
ATT_SUB = 256
ATT_KC = 256
ATT_EXT_CHUNKS = 6
CONV_TS = 512
CONV_HALO = 16
CONV_RB = 32
OUT_TM = 512
MOE_TM = 256
MOE_NT = TOKENS * TOP_K // MOE_TM + N_EXPERTS
MOE_ROWS = MOE_NT * MOE_TM
GMM1_TN = 1024
GMM2_TN = 2048
DSP_TB = 256
CMB_TB = 128
ROW_CHUNK = 64
DMA_UNROLL = 8

_MIB = 1024 * 1024


def _cparams(sem, vmem_mib):
    return pltpu.CompilerParams(dimension_semantics=sem, vmem_limit_bytes=vmem_mib * _MIB)


PACK_ROWS = 8
PACK_LANES = 128
_HI_MASK = 0xFFFF0000


def _pack_pair(lo, hi):
    lo_b = lax.shift_right_logical(
        lax.bitcast_convert_type(lo.astype(bf16).astype(f32), jnp.uint32), jnp.uint32(16))
    hi_b = lax.bitcast_convert_type(hi.astype(bf16).astype(f32), jnp.uint32) & jnp.uint32(_HI_MASK)
    return lo_b | hi_b


def _unpack_pair(w):
    lo = lax.bitcast_convert_type(lax.shift_left(w, jnp.uint32(16)), f32)
    hi = lax.bitcast_convert_type(w & jnp.uint32(_HI_MASK), f32)
    return lo, hi


def _store_packed(dst_ref, token0, n_tokens, get_cols):
    for s in range(PACK_ROWS):
        lo = get_cols(slice(2 * s * PACK_LANES, (2 * s + 1) * PACK_LANES))
        hi = get_cols(slice((2 * s + 1) * PACK_LANES, (2 * s + 2) * PACK_LANES))
        dst_ref[pl.ds(token0 * PACK_ROWS + s, n_tokens, stride=PACK_ROWS), :] = _pack_pair(lo, hi)


def _load_packed(src_ref, n_tokens, s):
    return _unpack_pair(src_ref[pl.ds(s, n_tokens, stride=PACK_ROWS), :])


def _row_chunks(n_rows, chunk, body):
    def step(i, carry):
        body(pl.multiple_of(i * chunk, chunk))
        return carry
    lax.fori_loop(0, n_rows // chunk, step, 0)


def _adaln_kernel(c_ref, w_ref, b_ref, o_ref):
    c = c_ref[...]
    cs = c * jax.nn.sigmoid(c)
    o_ref[0] = jnp.dot(cs.astype(bf16), w_ref[...].astype(bf16),
                       preferred_element_type=f32) + b_ref[...]


def _adaln(c, w_ada, b_ada):
    n = w_ada.shape[1]
    per_vec = D_MODEL // ADA_TN
    return pl.pallas_call(
        _adaln_kernel,
        grid=(n // ADA_TN,),
        in_specs=[pl.BlockSpec((BATCH, D_MODEL), lambda j: (0, 0)),
                  pl.BlockSpec((D_MODEL, ADA_TN), lambda j: (0, j)),
                  pl.BlockSpec((1, ADA_TN), lambda j: (0, j))],
        out_specs=pl.BlockSpec((1, BATCH, ADA_TN), lambda j: (j // per_vec, 0, j % per_vec)),
        out_shape=jax.ShapeDtypeStruct((n // D_MODEL, BATCH, D_MODEL), f32),
        compiler_params=_cparams(("arbitrary",), 40),
        name="adaln",
    )(c, w_ada, b_ada.reshape(1, n))


def _inproj_kernel(x_ref, g_ref, sc_ref, sh_ref, w_ref, o_ref, h_scr):
    @pl.when(pl.program_id(1) == 0)
    def _():
        g = g_ref[...]
        one_plus_scale = 1.0 + sc_ref[0]
        shift = sh_ref[0]

        def chunk(r0):
            x = x_ref[pl.ds(r0, ROW_CHUNK), :]
            ms = jnp.mean(x * x, axis=-1, keepdims=True)
            y = x * lax.rsqrt(ms + EPS) * g
            h_scr[pl.ds(r0, ROW_CHUNK), :] = (y * one_plus_scale + shift).astype(bf16)
        _row_chunks(INP_TM, ROW_CHUNK, chunk)

    out_scale = jnp.where(pl.program_id(1) < QK_COLS // INP_TN, Q_SCALE, 1.0).astype(f32)
    acc = jnp.dot(h_scr[...], w_ref[...], preferred_element_type=f32)
    o_ref[...] = (acc * out_scale).astype(bf16)


def _inproj(x2d, norm_g, scale, shift, w_bf):
    assert QK_COLS % INP_TN == 0
    tiles_per_batch = SEQ // INP_TM
    return pl.pallas_call(
        _inproj_kernel,
        grid=(TOKENS // INP_TM, IN_COLS // INP_TN),
        in_specs=[pl.BlockSpec((INP_TM, D_MODEL), lambda i, j: (i, 0)),
                  pl.BlockSpec((1, D_MODEL), lambda i, j: (0, 0)),
                  pl.BlockSpec((1, 1, D_MODEL), lambda i, j: (i // tiles_per_batch, 0, 0)),
                  pl.BlockSpec((1, 1, D_MODEL), lambda i, j: (i // tiles_per_batch, 0, 0)),
                  pl.BlockSpec((D_MODEL, INP_TN), lambda i, j: (0, j))],
        out_specs=pl.BlockSpec((INP_TM, INP_TN), lambda i, j: (i, j)),
        out_shape=jax.ShapeDtypeStruct((TOKENS, IN_COLS), bf16),
        scratch_shapes=[pltpu.VMEM((INP_TM, D_MODEL), bf16)],
        compiler_params=_cparams(("parallel", "arbitrary"), 48),
        name="inproj",
    )(x2d, norm_g.reshape(1, D_MODEL), scale.reshape(BATCH, 1, D_MODEL),
      shift.reshape(BATCH, 1, D_MODEL), w_bf)


def _t5_bucket(rel):
    half = N_BUCKETS // 2
    max_exact = half // 2
    ret = jnp.where(rel > 0, half, 0)
    n = jnp.abs(rel)
    nf = jnp.maximum(n, 1).astype(f32)
    large = max_exact + (jnp.log(nf / max_exact) / math.log(MAX_DISTANCE / max_exact)
                         * (half - max_exact)).astype(i32)
    large = jnp.minimum(large, half - 1)
    return ret + jnp.where(n < max_exact, n, large)


def _bias_band(rel_bias):
    r = jnp.arange(ATT_TQ)[:, None]
    j = jnp.arange(ATT_EXT_CHUNKS * ATT_KC)[None, :]
    rel = j - 2 * ATT_KC - r
    onehot = (_t5_bucket(rel)[..., None] == jnp.arange(N_BUCKETS)).astype(f32)
    band = jnp.einsum("rjn,nh->hrj", onehot, rel_bias.astype(f32), precision=lax.Precision.HIGHEST)
    return band * LOG2E


def _attn_kernel(q_ref, k_ref, v_ref, band_ref, lq1_ref, lk1_ref, lq2_ref, lk2_ref, sg_ref,
                 o_ref, va_scr):
    t = pl.program_id(2)
    n_chunks = SEQ // ATT_KC

    @pl.when(t == 0)
    def _():
        va_scr[:, :ATT_V_DIM] = v_ref[...]
        va_scr[:, ATT_V_DIM:] = jnp.ones((SEQ, ATT_V_DIM), bf16)

    lam = (jnp.exp(jnp.sum(lq1_ref[...] * lk1_ref[...], axis=-1, keepdims=True))
           - jnp.exp(jnp.sum(lq2_ref[...] * lk2_ref[...], axis=-1, keepdims=True))
           + LAMBDA_INIT)
    band_offsets = []
    for c in range(n_chunks):
        d = jnp.clip(c - t * (ATT_TQ // ATT_KC) + 2, 0, ATT_EXT_CHUNKS - 1)
        band_offsets.append(pl.multiple_of(d * ATT_KC, ATT_KC))

    def scores(qm, rows):
        chunks = []
        m = jnp.full((ATT_SUB, 1), -jnp.inf, f32)
        for c in range(n_chunks):
            s = lax.dot_general(qm, k_ref[c * ATT_KC:(c + 1) * ATT_KC, :],
                                (((1,), (1,)), ((), ())), preferred_element_type=f32)
            s = s + band_ref[0, rows, pl.ds(band_offsets[c], ATT_KC)]
            chunks.append(s)
            m = jnp.maximum(m, jnp.max(s, axis=-1, keepdims=True))
        return chunks, m

    def weighted_values(chunks, m):
        o = jnp.zeros((ATT_SUB, 2 * ATT_V_DIM), f32)
        for c in range(n_chunks):
            p = jnp.exp2(chunks[c] - m)
            o = o + jnp.dot(p.astype(bf16), va_scr[c * ATT_KC:(c + 1) * ATT_KC, :],
                            preferred_element_type=f32)
        return o[:, :ATT_V_DIM] / o[:, ATT_V_DIM:]

    def finish(rows, o1, o2):
        out = o1 - lam * o2
        ms = jnp.mean(out * out, axis=-1, keepdims=True)
        y = out * lax.rsqrt(ms + EPS) * sg_ref[...]
        o_ref[rows, :] = (y * (1.0 - LAMBDA_INIT)).astype(bf16)

    units = []
    for r in range(ATT_TQ // ATT_SUB):
        rows = slice(r * ATT_SUB, (r + 1) * ATT_SUB)
        q = q_ref[rows, :]
        lane = lax.broadcasted_iota(i32, q.shape, 1)
        zero = jnp.zeros_like(q)
        units.append((rows, jnp.where(lane < ATT_HALF_DIM, q, zero)))
        units.append((rows, jnp.where(lane >= ATT_HALF_DIM, q, zero)))
    first_map = {}
    pending = None
    for unit in units + [None]:
        current = (unit[0],) + scores(unit[1], unit[0]) if unit is not None else None
        if pending is not None:
            p_rows, p_chunks, p_m = pending
            o = weighted_values(p_chunks, p_m)
            if p_rows.start in first_map:
                finish(p_rows, first_map.pop(p_rows.start), o)
            else:
                first_map[p_rows.start] = o
        pending = current


def _attention(proj, band, lq1, lk1, lq2, lk2, subln_g):
    qt = SEQ // ATT_TQ
    k_col0 = QK_COLS // ATT_V_DIM
    v_col0 = 2 * QK_COLS // ATT_V_DIM
    vec = lambda n: pl.BlockSpec((1, n), lambda b, h, t: (0, 0))
    return pl.pallas_call(
        _attn_kernel,
        grid=(BATCH, ATT_HEADS, qt),
        in_specs=[pl.BlockSpec((ATT_TQ, ATT_V_DIM), lambda b, h, t: (b * qt + t, h)),
                  pl.BlockSpec((SEQ, ATT_V_DIM), lambda b, h, t: (b, k_col0 + h)),
                  pl.BlockSpec((SEQ, ATT_V_DIM), lambda b, h, t: (b, v_col0 + h)),
                  pl.BlockSpec((1, ATT_TQ, ATT_EXT_CHUNKS * ATT_KC), lambda b, h, t: (h, 0, 0)),
                  vec(ATT_HALF_DIM), vec(ATT_HALF_DIM), vec(ATT_HALF_DIM), vec(ATT_HALF_DIM),
                  vec(ATT_V_DIM)],
        out_specs=pl.BlockSpec((ATT_TQ, ATT_V_DIM), lambda b, h, t: (b * qt + t, h)),
        out_shape=jax.ShapeDtypeStruct((TOKENS, ATT_WIDTH), bf16),
        scratch_shapes=[pltpu.VMEM((SEQ, 2 * ATT_V_DIM), bf16)],
        compiler_params=_cparams(("parallel", "parallel", "arbitrary"), 40),
        name="attn",
    )(proj, proj, proj, band, lq1.reshape(1, -1), lk1.reshape(1, -1), lq2.reshape(1, -1),
      lk2.reshape(1, -1), subln_g.reshape(1, -1))


def _conv_kernel(ap_ref, ac_ref, an_ref, gp_ref, gc_ref, gn_ref, w_ref, cb_ref, lg_ref, lb_ref,
                 o_ref, z_scr, c_scr):
    s = pl.program_id(1)
    last = pl.num_programs(1) - 1

    lanes = 128
    n_lane_chunks = CONV_CH // lanes

    def store_z(rows, z):
        for lc in range(n_lane_chunks):
            z_scr[lc, rows, :] = z[:, lc * lanes:(lc + 1) * lanes]

    def glu(a, g):
        return a.astype(f32) * jax.nn.sigmoid(g.astype(f32))

    store_z(slice(0, CONV_HALO), jnp.where(s > 0, glu(ap_ref[...], gp_ref[...]), 0.0))
    store_z(slice(CONV_HALO + CONV_TS, CONV_TS + 2 * CONV_HALO),
            jnp.where(s < last, glu(an_ref[...], gn_ref[...]), 0.0))

    def glu_chunk(r0):
        rows = pl.ds(r0, ROW_CHUNK)
        store_z(pl.ds(CONV_HALO + r0, ROW_CHUNK), glu(ac_ref[rows, :], gc_ref[rows, :]))
    _row_chunks(CONV_TS, ROW_CHUNK, glu_chunk)

    tap0 = CONV_HALO - CONV_HALF

    def lane_chunk(lc, carry):
        cols = pl.ds(pl.multiple_of(lc * lanes, lanes), lanes)
        taps = [w_ref[j:j + 1, cols] for j in range(CONV_KERNEL)]
        bias = cb_ref[:, cols]
        for r0 in range(0, CONV_TS, CONV_RB):
            acc = jnp.zeros((CONV_RB, lanes), f32)
            for j in range(CONV_KERNEL):
                acc = acc + z_scr[lc, r0 + tap0 + j:r0 + tap0 + j + CONV_RB, :] * taps[j]
            c_scr[r0:r0 + CONV_RB, cols] = acc + bias
        return carry
    lax.fori_loop(0, n_lane_chunks, lane_chunk, 0)

    lg = lg_ref[...]
    lb = lb_ref[...]

    def ln_chunk(r0):
        c = c_scr[pl.ds(r0, ROW_CHUNK), :]
        mu = jnp.mean(c, axis=-1, keepdims=True)
        cc = c - mu
        var = jnp.mean(cc * cc, axis=-1, keepdims=True)
        y = cc * lax.rsqrt(var + EPS) * lg + lb
        o_ref[pl.ds(r0, ROW_CHUNK), :] = (y * jax.nn.sigmoid(y)).astype(bf16)
    _row_chunks(CONV_TS, ROW_CHUNK, ln_chunk)


def _conv(proj, conv_w, conv_b, ln_g, ln_b):
    st = SEQ // CONV_TS
    halo_per_tile = CONV_TS // CONV_HALO
    halo_per_seq = SEQ // CONV_HALO
    a_col = (2 * QK_COLS + ATT_WIDTH) // CONV_CH
    g_col = a_col + 1

    def cur(col):
        return pl.BlockSpec((CONV_TS, CONV_CH), lambda b, s: (b * st + s, col))

    def prev(col):
        return pl.BlockSpec(
            (CONV_HALO, CONV_CH),
            lambda b, s: (jnp.maximum(b * halo_per_seq + s * halo_per_tile - 1, 0), col))

    def nxt(col):
        return pl.BlockSpec(
            (CONV_HALO, CONV_CH),
            lambda b, s: (jnp.minimum(b * halo_per_seq + (s + 1) * halo_per_tile,
                                      TOKENS // CONV_HALO - 1), col))

    vec = pl.BlockSpec((1, CONV_CH), lambda b, s: (0, 0))
    return pl.pallas_call(
        _conv_kernel,
        grid=(BATCH, st),
        in_specs=[prev(a_col), cur(a_col), nxt(a_col), prev(g_col), cur(g_col), nxt(g_col),
                  pl.BlockSpec((CONV_KERNEL, CONV_CH), lambda b, s: (0, 0)), vec, vec, vec],
        out_specs=pl.BlockSpec((CONV_TS, CONV_CH), lambda b, s: (b * st + s, 0)),
        out_shape=jax.ShapeDtypeStruct((TOKENS, CONV_CH), bf16),
        scratch_shapes=[pltpu.VMEM((CONV_CH // 128, CONV_TS + 2 * CONV_HALO, 128), f32),
                        pltpu.VMEM((CONV_TS, CONV_CH), f32)],
        compiler_params=_cparams(("parallel", "arbitrary"), 32),
        name="conv",
    )(proj, proj, proj, proj, proj, proj, conv_w, conv_b.reshape(1, -1), ln_g.reshape(1, -1),
      ln_b.reshape(1, -1))


def _outproj_kernel(att_ref, cv_ref, wo_ref, x_ref, g1_ref, sc2_ref, sh2_ref, n2_ref, wr_ref,
                    br_ref, x1_ref, h2p_ref, ti_ref, tw_ref, rk_ref, cnt_ref, carry_scr, h2_scr):
    @pl.when(pl.program_id(0) == 0)
    def _():
        carry_scr[...] = jnp.zeros_like(carry_scr)

    mixed = (jnp.dot(att_ref[...], wo_ref[0:ATT_WIDTH, :], preferred_element_type=f32)
             + jnp.dot(cv_ref[...], wo_ref[ATT_WIDTH:, :], preferred_element_type=f32))
    x1_ref[...] = x_ref[...] + g1_ref[0] * mixed

    n2 = n2_ref[...]
    one_plus_scale = 1.0 + sc2_ref[0]
    shift = sh2_ref[0]

    def chunk(r0):
        rows = pl.ds(r0, ROW_CHUNK)
        x1 = x1_ref[rows, :]
        ms = jnp.mean(x1 * x1, axis=-1, keepdims=True)
        y = x1 * lax.rsqrt(ms + EPS) * n2
        h2_scr[rows, :] = y * one_plus_scale + shift
        _store_packed(h2p_ref, r0, ROW_CHUNK, lambda cols: h2_scr[rows, cols])
    _row_chunks(OUT_TM, ROW_CHUNK, chunk)

    logits = lax.dot_general(wr_ref[...], h2_scr[...], (((1,), (1,)), ((), ())),
                             precision=lax.Precision.HIGHEST,
                             preferred_element_type=f32) + br_ref[...]
    eidx = lax.broadcasted_iota(i32, logits.shape, 0)
    vals = logits
    sels, tops = [], []
    for k in range(TOP_K):
        m = jnp.max(vals, axis=0, keepdims=True)
        idx = jnp.min(jnp.where(vals == m, eidx, N_EXPERTS), axis=0, keepdims=True)
        sel = eidx == idx
        ti_ref[k:k + 1, :] = idx
        sels.append(sel)
        tops.append(m)
        vals = jnp.where(sel, -jnp.inf, vals)
    exps = [jnp.exp(v - tops[0]) for v in tops]
    denom = exps[0] + exps[1] + exps[2] + exps[3]
    for k in range(TOP_K):
        tw_ref[k:k + 1, :] = exps[k] / denom

    onehot = jnp.where(sels[0] | sels[1] | sels[2] | sels[3], 1.0, 0.0)
    r_i = lax.broadcasted_iota(i32, (OUT_TM, OUT_TM), 0)
    c_i = lax.broadcasted_iota(i32, (OUT_TM, OUT_TM), 1)
    earlier = jnp.where(r_i < c_i, 1.0, 0.0).astype(bf16)
    before = jnp.dot(onehot.astype(bf16), earlier, preferred_element_type=f32) + carry_scr[...]
    for k in range(TOP_K):
        rk_ref[k:k + 1, :] = jnp.sum(jnp.where(sels[k], before, 0.0), axis=0,
                                     keepdims=True).astype(i32)
    total = carry_scr[...] + jnp.sum(onehot, axis=1, keepdims=True)
    carry_scr[...] = total
    cnt_ref[...] = jnp.broadcast_to(total, cnt_ref.shape).astype(i32)


def _outproj(att, cv, wo_bf, x2d, gate1, scale2, shift2, norm2_g, w_router, b_router):
    tiles_per_batch = SEQ // OUT_TM
    row = lambda w: pl.BlockSpec((OUT_TM, w), lambda i: (i, 0))
    mod = pl.BlockSpec((1, 1, D_MODEL), lambda i: (i // tiles_per_batch, 0, 0))
    slot = pl.BlockSpec((TOP_K, OUT_TM), lambda i: (0, i))
    return pl.pallas_call(
        _outproj_kernel,
        grid=(TOKENS // OUT_TM,),
        in_specs=[row(ATT_WIDTH), row(CONV_CH),
                  pl.BlockSpec((D_MODEL, D_MODEL), lambda i: (0, 0), pipeline_mode=pl.Buffered(1)),
                  row(D_MODEL), mod, mod, mod,
                  pl.BlockSpec((1, D_MODEL), lambda i: (0, 0)),
                  pl.BlockSpec((N_EXPERTS, D_MODEL), lambda i: (0, 0)),
                  pl.BlockSpec((N_EXPERTS, 1), lambda i: (0, 0))],
        out_specs=[row(D_MODEL),
                   pl.BlockSpec((OUT_TM * PACK_ROWS, PACK_LANES), lambda i: (i, 0)),
                   slot, slot, slot,
                   pl.BlockSpec((N_EXPERTS, 128), lambda i: (0, 0))],
        out_shape=[jax.ShapeDtypeStruct((TOKENS, D_MODEL), f32),
                   jax.ShapeDtypeStruct((TOKENS * PACK_ROWS, PACK_LANES), jnp.uint32),
                   jax.ShapeDtypeStruct((TOP_K, TOKENS), i32),
                   jax.ShapeDtypeStruct((TOP_K, TOKENS), f32),
                   jax.ShapeDtypeStruct((TOP_K, TOKENS), i32),
                   jax.ShapeDtypeStruct((N_EXPERTS, 128), i32)],
        scratch_shapes=[pltpu.VMEM((N_EXPERTS, 1), f32), pltpu.VMEM((OUT_TM, D_MODEL), f32)],
        compiler_params=_cparams(("arbitrary",), 52),
        name="outproj",
    )(att, cv, wo_bf, x2d, gate1.reshape(BATCH, 1, D_MODEL), scale2.reshape(BATCH, 1, D_MODEL),
      shift2.reshape(BATCH, 1, D_MODEL), norm2_g.reshape(1, D_MODEL), w_router.T,
      b_router.reshape(N_EXPERTS, 1))


def _token_copy(src_hbm, token, dst_buf, row, sem):
    dst = dst_buf.at[pl.ds(pl.multiple_of(row * PACK_ROWS, PACK_ROWS), PACK_ROWS), :]
    return pltpu.make_async_copy(src_hbm.at[token], dst, sem)


_PAD_BITS = tuple(range(MOE_TM.bit_length() - 2, -1, -1))


def _pad_fill_copies(pad_start, pad_len, zeros, xs_hbm, sem):
    pairs = []
    for b in _PAD_BITS:
        size = 1 << b
        off = pad_start + lax.shift_left(lax.shift_right_logical(pad_len, b + 1), b + 1)
        copy = pltpu.make_async_copy(zeros.at[pl.ds(0, size)], xs_hbm.at[pl.ds(off, size)], sem)
        pairs.append((lax.shift_right_logical(pad_len, b) & 1 == 1, copy))
    return pairs


def _dispatch_kernel(pad_start_ref, pad_len_ref, nv_ref, pos_ref, h2p_ref, xs_hbm, zeros, sem, zsem):
    i = pl.program_id(0)
    last = pl.num_programs(0) - 1

    def for_each_fill(act):
        def per_expert(e, carry):
            for cond, copy in _pad_fill_copies(pad_start_ref[e], pad_len_ref[e], zeros, xs_hbm, zsem):
                @pl.when(cond)
                def _(copy=copy):
                    act(copy)
            return carry
        lax.fori_loop(0, N_EXPERTS, per_expert, 0)

        def per_tile(t, carry):
            row0 = pl.multiple_of(t * MOE_TM, MOE_TM)
            act(pltpu.make_async_copy(zeros, xs_hbm.at[pl.ds(row0, MOE_TM)], zsem))
            return carry
        lax.fori_loop(nv_ref[0], MOE_NT, per_tile, 0)

    @pl.when(i == 0)
    def _():
        zeros[...] = jnp.zeros_like(zeros)
        for_each_fill(lambda copy: copy.start())

    def body(g, carry):
        for u in range(DMA_UNROLL // TOP_K):
            r = g * (DMA_UNROLL // TOP_K) + u
            for k in range(TOP_K):
                pltpu.make_async_copy(h2p_ref.at[r], xs_hbm.at[pos_ref[0, k, r]], sem).start()
        return carry
    lax.fori_loop(0, DSP_TB * TOP_K // DMA_UNROLL, body, 0)
    pltpu.make_async_copy(xs_hbm.at[pl.ds(0, DSP_TB * TOP_K)], xs_hbm.at[pl.ds(0, DSP_TB * TOP_K)],
                          sem).wait()

    @pl.when(i == last)
    def _():
        for_each_fill(lambda copy: copy.wait())


def _dispatch(pad_start, pad_len, n_valid, pos, h2_packed):
    nt = TOKENS // DSP_TB
    pos3 = pos.reshape(TOP_K, nt, DSP_TB).transpose(1, 0, 2)
    grid_spec = pltpu.PrefetchScalarGridSpec(
        num_scalar_prefetch=3,
        grid=(nt,),
        in_specs=[pl.BlockSpec((1, TOP_K, DSP_TB), lambda i, *_: (i, 0, 0), memory_space=pltpu.SMEM),
                  pl.BlockSpec((DSP_TB, PACK_ROWS, PACK_LANES), lambda i, *_: (i, 0, 0))],
        out_specs=pl.BlockSpec(memory_space=pl.ANY),
        scratch_shapes=[pltpu.VMEM((MOE_TM, PACK_ROWS, PACK_LANES), jnp.uint32),
                        pltpu.SemaphoreType.DMA(()), pltpu.SemaphoreType.DMA(())])
    return pl.pallas_call(
        _dispatch_kernel,
        grid_spec=grid_spec,
        out_shape=jax.ShapeDtypeStruct((MOE_ROWS, PACK_ROWS, PACK_LANES), jnp.uint32),
        compiler_params=_cparams(("arbitrary",), 24),
        name="dispatch",
    )(pad_start, pad_len, n_valid, pos3, h2_packed.reshape(TOKENS, PACK_ROWS, PACK_LANES))


def _weights_changed(te_ref, m):
    prev = te_ref[jnp.maximum(m - 1, 0)]
    return (m == 0) | (te_ref[m] != prev)


def _stream_weights(te_ref, nx_ref, n, m, n_passes, copies_for, convert):
    first_expert = te_ref[0]

    def start(p, e):
        for copy in copies_for(p, e):
            copy.start()

    @pl.when((n == 0) & (m == 0))
    def _():
        start(0, first_expert)

    @pl.when(_weights_changed(te_ref, m))
    def _():
        for copy in copies_for(0, 0):
            copy.wait()
        convert()
        nxt = nx_ref[m]

        @pl.when(nxt >= 0)
        def _():
            start(n, nxt)

        @pl.when((nxt < 0) & (n + 1 < n_passes))
        def _():
            start(n + 1, first_expert)


def _gmm1_kernel(te_ref, nx_ref, nv_ref, xs_ref, w_hbm, bg_ref, bu_ref, o_ref, stage, wg_bf, wu_bf,
                 x_bf, sem):
    n = pl.program_id(0)
    m = pl.program_id(1)

    def copies_for(p, e):
        col = pl.multiple_of(p * GMM1_TN, GMM1_TN)
        return [pltpu.make_async_copy(w_hbm.at[e, :, pl.ds(col, GMM1_TN)], stage.at[0], sem),
                pltpu.make_async_copy(w_hbm.at[e, :, pl.ds(D_FF + col, GMM1_TN)], stage.at[1], sem)]

    def convert():
        wg_bf[...] = stage[0].astype(bf16)
        wu_bf[...] = stage[1].astype(bf16)

    _stream_weights(te_ref, nx_ref, n, m, pl.num_programs(0), copies_for, convert)

    @pl.when(m < nv_ref[0])
    def _():
        for s in range(PACK_ROWS):
            lo, hi = _load_packed(xs_ref, MOE_TM, s)
            x_bf[:, 2 * s * PACK_LANES:(2 * s + 1) * PACK_LANES] = lo.astype(bf16)
            x_bf[:, (2 * s + 1) * PACK_LANES:(2 * s + 2) * PACK_LANES] = hi.astype(bf16)
        x = x_bf[...]
        gate = jnp.dot(x, wg_bf[...], preferred_element_type=f32) + bg_ref[0]
        up = jnp.dot(x, wu_bf[...], preferred_element_type=f32) + bu_ref[0]
        gate = jnp.minimum(gate, SWIGLU_LIMIT)
        up = jnp.clip(up, -SWIGLU_LIMIT, SWIGLU_LIMIT)
        act = (up + 1.0) * (gate * jax.nn.sigmoid(SWIGLU_ALPHA * gate))
        o_ref[...] = act.astype(bf16)

    @pl.when(m >= nv_ref[0])
    def _():
        o_ref[...] = jnp.zeros_like(o_ref)


def _gmm1(tile_expert, next_expert, n_valid, xs_packed, w_gate_up, b_gate_up):
    up0 = D_FF // GMM1_TN
    used = lambda m, nv: jnp.minimum(m, nv[0] - 1)
    grid_spec = pltpu.PrefetchScalarGridSpec(
        num_scalar_prefetch=3,
        grid=(D_FF // GMM1_TN, MOE_NT),
        in_specs=[pl.BlockSpec((MOE_TM * PACK_ROWS, PACK_LANES),
                               lambda n, m, te, nx, nv: (used(m, nv), 0)),
                  pl.BlockSpec(memory_space=pl.ANY),
                  pl.BlockSpec((1, 1, GMM1_TN), lambda n, m, te, nx, nv: (te[m], 0, n)),
                  pl.BlockSpec((1, 1, GMM1_TN), lambda n, m, te, nx, nv: (te[m], 0, up0 + n))],
        out_specs=pl.BlockSpec((MOE_TM, GMM1_TN), lambda n, m, te, nx, nv: (m, n)),
        scratch_shapes=[pltpu.VMEM((2, D_MODEL, GMM1_TN), f32),
                        pltpu.VMEM((D_MODEL, GMM1_TN), bf16), pltpu.VMEM((D_MODEL, GMM1_TN), bf16),
                        pltpu.VMEM((MOE_TM, D_MODEL), bf16), pltpu.SemaphoreType.DMA(())])
    b3 = b_gate_up.reshape(N_EXPERTS, 1, 2 * D_FF)
    return pl.pallas_call(
        _gmm1_kernel,
        grid_spec=grid_spec,
        out_shape=jax.ShapeDtypeStruct((MOE_ROWS, D_FF), bf16),
        compiler_params=_cparams(("arbitrary", "arbitrary"), 40),
        name="gmm1",
    )(tile_expert, next_expert, n_valid, xs_packed.reshape(MOE_ROWS * PACK_ROWS, PACK_LANES),
      w_gate_up, b3, b3)


def _gmm2_kernel(te_ref, nx_ref, nv_ref, a_ref, w_hbm, bd_ref, o_ref, stage, wd_bf, y_scr, sem):
    m = pl.program_id(1)

    def copies_for(p, e):
        return [pltpu.make_async_copy(w_hbm.at[e], stage, sem)]

    def convert():
        wd_bf[...] = stage[...].astype(bf16)

    _stream_weights(te_ref, nx_ref, pl.program_id(0), m, pl.num_programs(0), copies_for, convert)

    @pl.when(m < nv_ref[0])
    def _():
        y_scr[...] = jnp.dot(a_ref[...], wd_bf[...], preferred_element_type=f32) + bd_ref[0]
        _store_packed(o_ref, 0, MOE_TM, lambda cols: y_scr[:, cols])

    @pl.when(m >= nv_ref[0])
    def _():
        o_ref[...] = jnp.zeros_like(o_ref)


def _gmm2(tile_expert, next_expert, n_valid, act, w_down, b_down):
    assert GMM2_TN == D_MODEL
    grid_spec = pltpu.PrefetchScalarGridSpec(
        num_scalar_prefetch=3,
        grid=(1, MOE_NT),
        in_specs=[pl.BlockSpec((MOE_TM, D_FF), lambda n, m, te, nx, nv: (m, 0)),
                  pl.BlockSpec(memory_space=pl.ANY),
                  pl.BlockSpec((1, 1, GMM2_TN), lambda n, m, te, nx, nv: (te[m], 0, 0))],
        out_specs=pl.BlockSpec((MOE_TM * PACK_ROWS, PACK_LANES), lambda n, m, te, nx, nv: (m, 0)),
        scratch_shapes=[pltpu.VMEM((D_FF, GMM2_TN), f32), pltpu.VMEM((D_FF, GMM2_TN), bf16),
                        pltpu.VMEM((MOE_TM, D_MODEL), f32), pltpu.SemaphoreType.DMA(())])
    return pl.pallas_call(
        _gmm2_kernel,
        grid_spec=grid_spec,
        out_shape=jax.ShapeDtypeStruct((MOE_ROWS * PACK_ROWS, PACK_LANES), jnp.uint32),
        compiler_params=_cparams(("arbitrary", "arbitrary"), 40),
        name="gmm2",
    )(tile_expert, next_expert, n_valid, act, w_down, b_down.reshape(N_EXPERTS, 1, D_MODEL))


def _combine_kernel(pos_cur, pos_nxt, y_hbm, x1_ref, g2_ref, tw_ref, fg_ref, o_ref, buf, sem):
    i = pl.program_id(0)
    n = pl.num_programs(0)
    slot = i % 2

    def start_tile(pos_ref, s):
        def body(g, carry):
            for u in range(DMA_UNROLL // TOP_K):
                r = g * (DMA_UNROLL // TOP_K) + u
                for k in range(TOP_K):
                    _token_copy(y_hbm, pos_ref[0, k, r], buf.at[s, k], r, sem.at[s]).start()
            return carry
        lax.fori_loop(0, CMB_TB * TOP_K // DMA_UNROLL, body, 0)

    @pl.when(i == 0)
    def _():
        start_tile(pos_cur, 0)

    @pl.when(i + 1 < n)
    def _():
        start_tile(pos_nxt, 1 - slot)

    pltpu.make_async_copy(buf.at[slot], buf.at[slot], sem.at[slot]).wait()

    weights = [jnp.broadcast_to(tw_ref[:, k:k + 1], (CMB_TB, PACK_LANES)) for k in range(TOP_K)]
    sumsq = jnp.zeros((CMB_TB, 1), f32)
    for s in range(PACK_ROWS):
        lo, hi = _load_packed(buf.at[slot, 0], CMB_TB, s)
        lo, hi = lo * weights[0], hi * weights[0]
        for k in range(1, TOP_K):
            lo_k, hi_k = _load_packed(buf.at[slot, k], CMB_TB, s)
            lo, hi = lo + lo_k * weights[k], hi + hi_k * weights[k]
        for c, moe in ((2 * s, lo), (2 * s + 1, hi)):
            cols = slice(c * PACK_LANES, (c + 1) * PACK_LANES)
            x2 = x1_ref[:, cols] + g2_ref[0, :, cols] * moe
            o_ref[:, cols] = x2
            sumsq = sumsq + jnp.sum(x2 * x2, axis=-1, keepdims=True)
    inv = lax.rsqrt(sumsq * (1.0 / D_MODEL) + EPS)
    o_ref[...] = o_ref[...] * inv * fg_ref[...]


def _combine(pos, y_packed, x1, gate2, top_w_rows, final_g):
    nt = TOKENS // CMB_TB
    tiles_per_batch = SEQ // CMB_TB
    pos3 = pos.reshape(TOP_K, nt, CMB_TB).transpose(1, 0, 2)
    return pl.pallas_call(
        _combine_kernel,
        grid=(nt,),
        in_specs=[pl.BlockSpec((1, TOP_K, CMB_TB), lambda i: (i, 0, 0), memory_space=pltpu.SMEM),
                  pl.BlockSpec((1, TOP_K, CMB_TB), lambda i: (jnp.minimum(i + 1, nt - 1), 0, 0),
                               memory_space=pltpu.SMEM),
                  pl.BlockSpec(memory_space=pl.ANY),
                  pl.BlockSpec((CMB_TB, D_MODEL), lambda i: (i, 0)),
                  pl.BlockSpec((1, 1, D_MODEL), lambda i: (i // tiles_per_batch, 0, 0)),
                  pl.BlockSpec((CMB_TB, TOP_K), lambda i: (i, 0)),
                  pl.BlockSpec((1, D_MODEL), lambda i: (0, 0))],
        out_specs=pl.BlockSpec((CMB_TB, D_MODEL), lambda i: (i, 0)),
        out_shape=jax.ShapeDtypeStruct((TOKENS, D_MODEL), f32),
        scratch_shapes=[pltpu.VMEM((2, TOP_K, CMB_TB * PACK_ROWS, PACK_LANES), jnp.uint32),
                        pltpu.SemaphoreType.DMA((2,))],
        compiler_params=_cparams(("arbitrary",), 32),
        name="combine",
    )(pos3, pos3, y_packed.reshape(MOE_ROWS, PACK_ROWS, PACK_LANES), x1,
      gate2.reshape(BATCH, 1, D_MODEL), top_w_rows, final_g.reshape(1, D_MODEL))


def _routing_tables(counts, top_i, rank):
    padded = (counts + MOE_TM - 1) // MOE_TM * MOE_TM
    ends = jnp.cumsum(padded)
    starts = ends - padded
    n_valid = ends[-1] // MOE_TM
    tile_ids = jnp.minimum(jnp.arange(MOE_NT, dtype=i32), n_valid - 1)
    tile_expert = jnp.sum(tile_ids[:, None] >= (ends // MOE_TM)[None, :], axis=1).astype(i32)
    experts = jnp.arange(N_EXPERTS, dtype=i32)
    pos = rank + jnp.sum(jnp.where(top_i[..., None] == experts, starts, 0), axis=-1)
    pad_start = (starts + counts).astype(i32)
    pad_len = (padded - counts).astype(i32)
    later = (experts[None, :] > tile_expert[:, None]) & (counts > 0)[None, :]
    next_expert = jnp.min(jnp.where(later, experts[None, :], N_EXPERTS), axis=1)
    next_expert = jnp.where(next_expert == N_EXPERTS, -1, next_expert).astype(i32)
    return (tile_expert, next_expert, n_valid.reshape(1).astype(i32), pos.astype(i32), pad_start,
            pad_len)


def kernel(x, c, w_ada, b_ada, norm1_g, w_in, lambda_q1, lambda_k1, lambda_q2, lambda_k2, subln_g,
           rel_bias, conv_w, conv_b, conv_ln_g, conv_ln_b, w_out, norm2_g, w_router, b_router,
           w_gate_up, b_gate_up, w_down, b_down, final_g):
    assert x.shape == (BATCH, SEQ, D_MODEL) and w_ada.shape[0] == 1
    l = 0
    x2d = x.reshape(TOKENS, D_MODEL)
    mod = _adaln(c, w_ada[l], b_ada[l])
    shift1, scale1, gate1, shift2, scale2, gate2 = [mod[i] for i in range(6)]

    proj = _inproj(x2d, norm1_g[l], scale1, shift1, w_in[l].astype(bf16))
    att = _attention(proj, _bias_band(rel_bias), lambda_q1[l], lambda_k1[l], lambda_q2[l],
                     lambda_k2[l], subln_g[l])
    cv = _conv(proj, conv_w[l], conv_b[l], conv_ln_g[l], conv_ln_b[l])

    x1, h2_packed, top_i, top_w, rank, counts = _outproj(
        att, cv, w_out[l].astype(bf16), x2d, gate1, scale2, shift2, norm2_g[l], w_router[l],
        b_router[l])
    tile_expert, next_expert, n_valid, pos, pad_start, pad_len = _routing_tables(
        counts[:, 0], top_i, rank)

    xs = _dispatch(pad_start, pad_len, n_valid, pos, h2_packed)
    act = _gmm1(tile_expert, next_expert, n_valid, xs, w_gate_up[l], b_gate_up[l])
    y = _gmm2(tile_expert, next_expert, n_valid, act, w_down[l], b_down[l])
    out = _combine(pos, y, x1, gate2, top_w.T, final_g)
    return out.reshape(BATCH, SEQ, D_MODEL)
```

```python
import functools
import math

import jax
import jax.numpy as jnp
from jax import lax
from jax.experimental import pallas as pl
from jax.experimental.pallas import tpu as pltpu

f32 = jnp.float32
bf16 = jnp.bfloat16
i32 = jnp.int32

D_MODEL = 2048
BATCH = 4
SEQ = 2048
TOKENS = BATCH * SEQ
ATT_HEADS = 8
ATT_HALF_DIM = 64
ATT_V_DIM = 128
ATT_WIDTH = ATT_HEADS * ATT_V_DIM
CONV_CH = D_MODEL - ATT_WIDTH
CONV_KERNEL = 31
CONV_HALF = CONV_KERNEL // 2
QK_COLS = ATT_HEADS * 2 * ATT_HALF_DIM
IN_COLS = 2 * QK_COLS + ATT_WIDTH + 2 * CONV_CH
N_BUCKETS = 32
MAX_DISTANCE = 128
N_EXPERTS = 32
TOP_K = 4
D_FF = D_MODEL
SWIGLU_LIMIT = 7.0
SWIGLU_ALPHA = 1.702
EPS = 1e-6
LAMBDA_INIT = 0.8 - 0.6 * math.exp(-0.3 * 0)
LOG2E = math.log2(math.e)
Q_SCALE = LOG2E * ATT_HALF_DIM ** -0.5

ADA_TN = 1024
INP_TM = 1024
INP_TN = 1024
ATT_TQ = 512
ATT_SUB = 256
ATT_KC = 256
ATT_EXT_CHUNKS = 6
CONV_TS = 512
CONV_HALO = 16
CONV_RB = 32
OUT_TM = 512
MOE_TM = 256
MOE_NT = TOKENS * TOP_K // MOE_TM + N_EXPERTS
MOE_ROWS = MOE_NT * MOE_TM
GMM1_TN = 1024
GMM2_TN = 2048
DSP_TB = 256
CMB_TB = 128
ROW_CHUNK = 64
DMA_UNROLL = 8
DMA_PRIORITIES = 2

_MIB = 1024 * 1024


def _cparams(sem, vmem_mib):
    return pltpu.CompilerParams(dimension_semantics=sem, vmem_limit_bytes=vmem_mib * _MIB)


PACK_ROWS = 8
PACK_LANES = 128
_HI_MASK = 0xFFFF0000


def _pack_pair(lo, hi):
    lo_b = lax.shift_right_logical(
        lax.bitcast_convert_type(lo.astype(bf16).astype(f32), jnp.uint32), jnp.uint32(16))
    hi_b = lax.bitcast_convert_type(hi.astype(bf16).astype(f32), jnp.uint32) & jnp.uint32(_HI_MASK)
    return lo_b | hi_b


def _unpack_pair(w):
    lo = lax.bitcast_convert_type(lax.shift_left(w, jnp.uint32(16)), f32)
    hi = lax.bitcast_convert_type(w & jnp.uint32(_HI_MASK), f32)
    return lo, hi


def _store_packed(dst_ref, token0, n_tokens, get_cols):
    for s in range(PACK_ROWS):
        lo = get_cols(slice(2 * s * PACK_LANES, (2 * s + 1) * PACK_LANES))
        hi = get_cols(slice((2 * s + 1) * PACK_LANES, (2 * s + 2) * PACK_LANES))
        dst_ref[pl.ds(token0 * PACK_ROWS + s, n_tokens, stride=PACK_ROWS), :] = _pack_pair(lo, hi)


def _load_packed(src_ref, n_tokens, s):
    return _unpack_pair(src_ref[pl.ds(s, n_tokens, stride=PACK_ROWS), :])


def _row_chunks(n_rows, chunk, body):
    def step(i, carry):
        body(pl.multiple_of(i * chunk, chunk))
        return carry
    lax.fori_loop(0, n_rows // chunk, step, 0)


def _adaln_kernel(c_ref, w_ref, b_ref, o_ref):
    c = c_ref[...]
    cs = c * jax.nn.sigmoid(c)
    o_ref[0] = jnp.dot(cs.astype(bf16), w_ref[...].astype(bf16),
                       preferred_element_type=f32) + b_ref[...]


def _adaln(c, w_ada, b_ada):
    n = w_ada.shape[1]
    per_vec = D_MODEL // ADA_TN
    return pl.pallas_call(
        _adaln_kernel,
        grid=(n // ADA_TN,),
        in_specs=[pl.BlockSpec((BATCH, D_MODEL), lambda j: (0, 0)),
                  pl.BlockSpec((D_MODEL, ADA_TN), lambda j: (0, j)),
                  pl.BlockSpec((1, ADA_TN), lambda j: (0, j))],
        out_specs=pl.BlockSpec((1, BATCH, ADA_TN), lambda j: (j // per_vec, 0, j % per_vec)),
        out_shape=jax.ShapeDtypeStruct((n // D_MODEL, BATCH, D_MODEL), f32),
        compiler_params=_cparams(("arbitrary",), 40),
        name="adaln",
    )(c, w_ada, b_ada.reshape(1, n))


def _inproj_kernel(x_ref, g_ref, sc_ref, sh_ref, w_ref, o_ref, h_scr):
    @pl.when(pl.program_id(1) == 0)
    def _():
        g = g_ref[...]
        one_plus_scale = 1.0 + sc_ref[0]
        shift = sh_ref[0]

        def chunk(r0):
            x = x_ref[pl.ds(r0, ROW_CHUNK), :]
            ms = jnp.mean(x * x, axis=-1, keepdims=True)
            y = x * lax.rsqrt(ms + EPS) * g
            h_scr[pl.ds(r0, ROW_CHUNK), :] = (y * one_plus_scale + shift).astype(bf16)
        _row_chunks(INP_TM, ROW_CHUNK, chunk)

    out_scale = jnp.where(pl.program_id(1) < QK_COLS // INP_TN, Q_SCALE, 1.0).astype(f32)
    acc = jnp.dot(h_scr[...], w_ref[...], preferred_element_type=f32)
    o_ref[...] = (acc * out_scale).astype(bf16)


def _inproj(x2d, norm_g, scale, shift, w_bf):
    assert QK_COLS % INP_TN == 0
    tiles_per_batch = SEQ // INP_TM
    return pl.pallas_call(
        _inproj_kernel,
        grid=(TOKENS // INP_TM, IN_COLS // INP_TN),
        in_specs=[pl.BlockSpec((INP_TM, D_MODEL), lambda i, j: (i, 0)),
                  pl.BlockSpec((1, D_MODEL), lambda i, j: (0, 0)),
                  pl.BlockSpec((1, 1, D_MODEL), lambda i, j: (i // tiles_per_batch, 0, 0)),
                  pl.BlockSpec((1, 1, D_MODEL), lambda i, j: (i // tiles_per_batch, 0, 0)),
                  pl.BlockSpec((D_MODEL, INP_TN), lambda i, j: (0, j))],
        out_specs=pl.BlockSpec((INP_TM, INP_TN), lambda i, j: (i, j)),
        out_shape=jax.ShapeDtypeStruct((TOKENS, IN_COLS), bf16),
        scratch_shapes=[pltpu.VMEM((INP_TM, D_MODEL), bf16)],
        compiler_params=_cparams(("parallel", "arbitrary"), 48),
        name="inproj",
    )(x2d, norm_g.reshape(1, D_MODEL), scale.reshape(BATCH, 1, D_MODEL),
      shift.reshape(BATCH, 1, D_MODEL), w_bf)


def _t5_bucket(rel):
    half = N_BUCKETS // 2
    max_exact = half // 2
    ret = jnp.where(rel > 0, half, 0)
    n = jnp.abs(rel)
    nf = jnp.maximum(n, 1).astype(f32)
    large = max_exact + (jnp.log(nf / max_exact) / math.log(MAX_DISTANCE / max_exact)
                         * (half - max_exact)).astype(i32)
    large = jnp.minimum(large, half - 1)
    return ret + jnp.where(n < max_exact, n, large)


def _bias_band(rel_bias):
    r = jnp.arange(ATT_TQ)[:, None]
    j = jnp.arange(ATT_EXT_CHUNKS * ATT_KC)[None, :]
    rel = j - 2 * ATT_KC - r
    onehot = (_t5_bucket(rel)[..., None] == jnp.arange(N_BUCKETS)).astype(f32)
    return jnp.einsum("rjn,nh->hrj", onehot, rel_bias.astype(f32) * LOG2E,
                      precision=lax.Precision.HIGHEST)


def _attn_kernel(q_ref, k_ref, v_ref, band_ref, lq1_ref, lk1_ref, lq2_ref, lk2_ref, sg_ref,
                 o_ref, va_scr):
    t = pl.program_id(2)
    n_chunks = SEQ // ATT_KC

    @pl.when(t == 0)
    def _():
        va_scr[:, :ATT_V_DIM] = v_ref[...]
        va_scr[:, ATT_V_DIM:] = jnp.ones((SEQ, ATT_V_DIM), bf16)

    lam = (jnp.exp(jnp.sum(lq1_ref[...] * lk1_ref[...], axis=-1, keepdims=True))
           - jnp.exp(jnp.sum(lq2_ref[...] * lk2_ref[...], axis=-1, keepdims=True))
           + LAMBDA_INIT)
    band_offsets = []
    for c in range(n_chunks):
        d = jnp.clip(c - t * (ATT_TQ // ATT_KC) + 2, 0, ATT_EXT_CHUNKS - 1)
        band_offsets.append(pl.multiple_of(d * ATT_KC, ATT_KC))

    def scores(qm, rows):
        chunks = []
        m = jnp.full((ATT_SUB, 1), -jnp.inf, f32)
        for c in range(n_chunks):
            s = lax.dot_general(qm, k_ref[c * ATT_KC:(c + 1) * ATT_KC, :],
                                (((1,), (1,)), ((), ())), preferred_element_type=f32)
            s = s + band_ref[0, rows, pl.ds(band_offsets[c], ATT_KC)]
            chunks.append(s)
            m = jnp.maximum(m, jnp.max(s, axis=-1, keepdims=True))
        return chunks, m

    def weighted_values(chunks, m):
        o = jnp.zeros((ATT_SUB, 2 * ATT_V_DIM), f32)
        for c in range(n_chunks):
            p = jnp.exp2(chunks[c] - m)
            o = o + jnp.dot(p.astype(bf16), va_scr[c * ATT_KC:(c + 1) * ATT_KC, :],
                            preferred_element_type=f32)
        return o[:, :ATT_V_DIM] / o[:, ATT_V_DIM:]

    def finish(rows, o1, o2):
        out = o1 - lam * o2
        ms = jnp.mean(out * out, axis=-1, keepdims=True)
        y = out * lax.rsqrt(ms + EPS) * sg_ref[...]
        o_ref[rows, :] = (y * (1.0 - LAMBDA_INIT)).astype(bf16)

    units = []
    for r in range(ATT_TQ // ATT_SUB):
        rows = slice(r * ATT_SUB, (r + 1) * ATT_SUB)
        q = q_ref[rows, :]
        lane = lax.broadcasted_iota(i32, q.shape, 1)
        zero = jnp.zeros_like(q)
        units.append((rows, jnp.where(lane < ATT_HALF_DIM, q, zero)))
        units.append((rows, jnp.where(lane >= ATT_HALF_DIM, q, zero)))
    first_map = {}
    pending = None
    for unit in units + [None]:
        current = (unit[0],) + scores(unit[1], unit[0]) if unit is not None else None
        if pending is not None:
            p_rows, p_chunks, p_m = pending
            o = weighted_values(p_chunks, p_m)
            if p_rows.start in first_map:
                finish(p_rows, first_map.pop(p_rows.start), o)
            else:
                first_map[p_rows.start] = o
        pending = current


def _attention(proj, band, lq1, lk1, lq2, lk2, subln_g):
    qt = SEQ // ATT_TQ
    k_col0 = QK_COLS // ATT_V_DIM
    v_col0 = 2 * QK_COLS // ATT_V_DIM
    vec = lambda n: pl.BlockSpec((1, n), lambda b, h, t: (0, 0))
    return pl.pallas_call(
        _attn_kernel,
        grid=(BATCH, ATT_HEADS, qt),
        in_specs=[pl.BlockSpec((ATT_TQ, ATT_V_DIM), lambda b, h, t: (b * qt + t, h)),
                  pl.BlockSpec((SEQ, ATT_V_DIM), lambda b, h, t: (b, k_col0 + h)),
                  pl.BlockSpec((SEQ, ATT_V_DIM), lambda b, h, t: (b, v_col0 + h)),
                  pl.BlockSpec((1, ATT_TQ, ATT_EXT_CHUNKS * ATT_KC), lambda b, h, t: (h, 0, 0)),
                  vec(ATT_HALF_DIM), vec(ATT_HALF_DIM), vec(ATT_HALF_DIM), vec(ATT_HALF_DIM),
                  vec(ATT_V_DIM)],
        out_specs=pl.BlockSpec((ATT_TQ, ATT_V_DIM), lambda b, h, t: (b * qt + t, h)),
        out_shape=jax.ShapeDtypeStruct((TOKENS, ATT_WIDTH), bf16),
        scratch_shapes=[pltpu.VMEM((SEQ, 2 * ATT_V_DIM), bf16)],
        compiler_params=_cparams(("parallel", "parallel", "arbitrary"), 40),
        name="attn",
    )(proj, proj, proj, band, lq1.reshape(1, -1), lk1.reshape(1, -1), lq2.reshape(1, -1),
      lk2.reshape(1, -1), subln_g.reshape(1, -1))


def _conv_kernel(ap_ref, ac_ref, an_ref, gp_ref, gc_ref, gn_ref, w_ref, cb_ref, lg_ref, lb_ref,
                 o_ref, z_scr, c_scr):
    s = pl.program_id(1)
    last = pl.num_programs(1) - 1

    lanes = 128
    n_lane_chunks = CONV_CH // lanes

    def store_z(rows, z):
        for lc in range(n_lane_chunks):
            z_scr[lc, rows, :] = z[:, lc * lanes:(lc + 1) * lanes]

    def glu(a, g):
        return a.astype(f32) * jax.nn.sigmoid(g.astype(f32))

    store_z(slice(0, CONV_HALO), jnp.where(s > 0, glu(ap_ref[...], gp_ref[...]), 0.0))
    store_z(slice(CONV_HALO + CONV_TS, CONV_TS + 2 * CONV_HALO),
            jnp.where(s < last, glu(an_ref[...], gn_ref[...]), 0.0))

    def glu_chunk(r0):
        rows = pl.ds(r0, ROW_CHUNK)
        store_z(pl.ds(CONV_HALO + r0, ROW_CHUNK), glu(ac_ref[rows, :], gc_ref[rows, :]))
    _row_chunks(CONV_TS, ROW_CHUNK, glu_chunk)

    tap0 = CONV_HALO - CONV_HALF

    def lane_chunk(lc, carry):
        cols = pl.ds(pl.multiple_of(lc * lanes, lanes), lanes)
        taps = [w_ref[j:j + 1, cols] for j in range(CONV_KERNEL)]
        bias = cb_ref[:, cols]
        for r0 in range(0, CONV_TS, CONV_RB):
            acc = jnp.zeros((CONV_RB, lanes), f32)
            for j in range(CONV_KERNEL):
                acc = acc + z_scr[lc, r0 + tap0 + j:r0 + tap0 + j + CONV_RB, :] * taps[j]
            c_scr[r0:r0 + CONV_RB, cols] = acc + bias
        return carry
    lax.fori_loop(0, n_lane_chunks, lane_chunk, 0)

    lg = lg_ref[...]
    lb = lb_ref[...]

    def ln_chunk(r0):
        c = c_scr[pl.ds(r0, ROW_CHUNK), :]
        mu = jnp.mean(c, axis=-1, keepdims=True)
        cc = c - mu
        var = jnp.mean(cc * cc, axis=-1, keepdims=True)
        y = cc * lax.rsqrt(var + EPS) * lg + lb
        o_ref[pl.ds(r0, ROW_CHUNK), :] = (y * jax.nn.sigmoid(y)).astype(bf16)
    _row_chunks(CONV_TS, ROW_CHUNK, ln_chunk)


def _conv(proj, conv_w, conv_b, ln_g, ln_b):
    st = SEQ // CONV_TS
    halo_per_tile = CONV_TS // CONV_HALO
    halo_per_seq = SEQ // CONV_HALO
    a_col = (2 * QK_COLS + ATT_WIDTH) // CONV_CH
    g_col = a_col + 1

    def cur(col):
        return pl.BlockSpec((CONV_TS, CONV_CH), lambda b, s: (b * st + s, col))

    def prev(col):
        return pl.BlockSpec(
            (CONV_HALO, CONV_CH),
            lambda b, s: (jnp.maximum(b * halo_per_seq + s * halo_per_tile - 1, 0), col))

    def nxt(col):
        return pl.BlockSpec(
            (CONV_HALO, CONV_CH),
            lambda b, s: (jnp.minimum(b * halo_per_seq + (s + 1) * halo_per_tile,
                                      TOKENS // CONV_HALO - 1), col))

    vec = pl.BlockSpec((1, CONV_CH), lambda b, s: (0, 0))
    return pl.pallas_call(
        _conv_kernel,
        grid=(BATCH, st),
        in_specs=[prev(a_col), cur(a_col), nxt(a_col), prev(g_col), cur(g_col), nxt(g_col),
                  pl.BlockSpec((CONV_KERNEL, CONV_CH), lambda b, s: (0, 0)), vec, vec, vec],
        out_specs=pl.BlockSpec((CONV_TS, CONV_CH), lambda b, s: (b * st + s, 0)),
        out_shape=jax.ShapeDtypeStruct((TOKENS, CONV_CH), bf16),
        scratch_shapes=[pltpu.VMEM((CONV_CH // 128, CONV_TS + 2 * CONV_HALO, 128), f32),
                        pltpu.VMEM((CONV_TS, CONV_CH), f32)],
        compiler_params=_cparams(("parallel", "arbitrary"), 32),
        name="conv",
    )(proj, proj, proj, proj, proj, proj, conv_w, conv_b.reshape(1, -1), ln_g.reshape(1, -1),
      ln_b.reshape(1, -1))


def _outproj_kernel(att_ref, cv_ref, wo_ref, x_ref, g1_ref, sc2_ref, sh2_ref, n2_ref, wr_ref,
                    br_ref, x1_ref, h2p_ref, ti_ref, tw_ref, rk_ref, cnt_ref, carry_scr, h2_scr):
    @pl.when(pl.program_id(0) == 0)
    def _():
        carry_scr[...] = jnp.zeros_like(carry_scr)

    mixed = (jnp.dot(att_ref[...], wo_ref[0:ATT_WIDTH, :], preferred_element_type=f32)
             + jnp.dot(cv_ref[...], wo_ref[ATT_WIDTH:, :], preferred_element_type=f32))
    x1_ref[...] = x_ref[...] + g1_ref[0] * mixed

    n2 = n2_ref[...]
    one_plus_scale = 1.0 + sc2_ref[0]
    shift = sh2_ref[0]

    def chunk(r0):
        rows = pl.ds(r0, ROW_CHUNK)
        x1 = x1_ref[rows, :]
        ms = jnp.mean(x1 * x1, axis=-1, keepdims=True)
        y = x1 * lax.rsqrt(ms + EPS) * n2
        h2_scr[rows, :] = y * one_plus_scale + shift
        _store_packed(h2p_ref, r0, ROW_CHUNK, lambda cols: h2_scr[rows, cols])
    _row_chunks(OUT_TM, ROW_CHUNK, chunk)

    logits = lax.dot_general(wr_ref[...], h2_scr[...], (((1,), (1,)), ((), ())),
                             precision=lax.Precision.HIGHEST,
                             preferred_element_type=f32) + br_ref[...]
    eidx = lax.broadcasted_iota(i32, logits.shape, 0)
    vals = logits
    sels, tops = [], []
    for k in range(TOP_K):
        m = jnp.max(vals, axis=0, keepdims=True)
        idx = jnp.min(jnp.where(vals == m, eidx, N_EXPERTS), axis=0, keepdims=True)
        sel = eidx == idx
        ti_ref[k:k + 1, :] = idx
        sels.append(sel)
        tops.append(m)
        vals = jnp.where(sel, -jnp.inf, vals)
    exps = [jnp.exp(v - tops[0]) for v in tops]
    denom = exps[0] + exps[1] + exps[2] + exps[3]
    for k in range(TOP_K):
        tw_ref[k:k + 1, :] = exps[k] / denom

    onehot = jnp.where(sels[0] | sels[1] | sels[2] | sels[3], 1.0, 0.0)
    r_i = lax.broadcasted_iota(i32, (OUT_TM, OUT_TM), 0)
    c_i = lax.broadcasted_iota(i32, (OUT_TM, OUT_TM), 1)
    earlier = jnp.where(r_i < c_i, 1.0, 0.0).astype(bf16)
    before = jnp.dot(onehot.astype(bf16), earlier, preferred_element_type=f32) + carry_scr[...]
    for k in range(TOP_K):
        rk_ref[k:k + 1, :] = jnp.sum(jnp.where(sels[k], before, 0.0), axis=0,
                                     keepdims=True).astype(i32)
    total = carry_scr[...] + jnp.sum(onehot, axis=1, keepdims=True)
    carry_scr[...] = total
    cnt_ref[...] = jnp.broadcast_to(total, cnt_ref.shape).astype(i32)


def _outproj(att, cv, wo_bf, x2d, gate1, scale2, shift2, norm2_g, w_router, b_router):
    tiles_per_batch = SEQ // OUT_TM
    row = lambda w: pl.BlockSpec((OUT_TM, w), lambda i: (i, 0))
    mod = pl.BlockSpec((1, 1, D_MODEL), lambda i: (i // tiles_per_batch, 0, 0))
    slot = pl.BlockSpec((TOP_K, OUT_TM), lambda i: (0, i))
    return pl.pallas_call(
        _outproj_kernel,
        grid=(TOKENS // OUT_TM,),
        in_specs=[row(ATT_WIDTH), row(CONV_CH),
                  pl.BlockSpec((D_MODEL, D_MODEL), lambda i: (0, 0), pipeline_mode=pl.Buffered(1)),
                  row(D_MODEL), mod, mod, mod,
                  pl.BlockSpec((1, D_MODEL), lambda i: (0, 0)),
                  pl.BlockSpec((N_EXPERTS, D_MODEL), lambda i: (0, 0)),
                  pl.BlockSpec((N_EXPERTS, 1), lambda i: (0, 0))],
        out_specs=[row(D_MODEL),
                   pl.BlockSpec((OUT_TM * PACK_ROWS, PACK_LANES), lambda i: (i, 0)),
                   slot, slot, slot,
                   pl.BlockSpec((N_EXPERTS, 128), lambda i: (0, 0))],
        out_shape=[jax.ShapeDtypeStruct((TOKENS, D_MODEL), f32),
                   jax.ShapeDtypeStruct((TOKENS * PACK_ROWS, PACK_LANES), jnp.uint32),
                   jax.ShapeDtypeStruct((TOP_K, TOKENS), i32),
                   jax.ShapeDtypeStruct((TOP_K, TOKENS), f32),
                   jax.ShapeDtypeStruct((TOP_K, TOKENS), i32),
                   jax.ShapeDtypeStruct((N_EXPERTS, 128), i32)],
        scratch_shapes=[pltpu.VMEM((N_EXPERTS, 1), f32), pltpu.VMEM((OUT_TM, D_MODEL), f32)],
        compiler_params=_cparams(("arbitrary",), 52),
        name="outproj",
    )(att, cv, wo_bf, x2d, gate1.reshape(BATCH, 1, D_MODEL), scale2.reshape(BATCH, 1, D_MODEL),
      shift2.reshape(BATCH, 1, D_MODEL), norm2_g.reshape(1, D_MODEL), w_router.T,
      b_router.reshape(N_EXPERTS, 1))


def _token_copy(src_hbm, token, dst_buf, row, sem):
    dst = dst_buf.at[pl.ds(pl.multiple_of(row * PACK_ROWS, PACK_ROWS), PACK_ROWS), :]
    return pltpu.make_async_copy(src_hbm.at[token], dst, sem)


_PAD_BITS = tuple(range(MOE_TM.bit_length() - 2, -1, -1))


def _pad_fill_copies(pad_start, pad_len, zeros, xs_hbm, sem):
    pairs = []
    for b in _PAD_BITS:
        size = 1 << b
        off = pad_start + lax.shift_left(lax.shift_right_logical(pad_len, b + 1), b + 1)
        copy = pltpu.make_async_copy(zeros.at[pl.ds(0, size)], xs_hbm.at[pl.ds(off, size)], sem)
        pairs.append((lax.shift_right_logical(pad_len, b) & 1 == 1, copy))
    return pairs


def _dispatch_kernel(pad_start_ref, pad_len_ref, nv_ref, pos_ref, h2p_ref, xs_hbm, zeros, sem, zsem):
    i = pl.program_id(0)
    last = pl.num_programs(0) - 1

    def for_each_fill(act):
        def per_expert(e, carry):
            for cond, copy in _pad_fill_copies(pad_start_ref[e], pad_len_ref[e], zeros, xs_hbm, zsem):
                @pl.when(cond)
                def _(copy=copy):
                    act(copy)
            return carry
        lax.fori_loop(0, N_EXPERTS, per_expert, 0)

        def per_tile(t, carry):
            row0 = pl.multiple_of(t * MOE_TM, MOE_TM)
            act(pltpu.make_async_copy(zeros, xs_hbm.at[pl.ds(row0, MOE_TM)], zsem))
            return carry
        lax.fori_loop(nv_ref[0], MOE_NT, per_tile, 0)

    @pl.when(i == 0)
    def _():
        zeros[...] = jnp.zeros_like(zeros)
        for_each_fill(lambda copy: copy.start())

    def body(g, carry):
        for u in range(DMA_UNROLL // TOP_K):
            r = g * (DMA_UNROLL // TOP_K) + u
            for k in range(TOP_K):
                pltpu.make_async_copy(h2p_ref.at[r], xs_hbm.at[pos_ref[0, k, r]], sem).start(
                    priority=k % DMA_PRIORITIES)
        return carry
    lax.fori_loop(0, DSP_TB * TOP_K // DMA_UNROLL, body, 0)
    pltpu.make_async_copy(xs_hbm.at[pl.ds(0, DSP_TB * TOP_K)], xs_hbm.at[pl.ds(0, DSP_TB * TOP_K)],
                          sem).wait()

    @pl.when(i == last)
    def _():
        for_each_fill(lambda copy: copy.wait())


def _dispatch(pad_start, pad_len, n_valid, pos, h2_packed):
    nt = TOKENS // DSP_TB
    pos3 = pos.reshape(TOP_K, nt, DSP_TB).transpose(1, 0, 2)
    grid_spec = pltpu.PrefetchScalarGridSpec(
        num_scalar_prefetch=3,
        grid=(nt,),
        in_specs=[pl.BlockSpec((1, TOP_K, DSP_TB), lambda i, *_: (i, 0, 0), memory_space=pltpu.SMEM),
                  pl.BlockSpec((DSP_TB, PACK_ROWS, PACK_LANES), lambda i, *_: (i, 0, 0))],
        out_specs=pl.BlockSpec(memory_space=pl.ANY),
        scratch_shapes=[pltpu.VMEM((MOE_TM, PACK_ROWS, PACK_LANES), jnp.uint32),
                        pltpu.SemaphoreType.DMA(()), pltpu.SemaphoreType.DMA(())])
    return pl.pallas_call(
        _dispatch_kernel,
        grid_spec=grid_spec,
        out_shape=jax.ShapeDtypeStruct((MOE_ROWS, PACK_ROWS, PACK_LANES), jnp.uint32),
        compiler_params=_cparams(("arbitrary",), 24),
        name="dispatch",
    )(pad_start, pad_len, n_valid, pos3, h2_packed.reshape(TOKENS, PACK_ROWS, PACK_LANES))


def _weights_changed(te_ref, m):
    prev = te_ref[jnp.maximum(m - 1, 0)]
    return (m == 0) | (te_ref[m] != prev)


def _expert_tile_step(te_ref, nx_ref, nv_ref, n, m, n_passes, count_ref, copies_for, convert,
                      compute, write_zeros):
    first_expert = te_ref[0]
    first = _weights_changed(te_ref, m)
    valid = m < nv_ref[0]

    @pl.when((n == 0) & (m == 0))
    def _():
        count_ref[0] = 0
        for copy in copies_for(0, first_expert, 0):
            copy.start()

    @pl.when(first)
    def _():
        slot = count_ref[0] % 2
        nxt = nx_ref[m]

        @pl.when(nxt >= 0)
        def _():
            for copy in copies_for(n, nxt, 1 - slot):
                copy.start()

        @pl.when((nxt < 0) & (n + 1 < n_passes))
        def _():
            for copy in copies_for(n + 1, first_expert, 1 - slot):
                copy.start()

        for copy in copies_for(0, 0, slot):
            copy.wait()
        convert(slot)
        compute()
        count_ref[0] = count_ref[0] + 1

    @pl.when(jnp.logical_not(first) & valid)
    def _():
        compute()

    @pl.when(jnp.logical_not(valid))
    def _():
        write_zeros()


def _gmm1_kernel(te_ref, nx_ref, nv_ref, xs_ref, w_hbm, bg_ref, bu_ref, o_ref, stage, wg_bf, wu_bf,
                 count_ref, sem):
    n = pl.program_id(0)
    m = pl.program_id(1)

    def copies_for(p, e, slot):
        col = pl.multiple_of(p * GMM1_TN, GMM1_TN)
        return [pltpu.make_async_copy(w_hbm.at[e, :, pl.ds(col, GMM1_TN)], stage.at[slot, 0],
                                      sem.at[slot]),
                pltpu.make_async_copy(w_hbm.at[e, :, pl.ds(D_FF + col, GMM1_TN)],
                                      stage.at[slot, 1], sem.at[slot])]

    def convert(slot):
        wg_bf[...] = stage[slot, 0].astype(bf16)
        wu_bf[...] = stage[slot, 1].astype(bf16)

    def compute():
        parts = []
        for s in range(PACK_ROWS):
            lo, hi = _load_packed(xs_ref, MOE_TM, s)
            parts += [lo.astype(bf16), hi.astype(bf16)]
        x = jnp.concatenate(parts, axis=1)
        gate = jnp.dot(x, wg_bf[...], preferred_element_type=f32) + bg_ref[0]
        up = jnp.dot(x, wu_bf[...], preferred_element_type=f32) + bu_ref[0]
        gate = jnp.minimum(gate, SWIGLU_LIMIT)
        up = jnp.clip(up, -SWIGLU_LIMIT, SWIGLU_LIMIT)
        act = (up + 1.0) * (gate * jax.nn.sigmoid(SWIGLU_ALPHA * gate))
        o_ref[...] = act.astype(bf16)

    def write_zeros():
        o_ref[...] = jnp.zeros_like(o_ref)

    _expert_tile_step(te_ref, nx_ref, nv_ref, n, m, pl.num_programs(0), count_ref, copies_for,
                      convert, compute, write_zeros)


def _gmm1(tile_expert, next_expert, n_valid, xs_packed, w_gate_up, b_gate_up):
    up0 = D_FF // GMM1_TN
    used = lambda m, nv: jnp.minimum(m, nv[0] - 1)
    grid_spec = pltpu.PrefetchScalarGridSpec(
        num_scalar_prefetch=3,
        grid=(D_FF // GMM1_TN, MOE_NT),
        in_specs=[pl.BlockSpec((MOE_TM * PACK_ROWS, PACK_LANES),
                               lambda n, m, te, nx, nv: (used(m, nv), 0)),
                  pl.BlockSpec(memory_space=pl.ANY),
                  pl.BlockSpec((1, 1, GMM1_TN), lambda n, m, te, nx, nv: (te[m], 0, n)),
                  pl.BlockSpec((1, 1, GMM1_TN), lambda n, m, te, nx, nv: (te[m], 0, up0 + n))],
        out_specs=pl.BlockSpec((MOE_TM, GMM1_TN), lambda n, m, te, nx, nv: (m, n)),
        scratch_shapes=[pltpu.VMEM((2, 2, D_MODEL, GMM1_TN), f32),
                        pltpu.VMEM((D_MODEL, GMM1_TN), bf16), pltpu.VMEM((D_MODEL, GMM1_TN), bf16),
                        pltpu.SMEM((1,), i32), pltpu.SemaphoreType.DMA((2,))])
    b3 = b_gate_up.reshape(N_EXPERTS, 1, 2 * D_FF)
    return pl.pallas_call(
        _gmm1_kernel,
        grid_spec=grid_spec,
        out_shape=jax.ShapeDtypeStruct((MOE_ROWS, D_FF), bf16),
        compiler_params=_cparams(("arbitrary", "arbitrary"), 54),
        name="gmm1",
    )(tile_expert, next_expert, n_valid, xs_packed.reshape(MOE_ROWS * PACK_ROWS, PACK_LANES),
      w_gate_up, b3, b3)


def _gmm2_kernel(te_ref, nx_ref, nv_ref, a_ref, w_hbm, bd_ref, o_ref, stage, wd_bf, y_scr,
                 count_ref, sem):
    def copies_for(p, e, slot):
        return [pltpu.make_async_copy(w_hbm.at[e], stage.at[slot], sem.at[slot])]

    def convert(slot):
        wd_bf[...] = stage[slot].astype(bf16)

    def compute():
        y_scr[...] = jnp.dot(a_ref[...], wd_bf[...], preferred_element_type=f32) + bd_ref[0]
        _store_packed(o_ref, 0, MOE_TM, lambda cols: y_scr[:, cols])

    def write_zeros():
        o_ref[...] = jnp.zeros_like(o_ref)

    _expert_tile_step(te_ref, nx_ref, nv_ref, pl.program_id(0), pl.program_id(1),
                      pl.num_programs(0), count_ref, copies_for, convert, compute, write_zeros)


def _gmm2(tile_expert, next_expert, n_valid, act, w_down, b_down):
    assert GMM2_TN == D_MODEL
    grid_spec = pltpu.PrefetchScalarGridSpec(
        num_scalar_prefetch=3,
        grid=(1, MOE_NT),
        in_specs=[pl.BlockSpec((MOE_TM, D_FF), lambda n, m, te, nx, nv: (m, 0)),
                  pl.BlockSpec(memory_space=pl.ANY),
                  pl.BlockSpec((1, 1, GMM2_TN), lambda n, m, te, nx, nv: (te[m], 0, 0))],
        out_specs=pl.BlockSpec((MOE_TM * PACK_ROWS, PACK_LANES), lambda n, m, te, nx, nv: (m, 0)),
        scratch_shapes=[pltpu.VMEM((2, D_FF, GMM2_TN), f32), pltpu.VMEM((D_FF, GMM2_TN), bf16),
                        pltpu.VMEM((MOE_TM, D_MODEL), f32), pltpu.SMEM((1,), i32),
                        pltpu.SemaphoreType.DMA((2,))])
    return pl.pallas_call(
        _gmm2_kernel,
        grid_spec=grid_spec,
        out_shape=jax.ShapeDtypeStruct((MOE_ROWS * PACK_ROWS, PACK_LANES), jnp.uint32),
        compiler_params=_cparams(("arbitrary", "arbitrary"), 56),
        name="gmm2",
    )(tile_expert, next_expert, n_valid, act, w_down, b_down.reshape(N_EXPERTS, 1, D_MODEL))


def _combine_kernel(pos_cur, pos_nxt, y_hbm, x1_ref, g2_ref, tw_ref, fg_ref, o_ref, buf, sem):
    i = pl.program_id(0)
    n = pl.num_programs(0)
    slot = i % 2

    def start_tile(pos_ref, s):
        def body(g, carry):
            for u in range(DMA_UNROLL // TOP_K):
                r = g * (DMA_UNROLL // TOP_K) + u
                for k in range(TOP_K):
                    _token_copy(y_hbm, pos_ref[0, k, r], buf.at[s, k], r, sem.at[s]).start(
                        priority=k % DMA_PRIORITIES)
            return carry
        lax.fori_loop(0, CMB_TB * TOP_K // DMA_UNROLL, body, 0)

    @pl.when(i == 0)
    def _():
        start_tile(pos_cur, 0)

    @pl.when(i + 1 < n)
    def _():
        start_tile(pos_nxt, 1 - slot)

    pltpu.make_async_copy(buf.at[slot], buf.at[slot], sem.at[slot]).wait()

    weights = [jnp.broadcast_to(tw_ref[:, k:k + 1], (CMB_TB, PACK_LANES)) for k in range(TOP_K)]
    sumsq = jnp.zeros((CMB_TB, 1), f32)
    for s in range(PACK_ROWS):
        lo, hi = _load_packed(buf.at[slot, 0], CMB_TB, s)
        lo, hi = lo * weights[0], hi * weights[0]
        for k in range(1, TOP_K):
            lo_k, hi_k = _load_packed(buf.at[slot, k], CMB_TB, s)
            lo, hi = lo + lo_k * weights[k], hi + hi_k * weights[k]
        for c, moe in ((2 * s, lo), (2 * s + 1, hi)):
            cols = slice(c * PACK_LANES, (c + 1) * PACK_LANES)
            x2 = x1_ref[:, cols] + g2_ref[0, :, cols] * moe
            o_ref[:, cols] = x2
            sumsq = sumsq + jnp.sum(x2 * x2, axis=-1, keepdims=True)
    inv = lax.rsqrt(sumsq * (1.0 / D_MODEL) + EPS)
    o_ref[...] = o_ref[...] * inv * fg_ref[...]


def _combine(pos, y_packed, x1, gate2, top_w_rows, final_g):
    nt = TOKENS // CMB_TB
    tiles_per_batch = SEQ // CMB_TB
    pos3 = pos.reshape(TOP_K, nt, CMB_TB).transpose(1, 0, 2)
    return pl.pallas_call(
        _combine_kernel,
        grid=(nt,),
        in_specs=[pl.BlockSpec((1, TOP_K, CMB_TB), lambda i: (i, 0, 0), memory_space=pltpu.SMEM),
                  pl.BlockSpec((1, TOP_K, CMB_TB), lambda i: (jnp.minimum(i + 1, nt - 1), 0, 0),
                               memory_space=pltpu.SMEM),
                  pl.BlockSpec(memory_space=pl.ANY),
                  pl.BlockSpec((CMB_TB, D_MODEL), lambda i: (i, 0)),
                  pl.BlockSpec((1, 1, D_MODEL), lambda i: (i // tiles_per_batch, 0, 0)),
                  pl.BlockSpec((CMB_TB, TOP_K), lambda i: (i, 0)),
                  pl.BlockSpec((1, D_MODEL), lambda i: (0, 0))],
        out_specs=pl.BlockSpec((CMB_TB, D_MODEL), lambda i: (i, 0)),
        out_shape=jax.ShapeDtypeStruct((TOKENS, D_MODEL), f32),
        scratch_shapes=[pltpu.VMEM((2, TOP_K, CMB_TB * PACK_ROWS, PACK_LANES), jnp.uint32),
                        pltpu.SemaphoreType.DMA((2,))],
        compiler_params=_cparams(("arbitrary",), 32),
        name="combine",
    )(pos3, pos3, y_packed.reshape(MOE_ROWS, PACK_ROWS, PACK_LANES), x1,
      gate2.reshape(BATCH, 1, D_MODEL), top_w_rows, final_g.reshape(1, D_MODEL))


def _routing_tables(counts, top_i, rank):
    padded = (counts + MOE_TM - 1) // MOE_TM * MOE_TM
    ends = jnp.cumsum(padded)
    starts = ends - padded
    n_valid = ends[-1] // MOE_TM
    tile_ids = jnp.minimum(jnp.arange(MOE_NT, dtype=i32), n_valid - 1)
    tile_expert = jnp.sum(tile_ids[:, None] >= (ends // MOE_TM)[None, :], axis=1).astype(i32)
    experts = jnp.arange(N_EXPERTS, dtype=i32)
    pos = rank + jnp.sum(jnp.where(top_i[..., None] == experts, starts, 0), axis=-1)
    pad_start = (starts + counts).astype(i32)
    pad_len = (padded - counts).astype(i32)
    later = (experts[None, :] > tile_expert[:, None]) & (counts > 0)[None, :]
    next_expert = jnp.min(jnp.where(later, experts[None, :], N_EXPERTS), axis=1)
    next_expert = jnp.where(next_expert == N_EXPERTS, -1, next_expert).astype(i32)
    return (tile_expert, next_expert, n_valid.reshape(1).astype(i32), pos.astype(i32), pad_start,
            pad_len)


def kernel(x, c, w_ada, b_ada, norm1_g, w_in, lambda_q1, lambda_k1, lambda_q2, lambda_k2, subln_g,
           rel_bias, conv_w, conv_b, conv_ln_g, conv_ln_b, w_out, norm2_g, w_router, b_router,
           w_gate_up, b_gate_up, w_down, b_down, final_g):
    assert x.shape == (BATCH, SEQ, D_MODEL) and w_ada.shape[0] == 1
    l = 0
    x2d = x.reshape(TOKENS, D_MODEL)
    mod = _adaln(c, w_ada[l], b_ada[l])
    shift1, scale1, gate1, shift2, scale2, gate2 = [mod[i] for i in range(6)]

    proj = _inproj(x2d, norm1_g[l], scale1, shift1, w_in[l].astype(bf16))
    att = _attention(proj, _bias_band(rel_bias), lambda_q1[l], lambda_k1[l], lambda_q2[l],
                     lambda_k2[l], subln_g[l])
    cv = _conv(proj, conv_w[l], conv_b[l], conv_ln_g[l], conv_ln_b[l])

    x1, h2_packed, top_i, top_w, rank, counts = _outproj(
        att, cv, w_out[l].astype(bf16), x2d, gate1, scale2, shift2, norm2_g[l], w_router[l],
        b_router[l])
    tile_expert, next_expert, n_valid, pos, pad_start, pad_len = _routing_tables(
        counts[:, 0], top_i, rank)

    xs = _dispatch(pad_start, pad_len, n_valid, pos, h2_packed)
    act = _gmm1(tile_expert, next_expert, n_valid, xs, w_gate_up[l], b_gate_up[l])
    y = _gmm2(tile_expert, next_expert, n_valid, act, w_down[l], b_down[l])
    out = _combine(pos, y, x1, gate2, top_w.T, final_g)
    return out.reshape(BATCH, SEQ, D_MODEL)
```

```python
import functools
import math

import jax
import jax.numpy as jnp
from jax import lax
from jax.experimental import pallas as pl
from jax.experimental.pallas import tpu as pltpu

f32 = jnp.float32
bf16 = jnp.bfloat16
i32 = jnp.int32

D_MODEL = 2048
BATCH = 4
SEQ = 2048
TOKENS = BATCH * SEQ
ATT_HEADS = 8
ATT_HALF_DIM = 64
ATT_V_DIM = 128
ATT_WIDTH = ATT_HEADS * ATT_V_DIM
CONV_CH = D_MODEL - ATT_WIDTH
CONV_KERNEL = 31
CONV_HALF = CONV_KERNEL // 2
QK_COLS = ATT_HEADS * 2 * ATT_HALF_DIM
IN_COLS = 2 * QK_COLS + ATT_WIDTH + 2 * CONV_CH
N_BUCKETS = 32
MAX_DISTANCE = 128
N_EXPERTS = 32
TOP_K = 4
D_FF = D_MODEL
SWIGLU_LIMIT = 7.0
SWIGLU_ALPHA = 1.702
EPS = 1e-6
LAMBDA_INIT = 0.8 - 0.6 * math.exp(-0.3 * 0)
LOG2E = math.log2(math.e)
Q_SCALE = LOG2E * ATT_HALF_DIM ** -0.5

ADA_TN = 1024
INP_TM = 1024
INP_TN = 1024
ATT_TQ = 1024
ATT_SUB = 256
ATT_KC = 256
ATT_EXT_CHUNKS = 5
CONV_TS = 512
CONV_HALO = 16
CONV_RB = 32
OUT_TM = 512
MOE_TM = 256
MOE_NT = TOKENS * TOP_K // MOE_TM + N_EXPERTS
MOE_ROWS = MOE_NT * MOE_TM
GMM1_TN = 1024
GMM2_TN = 2048
DSP_TB = 256
CMB_TB = 128
ROW_CHUNK = 64
DMA_UNROLL = 8
DMA_PRIORITIES = 2
WEIGHT_DMA_PRIORITY = 1

_MIB = 1024 * 1024


def _cparams(sem, vmem_mib):
    return pltpu.CompilerParams(dimension_semantics=sem, vmem_limit_bytes=vmem_mib * _MIB)


PACK_ROWS = 8
PACK_LANES = 128
_HI_MASK = 0xFFFF0000


def _pack_pair(lo, hi):
    lo_b = lax.shift_right_logical(
        lax.bitcast_convert_type(lo.astype(bf16).astype(f32), jnp.uint32), jnp.uint32(16))
    hi_b = lax.bitcast_convert_type(hi.astype(bf16).astype(f32), jnp.uint32) & jnp.uint32(_HI_MASK)
    return lo_b | hi_b


def _unpack_pair(w):
    lo = lax.bitcast_convert_type(lax.shift_left(w, jnp.uint32(16)), f32)
    hi = lax.bitcast_convert_type(w & jnp.uint32(_HI_MASK), f32)
    return lo, hi


def _store_packed(dst_ref, token0, n_tokens, get_cols):
    for s in range(PACK_ROWS):
        lo = get_cols(slice(2 * s * PACK_LANES, (2 * s + 1) * PACK_LANES))
        hi = get_cols(slice((2 * s + 1) * PACK_LANES, (2 * s + 2) * PACK_LANES))
        dst_ref[pl.ds(token0 * PACK_ROWS + s, n_tokens, stride=PACK_ROWS), :] = _pack_pair(lo, hi)


def _load_packed(src_ref, n_tokens, s):
    return _unpack_pair(src_ref[pl.ds(s, n_tokens, stride=PACK_ROWS), :])


def _row_chunks(n_rows, chunk, body):
    def step(i, carry):
        body(pl.multiple_of(i * chunk, chunk))
        return carry
    lax.fori_loop(0, n_rows // chunk, step, 0)


def _adaln_kernel(c_ref, w_ref, b_ref, o_ref):
    c = c_ref[...]
    cs = c * jax.nn.sigmoid(c)
    o_ref[0] = jnp.dot(cs.astype(bf16), w_ref[...].astype(bf16),
                       preferred_element_type=f32) + b_ref[...]


def _adaln(c, w_ada, b_ada):
    n = w_ada.shape[1]
    per_vec = D_MODEL // ADA_TN
    return pl.pallas_call(
        _adaln_kernel,
        grid=(n // ADA_TN,),
        in_specs=[pl.BlockSpec((BATCH, D_MODEL), lambda j: (0, 0)),
                  pl.BlockSpec((D_MODEL, ADA_TN), lambda j: (0, j)),
                  pl.BlockSpec((1, ADA_TN), lambda j: (0, j))],
        out_specs=pl.BlockSpec((1, BATCH, ADA_TN), lambda j: (j // per_vec, 0, j % per_vec)),
        out_shape=jax.ShapeDtypeStruct((n // D_MODEL, BATCH, D_MODEL), f32),
        compiler_params=_cparams(("arbitrary",), 40),
        name="adaln",
    )(c, w_ada, b_ada.reshape(1, n))


def _inproj_kernel(x_ref, g_ref, sc_ref, sh_ref, w_ref, o_ref, h_scr):
    @pl.when(pl.program_id(1) == 0)
    def _():
        g = g_ref[...]
        one_plus_scale = 1.0 + sc_ref[0]
        shift = sh_ref[0]

        def chunk(r0):
            x = x_ref[pl.ds(r0, ROW_CHUNK), :]
            ms = jnp.mean(x * x, axis=-1, keepdims=True)
            y = x * lax.rsqrt(ms + EPS) * g
            h_scr[pl.ds(r0, ROW_CHUNK), :] = (y * one_plus_scale + shift).astype(bf16)
        _row_chunks(INP_TM, ROW_CHUNK, chunk)

    out_scale = jnp.where(pl.program_id(1) < QK_COLS // INP_TN, Q_SCALE, 1.0).astype(f32)
    acc = jnp.dot(h_scr[...], w_ref[...].astype(bf16), preferred_element_type=f32)
    o_ref[...] = (acc * out_scale).astype(bf16)


def _inproj(x2d, norm_g, scale, shift, w_in):
    assert QK_COLS % INP_TN == 0
    tiles_per_batch = SEQ // INP_TM
    return pl.pallas_call(
        _inproj_kernel,
        grid=(TOKENS // INP_TM, IN_COLS // INP_TN),
        in_specs=[pl.BlockSpec((INP_TM, D_MODEL), lambda i, j: (i, 0)),
                  pl.BlockSpec((1, D_MODEL), lambda i, j: (0, 0)),
                  pl.BlockSpec((1, 1, D_MODEL), lambda i, j: (i // tiles_per_batch, 0, 0)),
                  pl.BlockSpec((1, 1, D_MODEL), lambda i, j: (i // tiles_per_batch, 0, 0)),
                  pl.BlockSpec((D_MODEL, INP_TN), lambda i, j: (0, j))],
        out_specs=pl.BlockSpec((INP_TM, INP_TN), lambda i, j: (i, j)),
        out_shape=jax.ShapeDtypeStruct((TOKENS, IN_COLS), bf16),
        scratch_shapes=[pltpu.VMEM((INP_TM, D_MODEL), bf16)],
        compiler_params=_cparams(("parallel", "arbitrary"), 48),
        name="inproj",
    )(x2d, norm_g.reshape(1, D_MODEL), scale.reshape(BATCH, 1, D_MODEL),
      shift.reshape(BATCH, 1, D_MODEL), w_in)


def _t5_bucket(rel):
    half = N_BUCKETS // 2
    max_exact = half // 2
    ret = jnp.where(rel > 0, half, 0)
    n = jnp.abs(rel)
    nf = jnp.maximum(n, 1).astype(f32)
    large = max_exact + (jnp.log(nf / max_exact) / math.log(MAX_DISTANCE / max_exact)
                         * (half - max_exact)).astype(i32)
    large = jnp.minimum(large, half - 1)
    return ret + jnp.where(n < max_exact, n, large)


def _bias_band(rel_bias):
    assert ATT_SUB == ATT_KC and ATT_EXT_CHUNKS == 5 and ATT_KC >= MAX_DISTANCE
    r = jnp.arange(ATT_SUB)[:, None]
    j = jnp.arange(ATT_EXT_CHUNKS * ATT_KC)[None, :]
    rel = j - 2 * ATT_KC - r
    onehot = (_t5_bucket(rel)[..., None] == jnp.arange(N_BUCKETS)).astype(f32)
    return jnp.einsum("rjn,nh->hrj", onehot, rel_bias.astype(f32) * LOG2E,
                      precision=lax.Precision.HIGHEST)


def _attn_kernel(q_ref, k_ref, v_ref, band_ref, lq1_ref, lk1_ref, lq2_ref, lk2_ref, sg_ref,
                 o_ref, va_scr):
    t = pl.program_id(2)
    n_chunks = SEQ // ATT_KC

    @pl.when(t == 0)
    def _():
        va_scr[:, :ATT_V_DIM] = v_ref[...]
        va_scr[:, ATT_V_DIM:] = jnp.ones((SEQ, ATT_V_DIM), bf16)

    lam = (jnp.exp(jnp.sum(lq1_ref[...] * lk1_ref[...], axis=-1, keepdims=True))
           - jnp.exp(jnp.sum(lq2_ref[...] * lk2_ref[...], axis=-1, keepdims=True))
           + LAMBDA_INIT)
    def band_offset(rows, c):
        block = t * (ATT_TQ // ATT_SUB) + rows.start // ATT_SUB
        d = jnp.clip(c - block + 2, 0, ATT_EXT_CHUNKS - 1)
        return pl.multiple_of(d * ATT_KC, ATT_KC)

    def scores(qm, rows):
        chunks = []
        m = jnp.full((ATT_SUB, 1), -jnp.inf, f32)
        for c in range(n_chunks):
            s = lax.dot_general(qm, k_ref[c * ATT_KC:(c + 1) * ATT_KC, :],
                                (((1,), (1,)), ((), ())), preferred_element_type=f32)
            s = s + band_ref[0, :, pl.ds(band_offset(rows, c), ATT_KC)]
            chunks.append(s)
            m = jnp.maximum(m, jnp.max(s, axis=-1, keepdims=True))
        return chunks, m

    def weighted_values(chunks, m):
        o = jnp.zeros((ATT_SUB, 2 * ATT_V_DIM), f32)
        for c in range(n_chunks):
            p = jnp.exp2(chunks[c] - m)
            o = o + jnp.dot(p.astype(bf16), va_scr[c * ATT_KC:(c + 1) * ATT_KC, :],
                            preferred_element_type=f32)
        return o[:, :ATT_V_DIM] / o[:, ATT_V_DIM:]

    def finish(rows, o1, o2):
        out = o1 - lam * o2
        ms = jnp.mean(out * out, axis=-1, keepdims=True)
        y = out * lax.rsqrt(ms + EPS) * sg_ref[...]
        o_ref[rows, :] = (y * (1.0 - LAMBDA_INIT)).astype(bf16)

    units = []
    for r in range(ATT_TQ // ATT_SUB):
        rows = slice(r * ATT_SUB, (r + 1) * ATT_SUB)
        q = q_ref[rows, :]
        lane = lax.broadcasted_iota(i32, q.shape, 1)
        zero = jnp.zeros_like(q)
        units.append((rows, jnp.where(lane < ATT_HALF_DIM, q, zero)))
        units.append((rows, jnp.where(lane >= ATT_HALF_DIM, q, zero)))
    first_map = {}
    pending = None
    for unit in units + [None]:
        current = (unit[0],) + scores(unit[1], unit[0]) if unit is not None else None
        if pending is not None:
            p_rows, p_chunks, p_m = pending
            o = weighted_values(p_chunks, p_m)
            if p_rows.start in first_map:
                finish(p_rows, first_map.pop(p_rows.start), o)
            else:
                first_map[p_rows.start] = o
        pending = current


def _attention(proj, band, lq1, lk1, lq2, lk2, subln_g):
    qt = SEQ // ATT_TQ
    k_col0 = QK_COLS // ATT_V_DIM
    v_col0 = 2 * QK_COLS // ATT_V_DIM
    vec = lambda n: pl.BlockSpec((1, n), lambda b, h, t: (0, 0))
    return pl.pallas_call(
        _attn_kernel,
        grid=(BATCH, ATT_HEADS, qt),
        in_specs=[pl.BlockSpec((ATT_TQ, ATT_V_DIM), lambda b, h, t: (b * qt + t, h)),
                  pl.BlockSpec((SEQ, ATT_V_DIM), lambda b, h, t: (b, k_col0 + h)),
                  pl.BlockSpec((SEQ, ATT_V_DIM), lambda b, h, t: (b, v_col0 + h)),
                  pl.BlockSpec((1, ATT_SUB, ATT_EXT_CHUNKS * ATT_KC), lambda b, h, t: (h, 0, 0)),
                  vec(ATT_HALF_DIM), vec(ATT_HALF_DIM), vec(ATT_HALF_DIM), vec(ATT_HALF_DIM),
                  vec(ATT_V_DIM)],
        out_specs=pl.BlockSpec((ATT_TQ, ATT_V_DIM), lambda b, h, t: (b * qt + t, h)),
        out_shape=jax.ShapeDtypeStruct((TOKENS, ATT_WIDTH), bf16),
        scratch_shapes=[pltpu.VMEM((SEQ, 2 * ATT_V_DIM), bf16)],
        compiler_params=_cparams(("parallel", "parallel", "arbitrary"), 40),
        name="attn",
    )(proj, proj, proj, band, lq1.reshape(1, -1), lk1.reshape(1, -1), lq2.reshape(1, -1),
      lk2.reshape(1, -1), subln_g.reshape(1, -1))


def _conv_kernel(ap_ref, ac_ref, an_ref, gp_ref, gc_ref, gn_ref, w_ref, cb_ref, lg_ref, lb_ref,
                 o_ref, z_scr, c_scr):
    s = pl.program_id(1)
    last = pl.num_programs(1) - 1

    lanes = 128
    n_lane_chunks = CONV_CH // lanes

    def store_z(rows, z):
        for lc in range(n_lane_chunks):
            z_scr[lc, rows, :] = z[:, lc * lanes:(lc + 1) * lanes]

    def glu(a, g):
        return a.astype(f32) * jax.nn.sigmoid(g.astype(f32))

    store_z(slice(0, CONV_HALO), jnp.where(s > 0, glu(ap_ref[...], gp_ref[...]), 0.0))
    store_z(slice(CONV_HALO + CONV_TS, CONV_TS + 2 * CONV_HALO),
            jnp.where(s < last, glu(an_ref[...], gn_ref[...]), 0.0))

    def glu_chunk(r0):
        rows = pl.ds(r0, ROW_CHUNK)
        store_z(pl.ds(CONV_HALO + r0, ROW_CHUNK), glu(ac_ref[rows, :], gc_ref[rows, :]))
    _row_chunks(CONV_TS, ROW_CHUNK, glu_chunk)

    tap0 = CONV_HALO - CONV_HALF

    def lane_chunk(lc, carry):
        cols = pl.ds(pl.multiple_of(lc * lanes, lanes), lanes)
        taps = [w_ref[j:j + 1, cols] for j in range(CONV_KERNEL)]
        bias = cb_ref[:, cols]
        for r0 in range(0, CONV_TS, CONV_RB):
            acc = jnp.zeros((CONV_RB, lanes), f32)
            for j in range(CONV_KERNEL):
                acc = acc + z_scr[lc, r0 + tap0 + j:r0 + tap0 + j + CONV_RB, :] * taps[j]
            c_scr[r0:r0 + CONV_RB, cols] = acc + bias
        return carry
    lax.fori_loop(0, n_lane_chunks, lane_chunk, 0)

    lg = lg_ref[...]
    lb = lb_ref[...]

    def ln_chunk(r0):
        c = c_scr[pl.ds(r0, ROW_CHUNK), :]
        mu = jnp.mean(c, axis=-1, keepdims=True)
        cc = c - mu
        var = jnp.mean(cc * cc, axis=-1, keepdims=True)
        y = cc * lax.rsqrt(var + EPS) * lg + lb
        o_ref[pl.ds(r0, ROW_CHUNK), :] = (y * jax.nn.sigmoid(y)).astype(bf16)
    _row_chunks(CONV_TS, ROW_CHUNK, ln_chunk)


def _conv(proj, conv_w, conv_b, ln_g, ln_b):
    st = SEQ // CONV_TS
    halo_per_tile = CONV_TS // CONV_HALO
    halo_per_seq = SEQ // CONV_HALO
    a_col = (2 * QK_COLS + ATT_WIDTH) // CONV_CH
    g_col = a_col + 1

    def cur(col):
        return pl.BlockSpec((CONV_TS, CONV_CH), lambda b, s: (b * st + s, col))

    def prev(col):
        return pl.BlockSpec(
            (CONV_HALO, CONV_CH),
            lambda b, s: (jnp.maximum(b * halo_per_seq + s * halo_per_tile - 1, 0), col))

    def nxt(col):
        return pl.BlockSpec(
            (CONV_HALO, CONV_CH),
            lambda b, s: (jnp.minimum(b * halo_per_seq + (s + 1) * halo_per_tile,
                                      TOKENS // CONV_HALO - 1), col))

    vec = pl.BlockSpec((1, CONV_CH), lambda b, s: (0, 0))
    return pl.pallas_call(
        _conv_kernel,
        grid=(BATCH, st),
        in_specs=[prev(a_col), cur(a_col), nxt(a_col), prev(g_col), cur(g_col), nxt(g_col),
                  pl.BlockSpec((CONV_KERNEL, CONV_CH), lambda b, s: (0, 0)), vec, vec, vec],
        out_specs=pl.BlockSpec((CONV_TS, CONV_CH), lambda b, s: (b * st + s, 0)),
        out_shape=jax.ShapeDtypeStruct((TOKENS, CONV_CH), bf16),
        scratch_shapes=[pltpu.VMEM((CONV_CH // 128, CONV_TS + 2 * CONV_HALO, 128), f32),
                        pltpu.VMEM((CONV_TS, CONV_CH), f32)],
        compiler_params=_cparams(("parallel", "arbitrary"), 32),
        name="conv",
    )(proj, proj, proj, proj, proj, proj, conv_w, conv_b.reshape(1, -1), ln_g.reshape(1, -1),
      ln_b.reshape(1, -1))


def _outproj_kernel(att_ref, cv_ref, wo_ref, x_ref, g1_ref, sc2_ref, sh2_ref, n2_ref, wr_ref,
                    br_ref, x1_ref, h2p_ref, ti_ref, tw_ref, rk_ref, cnt_ref, carry_scr, h2_scr):
    @pl.when(pl.program_id(0) == 0)
    def _():
        carry_scr[...] = jnp.zeros_like(carry_scr)

    mixed = (jnp.dot(att_ref[...], wo_ref[0:ATT_WIDTH, :], preferred_element_type=f32)
             + jnp.dot(cv_ref[...], wo_ref[ATT_WIDTH:, :], preferred_element_type=f32))
    x1_ref[...] = x_ref[...] + g1_ref[0] * mixed

    n2 = n2_ref[...]
    one_plus_scale = 1.0 + sc2_ref[0]
    shift = sh2_ref[0]

    def chunk(r0):
        rows = pl.ds(r0, ROW_CHUNK)
        x1 = x1_ref[rows, :]
        ms = jnp.mean(x1 * x1, axis=-1, keepdims=True)
        y = x1 * lax.rsqrt(ms + EPS) * n2
        h2_scr[rows, :] = y * one_plus_scale + shift
        _store_packed(h2p_ref, r0, ROW_CHUNK, lambda cols: h2_scr[rows, cols])
    _row_chunks(OUT_TM, ROW_CHUNK, chunk)

    logits = lax.dot_general(wr_ref[...], h2_scr[...], (((1,), (1,)), ((), ())),
                             precision=lax.Precision.HIGHEST,
                             preferred_element_type=f32) + br_ref[...]
    eidx = lax.broadcasted_iota(i32, logits.shape, 0)
    vals = logits
    sels, tops = [], []
    for k in range(TOP_K):
        m = jnp.max(vals, axis=0, keepdims=True)
        idx = jnp.min(jnp.where(vals == m, eidx, N_EXPERTS), axis=0, keepdims=True)
        sel = eidx == idx
        ti_ref[k:k + 1, :] = idx
        sels.append(sel)
        tops.append(m)
        vals = jnp.where(sel, -jnp.inf, vals)
    exps = [jnp.exp(v - tops[0]) for v in tops]
    denom = exps[0] + exps[1] + exps[2] + exps[3]
    for k in range(TOP_K):
        tw_ref[k:k + 1, :] = exps[k] / denom

    onehot = jnp.where(sels[0] | sels[1] | sels[2] | sels[3], 1.0, 0.0)
    r_i = lax.broadcasted_iota(i32, (OUT_TM, OUT_TM), 0)
    c_i = lax.broadcasted_iota(i32, (OUT_TM, OUT_TM), 1)
    earlier = jnp.where(r_i < c_i, 1.0, 0.0).astype(bf16)
    before = jnp.dot(onehot.astype(bf16), earlier, preferred_element_type=f32) + carry_scr[...]
    for k in range(TOP_K):
        rk_ref[k:k + 1, :] = jnp.sum(jnp.where(sels[k], before, 0.0), axis=0,
                                     keepdims=True).astype(i32)
    total = carry_scr[...] + jnp.sum(onehot, axis=1, keepdims=True)
    carry_scr[...] = total
    cnt_ref[...] = jnp.broadcast_to(total, cnt_ref.shape).astype(i32)


def _outproj(att, cv, wo_bf, x2d, gate1, scale2, shift2, norm2_g, w_router, b_router):
    tiles_per_batch = SEQ // OUT_TM
    row = lambda w: pl.BlockSpec((OUT_TM, w), lambda i: (i, 0))
    mod = pl.BlockSpec((1, 1, D_MODEL), lambda i: (i // tiles_per_batch, 0, 0))
    slot = pl.BlockSpec((TOP_K, OUT_TM), lambda i: (0, i))
    return pl.pallas_call(
        _outproj_kernel,
        grid=(TOKENS // OUT_TM,),
        in_specs=[row(ATT_WIDTH), row(CONV_CH),
                  pl.BlockSpec((D_MODEL, D_MODEL), lambda i: (0, 0), pipeline_mode=pl.Buffered(1)),
                  row(D_MODEL), mod, mod, mod,
                  pl.BlockSpec((1, D_MODEL), lambda i: (0, 0)),
                  pl.BlockSpec((N_EXPERTS, D_MODEL), lambda i: (0, 0)),
                  pl.BlockSpec((N_EXPERTS, 1), lambda i: (0, 0))],
        out_specs=[row(D_MODEL),
                   pl.BlockSpec((OUT_TM * PACK_ROWS, PACK_LANES), lambda i: (i, 0)),
                   slot, slot, slot,
                   pl.BlockSpec((N_EXPERTS, 128), lambda i: (0, 0))],
        out_shape=[jax.ShapeDtypeStruct((TOKENS, D_MODEL), f32),
                   jax.ShapeDtypeStruct((TOKENS * PACK_ROWS, PACK_LANES), jnp.uint32),
                   jax.ShapeDtypeStruct((TOP_K, TOKENS), i32),
                   jax.ShapeDtypeStruct((TOP_K, TOKENS), f32),
                   jax.ShapeDtypeStruct((TOP_K, TOKENS), i32),
                   jax.ShapeDtypeStruct((N_EXPERTS, 128), i32)],
        scratch_shapes=[pltpu.VMEM((N_EXPERTS, 1), f32), pltpu.VMEM((OUT_TM, D_MODEL), f32)],
        compiler_params=_cparams(("arbitrary",), 52),
        name="outproj",
    )(att, cv, wo_bf, x2d, gate1.reshape(BATCH, 1, D_MODEL), scale2.reshape(BATCH, 1, D_MODEL),
      shift2.reshape(BATCH, 1, D_MODEL), norm2_g.reshape(1, D_MODEL), w_router.T,
      b_router.reshape(N_EXPERTS, 1))


def _token_copy(src_hbm, token, dst_buf, row, sem):
    dst = dst_buf.at[pl.ds(pl.multiple_of(row * PACK_ROWS, PACK_ROWS), PACK_ROWS), :]
    return pltpu.make_async_copy(src_hbm.at[token], dst, sem)


_PAD_BITS = tuple(range(MOE_TM.bit_length() - 2, -1, -1))


def _pad_fill_copies(pad_start, pad_len, zeros, xs_hbm, sem):
    pairs = []
    for b in _PAD_BITS:
        size = 1 << b
        off = pad_start + lax.shift_left(lax.shift_right_logical(pad_len, b + 1), b + 1)
        copy = pltpu.make_async_copy(zeros.at[pl.ds(0, size)], xs_hbm.at[pl.ds(off, size)], sem)
        pairs.append((lax.shift_right_logical(pad_len, b) & 1 == 1, copy))
    return pairs


def _dispatch_kernel(pad_start_ref, pad_len_ref, nv_ref, pos_ref, h2p_ref, xs_hbm, zeros, sem, zsem):
    i = pl.program_id(0)
    last = pl.num_programs(0) - 1

    def for_each_fill(act):
        def per_expert(e, carry):
            for cond, copy in _pad_fill_copies(pad_start_ref[e], pad_len_ref[e], zeros, xs_hbm, zsem):
                @pl.when(cond)
                def _(copy=copy):
                    act(copy)
            return carry
        lax.fori_loop(0, N_EXPERTS, per_expert, 0)

        def per_tile(t, carry):
            row0 = pl.multiple_of(t * MOE_TM, MOE_TM)
            act(pltpu.make_async_copy(zeros, xs_hbm.at[pl.ds(row0, MOE_TM)], zsem))
            return carry
        lax.fori_loop(nv_ref[0], MOE_NT, per_tile, 0)

    @pl.when(i == 0)
    def _():
        zeros[...] = jnp.zeros_like(zeros)
        for_each_fill(lambda copy: copy.start())

    def body(g, carry):
        for u in range(DMA_UNROLL // TOP_K):
            r = g * (DMA_UNROLL // TOP_K) + u
            for k in range(TOP_K):
                pltpu.make_async_copy(h2p_ref.at[r], xs_hbm.at[pos_ref[0, k, r]], sem).start(
                    priority=k % DMA_PRIORITIES)
        return carry
    lax.fori_loop(0, DSP_TB * TOP_K // DMA_UNROLL, body, 0)
    pltpu.make_async_copy(xs_hbm.at[pl.ds(0, DSP_TB * TOP_K)], xs_hbm.at[pl.ds(0, DSP_TB * TOP_K)],
                          sem).wait()

    @pl.when(i == last)
    def _():
        for_each_fill(lambda copy: copy.wait())


def _dispatch(pad_start, pad_len, n_valid, pos, h2_packed):
    nt = TOKENS // DSP_TB
    pos3 = pos.reshape(TOP_K, nt, DSP_TB).transpose(1, 0, 2)
    grid_spec = pltpu.PrefetchScalarGridSpec(
        num_scalar_prefetch=3,
        grid=(nt,),
        in_specs=[pl.BlockSpec((1, TOP_K, DSP_TB), lambda i, *_: (i, 0, 0), memory_space=pltpu.SMEM),
                  pl.BlockSpec((DSP_TB, PACK_ROWS, PACK_LANES), lambda i, *_: (i, 0, 0))],
        out_specs=pl.BlockSpec(memory_space=pl.ANY),
        scratch_shapes=[pltpu.VMEM((MOE_TM, PACK_ROWS, PACK_LANES), jnp.uint32),
                        pltpu.SemaphoreType.DMA(()), pltpu.SemaphoreType.DMA(())])
    return pl.pallas_call(
        _dispatch_kernel,
        grid_spec=grid_spec,
        out_shape=jax.ShapeDtypeStruct((MOE_ROWS, PACK_ROWS, PACK_LANES), jnp.uint32),
        compiler_params=_cparams(("arbitrary",), 24),
        name="dispatch",
    )(pad_start, pad_len, n_valid, pos3, h2_packed.reshape(TOKENS, PACK_ROWS, PACK_LANES))


def _weights_changed(te_ref, m):
    prev = te_ref[jnp.maximum(m - 1, 0)]
    return (m == 0) | (te_ref[m] != prev)


def _expert_tile_step(te_ref, nx_ref, nv_ref, n, m, n_passes, count_ref, copies_for, convert,
                      compute, write_zeros):
    first_expert = te_ref[0]
    first = _weights_changed(te_ref, m)
    valid = m < nv_ref[0]

    def start(copies):
        for copy in copies:
            copy.start(priority=WEIGHT_DMA_PRIORITY)

    @pl.when((n == 0) & (m == 0))
    def _():
        count_ref[0] = 0
        start(copies_for(0, first_expert, 0))

    @pl.when(first)
    def _():
        slot = count_ref[0] % 2
        nxt = nx_ref[m]

        @pl.when(nxt >= 0)
        def _():
            start(copies_for(n, nxt, 1 - slot))

        @pl.when((nxt < 0) & (n + 1 < n_passes))
        def _():
            start(copies_for(n + 1, first_expert, 1 - slot))

        for copy in copies_for(0, 0, slot):
            copy.wait()
        convert(slot)
        compute()
        count_ref[0] = count_ref[0] + 1

    @pl.when(jnp.logical_not(first) & valid)
    def _():
        compute()

    @pl.when(jnp.logical_not(valid))
    def _():
        write_zeros()


def _gmm1_kernel(te_ref, nx_ref, nv_ref, xs_ref, w_hbm, bg_ref, bu_ref, o_ref, stage, wg_bf, wu_bf,
                 count_ref, sem):
    n = pl.program_id(0)
    m = pl.program_id(1)

    def copies_for(p, e, slot):
        col = pl.multiple_of(p * GMM1_TN, GMM1_TN)
        return [pltpu.make_async_copy(w_hbm.at[e, :, pl.ds(col, GMM1_TN)], stage.at[slot, 0],
                                      sem.at[slot]),
                pltpu.make_async_copy(w_hbm.at[e, :, pl.ds(D_FF + col, GMM1_TN)],
                                      stage.at[slot, 1], sem.at[slot])]

    def convert(slot):
        wg_bf[...] = stage[slot, 0].astype(bf16)
        wu_bf[...] = stage[slot, 1].astype(bf16)

    def compute():
        parts = []
        for s in range(PACK_ROWS):
            lo, hi = _load_packed(xs_ref, MOE_TM, s)
            parts += [lo.astype(bf16), hi.astype(bf16)]
        x = jnp.concatenate(parts, axis=1)
        gate = jnp.dot(x, wg_bf[...], preferred_element_type=f32) + bg_ref[0]
        up = jnp.dot(x, wu_bf[...], preferred_element_type=f32) + bu_ref[0]
        gate = jnp.minimum(gate, SWIGLU_LIMIT)
        up = jnp.clip(up, -SWIGLU_LIMIT, SWIGLU_LIMIT)
        act = (up + 1.0) * (gate * jax.nn.sigmoid(SWIGLU_ALPHA * gate))
        o_ref[...] = act.astype(bf16)

    def write_zeros():
        o_ref[...] = jnp.zeros_like(o_ref)

    _expert_tile_step(te_ref, nx_ref, nv_ref, n, m, pl.num_programs(0), count_ref, copies_for,
                      convert, compute, write_zeros)


def _gmm1(tile_expert, next_expert, n_valid, xs_packed, w_gate_up, b_gate_up):
    up0 = D_FF // GMM1_TN
    used = lambda m, nv: jnp.minimum(m, nv[0] - 1)
    grid_spec = pltpu.PrefetchScalarGridSpec(
        num_scalar_prefetch=3,
        grid=(D_FF // GMM1_TN, MOE_NT),
        in_specs=[pl.BlockSpec((MOE_TM * PACK_ROWS, PACK_LANES),
                               lambda n, m, te, nx, nv: (used(m, nv), 0)),
                  pl.BlockSpec(memory_space=pl.ANY),
                  pl.BlockSpec((1, 1, GMM1_TN), lambda n, m, te, nx, nv: (te[m], 0, n)),
                  pl.BlockSpec((1, 1, GMM1_TN), lambda n, m, te, nx, nv: (te[m], 0, up0 + n))],
        out_specs=pl.BlockSpec((MOE_TM, GMM1_TN), lambda n, m, te, nx, nv: (m, n)),
        scratch_shapes=[pltpu.VMEM((2, 2, D_MODEL, GMM1_TN), f32),
                        pltpu.VMEM((D_MODEL, GMM1_TN), bf16), pltpu.VMEM((D_MODEL, GMM1_TN), bf16),
                        pltpu.SMEM((1,), i32), pltpu.SemaphoreType.DMA((2,))])
    b3 = b_gate_up.reshape(N_EXPERTS, 1, 2 * D_FF)
    return pl.pallas_call(
        _gmm1_kernel,
        grid_spec=grid_spec,
        out_shape=jax.ShapeDtypeStruct((MOE_ROWS, D_FF), bf16),
        compiler_params=_cparams(("arbitrary", "arbitrary"), 54),
        name="gmm1",
    )(tile_expert, next_expert, n_valid, xs_packed.reshape(MOE_ROWS * PACK_ROWS, PACK_LANES),
      w_gate_up, b3, b3)


def _gmm2_kernel(te_ref, nx_ref, nv_ref, a_ref, w_hbm, bd_ref, o_ref, stage, wd_bf, y_scr,
                 count_ref, sem):
    def copies_for(p, e, slot):
        return [pltpu.make_async_copy(w_hbm.at[e], stage.at[slot], sem.at[slot])]

    def convert(slot):
        wd_bf[...] = stage[slot].astype(bf16)

    def compute():
        y_scr[...] = jnp.dot(a_ref[...], wd_bf[...], preferred_element_type=f32) + bd_ref[0]
        _store_packed(o_ref, 0, MOE_TM, lambda cols: y_scr[:, cols])

    def write_zeros():
        o_ref[...] = jnp.zeros_like(o_ref)

    _expert_tile_step(te_ref, nx_ref, nv_ref, pl.program_id(0), pl.program_id(1),
                      pl.num_programs(0), count_ref, copies_for, convert, compute, write_zeros)


def _gmm2(tile_expert, next_expert, n_valid, act, w_down, b_down):
    assert GMM2_TN == D_MODEL
    grid_spec = pltpu.PrefetchScalarGridSpec(
        num_scalar_prefetch=3,
        grid=(1, MOE_NT),
        in_specs=[pl.BlockSpec((MOE_TM, D_FF), lambda n, m, te, nx, nv: (m, 0)),
                  pl.BlockSpec(memory_space=pl.ANY),
                  pl.BlockSpec((1, 1, GMM2_TN), lambda n, m, te, nx, nv: (te[m], 0, 0))],
        out_specs=pl.BlockSpec((MOE_TM * PACK_ROWS, PACK_LANES), lambda n, m, te, nx, nv: (m, 0)),
        scratch_shapes=[pltpu.VMEM((2, D_FF, GMM2_TN), f32), pltpu.VMEM((D_FF, GMM2_TN), bf16),
                        pltpu.VMEM((MOE_TM, D_MODEL), f32), pltpu.SMEM((1,), i32),
                        pltpu.SemaphoreType.DMA((2,))])
    return pl.pallas_call(
        _gmm2_kernel,
        grid_spec=grid_spec,
        out_shape=jax.ShapeDtypeStruct((MOE_ROWS * PACK_ROWS, PACK_LANES), jnp.uint32),
        compiler_params=_cparams(("arbitrary", "arbitrary"), 56),
        name="gmm2",
    )(tile_expert, next_expert, n_valid, act, w_down, b_down.reshape(N_EXPERTS, 1, D_MODEL))


def _combine_kernel(pos_cur, pos_nxt, y_hbm, x1_ref, g2_ref, tw_ref, fg_ref, o_ref, buf, sem):
    i = pl.program_id(0)
    n = pl.num_programs(0)
    slot = i % 2

    def start_tile(pos_ref, s):
        def body(g, carry):
            for u in range(DMA_UNROLL // TOP_K):
                r = g * (DMA_UNROLL // TOP_K) + u
                for k in range(TOP_K):
                    _token_copy(y_hbm, pos_ref[0, k, r], buf.at[s, k], r, sem.at[s]).start(
                        priority=k % DMA_PRIORITIES)
            return carry
        lax.fori_loop(0, CMB_TB * TOP_K // DMA_UNROLL, body, 0)

    @pl.when(i == 0)
    def _():
        start_tile(pos_cur, 0)

    @pl.when(i + 1 < n)
    def _():
        start_tile(pos_nxt, 1 - slot)

    pltpu.make_async_copy(buf.at[slot], buf.at[slot], sem.at[slot]).wait()

    weights = [jnp.broadcast_to(tw_ref[:, k:k + 1], (CMB_TB, PACK_LANES)) for k in range(TOP_K)]
    sumsq = jnp.zeros((CMB_TB, 1), f32)
    for s in range(PACK_ROWS):
        lo, hi = _load_packed(buf.at[slot, 0], CMB_TB, s)
        lo, hi = lo * weights[0], hi * weights[0]
        for k in range(1, TOP_K):
            lo_k, hi_k = _load_packed(buf.at[slot, k], CMB_TB, s)
            lo, hi = lo + lo_k * weights[k], hi + hi_k * weights[k]
        for c, moe in ((2 * s, lo), (2 * s + 1, hi)):
            cols = slice(c * PACK_LANES, (c + 1) * PACK_LANES)
            x2 = x1_ref[:, cols] + g2_ref[0, :, cols] * moe
            o_ref[:, cols] = x2
            sumsq = sumsq + jnp.sum(x2 * x2, axis=-1, keepdims=True)
    inv = lax.rsqrt(sumsq * (1.0 / D_MODEL) + EPS)
    o_ref[...] = o_ref[...] * inv * fg_ref[...]


def _combine(pos, y_packed, x1, gate2, top_w_rows, final_g):
    nt = TOKENS // CMB_TB
    tiles_per_batch = SEQ // CMB_TB
    pos3 = pos.reshape(TOP_K, nt, CMB_TB).transpose(1, 0, 2)
    return pl.pallas_call(
        _combine_kernel,
        grid=(nt,),
        in_specs=[pl.BlockSpec((1, TOP_K, CMB_TB), lambda i: (i, 0, 0), memory_space=pltpu.SMEM),
                  pl.BlockSpec((1, TOP_K, CMB_TB), lambda i: (jnp.minimum(i + 1, nt - 1), 0, 0),
                               memory_space=pltpu.SMEM),
                  pl.BlockSpec(memory_space=pl.ANY),
                  pl.BlockSpec((CMB_TB, D_MODEL), lambda i: (i, 0)),
                  pl.BlockSpec((1, 1, D_MODEL), lambda i: (i // tiles_per_batch, 0, 0)),
                  pl.BlockSpec((CMB_TB, TOP_K), lambda i: (i, 0)),
                  pl.BlockSpec((1, D_MODEL), lambda i: (0, 0))],
        out_specs=pl.BlockSpec((CMB_TB, D_MODEL), lambda i: (i, 0)),
        out_shape=jax.ShapeDtypeStruct((TOKENS, D_MODEL), f32),
        scratch_shapes=[pltpu.VMEM((2, TOP_K, CMB_TB * PACK_ROWS, PACK_LANES), jnp.uint32),
                        pltpu.SemaphoreType.DMA((2,))],
        compiler_params=_cparams(("arbitrary",), 32),
        name="combine",
    )(pos3, pos3, y_packed.reshape(MOE_ROWS, PACK_ROWS, PACK_LANES), x1,
      gate2.reshape(BATCH, 1, D_MODEL), top_w_rows, final_g.reshape(1, D_MODEL))


def _routing_tables(counts, top_i, rank):
    padded = (counts + MOE_TM - 1) // MOE_TM * MOE_TM
    ends = jnp.cumsum(padded)
    starts = ends - padded
    n_valid = ends[-1] // MOE_TM
    tile_ids = jnp.minimum(jnp.arange(MOE_NT, dtype=i32), n_valid - 1)
    tile_expert = jnp.sum(tile_ids[:, None] >= (ends // MOE_TM)[None, :], axis=1).astype(i32)
    experts = jnp.arange(N_EXPERTS, dtype=i32)
    pos = rank + jnp.sum(jnp.where(top_i[..., None] == experts, starts, 0), axis=-1)
    pad_start = (starts + counts).astype(i32)
    pad_len = (padded - counts).astype(i32)
    later = (experts[None, :] > tile_expert[:, None]) & (counts > 0)[None, :]
    next_expert = jnp.min(jnp.where(later, experts[None, :], N_EXPERTS), axis=1)
    next_expert = jnp.where(next_expert == N_EXPERTS, -1, next_expert).astype(i32)
    return (tile_expert, next_expert, n_valid.reshape(1).astype(i32), pos.astype(i32), pad_start,
            pad_len)


def kernel(x, c, w_ada, b_ada, norm1_g, w_in, lambda_q1, lambda_k1, lambda_q2, lambda_k2, subln_g,
           rel_bias, conv_w, conv_b, conv_ln_g, conv_ln_b, w_out, norm2_g, w_router, b_router,
           w_gate_up, b_gate_up, w_down, b_down, final_g):
    assert x.shape == (BATCH, SEQ, D_MODEL) and w_ada.shape[0] == 1
    l = 0
    x2d = x.reshape(TOKENS, D_MODEL)
    mod = _adaln(c, w_ada[l], b_ada[l])
    shift1, scale1, gate1, shift2, scale2, gate2 = [mod[i] for i in range(6)]

    proj = _inproj(x2d, norm1_g[l], scale1, shift1, w_in[l])
    att = _attention(proj, _bias_band(rel_bias), lambda_q1[l], lambda_k1[l], lambda_q2[l],
                     lambda_k2[l], subln_g[l])
    cv = _conv(proj, conv_w[l], conv_b[l], conv_ln_g[l], conv_ln_b[l])

    x1, h2_packed, top_i, top_w, rank, counts = _outproj(
        att, cv, w_out[l].astype(bf16), x2d, gate1, scale2, shift2, norm2_g[l], w_router[l],
        b_router[l])
    tile_expert, next_expert, n_valid, pos, pad_start, pad_len = _routing_tables(
        counts[:, 0], top_i, rank)

    xs = _dispatch(pad_start, pad_len, n_valid, pos, h2_packed)
    act = _gmm1(tile_expert, next_expert, n_valid, xs, w_gate_up[l], b_gate_up[l])
    y = _gmm2(tile_expert, next_expert, n_valid, act, w_down[l], b_down[l])
    out = _combine(pos, y, x1, gate2, top_w.T, final_g)
    return out.reshape(BATCH, SEQ, D_MODEL)
```

```python
import functools
import math

import jax
import jax.numpy as jnp
from jax import lax
from jax.experimental import pallas as pl
from jax.experimental.pallas import tpu as pltpu

f32 = jnp.float32
bf16 = jnp.bfloat16
i32 = jnp.int32

D_MODEL = 2048
BATCH = 4
SEQ = 2048
TOKENS = BATCH * SEQ
ATT_HEADS = 8
ATT_HALF_DIM = 64
ATT_V_DIM = 128
ATT_WIDTH = ATT_HEADS * ATT_V_DIM
CONV_CH = D_MODEL - ATT_WIDTH
CONV_KERNEL = 31
CONV_HALF = CONV_KERNEL // 2
QK_COLS = ATT_HEADS * 2 * ATT_HALF_DIM
IN_COLS = 2 * QK_COLS + ATT_WIDTH + 2 * CONV_CH
N_BUCKETS = 32
MAX_DISTANCE = 128
N_EXPERTS = 32
TOP_K = 4
D_FF = D_MODEL
SWIGLU_LIMIT = 7.0
SWIGLU_ALPHA = 1.702
EPS = 1e-6
LAMBDA_INIT = 0.8 - 0.6 * math.exp(-0.3 * 0)
LOG2E = math.log2(math.e)
Q_SCALE = LOG2E * ATT_HALF_DIM ** -0.5

ADA_TN = 1024
INP_TM = 1024
INP_TN = 1024
ATT_TQ = 1024
ATT_SUB = 256
ATT_KC = 256
ATT_EXT_CHUNKS = 5
CONV_TS = 512
CONV_HALO = 16
CONV_RB = 32
OUT_TM = 512
MOE_TM = 256
MOE_NT = TOKENS * TOP_K // MOE_TM + N_EXPERTS
MOE_ROWS = MOE_NT * MOE_TM
GMM1_TN = 1024
GMM2_TN = 2048
DSP_TB = 256
CMB_TB = 128
ROW_CHUNK = 64
DMA_UNROLL = 8
DMA_PRIORITIES = 2
WEIGHT_DMA_PRIORITY = 1

_MIB = 1024 * 1024


def _cparams(sem, vmem_mib):
    return pltpu.CompilerParams(dimension_semantics=sem, vmem_limit_bytes=vmem_mib * _MIB)


PACK_ROWS = 8
PACK_LANES = 128
_HI_MASK = 0xFFFF0000


def _pack_pair(lo, hi):
    lo_b = lax.shift_right_logical(
        lax.bitcast_convert_type(lo.astype(bf16).astype(f32), jnp.uint32), jnp.uint32(16))
    hi_b = lax.bitcast_convert_type(hi.astype(bf16).astype(f32), jnp.uint32) & jnp.uint32(_HI_MASK)
    return lo_b | hi_b


def _unpack_pair(w):
    lo = lax.bitcast_convert_type(lax.shift_left(w, jnp.uint32(16)), f32)
    hi = lax.bitcast_convert_type(w & jnp.uint32(_HI_MASK), f32)
    return lo, hi


def _store_packed(dst_ref, token0, n_tokens, get_cols):
    for s in range(PACK_ROWS):
        lo = get_cols(slice(2 * s * PACK_LANES, (2 * s + 1) * PACK_LANES))
        hi = get_cols(slice((2 * s + 1) * PACK_LANES, (2 * s + 2) * PACK_LANES))
        dst_ref[pl.ds(token0 * PACK_ROWS + s, n_tokens, stride=PACK_ROWS), :] = _pack_pair(lo, hi)


def _load_packed(src_ref, n_tokens, s):
    return _unpack_pair(src_ref[pl.ds(s, n_tokens, stride=PACK_ROWS), :])


def _row_chunks(n_rows, chunk, body):
    def step(i, carry):
        body(pl.multiple_of(i * chunk, chunk))
        return carry
    lax.fori_loop(0, n_rows // chunk, step, 0)


def _adaln_kernel(c_ref, w_ref, b_ref, o_ref):
    c = c_ref[...]
    cs = c * jax.nn.sigmoid(c)
    o_ref[0] = jnp.dot(cs.astype(bf16), w_ref[...].astype(bf16),
                       preferred_element_type=f32) + b_ref[...]


def _adaln(c, w_ada, b_ada):
    n = w_ada.shape[1]
    per_vec = D_MODEL // ADA_TN
    return pl.pallas_call(
        _adaln_kernel,
        grid=(n // ADA_TN,),
        in_specs=[pl.BlockSpec((BATCH, D_MODEL), lambda j: (0, 0)),
                  pl.BlockSpec((D_MODEL, ADA_TN), lambda j: (0, j)),
                  pl.BlockSpec((1, ADA_TN), lambda j: (0, j))],
        out_specs=pl.BlockSpec((1, BATCH, ADA_TN), lambda j: (j // per_vec, 0, j % per_vec)),
        out_shape=jax.ShapeDtypeStruct((n // D_MODEL, BATCH, D_MODEL), f32),
        compiler_params=_cparams(("arbitrary",), 40),
        name="adaln",
    )(c, w_ada, b_ada.reshape(1, n))


def _inproj_kernel(x_ref, g_ref, sc_ref, sh_ref, w_ref, o_ref, h_scr):
    @pl.when(pl.program_id(1) == 0)
    def _():
        g = g_ref[...]
        one_plus_scale = 1.0 + sc_ref[0]
        shift = sh_ref[0]

        def chunk(r0):
            x = x_ref[pl.ds(r0, ROW_CHUNK), :]
            ms = jnp.mean(x * x, axis=-1, keepdims=True)
            y = x * lax.rsqrt(ms + EPS) * g
            h_scr[pl.ds(r0, ROW_CHUNK), :] = (y * one_plus_scale + shift).astype(bf16)
        _row_chunks(INP_TM, ROW_CHUNK, chunk)

    out_scale = jnp.where(pl.program_id(1) < QK_COLS // INP_TN, Q_SCALE, 1.0).astype(f32)
    acc = jnp.dot(h_scr[...], w_ref[...].astype(bf16), preferred_element_type=f32)
    o_ref[...] = (acc * out_scale).astype(bf16)


def _inproj(x2d, norm_g, scale, shift, w_in):
    assert QK_COLS % INP_TN == 0
    tiles_per_batch = SEQ // INP_TM
    return pl.pallas_call(
        _inproj_kernel,
        grid=(TOKENS // INP_TM, IN_COLS // INP_TN),
        in_specs=[pl.BlockSpec((INP_TM, D_MODEL), lambda i, j: (i, 0)),
                  pl.BlockSpec((1, D_MODEL), lambda i, j: (0, 0)),
                  pl.BlockSpec((1, 1, D_MODEL), lambda i, j: (i // tiles_per_batch, 0, 0)),
                  pl.BlockSpec((1, 1, D_MODEL), lambda i, j: (i // tiles_per_batch, 0, 0)),
                  pl.BlockSpec((D_MODEL, INP_TN), lambda i, j: (0, j))],
        out_specs=pl.BlockSpec((INP_TM, INP_TN), lambda i, j: (i, j)),
        out_shape=jax.ShapeDtypeStruct((TOKENS, IN_COLS), bf16),
        scratch_shapes=[pltpu.VMEM((INP_TM, D_MODEL), bf16)],
        compiler_params=_cparams(("parallel", "arbitrary"), 48),
        name="inproj",
    )(x2d, norm_g.reshape(1, D_MODEL), scale.reshape(BATCH, 1, D_MODEL),
      shift.reshape(BATCH, 1, D_MODEL), w_in)


def _t5_bucket(rel):
    half = N_BUCKETS // 2
    max_exact = half // 2
    ret = jnp.where(rel > 0, half, 0)
    n = jnp.abs(rel)
    nf = jnp.maximum(n, 1).astype(f32)
    large = max_exact + (jnp.log(nf / max_exact) / math.log(MAX_DISTANCE / max_exact)
                         * (half - max_exact)).astype(i32)
    large = jnp.minimum(large, half - 1)
    return ret + jnp.where(n < max_exact, n, large)


def _bias_band(rel_bias):
    assert ATT_SUB == ATT_KC and ATT_EXT_CHUNKS == 5 and ATT_KC >= MAX_DISTANCE
    r = jnp.arange(ATT_SUB)[:, None]
    j = jnp.arange(ATT_EXT_CHUNKS * ATT_KC)[None, :]
    rel = j - 2 * ATT_KC - r
    onehot = (_t5_bucket(rel)[..., None] == jnp.arange(N_BUCKETS)).astype(f32)
    return jnp.einsum("rjn,nh->hrj", onehot, rel_bias.astype(f32) * LOG2E,
                      precision=lax.Precision.HIGHEST)


def _attn_kernel(q_ref, k_ref, v_ref, band_ref, lq1_ref, lk1_ref, lq2_ref, lk2_ref, sg_ref,
                 o_ref, va_scr):
    t = pl.program_id(2)
    n_chunks = SEQ // ATT_KC

    @pl.when(t == 0)
    def _():
        va_scr[:, :ATT_V_DIM] = v_ref[...]
        va_scr[:, ATT_V_DIM:] = jnp.ones((SEQ, ATT_V_DIM), bf16)

    lam = (jnp.exp(jnp.sum(lq1_ref[...] * lk1_ref[...], axis=-1, keepdims=True))
           - jnp.exp(jnp.sum(lq2_ref[...] * lk2_ref[...], axis=-1, keepdims=True))
           + LAMBDA_INIT)
    def band_offset(rows, c):
        block = t * (ATT_TQ // ATT_SUB) + rows.start // ATT_SUB
        d = jnp.clip(c - block + 2, 0, ATT_EXT_CHUNKS - 1)
        return pl.multiple_of(d * ATT_KC, ATT_KC)

    def scores(qm, rows):
        chunks = []
        m = jnp.full((ATT_SUB, 1), -jnp.inf, f32)
        for c in range(n_chunks):
            s = lax.dot_general(qm, k_ref[c * ATT_KC:(c + 1) * ATT_KC, :],
                                (((1,), (1,)), ((), ())), preferred_element_type=f32)
            s = s + band_ref[0, :, pl.ds(band_offset(rows, c), ATT_KC)]
            chunks.append(s)
            m = jnp.maximum(m, jnp.max(s, axis=-1, keepdims=True))
        return chunks, m

    def weighted_values(chunks, m):
        o = jnp.zeros((ATT_SUB, 2 * ATT_V_DIM), f32)
        for c in range(n_chunks):
            p = jnp.exp2(chunks[c] - m)
            o = o + jnp.dot(p.astype(bf16), va_scr[c * ATT_KC:(c + 1) * ATT_KC, :],
                            preferred_element_type=f32)
        return o[:, :ATT_V_DIM] / o[:, ATT_V_DIM:]

    def finish(rows, o1, o2):
        out = o1 - lam * o2
        ms = jnp.mean(out * out, axis=-1, keepdims=True)
        y = out * lax.rsqrt(ms + EPS) * sg_ref[...]
        o_ref[rows, :] = (y * (1.0 - LAMBDA_INIT)).astype(bf16)

    units = []
    for r in range(ATT_TQ // ATT_SUB):
        rows = slice(r * ATT_SUB, (r + 1) * ATT_SUB)
        q = q_ref[rows, :]
        lane = lax.broadcasted_iota(i32, q.shape, 1)
        zero = jnp.zeros_like(q)
        units.append((rows, jnp.where(lane < ATT_HALF_DIM, q, zero)))
        units.append((rows, jnp.where(lane >= ATT_HALF_DIM, q, zero)))
    first_map = {}
    pending = None
    for unit in units + [None]:
        current = (unit[0],) + scores(unit[1], unit[0]) if unit is not None else None
        if pending is not None:
            p_rows, p_chunks, p_m = pending
            o = weighted_values(p_chunks, p_m)
            if p_rows.start in first_map:
                finish(p_rows, first_map.pop(p_rows.start), o)
            else:
                first_map[p_rows.start] = o
        pending = current


def _attention(proj, band, lq1, lk1, lq2, lk2, subln_g):
    qt = SEQ // ATT_TQ
    k_col0 = QK_COLS // ATT_V_DIM
    v_col0 = 2 * QK_COLS // ATT_V_DIM
    vec = lambda n: pl.BlockSpec((1, n), lambda b, h, t: (0, 0))
    return pl.pallas_call(
        _attn_kernel,
        grid=(BATCH, ATT_HEADS, qt),
        in_specs=[pl.BlockSpec((ATT_TQ, ATT_V_DIM), lambda b, h, t: (b * qt + t, h)),
                  pl.BlockSpec((SEQ, ATT_V_DIM), lambda b, h, t: (b, k_col0 + h)),
                  pl.BlockSpec((SEQ, ATT_V_DIM), lambda b, h, t: (b, v_col0 + h)),
                  pl.BlockSpec((1, ATT_SUB, ATT_EXT_CHUNKS * ATT_KC), lambda b, h, t: (h, 0, 0)),
                  vec(ATT_HALF_DIM), vec(ATT_HALF_DIM), vec(ATT_HALF_DIM), vec(ATT_HALF_DIM),
                  vec(ATT_V_DIM)],
        out_specs=pl.BlockSpec((ATT_TQ, ATT_V_DIM), lambda b, h, t: (b * qt + t, h)),
        out_shape=jax.ShapeDtypeStruct((TOKENS, ATT_WIDTH), bf16),
        scratch_shapes=[pltpu.VMEM((SEQ, 2 * ATT_V_DIM), bf16)],
        compiler_params=_cparams(("parallel", "parallel", "arbitrary"), 40),
        name="attn",
    )(proj, proj, proj, band, lq1.reshape(1, -1), lk1.reshape(1, -1), lq2.reshape(1, -1),
      lk2.reshape(1, -1), subln_g.reshape(1, -1))


def _conv_kernel(ap_ref, ac_ref, an_ref, gp_ref, gc_ref, gn_ref, w_ref, cb_ref, lg_ref, lb_ref,
                 o_ref, z_scr, c_scr):
    s = pl.program_id(1)
    last = pl.num_programs(1) - 1

    lanes = 128
    n_lane_chunks = CONV_CH // lanes

    def store_z(rows, z):
        for lc in range(n_lane_chunks):
            z_scr[lc, rows, :] = z[:, lc * lanes:(lc + 1) * lanes]

    def glu(a, g):
        return a.astype(f32) * jax.nn.sigmoid(g.astype(f32))

    store_z(slice(0, CONV_HALO), jnp.where(s > 0, glu(ap_ref[...], gp_ref[...]), 0.0))
    store_z(slice(CONV_HALO + CONV_TS, CONV_TS + 2 * CONV_HALO),
            jnp.where(s < last, glu(an_ref[...], gn_ref[...]), 0.0))

    def glu_chunk(r0):
        rows = pl.ds(r0, ROW_CHUNK)
        store_z(pl.ds(CONV_HALO + r0, ROW_CHUNK), glu(ac_ref[rows, :], gc_ref[rows, :]))
    _row_chunks(CONV_TS, ROW_CHUNK, glu_chunk)

    tap0 = CONV_HALO - CONV_HALF

    def lane_chunk(lc, carry):
        cols = pl.ds(pl.multiple_of(lc * lanes, lanes), lanes)
        taps = [w_ref[j:j + 1, cols] for j in range(CONV_KERNEL)]
        bias = cb_ref[:, cols]
        for r0 in range(0, CONV_TS, CONV_RB):
            acc = jnp.zeros((CONV_RB, lanes), f32)
            for j in range(CONV_KERNEL):
                acc = acc + z_scr[lc, r0 + tap0 + j:r0 + tap0 + j + CONV_RB, :] * taps[j]
            c_scr[r0:r0 + CONV_RB, cols] = acc + bias
        return carry
    lax.fori_loop(0, n_lane_chunks, lane_chunk, 0)

    lg = lg_ref[...]
    lb = lb_ref[...]

    def ln_chunk(r0):
        c = c_scr[pl.ds(r0, ROW_CHUNK), :]
        mu = jnp.mean(c, axis=-1, keepdims=True)
        cc = c - mu
        var = jnp.mean(cc * cc, axis=-1, keepdims=True)
        y = cc * lax.rsqrt(var + EPS) * lg + lb
        o_ref[pl.ds(r0, ROW_CHUNK), :] = (y * jax.nn.sigmoid(y)).astype(bf16)
    _row_chunks(CONV_TS, ROW_CHUNK, ln_chunk)


def _conv(proj, conv_w, conv_b, ln_g, ln_b):
    st = SEQ // CONV_TS
    halo_per_tile = CONV_TS // CONV_HALO
    halo_per_seq = SEQ // CONV_HALO
    a_col = (2 * QK_COLS + ATT_WIDTH) // CONV_CH
    g_col = a_col + 1

    def cur(col):
        return pl.BlockSpec((CONV_TS, CONV_CH), lambda b, s: (b * st + s, col))

    def prev(col):
        return pl.BlockSpec(
            (CONV_HALO, CONV_CH),
            lambda b, s: (jnp.maximum(b * halo_per_seq + s * halo_per_tile - 1, 0), col))

    def nxt(col):
        return pl.BlockSpec(
            (CONV_HALO, CONV_CH),
            lambda b, s: (jnp.minimum(b * halo_per_seq + (s + 1) * halo_per_tile,
                                      TOKENS // CONV_HALO - 1), col))

    vec = pl.BlockSpec((1, CONV_CH), lambda b, s: (0, 0))
    return pl.pallas_call(
        _conv_kernel,
        grid=(BATCH, st),
        in_specs=[prev(a_col), cur(a_col), nxt(a_col), prev(g_col), cur(g_col), nxt(g_col),
                  pl.BlockSpec((CONV_KERNEL, CONV_CH), lambda b, s: (0, 0)), vec, vec, vec],
        out_specs=pl.BlockSpec((CONV_TS, CONV_CH), lambda b, s: (b * st + s, 0)),
        out_shape=jax.ShapeDtypeStruct((TOKENS, CONV_CH), bf16),
        scratch_shapes=[pltpu.VMEM((CONV_CH // 128, CONV_TS + 2 * CONV_HALO, 128), f32),
                        pltpu.VMEM((CONV_TS, CONV_CH), f32)],
        compiler_params=_cparams(("parallel", "arbitrary"), 32),
        name="conv",
    )(proj, proj, proj, proj, proj, proj, conv_w, conv_b.reshape(1, -1), ln_g.reshape(1, -1),
      ln_b.reshape(1, -1))


def _outproj_kernel(att_ref, cv_ref, wo_ref, x_ref, g1_ref, sc2_ref, sh2_ref, n2_ref, wr_ref,
                    br_ref, x1_ref, h2p_ref, ti_ref, tw_ref, rk_ref, cnt_ref, carry_scr, hi_scr,
                    lo_scr):
    @pl.when(pl.program_id(0) == 0)
    def _():
        carry_scr[...] = jnp.zeros_like(carry_scr)

    mixed = (jnp.dot(att_ref[...], wo_ref[0:ATT_WIDTH, :], preferred_element_type=f32)
             + jnp.dot(cv_ref[...], wo_ref[ATT_WIDTH:, :], preferred_element_type=f32))
    x1_ref[...] = x_ref[...] + g1_ref[0] * mixed

    n2 = n2_ref[...]
    one_plus_scale = 1.0 + sc2_ref[0]
    shift = sh2_ref[0]

    def split(v):
        hi = v.astype(bf16)
        return hi, (v - hi.astype(f32)).astype(bf16)

    def chunk(r0):
        rows = pl.ds(r0, ROW_CHUNK)
        x1 = x1_ref[rows, :]
        inv = lax.rsqrt(jnp.mean(x1 * x1, axis=-1, keepdims=True) + EPS)
        for s in range(PACK_ROWS):
            pair = []
            for c in (2 * s, 2 * s + 1):
                cols = slice(c * PACK_LANES, (c + 1) * PACK_LANES)
                y = x1_ref[rows, cols] * inv * n2[:, cols]
                h2 = y * one_plus_scale[:, cols] + shift[:, cols]
                hi_scr[rows, cols], lo_scr[rows, cols] = split(h2)
                pair.append(h2)
            h2p_ref[pl.ds(r0 * PACK_ROWS + s, ROW_CHUNK, stride=PACK_ROWS), :] = _pack_pair(*pair)
    _row_chunks(OUT_TM, ROW_CHUNK, chunk)

    w_hi, w_lo = split(wr_ref[...])
    h_hi, h_lo = hi_scr[...], lo_scr[...]
    rows_x_rows = (((1,), (1,)), ((), ()))
    logits = (lax.dot_general(w_hi, h_hi, rows_x_rows, preferred_element_type=f32)
              + lax.dot_general(w_hi, h_lo, rows_x_rows, preferred_element_type=f32)
              + lax.dot_general(w_lo, h_hi, rows_x_rows, preferred_element_type=f32)
              + br_ref[...])
    eidx = lax.broadcasted_iota(i32, logits.shape, 0)
    vals = logits
    sels, tops = [], []
    for k in range(TOP_K):
        m = jnp.max(vals, axis=0, keepdims=True)
        idx = jnp.min(jnp.where(vals == m, eidx, N_EXPERTS), axis=0, keepdims=True)
        sel = eidx == idx
        ti_ref[k:k + 1, :] = idx
        sels.append(sel)
        tops.append(m)
        vals = jnp.where(sel, -jnp.inf, vals)
    exps = [jnp.exp(v - tops[0]) for v in tops]
    denom = exps[0] + exps[1] + exps[2] + exps[3]
    for k in range(TOP_K):
        tw_ref[k:k + 1, :] = exps[k] / denom

    onehot = jnp.where(sels[0] | sels[1] | sels[2] | sels[3], 1.0, 0.0)
    r_i = lax.broadcasted_iota(i32, (OUT_TM, OUT_TM), 0)
    c_i = lax.broadcasted_iota(i32, (OUT_TM, OUT_TM), 1)
    earlier = jnp.where(r_i < c_i, 1.0, 0.0).astype(bf16)
    before = jnp.dot(onehot.astype(bf16), earlier, preferred_element_type=f32) + carry_scr[...]
    for k in range(TOP_K):
        rk_ref[k:k + 1, :] = jnp.sum(jnp.where(sels[k], before, 0.0), axis=0,
                                     keepdims=True).astype(i32)
    total = carry_scr[...] + jnp.sum(onehot, axis=1, keepdims=True)
    carry_scr[...] = total
    cnt_ref[...] = jnp.broadcast_to(total, cnt_ref.shape).astype(i32)


def _outproj(att, cv, wo_bf, x2d, gate1, scale2, shift2, norm2_g, w_router, b_router):
    tiles_per_batch = SEQ // OUT_TM
    row = lambda w: pl.BlockSpec((OUT_TM, w), lambda i: (i, 0))
    mod = pl.BlockSpec((1, 1, D_MODEL), lambda i: (i // tiles_per_batch, 0, 0))
    slot = pl.BlockSpec((TOP_K, OUT_TM), lambda i: (0, i))
    return pl.pallas_call(
        _outproj_kernel,
        grid=(TOKENS // OUT_TM,),
        in_specs=[row(ATT_WIDTH), row(CONV_CH),
                  pl.BlockSpec((D_MODEL, D_MODEL), lambda i: (0, 0), pipeline_mode=pl.Buffered(1)),
                  row(D_MODEL), mod, mod, mod,
                  pl.BlockSpec((1, D_MODEL), lambda i: (0, 0)),
                  pl.BlockSpec((N_EXPERTS, D_MODEL), lambda i: (0, 0)),
                  pl.BlockSpec((N_EXPERTS, 1), lambda i: (0, 0))],
        out_specs=[row(D_MODEL),
                   pl.BlockSpec((OUT_TM * PACK_ROWS, PACK_LANES), lambda i: (i, 0)),
                   slot, slot, slot,
                   pl.BlockSpec((N_EXPERTS, 128), lambda i: (0, 0))],
        out_shape=[jax.ShapeDtypeStruct((TOKENS, D_MODEL), f32),
                   jax.ShapeDtypeStruct((TOKENS * PACK_ROWS, PACK_LANES), jnp.uint32),
                   jax.ShapeDtypeStruct((TOP_K, TOKENS), i32),
                   jax.ShapeDtypeStruct((TOP_K, TOKENS), f32),
                   jax.ShapeDtypeStruct((TOP_K, TOKENS), i32),
                   jax.ShapeDtypeStruct((N_EXPERTS, 128), i32)],
        scratch_shapes=[pltpu.VMEM((N_EXPERTS, 1), f32), pltpu.VMEM((OUT_TM, D_MODEL), bf16),
                        pltpu.VMEM((OUT_TM, D_MODEL), bf16)],
        compiler_params=_cparams(("arbitrary",), 52),
        name="outproj",
    )(att, cv, wo_bf, x2d, gate1.reshape(BATCH, 1, D_MODEL), scale2.reshape(BATCH, 1, D_MODEL),
      shift2.reshape(BATCH, 1, D_MODEL), norm2_g.reshape(1, D_MODEL), w_router.T,
      b_router.reshape(N_EXPERTS, 1))


def _token_copy(src_hbm, token, dst_buf, row, sem):
    dst = dst_buf.at[pl.ds(pl.multiple_of(row * PACK_ROWS, PACK_ROWS), PACK_ROWS), :]
    return pltpu.make_async_copy(src_hbm.at[token], dst, sem)


_PAD_BITS = tuple(range(MOE_TM.bit_length() - 2, -1, -1))


def _pad_fill_copies(pad_start, pad_len, zeros, xs_hbm, sem):
    pairs = []
    for b in _PAD_BITS:
        size = 1 << b
        off = pad_start + lax.shift_left(lax.shift_right_logical(pad_len, b + 1), b + 1)
        copy = pltpu.make_async_copy(zeros.at[pl.ds(0, size)], xs_hbm.at[pl.ds(off, size)], sem)
        pairs.append((lax.shift_right_logical(pad_len, b) & 1 == 1, copy))
    return pairs


def _dispatch_kernel(pad_start_ref, pad_len_ref, nv_ref, pos_ref, h2p_ref, xs_hbm, zeros, sem, zsem):
    i = pl.program_id(0)
    last = pl.num_programs(0) - 1

    def for_each_fill(act):
        def per_expert(e, carry):
            for cond, copy in _pad_fill_copies(pad_start_ref[e], pad_len_ref[e], zeros, xs_hbm, zsem):
                @pl.when(cond)
                def _(copy=copy):
                    act(copy)
            return carry
        lax.fori_loop(0, N_EXPERTS, per_expert, 0)

        def per_tile(t, carry):
            row0 = pl.multiple_of(t * MOE_TM, MOE_TM)
            act(pltpu.make_async_copy(zeros, xs_hbm.at[pl.ds(row0, MOE_TM)], zsem))
            return carry
        lax.fori_loop(nv_ref[0], MOE_NT, per_tile, 0)

    @pl.when(i == 0)
    def _():
        zeros[...] = jnp.zeros_like(zeros)
        for_each_fill(lambda copy: copy.start())

    def body(g, carry):
        for u in range(DMA_UNROLL // TOP_K):
            r = g * (DMA_UNROLL // TOP_K) + u
            for k in range(TOP_K):
                pltpu.make_async_copy(h2p_ref.at[r], xs_hbm.at[pos_ref[0, k, r]], sem).start(
                    priority=k % DMA_PRIORITIES)
        return carry
    lax.fori_loop(0, DSP_TB * TOP_K // DMA_UNROLL, body, 0)
    pltpu.make_async_copy(xs_hbm.at[pl.ds(0, DSP_TB * TOP_K)], xs_hbm.at[pl.ds(0, DSP_TB * TOP_K)],
                          sem).wait()

    @pl.when(i == last)
    def _():
        for_each_fill(lambda copy: copy.wait())


def _dispatch(pad_start, pad_len, n_valid, pos, h2_packed):
    nt = TOKENS // DSP_TB
    pos3 = pos.reshape(TOP_K, nt, DSP_TB).transpose(1, 0, 2)
    grid_spec = pltpu.PrefetchScalarGridSpec(
        num_scalar_prefetch=3,
        grid=(nt,),
        in_specs=[pl.BlockSpec((1, TOP_K, DSP_TB), lambda i, *_: (i, 0, 0), memory_space=pltpu.SMEM),
                  pl.BlockSpec((DSP_TB, PACK_ROWS, PACK_LANES), lambda i, *_: (i, 0, 0))],
        out_specs=pl.BlockSpec(memory_space=pl.ANY),
        scratch_shapes=[pltpu.VMEM((MOE_TM, PACK_ROWS, PACK_LANES), jnp.uint32),
                        pltpu.SemaphoreType.DMA(()), pltpu.SemaphoreType.DMA(())])
    return pl.pallas_call(
        _dispatch_kernel,
        grid_spec=grid_spec,
        out_shape=jax.ShapeDtypeStruct((MOE_ROWS, PACK_ROWS, PACK_LANES), jnp.uint32),
        compiler_params=_cparams(("arbitrary",), 24),
        name="dispatch",
    )(pad_start, pad_len, n_valid, pos3, h2_packed.reshape(TOKENS, PACK_ROWS, PACK_LANES))


def _weights_changed(te_ref, m):
    prev = te_ref[jnp.maximum(m - 1, 0)]
    return (m == 0) | (te_ref[m] != prev)


def _expert_tile_step(te_ref, nx_ref, nv_ref, n, m, n_passes, count_ref, copies_for, convert,
                      compute, write_zeros):
    first_expert = te_ref[0]
    first = _weights_changed(te_ref, m)
    valid = m < nv_ref[0]

    def start(copies):
        for copy in copies:
            copy.start(priority=WEIGHT_DMA_PRIORITY)

    @pl.when((n == 0) & (m == 0))
    def _():
        count_ref[0] = 0
        start(copies_for(0, first_expert, 0))

    @pl.when(first)
    def _():
        slot = count_ref[0] % 2
        nxt = nx_ref[m]

        @pl.when(nxt >= 0)
        def _():
            start(copies_for(n, nxt, 1 - slot))

        @pl.when((nxt < 0) & (n + 1 < n_passes))
        def _():
            start(copies_for(n + 1, first_expert, 1 - slot))

        for copy in copies_for(0, 0, slot):
            copy.wait()
        convert(slot)
        compute()
        count_ref[0] = count_ref[0] + 1

    @pl.when(jnp.logical_not(first) & valid)
    def _():
        compute()

    @pl.when(jnp.logical_not(valid))
    def _():
        write_zeros()


def _gmm1_kernel(te_ref, nx_ref, nv_ref, xs_ref, w_hbm, bg_ref, bu_ref, o_ref, stage, wg_bf, wu_bf,
                 count_ref, sem):
    n = pl.program_id(0)
    m = pl.program_id(1)

    def copies_for(p, e, slot):
        col = pl.multiple_of(p * GMM1_TN, GMM1_TN)
        return [pltpu.make_async_copy(w_hbm.at[e, :, pl.ds(col, GMM1_TN)], stage.at[slot, 0],
                                      sem.at[slot]),
                pltpu.make_async_copy(w_hbm.at[e, :, pl.ds(D_FF + col, GMM1_TN)],
                                      stage.at[slot, 1], sem.at[slot])]

    def convert(slot):
        wg_bf[...] = stage[slot, 0].astype(bf16)
        wu_bf[...] = stage[slot, 1].astype(bf16)

    def compute():
        parts = []
        for s in range(PACK_ROWS):
            lo, hi = _load_packed(xs_ref, MOE_TM, s)
            parts += [lo.astype(bf16), hi.astype(bf16)]
        x = jnp.concatenate(parts, axis=1)
        gate = jnp.dot(x, wg_bf[...], preferred_element_type=f32) + bg_ref[0]
        up = jnp.dot(x, wu_bf[...], preferred_element_type=f32) + bu_ref[0]
        gate = jnp.minimum(gate, SWIGLU_LIMIT)
        up = jnp.clip(up, -SWIGLU_LIMIT, SWIGLU_LIMIT)
        act = (up + 1.0) * (gate * jax.nn.sigmoid(SWIGLU_ALPHA * gate))
        o_ref[...] = act.astype(bf16)

    def write_zeros():
        o_ref[...] = jnp.zeros_like(o_ref)

    _expert_tile_step(te_ref, nx_ref, nv_ref, n, m, pl.num_programs(0), count_ref, copies_for,
                      convert, compute, write_zeros)


def _gmm1(tile_expert, next_expert, n_valid, xs_packed, w_gate_up, b_gate_up):
    up0 = D_FF // GMM1_TN
    used = lambda m, nv: jnp.minimum(m, nv[0] - 1)
    grid_spec = pltpu.PrefetchScalarGridSpec(
        num_scalar_prefetch=3,
        grid=(D_FF // GMM1_TN, MOE_NT),
        in_specs=[pl.BlockSpec((MOE_TM * PACK_ROWS, PACK_LANES),
                               lambda n, m, te, nx, nv: (used(m, nv), 0)),
                  pl.BlockSpec(memory_space=pl.ANY),
                  pl.BlockSpec((1, 1, GMM1_TN), lambda n, m, te, nx, nv: (te[m], 0, n)),
                  pl.BlockSpec((1, 1, GMM1_TN), lambda n, m, te, nx, nv: (te[m], 0, up0 + n))],
        out_specs=pl.BlockSpec((MOE_TM, GMM1_TN), lambda n, m, te, nx, nv: (m, n)),
        scratch_shapes=[pltpu.VMEM((2, 2, D_MODEL, GMM1_TN), f32),
                        pltpu.VMEM((D_MODEL, GMM1_TN), bf16), pltpu.VMEM((D_MODEL, GMM1_TN), bf16),
                        pltpu.SMEM((1,), i32), pltpu.SemaphoreType.DMA((2,))])
    b3 = b_gate_up.reshape(N_EXPERTS, 1, 2 * D_FF)
    return pl.pallas_call(
        _gmm1_kernel,
        grid_spec=grid_spec,
        out_shape=jax.ShapeDtypeStruct((MOE_ROWS, D_FF), bf16),
        compiler_params=_cparams(("arbitrary", "arbitrary"), 54),
        name="gmm1",
    )(tile_expert, next_expert, n_valid, xs_packed.reshape(MOE_ROWS * PACK_ROWS, PACK_LANES),
      w_gate_up, b3, b3)


def _gmm2_kernel(te_ref, nx_ref, nv_ref, a_ref, w_hbm, bd_ref, o_ref, stage, wd_bf, y_scr,
                 count_ref, sem):
    def copies_for(p, e, slot):
        return [pltpu.make_async_copy(w_hbm.at[e], stage.at[slot], sem.at[slot])]

    def convert(slot):
        wd_bf[...] = stage[slot].astype(bf16)

    def compute():
        y_scr[...] = jnp.dot(a_ref[...], wd_bf[...], preferred_element_type=f32) + bd_ref[0]
        _store_packed(o_ref, 0, MOE_TM, lambda cols: y_scr[:, cols])

    def write_zeros():
        o_ref[...] = jnp.zeros_like(o_ref)

    _expert_tile_step(te_ref, nx_ref, nv_ref, pl.program_id(0), pl.program_id(1),
                      pl.num_programs(0), count_ref, copies_for, convert, compute, write_zeros)


def _gmm2(tile_expert, next_expert, n_valid, act, w_down, b_down):
    assert GMM2_TN == D_MODEL
    grid_spec = pltpu.PrefetchScalarGridSpec(
        num_scalar_prefetch=3,
        grid=(1, MOE_NT),
        in_specs=[pl.BlockSpec((MOE_TM, D_FF), lambda n, m, te, nx, nv: (m, 0)),
                  pl.BlockSpec(memory_space=pl.ANY),
                  pl.BlockSpec((1, 1, GMM2_TN), lambda n, m, te, nx, nv: (te[m], 0, 0))],
        out_specs=pl.BlockSpec((MOE_TM * PACK_ROWS, PACK_LANES), lambda n, m, te, nx, nv: (m, 0)),
        scratch_shapes=[pltpu.VMEM((2, D_FF, GMM2_TN), f32), pltpu.VMEM((D_FF, GMM2_TN), bf16),
                        pltpu.VMEM((MOE_TM, D_MODEL), f32), pltpu.SMEM((1,), i32),
                        pltpu.SemaphoreType.DMA((2,))])
    return pl.pallas_call(
        _gmm2_kernel,
        grid_spec=grid_spec,
        out_shape=jax.ShapeDtypeStruct((MOE_ROWS * PACK_ROWS, PACK_LANES), jnp.uint32),
        compiler_params=_cparams(("arbitrary", "arbitrary"), 56),
        name="gmm2",
    )(tile_expert, next_expert, n_valid, act, w_down, b_down.reshape(N_EXPERTS, 1, D_MODEL))


def _combine_kernel(pos_cur, pos_nxt, y_hbm, x1_ref, g2_ref, tw_ref, fg_ref, o_ref, buf, sem):
    i = pl.program_id(0)
    n = pl.num_programs(0)
    slot = i % 2

    def start_rows(pos_ref, s, rows):
        for r in rows:
            for k in range(TOP_K):
                _token_copy(y_hbm, pos_ref[0, k, r], buf.at[s, k], r, sem.at[s]).start(
                    priority=k % DMA_PRIORITIES)

    @pl.when(i == 0)
    def _():
        def body(g, carry):
            start_rows(pos_cur, 0, [g * (DMA_UNROLL // TOP_K) + u
                                    for u in range(DMA_UNROLL // TOP_K)])
            return carry
        lax.fori_loop(0, CMB_TB * TOP_K // DMA_UNROLL, body, 0)

    pltpu.make_async_copy(buf.at[slot], buf.at[slot], sem.at[slot]).wait()

    def combine_tile(prefetch_next):
        weights = [jnp.broadcast_to(tw_ref[:, k:k + 1], (CMB_TB, PACK_LANES))
                   for k in range(TOP_K)]
        sumsq = jnp.zeros((CMB_TB, 1), f32)
        rows_per_chunk = CMB_TB // PACK_ROWS
        for s in range(PACK_ROWS):
            if prefetch_next:
                start_rows(pos_nxt, 1 - slot, range(s * rows_per_chunk, (s + 1) * rows_per_chunk))
            lo, hi = _load_packed(buf.at[slot, 0], CMB_TB, s)
            lo, hi = lo * weights[0], hi * weights[0]
            for k in range(1, TOP_K):
                lo_k, hi_k = _load_packed(buf.at[slot, k], CMB_TB, s)
                lo, hi = lo + lo_k * weights[k], hi + hi_k * weights[k]
            for c, moe in ((2 * s, lo), (2 * s + 1, hi)):
                cols = slice(c * PACK_LANES, (c + 1) * PACK_LANES)
                x2 = x1_ref[:, cols] + g2_ref[0, :, cols] * moe
                o_ref[:, cols] = x2
                sumsq = sumsq + jnp.sum(x2 * x2, axis=-1, keepdims=True)
        inv = lax.rsqrt(sumsq * (1.0 / D_MODEL) + EPS)
        o_ref[...] = o_ref[...] * inv * fg_ref[...]

    @pl.when(i + 1 < n)
    def _():
        combine_tile(True)

    @pl.when(i + 1 >= n)
    def _():
        combine_tile(False)


def _combine(pos, y_packed, x1, gate2, top_w_rows, final_g):
    nt = TOKENS // CMB_TB
    tiles_per_batch = SEQ // CMB_TB
    pos3 = pos.reshape(TOP_K, nt, CMB_TB).transpose(1, 0, 2)
    return pl.pallas_call(
        _combine_kernel,
        grid=(nt,),
        in_specs=[pl.BlockSpec((1, TOP_K, CMB_TB), lambda i: (i, 0, 0), memory_space=pltpu.SMEM),
                  pl.BlockSpec((1, TOP_K, CMB_TB), lambda i: (jnp.minimum(i + 1, nt - 1), 0, 0),
                               memory_space=pltpu.SMEM),
                  pl.BlockSpec(memory_space=pl.ANY),
                  pl.BlockSpec((CMB_TB, D_MODEL), lambda i: (i, 0)),
                  pl.BlockSpec((1, 1, D_MODEL), lambda i: (i // tiles_per_batch, 0, 0)),
                  pl.BlockSpec((CMB_TB, TOP_K), lambda i: (i, 0)),
                  pl.BlockSpec((1, D_MODEL), lambda i: (0, 0))],
        out_specs=pl.BlockSpec((CMB_TB, D_MODEL), lambda i: (i, 0)),
        out_shape=jax.ShapeDtypeStruct((TOKENS, D_MODEL), f32),
        scratch_shapes=[pltpu.VMEM((2, TOP_K, CMB_TB * PACK_ROWS, PACK_LANES), jnp.uint32),
                        pltpu.SemaphoreType.DMA((2,))],
        compiler_params=_cparams(("arbitrary",), 32),
        name="combine",
    )(pos3, pos3, y_packed.reshape(MOE_ROWS, PACK_ROWS, PACK_LANES), x1,
      gate2.reshape(BATCH, 1, D_MODEL), top_w_rows, final_g.reshape(1, D_MODEL))


def _routing_tables(counts, top_i, rank):
    padded = (counts + MOE_TM - 1) // MOE_TM * MOE_TM
    ends = jnp.cumsum(padded)
    starts = ends - padded
    n_valid = ends[-1] // MOE_TM
    tile_ids = jnp.minimum(jnp.arange(MOE_NT, dtype=i32), n_valid - 1)
    tile_expert = jnp.sum(tile_ids[:, None] >= (ends // MOE_TM)[None, :], axis=1).astype(i32)
    experts = jnp.arange(N_EXPERTS, dtype=i32)
    pos = rank + jnp.sum(jnp.where(top_i[..., None] == experts, starts, 0), axis=-1)
    pad_start = (starts + counts).astype(i32)
    pad_len = (padded - counts).astype(i32)
    later = (experts[None, :] > tile_expert[:, None]) & (counts > 0)[None, :]
    next_expert = jnp.min(jnp.where(later, experts[None, :], N_EXPERTS), axis=1)
    next_expert = jnp.where(next_expert == N_EXPERTS, -1, next_expert).astype(i32)
    return (tile_expert, next_expert, n_valid.reshape(1).astype(i32), pos.astype(i32), pad_start,
            pad_len)


def kernel(x, c, w_ada, b_ada, norm1_g, w_in, lambda_q1, lambda_k1, lambda_q2, lambda_k2, subln_g,
           rel_bias, conv_w, conv_b, conv_ln_g, conv_ln_b, w_out, norm2_g, w_router, b_router,
           w_gate_up, b_gate_up, w_down, b_down, final_g):
    assert x.shape == (BATCH, SEQ, D_MODEL) and w_ada.shape[0] == 1
    l = 0
    x2d = x.reshape(TOKENS, D_MODEL)
    mod = _adaln(c, w_ada[l], b_ada[l])
    shift1, scale1, gate1, shift2, scale2, gate2 = [mod[i] for i in range(6)]

    proj = _inproj(x2d, norm1_g[l], scale1, shift1, w_in[l])
    att = _attention(proj, _bias_band(rel_bias), lambda_q1[l], lambda_k1[l], lambda_q2[l],
                     lambda_k2[l], subln_g[l])
    cv = _conv(proj, conv_w[l], conv_b[l], conv_ln_g[l], conv_ln_b[l])

    x1, h2_packed, top_i, top_w, rank, counts = _outproj(
        att, cv, w_out[l].astype(bf16), x2d, gate1, scale2, shift2, norm2_g[l], w_router[l],
        b_router[l])
    tile_expert, next_expert, n_valid, pos, pad_start, pad_len = _routing_tables(
        counts[:, 0], top_i, rank)

    xs = _dispatch(pad_start, pad_len, n_valid, pos, h2_packed)
    act = _gmm1(tile_expert, next_expert, n_valid, xs, w_gate_up[l], b_gate_up[l])
    y = _gmm2(tile_expert, next_expert, n_valid, act, w_down[l], b_down[l])
    out = _combine(pos, y, x1, gate2, top_w.T, final_g)
    return out.reshape(BATCH, SEQ, D_MODEL)
```

```python
import functools
import math

import jax
import jax.numpy as jnp
from jax import lax
from jax.experimental import pallas as pl
from jax.experimental.pallas import tpu as pltpu

f32 = jnp.float32
bf16 = jnp.bfloat16
i32 = jnp.int32

D_MODEL = 2048
BATCH = 4
SEQ = 2048
TOKENS = BATCH * SEQ
ATT_HEADS = 8
ATT_HALF_DIM = 64
ATT_V_DIM = 128
ATT_WIDTH = ATT_HEADS * ATT_V_DIM
CONV_CH = D_MODEL - ATT_WIDTH
CONV_KERNEL = 31
CONV_HALF = CONV_KERNEL // 2
QK_COLS = ATT_HEADS * 2 * ATT_HALF_DIM
IN_COLS = 2 * QK_COLS + ATT_WIDTH + 2 * CONV_CH
N_BUCKETS = 32
MAX_DISTANCE = 128
N_EXPERTS = 32
TOP_K = 4
D_FF = D_MODEL
SWIGLU_LIMIT = 7.0
SWIGLU_ALPHA = 1.702
EPS = 1e-6
LAMBDA_INIT = 0.8 - 0.6 * math.exp(-0.3 * 0)
LOG2E = math.log2(math.e)
Q_SCALE = LOG2E * ATT_HALF_DIM ** -0.5

ADA_TN = 1024
INP_TM = 1024
INP_TN = 1024
ATT_TQ = 2048
ATT_SUB = 256
ATT_KC = 256
ATT_EXT_CHUNKS = 5
CONV_TS = 512
CONV_HALO = 16
CONV_RB = 32
CONV_LN_ROWS = 128
OUT_TM = 512
MOE_TM = 256
MOE_NT = TOKENS * TOP_K // MOE_TM + N_EXPERTS
MOE_ROWS = MOE_NT * MOE_TM
GMM1_TN = 1024
GMM2_TN = 2048
DSP_TB = 256
CMB_TB = 128
ROW_CHUNK = 64
DMA_UNROLL = 8
DMA_PRIORITIES = 2
WEIGHT_DMA_PRIORITY = 1

_MIB = 1024 * 1024


def _cparams(sem, vmem_mib):
    return pltpu.CompilerParams(dimension_semantics=sem, vmem_limit_bytes=vmem_mib * _MIB)


PACK_ROWS = 8
PACK_LANES = 128
_HI_MASK = 0xFFFF0000


def _pack_pair(lo, hi):
    lo_b = lax.shift_right_logical(
        lax.bitcast_convert_type(lo.astype(bf16).astype(f32), jnp.uint32), jnp.uint32(16))
    hi_b = lax.bitcast_convert_type(hi.astype(bf16).astype(f32), jnp.uint32) & jnp.uint32(_HI_MASK)
    return lo_b | hi_b


def _unpack_pair(w):
    lo = lax.bitcast_convert_type(lax.shift_left(w, jnp.uint32(16)), f32)
    hi = lax.bitcast_convert_type(w & jnp.uint32(_HI_MASK), f32)
    return lo, hi


def _store_packed(dst_ref, token0, n_tokens, get_cols):
    for s in range(PACK_ROWS):
        lo = get_cols(slice(2 * s * PACK_LANES, (2 * s + 1) * PACK_LANES))
        hi = get_cols(slice((2 * s + 1) * PACK_LANES, (2 * s + 2) * PACK_LANES))
        dst_ref[pl.ds(token0 * PACK_ROWS + s, n_tokens, stride=PACK_ROWS), :] = _pack_pair(lo, hi)


def _load_packed(src_ref, n_tokens, s):
    return _unpack_pair(src_ref[pl.ds(s, n_tokens, stride=PACK_ROWS), :])


def _row_chunks(n_rows, chunk, body):
    def step(i, carry):
        body(pl.multiple_of(i * chunk, chunk))
        return carry
    lax.fori_loop(0, n_rows // chunk, step, 0)


def _adaln_kernel(c_ref, w_ref, b_ref, o_ref):
    c = c_ref[...]
    cs = c * jax.nn.sigmoid(c)
    o_ref[0] = jnp.dot(cs.astype(bf16), w_ref[...].astype(bf16),
                       preferred_element_type=f32) + b_ref[...]


def _adaln(c, w_ada, b_ada):
    n = w_ada.shape[1]
    per_vec = D_MODEL // ADA_TN
    return pl.pallas_call(
        _adaln_kernel,
        grid=(n // ADA_TN,),
        in_specs=[pl.BlockSpec((BATCH, D_MODEL), lambda j: (0, 0)),
                  pl.BlockSpec((D_MODEL, ADA_TN), lambda j: (0, j)),
                  pl.BlockSpec((1, ADA_TN), lambda j: (0, j))],
        out_specs=pl.BlockSpec((1, BATCH, ADA_TN), lambda j: (j // per_vec, 0, j % per_vec)),
        out_shape=jax.ShapeDtypeStruct((n // D_MODEL, BATCH, D_MODEL), f32),
        compiler_params=_cparams(("arbitrary",), 40),
        name="adaln",
    )(c, w_ada, b_ada.reshape(1, n))


def _inproj_kernel(x_ref, g_ref, sc_ref, sh_ref, w_ref, o_ref, h_scr):
    @pl.when(pl.program_id(1) == 0)
    def _():
        g = g_ref[...]
        one_plus_scale = 1.0 + sc_ref[0]
        shift = sh_ref[0]

        def chunk(r0):
            rows = pl.ds(r0, ROW_CHUNK)
            x = x_ref[rows, :]
            inv = lax.rsqrt(jnp.mean(x * x, axis=-1, keepdims=True) + EPS)
            for c in range(D_MODEL // 128):
                cols = slice(c * 128, (c + 1) * 128)
                y = x_ref[rows, cols] * inv * g[:, cols]
                h_scr[rows, cols] = (y * one_plus_scale[:, cols] + shift[:, cols]).astype(bf16)
        _row_chunks(INP_TM, ROW_CHUNK, chunk)

    out_scale = jnp.where(pl.program_id(1) < QK_COLS // INP_TN, Q_SCALE, 1.0).astype(f32)
    acc = jnp.dot(h_scr[...], w_ref[...].astype(bf16), preferred_element_type=f32)
    o_ref[...] = (acc * out_scale).astype(bf16)


def _inproj(x2d, norm_g, scale, shift, w_in):
    assert QK_COLS % INP_TN == 0
    tiles_per_batch = SEQ // INP_TM
    return pl.pallas_call(
        _inproj_kernel,
        grid=(TOKENS // INP_TM, IN_COLS // INP_TN),
        in_specs=[pl.BlockSpec((INP_TM, D_MODEL), lambda i, j: (i, 0)),
                  pl.BlockSpec((1, D_MODEL), lambda i, j: (0, 0)),
                  pl.BlockSpec((1, 1, D_MODEL), lambda i, j: (i // tiles_per_batch, 0, 0)),
                  pl.BlockSpec((1, 1, D_MODEL), lambda i, j: (i // tiles_per_batch, 0, 0)),
                  pl.BlockSpec((D_MODEL, INP_TN), lambda i, j: (0, j))],
        out_specs=pl.BlockSpec((INP_TM, INP_TN), lambda i, j: (i, j)),
        out_shape=jax.ShapeDtypeStruct((TOKENS, IN_COLS), bf16),
        scratch_shapes=[pltpu.VMEM((INP_TM, D_MODEL), bf16)],
        compiler_params=_cparams(("parallel", "arbitrary"), 48),
        name="inproj",
    )(x2d, norm_g.reshape(1, D_MODEL), scale.reshape(BATCH, 1, D_MODEL),
      shift.reshape(BATCH, 1, D_MODEL), w_in)


def _t5_bucket(rel):
    half = N_BUCKETS // 2
    max_exact = half // 2
    ret = jnp.where(rel > 0, half, 0)
    n = jnp.abs(rel)
    nf = jnp.maximum(n, 1).astype(f32)
    large = max_exact + (jnp.log(nf / max_exact) / math.log(MAX_DISTANCE / max_exact)
                         * (half - max_exact)).astype(i32)
    large = jnp.minimum(large, half - 1)
    return ret + jnp.where(n < max_exact, n, large)


def _bias_band(rel_bias):
    assert ATT_SUB == ATT_KC and ATT_EXT_CHUNKS == 5 and ATT_KC >= MAX_DISTANCE
    r = jnp.arange(ATT_SUB)[:, None]
    j = jnp.arange(ATT_EXT_CHUNKS * ATT_KC)[None, :]
    rel = j - 2 * ATT_KC - r
    onehot = (_t5_bucket(rel)[..., None] == jnp.arange(N_BUCKETS)).astype(f32)
    return jnp.einsum("rjn,nh->hrj", onehot, rel_bias.astype(f32) * LOG2E,
                      precision=lax.Precision.HIGHEST)


def _attn_kernel(q_ref, k_ref, v_ref, band_ref, lq1_ref, lk1_ref, lq2_ref, lk2_ref, sg_ref,
                 o_ref, va_scr):
    t = pl.program_id(2)
    n_chunks = SEQ // ATT_KC

    @pl.when(t == 0)
    def _():
        va_scr[:, :ATT_V_DIM] = v_ref[...]
        va_scr[:, ATT_V_DIM:] = jnp.ones((SEQ, ATT_V_DIM), bf16)

    lam = (jnp.exp(jnp.sum(lq1_ref[...] * lk1_ref[...], axis=-1, keepdims=True))
           - jnp.exp(jnp.sum(lq2_ref[...] * lk2_ref[...], axis=-1, keepdims=True))
           + LAMBDA_INIT)
    def band_offset(rows, c):
        block = t * (ATT_TQ // ATT_SUB) + rows.start // ATT_SUB
        d = jnp.clip(c - block + 2, 0, ATT_EXT_CHUNKS - 1)
        return pl.multiple_of(d * ATT_KC, ATT_KC)

    def scores(qm, rows):
        chunks = []
        m = jnp.full((ATT_SUB, 1), -jnp.inf, f32)
        for c in range(n_chunks):
            s = lax.dot_general(qm, k_ref[c * ATT_KC:(c + 1) * ATT_KC, :],
                                (((1,), (1,)), ((), ())), preferred_element_type=f32)
            s = s + band_ref[0, :, pl.ds(band_offset(rows, c), ATT_KC)]
            chunks.append(s)
            m = jnp.maximum(m, jnp.max(s, axis=-1, keepdims=True))
        return chunks, m

    def weighted_values(chunks, m):
        o = jnp.zeros((ATT_SUB, 2 * ATT_V_DIM), f32)
        for c in range(n_chunks):
            p = jnp.exp2(chunks[c] - m)
            o = o + jnp.dot(p.astype(bf16), va_scr[c * ATT_KC:(c + 1) * ATT_KC, :],
                            preferred_element_type=f32)
        return o[:, :ATT_V_DIM] / o[:, ATT_V_DIM:]

    def finish(rows, o1, o2):
        out = o1 - lam * o2
        ms = jnp.mean(out * out, axis=-1, keepdims=True)
        y = out * lax.rsqrt(ms + EPS) * sg_ref[...]
        o_ref[rows, :] = (y * (1.0 - LAMBDA_INIT)).astype(bf16)

    units = []
    for r in range(ATT_TQ // ATT_SUB):
        rows = slice(r * ATT_SUB, (r + 1) * ATT_SUB)
        q = q_ref[rows, :]
        lane = lax.broadcasted_iota(i32, q.shape, 1)
        zero = jnp.zeros_like(q)
        units.append((rows, jnp.where(lane < ATT_HALF_DIM, q, zero)))
        units.append((rows, jnp.where(lane >= ATT_HALF_DIM, q, zero)))
    first_map = {}
    pending = None
    for unit in units + [None]:
        current = (unit[0],) + scores(unit[1], unit[0]) if unit is not None else None
        if pending is not None:
            p_rows, p_chunks, p_m = pending
            o = weighted_values(p_chunks, p_m)
            if p_rows.start in first_map:
                finish(p_rows, first_map.pop(p_rows.start), o)
            else:
                first_map[p_rows.start] = o
        pending = current


def _attention(proj, band, lq1, lk1, lq2, lk2, subln_g):
    qt = SEQ // ATT_TQ
    k_col0 = QK_COLS // ATT_V_DIM
    v_col0 = 2 * QK_COLS // ATT_V_DIM
    vec = lambda n: pl.BlockSpec((1, n), lambda b, h, t: (0, 0))
    return pl.pallas_call(
        _attn_kernel,
        grid=(BATCH, ATT_HEADS, qt),
        in_specs=[pl.BlockSpec((ATT_TQ, ATT_V_DIM), lambda b, h, t: (b * qt + t, h)),
                  pl.BlockSpec((SEQ, ATT_V_DIM), lambda b, h, t: (b, k_col0 + h)),
                  pl.BlockSpec((SEQ, ATT_V_DIM), lambda b, h, t: (b, v_col0 + h)),
                  pl.BlockSpec((1, ATT_SUB, ATT_EXT_CHUNKS * ATT_KC), lambda b, h, t: (h, 0, 0)),
                  vec(ATT_HALF_DIM), vec(ATT_HALF_DIM), vec(ATT_HALF_DIM), vec(ATT_HALF_DIM),
                  vec(ATT_V_DIM)],
        out_specs=pl.BlockSpec((ATT_TQ, ATT_V_DIM), lambda b, h, t: (b * qt + t, h)),
        out_shape=jax.ShapeDtypeStruct((TOKENS, ATT_WIDTH), bf16),
        scratch_shapes=[pltpu.VMEM((SEQ, 2 * ATT_V_DIM), bf16)],
        compiler_params=_cparams(("parallel", "parallel", "arbitrary"), 40),
        name="attn",
    )(proj, proj, proj, band, lq1.reshape(1, -1), lk1.reshape(1, -1), lq2.reshape(1, -1),
      lk2.reshape(1, -1), subln_g.reshape(1, -1))


def _conv_kernel(ap_ref, ac_ref, an_ref, gp_ref, gc_ref, gn_ref, w_ref, cb_ref, lg_ref, lb_ref,
                 o_ref, z_scr, c_scr):
    s = pl.program_id(1)
    last = pl.num_programs(1) - 1

    lanes = 128
    n_lane_chunks = CONV_CH // lanes

    def store_z(rows, z):
        for lc in range(n_lane_chunks):
            z_scr[lc, rows, :] = z[:, lc * lanes:(lc + 1) * lanes]

    def glu(a, g):
        return a.astype(f32) * jax.nn.sigmoid(g.astype(f32))

    store_z(slice(0, CONV_HALO), jnp.where(s > 0, glu(ap_ref[...], gp_ref[...]), 0.0))
    store_z(slice(CONV_HALO + CONV_TS, CONV_TS + 2 * CONV_HALO),
            jnp.where(s < last, glu(an_ref[...], gn_ref[...]), 0.0))

    def glu_chunk(r0):
        rows = pl.ds(r0, ROW_CHUNK)
        store_z(pl.ds(CONV_HALO + r0, ROW_CHUNK), glu(ac_ref[rows, :], gc_ref[rows, :]))
    _row_chunks(CONV_TS, ROW_CHUNK, glu_chunk)

    tap0 = CONV_HALO - CONV_HALF

    def lane_chunk(lc, carry):
        cols = pl.ds(pl.multiple_of(lc * lanes, lanes), lanes)
        taps = [w_ref[j:j + 1, cols] for j in range(CONV_KERNEL)]
        bias = cb_ref[:, cols]
        for r0 in range(0, CONV_TS, CONV_RB):
            acc = jnp.zeros((CONV_RB, lanes), f32)
            for j in range(CONV_KERNEL):
                acc = acc + z_scr[lc, r0 + tap0 + j:r0 + tap0 + j + CONV_RB, :] * taps[j]
            c_scr[r0:r0 + CONV_RB, cols] = acc + bias
        return carry
    lax.fori_loop(0, n_lane_chunks, lane_chunk, 0)

    lg = lg_ref[...]
    lb = lb_ref[...]

    def ln_chunk(r0):
        c = c_scr[pl.ds(r0, CONV_LN_ROWS), :]
        mu = jnp.mean(c, axis=-1, keepdims=True)
        cc = c - mu
        var = jnp.mean(cc * cc, axis=-1, keepdims=True)
        y = cc * lax.rsqrt(var + EPS) * lg + lb
        o_ref[pl.ds(r0, CONV_LN_ROWS), :] = (y * jax.nn.sigmoid(y)).astype(bf16)
    _row_chunks(CONV_TS, CONV_LN_ROWS, ln_chunk)


def _conv(proj, conv_w, conv_b, ln_g, ln_b):
    st = SEQ // CONV_TS
    halo_per_tile = CONV_TS // CONV_HALO
    halo_per_seq = SEQ // CONV_HALO
    a_col = (2 * QK_COLS + ATT_WIDTH) // CONV_CH
    g_col = a_col + 1

    def cur(col):
        return pl.BlockSpec((CONV_TS, CONV_CH), lambda b, s: (b * st + s, col))

    def prev(col):
        return pl.BlockSpec(
            (CONV_HALO, CONV_CH),
            lambda b, s: (jnp.maximum(b * halo_per_seq + s * halo_per_tile - 1, 0), col))

    def nxt(col):
        return pl.BlockSpec(
            (CONV_HALO, CONV_CH),
            lambda b, s: (jnp.minimum(b * halo_per_seq + (s + 1) * halo_per_tile,
                                      TOKENS // CONV_HALO - 1), col))

    vec = pl.BlockSpec((1, CONV_CH), lambda b, s: (0, 0))
    return pl.pallas_call(
        _conv_kernel,
        grid=(BATCH, st),
        in_specs=[prev(a_col), cur(a_col), nxt(a_col), prev(g_col), cur(g_col), nxt(g_col),
                  pl.BlockSpec((CONV_KERNEL, CONV_CH), lambda b, s: (0, 0)), vec, vec, vec],
        out_specs=pl.BlockSpec((CONV_TS, CONV_CH), lambda b, s: (b * st + s, 0)),
        out_shape=jax.ShapeDtypeStruct((TOKENS, CONV_CH), bf16),
        scratch_shapes=[pltpu.VMEM((CONV_CH // 128, CONV_TS + 2 * CONV_HALO, 128), f32),
                        pltpu.VMEM((CONV_TS, CONV_CH), f32)],
        compiler_params=_cparams(("parallel", "arbitrary"), 32),
        name="conv",
    )(proj, proj, proj, proj, proj, proj, conv_w, conv_b.reshape(1, -1), ln_g.reshape(1, -1),
      ln_b.reshape(1, -1))


def _outproj_kernel(att_ref, cv_ref, wo_ref, x_ref, g1_ref, sc2_ref, sh2_ref, n2_ref, wr_ref,
                    br_ref, x1_ref, h2p_ref, ti_ref, tw_ref, rk_ref, cnt_ref, carry_scr, hi_scr,
                    lo_scr):
    @pl.when(pl.program_id(0) == 0)
    def _():
        carry_scr[...] = jnp.zeros_like(carry_scr)

    mixed = (jnp.dot(att_ref[...], wo_ref[0:ATT_WIDTH, :], preferred_element_type=f32)
             + jnp.dot(cv_ref[...], wo_ref[ATT_WIDTH:, :], preferred_element_type=f32))
    x1_ref[...] = x_ref[...] + g1_ref[0] * mixed

    n2 = n2_ref[...]
    one_plus_scale = 1.0 + sc2_ref[0]
    shift = sh2_ref[0]

    def split(v):
        hi = v.astype(bf16)
        return hi, (v - hi.astype(f32)).astype(bf16)

    def chunk(r0):
        rows = pl.ds(r0, ROW_CHUNK)
        x1 = x1_ref[rows, :]
        inv = lax.rsqrt(jnp.mean(x1 * x1, axis=-1, keepdims=True) + EPS)
        for s in range(PACK_ROWS):
            pair = []
            for c in (2 * s, 2 * s + 1):
                cols = slice(c * PACK_LANES, (c + 1) * PACK_LANES)
                y = x1_ref[rows, cols] * inv * n2[:, cols]
                h2 = y * one_plus_scale[:, cols] + shift[:, cols]
                hi_scr[rows, cols], lo_scr[rows, cols] = split(h2)
                pair.append(h2)
            h2p_ref[pl.ds(r0 * PACK_ROWS + s, ROW_CHUNK, stride=PACK_ROWS), :] = _pack_pair(*pair)
    _row_chunks(OUT_TM, ROW_CHUNK, chunk)

    w_hi, w_lo = split(wr_ref[...])
    h_hi, h_lo = hi_scr[...], lo_scr[...]
    rows_x_rows = (((1,), (1,)), ((), ()))
    logits = (lax.dot_general(w_hi, h_hi, rows_x_rows, preferred_element_type=f32)
              + lax.dot_general(w_hi, h_lo, rows_x_rows, preferred_element_type=f32)
              + lax.dot_general(w_lo, h_hi, rows_x_rows, preferred_element_type=f32)
              + br_ref[...])
    eidx = lax.broadcasted_iota(i32, logits.shape, 0)
    vals = logits
    sels, tops = [], []
    for k in range(TOP_K):
        m = jnp.max(vals, axis=0, keepdims=True)
        idx = jnp.min(jnp.where(vals == m, eidx, N_EXPERTS), axis=0, keepdims=True)
        sel = eidx == idx
        ti_ref[k:k + 1, :] = idx
        sels.append(sel)
        tops.append(m)
        vals = jnp.where(sel, -jnp.inf, vals)
    exps = [jnp.exp(v - tops[0]) for v in tops]
    denom = exps[0] + exps[1] + exps[2] + exps[3]
    for k in range(TOP_K):
        tw_ref[k:k + 1, :] = exps[k] / denom

    onehot = jnp.where(sels[0] | sels[1] | sels[2] | sels[3], 1.0, 0.0)
    r_i = lax.broadcasted_iota(i32, (OUT_TM, OUT_TM), 0)
    c_i = lax.broadcasted_iota(i32, (OUT_TM, OUT_TM), 1)
    earlier = jnp.where(r_i < c_i, 1.0, 0.0).astype(bf16)
    before = jnp.dot(onehot.astype(bf16), earlier, preferred_element_type=f32) + carry_scr[...]
    for k in range(TOP_K):
        rk_ref[k:k + 1, :] = jnp.sum(jnp.where(sels[k], before, 0.0), axis=0,
                                     keepdims=True).astype(i32)
    total = carry_scr[...] + jnp.sum(onehot, axis=1, keepdims=True)
    carry_scr[...] = total
    cnt_ref[...] = jnp.broadcast_to(total, cnt_ref.shape).astype(i32)


def _outproj(att, cv, wo_bf, x2d, gate1, scale2, shift2, norm2_g, w_router, b_router):
    tiles_per_batch = SEQ // OUT_TM
    row = lambda w: pl.BlockSpec((OUT_TM, w), lambda i: (i, 0))
    mod = pl.BlockSpec((1, 1, D_MODEL), lambda i: (i // tiles_per_batch, 0, 0))
    slot = pl.BlockSpec((TOP_K, OUT_TM), lambda i: (0, i))
    return pl.pallas_call(
        _outproj_kernel,
        grid=(TOKENS // OUT_TM,),
        in_specs=[row(ATT_WIDTH), row(CONV_CH),
                  pl.BlockSpec((D_MODEL, D_MODEL), lambda i: (0, 0), pipeline_mode=pl.Buffered(1)),
                  row(D_MODEL), mod, mod, mod,
                  pl.BlockSpec((1, D_MODEL), lambda i: (0, 0)),
                  pl.BlockSpec((N_EXPERTS, D_MODEL), lambda i: (0, 0)),
                  pl.BlockSpec((N_EXPERTS, 1), lambda i: (0, 0))],
        out_specs=[row(D_MODEL),
                   pl.BlockSpec((OUT_TM * PACK_ROWS, PACK_LANES), lambda i: (i, 0)),
                   slot, slot, slot,
                   pl.BlockSpec((N_EXPERTS, 128), lambda i: (0, 0))],
        out_shape=[jax.ShapeDtypeStruct((TOKENS, D_MODEL), f32),
                   jax.ShapeDtypeStruct((TOKENS * PACK_ROWS, PACK_LANES), jnp.uint32),
                   jax.ShapeDtypeStruct((TOP_K, TOKENS), i32),
                   jax.ShapeDtypeStruct((TOP_K, TOKENS), f32),
                   jax.ShapeDtypeStruct((TOP_K, TOKENS), i32),
                   jax.ShapeDtypeStruct((N_EXPERTS, 128), i32)],
        scratch_shapes=[pltpu.VMEM((N_EXPERTS, 1), f32), pltpu.VMEM((OUT_TM, D_MODEL), bf16),
                        pltpu.VMEM((OUT_TM, D_MODEL), bf16)],
        compiler_params=_cparams(("arbitrary",), 52),
        name="outproj",
    )(att, cv, wo_bf, x2d, gate1.reshape(BATCH, 1, D_MODEL), scale2.reshape(BATCH, 1, D_MODEL),
      shift2.reshape(BATCH, 1, D_MODEL), norm2_g.reshape(1, D_MODEL), w_router.T,
      b_router.reshape(N_EXPERTS, 1))


def _token_copy(src_hbm, token, dst_buf, row, sem):
    dst = dst_buf.at[pl.ds(pl.multiple_of(row * PACK_ROWS, PACK_ROWS), PACK_ROWS), :]
    return pltpu.make_async_copy(src_hbm.at[token], dst, sem)


_PAD_BITS = tuple(range(MOE_TM.bit_length() - 2, -1, -1))


def _pad_fill_copies(pad_start, pad_len, zeros, xs_hbm, sem):
    pairs = []
    for b in _PAD_BITS:
        size = 1 << b
        off = pad_start + lax.shift_left(lax.shift_right_logical(pad_len, b + 1), b + 1)
        copy = pltpu.make_async_copy(zeros.at[pl.ds(0, size)], xs_hbm.at[pl.ds(off, size)], sem)
        pairs.append((lax.shift_right_logical(pad_len, b) & 1 == 1, copy))
    return pairs


def _dispatch_kernel(pad_start_ref, pad_len_ref, nv_ref, pos_ref, h2p_ref, xs_hbm, zeros, sem, zsem):
    i = pl.program_id(0)
    last = pl.num_programs(0) - 1

    def for_each_fill(act):
        def per_expert(e, carry):
            for cond, copy in _pad_fill_copies(pad_start_ref[e], pad_len_ref[e], zeros, xs_hbm, zsem):
                @pl.when(cond)
                def _(copy=copy):
                    act(copy)
            return carry
        lax.fori_loop(0, N_EXPERTS, per_expert, 0)

        def per_tile(t, carry):
            row0 = pl.multiple_of(t * MOE_TM, MOE_TM)
            act(pltpu.make_async_copy(zeros, xs_hbm.at[pl.ds(row0, MOE_TM)], zsem))
            return carry
        lax.fori_loop(nv_ref[0], MOE_NT, per_tile, 0)

    @pl.when(i == 0)
    def _():
        zeros[...] = jnp.zeros_like(zeros)
        for_each_fill(lambda copy: copy.start())

    def body(g, carry):
        for u in range(DMA_UNROLL // TOP_K):
            r = g * (DMA_UNROLL // TOP_K) + u
            for k in range(TOP_K):
                pltpu.make_async_copy(h2p_ref.at[r], xs_hbm.at[pos_ref[0, k, r]], sem).start(
                    priority=k % DMA_PRIORITIES)
        return carry
    lax.fori_loop(0, DSP_TB * TOP_K // DMA_UNROLL, body, 0)
    pltpu.make_async_copy(xs_hbm.at[pl.ds(0, DSP_TB * TOP_K)], xs_hbm.at[pl.ds(0, DSP_TB * TOP_K)],
                          sem).wait()

    @pl.when(i == last)
    def _():
        for_each_fill(lambda copy: copy.wait())


def _dispatch(pad_start, pad_len, n_valid, pos, h2_packed):
    nt = TOKENS // DSP_TB
    pos3 = pos.reshape(TOP_K, nt, DSP_TB).transpose(1, 0, 2)
    grid_spec = pltpu.PrefetchScalarGridSpec(
        num_scalar_prefetch=3,
        grid=(nt,),
        in_specs=[pl.BlockSpec((1, TOP_K, DSP_TB), lambda i, *_: (i, 0, 0), memory_space=pltpu.SMEM),
                  pl.BlockSpec((DSP_TB, PACK_ROWS, PACK_LANES), lambda i, *_: (i, 0, 0))],
        out_specs=pl.BlockSpec(memory_space=pl.ANY),
        scratch_shapes=[pltpu.VMEM((MOE_TM, PACK_ROWS, PACK_LANES), jnp.uint32),
                        pltpu.SemaphoreType.DMA(()), pltpu.SemaphoreType.DMA(())])
    return pl.pallas_call(
        _dispatch_kernel,
        grid_spec=grid_spec,
        out_shape=jax.ShapeDtypeStruct((MOE_ROWS, PACK_ROWS, PACK_LANES), jnp.uint32),
        compiler_params=_cparams(("arbitrary",), 24),
        name="dispatch",
    )(pad_start, pad_len, n_valid, pos3, h2_packed.reshape(TOKENS, PACK_ROWS, PACK_LANES))


def _weights_changed(te_ref, m):
    prev = te_ref[jnp.maximum(m - 1, 0)]
    return (m == 0) | (te_ref[m] != prev)


def _expert_tile_step(te_ref, nx_ref, nv_ref, n, m, n_passes, count_ref, copies_for, convert,
                      compute, write_zeros):
    first_expert = te_ref[0]
    first = _weights_changed(te_ref, m)
    valid = m < nv_ref[0]

    def start(copies):
        for copy in copies:
            copy.start(priority=WEIGHT_DMA_PRIORITY)

    @pl.when((n == 0) & (m == 0))
    def _():
        count_ref[0] = 0
        start(copies_for(0, first_expert, 0))

    @pl.when(first)
    def _():
        slot = count_ref[0] % 2
        nxt = nx_ref[m]

        @pl.when(nxt >= 0)
        def _():
            start(copies_for(n, nxt, 1 - slot))

        @pl.when((nxt < 0) & (n + 1 < n_passes))
        def _():
            start(copies_for(n + 1, first_expert, 1 - slot))

        for copy in copies_for(0, 0, slot):
            copy.wait()
        convert(slot)
        compute()
        count_ref[0] = count_ref[0] + 1

    @pl.when(jnp.logical_not(first) & valid)
    def _():
        compute()

    @pl.when(jnp.logical_not(valid))
    def _():
        write_zeros()


def _gmm1_kernel(te_ref, nx_ref, nv_ref, xs_ref, w_hbm, bg_ref, bu_ref, o_ref, stage, wg_bf, wu_bf,
                 count_ref, sem):
    n = pl.program_id(0)
    m = pl.program_id(1)

    def copies_for(p, e, slot):
        col = pl.multiple_of(p * GMM1_TN, GMM1_TN)
        return [pltpu.make_async_copy(w_hbm.at[e, :, pl.ds(col, GMM1_TN)], stage.at[slot, 0],
                                      sem.at[slot]),
                pltpu.make_async_copy(w_hbm.at[e, :, pl.ds(D_FF + col, GMM1_TN)],
                                      stage.at[slot, 1], sem.at[slot])]

    def convert(slot):
        wg_bf[...] = stage[slot, 0].astype(bf16)
        wu_bf[...] = stage[slot, 1].astype(bf16)

    def compute():
        parts = []
        for s in range(PACK_ROWS):
            lo, hi = _load_packed(xs_ref, MOE_TM, s)
            parts += [lo.astype(bf16), hi.astype(bf16)]
        x = jnp.concatenate(parts, axis=1)
        gate = jnp.dot(x, wg_bf[...], preferred_element_type=f32) + bg_ref[0]
        up = jnp.dot(x, wu_bf[...], preferred_element_type=f32) + bu_ref[0]
        gate = jnp.minimum(gate, SWIGLU_LIMIT)
        up = jnp.clip(up, -SWIGLU_LIMIT, SWIGLU_LIMIT)
        act = (up + 1.0) * (gate * jax.nn.sigmoid(SWIGLU_ALPHA * gate))
        o_ref[...] = act.astype(bf16)

    def write_zeros():
        o_ref[...] = jnp.zeros_like(o_ref)

    _expert_tile_step(te_ref, nx_ref, nv_ref, n, m, pl.num_programs(0), count_ref, copies_for,
                      convert, compute, write_zeros)


def _gmm1(tile_expert, next_expert, n_valid, xs_packed, w_gate_up, b_gate_up):
    up0 = D_FF // GMM1_TN
    used = lambda m, nv: jnp.minimum(m, nv[0] - 1)
    grid_spec = pltpu.PrefetchScalarGridSpec(
        num_scalar_prefetch=3,
        grid=(D_FF // GMM1_TN, MOE_NT),
        in_specs=[pl.BlockSpec((MOE_TM * PACK_ROWS, PACK_LANES),
                               lambda n, m, te, nx, nv: (used(m, nv), 0)),
                  pl.BlockSpec(memory_space=pl.ANY),
                  pl.BlockSpec((1, 1, GMM1_TN), lambda n, m, te, nx, nv: (te[m], 0, n)),
                  pl.BlockSpec((1, 1, GMM1_TN), lambda n, m, te, nx, nv: (te[m], 0, up0 + n))],
        out_specs=pl.BlockSpec((MOE_TM, GMM1_TN), lambda n, m, te, nx, nv: (m, n)),
        scratch_shapes=[pltpu.VMEM((2, 2, D_MODEL, GMM1_TN), f32),
                        pltpu.VMEM((D_MODEL, GMM1_TN), bf16), pltpu.VMEM((D_MODEL, GMM1_TN), bf16),
                        pltpu.SMEM((1,), i32), pltpu.SemaphoreType.DMA((2,))])
    b3 = b_gate_up.reshape(N_EXPERTS, 1, 2 * D_FF)
    return pl.pallas_call(
        _gmm1_kernel,
        grid_spec=grid_spec,
        out_shape=jax.ShapeDtypeStruct((MOE_ROWS, D_FF), bf16),
        compiler_params=_cparams(("arbitrary", "arbitrary"), 54),
        name="gmm1",
    )(tile_expert, next_expert, n_valid, xs_packed.reshape(MOE_ROWS * PACK_ROWS, PACK_LANES),
      w_gate_up, b3, b3)


def _gmm2_kernel(te_ref, nx_ref, nv_ref, a_ref, w_hbm, bd_ref, o_ref, stage, wd_bf, y_scr,
                 count_ref, sem):
    def copies_for(p, e, slot):
        return [pltpu.make_async_copy(w_hbm.at[e], stage.at[slot], sem.at[slot])]

    def convert(slot):
        wd_bf[...] = stage[slot].astype(bf16)

    def compute():
        y_scr[...] = jnp.dot(a_ref[...], wd_bf[...], preferred_element_type=f32) + bd_ref[0]
        _store_packed(o_ref, 0, MOE_TM, lambda cols: y_scr[:, cols])

    def write_zeros():
        o_ref[...] = jnp.zeros_like(o_ref)

    _expert_tile_step(te_ref, nx_ref, nv_ref, pl.program_id(0), pl.program_id(1),
                      pl.num_programs(0), count_ref, copies_for, convert, compute, write_zeros)


def _gmm2(tile_expert, next_expert, n_valid, act, w_down, b_down):
    assert GMM2_TN == D_MODEL
    grid_spec = pltpu.PrefetchScalarGridSpec(
        num_scalar_prefetch=3,
        grid=(1, MOE_NT),
        in_specs=[pl.BlockSpec((MOE_TM, D_FF), lambda n, m, te, nx, nv: (m, 0)),
                  pl.BlockSpec(memory_space=pl.ANY),
                  pl.BlockSpec((1, 1, GMM2_TN), lambda n, m, te, nx, nv: (te[m], 0, 0))],
        out_specs=pl.BlockSpec((MOE_TM * PACK_ROWS, PACK_LANES), lambda n, m, te, nx, nv: (m, 0)),
        scratch_shapes=[pltpu.VMEM((2, D_FF, GMM2_TN), f32), pltpu.VMEM((D_FF, GMM2_TN), bf16),
                        pltpu.VMEM((MOE_TM, D_MODEL), f32), pltpu.SMEM((1,), i32),
                        pltpu.SemaphoreType.DMA((2,))])
    return pl.pallas_call(
        _gmm2_kernel,
        grid_spec=grid_spec,
        out_shape=jax.ShapeDtypeStruct((MOE_ROWS * PACK_ROWS, PACK_LANES), jnp.uint32),
        compiler_params=_cparams(("arbitrary", "arbitrary"), 56),
        name="gmm2",
    )(tile_expert, next_expert, n_valid, act, w_down, b_down.reshape(N_EXPERTS, 1, D_MODEL))


def _combine_kernel(pos_cur, pos_nxt, y_hbm, x1_ref, g2_ref, tw_ref, fg_ref, o_ref, buf, sem):
    i = pl.program_id(0)
    n = pl.num_programs(0)
    slot = i % 2

    def start_rows(pos_ref, s, rows):
        for r in rows:
            for k in range(TOP_K):
                _token_copy(y_hbm, pos_ref[0, k, r], buf.at[s, k], r, sem.at[s]).start(
                    priority=k % DMA_PRIORITIES)

    @pl.when(i == 0)
    def _():
        def body(g, carry):
            start_rows(pos_cur, 0, [g * (DMA_UNROLL // TOP_K) + u
                                    for u in range(DMA_UNROLL // TOP_K)])
            return carry
        lax.fori_loop(0, CMB_TB * TOP_K // DMA_UNROLL, body, 0)

    pltpu.make_async_copy(buf.at[slot], buf.at[slot], sem.at[slot]).wait()

    def combine_tile(prefetch_next):
        weights = [jnp.broadcast_to(tw_ref[:, k:k + 1], (CMB_TB, PACK_LANES))
                   for k in range(TOP_K)]
        sumsq = jnp.zeros((CMB_TB, 1), f32)
        rows_per_chunk = CMB_TB // PACK_ROWS
        for s in range(PACK_ROWS):
            if prefetch_next:
                start_rows(pos_nxt, 1 - slot, range(s * rows_per_chunk, (s + 1) * rows_per_chunk))
            lo, hi = _load_packed(buf.at[slot, 0], CMB_TB, s)
            lo, hi = lo * weights[0], hi * weights[0]
            for k in range(1, TOP_K):
                lo_k, hi_k = _load_packed(buf.at[slot, k], CMB_TB, s)
                lo, hi = lo + lo_k * weights[k], hi + hi_k * weights[k]
            for c, moe in ((2 * s, lo), (2 * s + 1, hi)):
                cols = slice(c * PACK_LANES, (c + 1) * PACK_LANES)
                x2 = x1_ref[:, cols] + g2_ref[0, :, cols] * moe
                o_ref[:, cols] = x2
                sumsq = sumsq + jnp.sum(x2 * x2, axis=-1, keepdims=True)
        inv = lax.rsqrt(sumsq * (1.0 / D_MODEL) + EPS)
        o_ref[...] = o_ref[...] * inv * fg_ref[...]

    @pl.when(i + 1 < n)
    def _():
        combine_tile(True)

    @pl.when(i + 1 >= n)
    def _():
        combine_tile(False)


def _combine(pos, y_packed, x1, gate2, top_w_rows, final_g):
    nt = TOKENS // CMB_TB
    tiles_per_batch = SEQ // CMB_TB
    pos3 = pos.reshape(TOP_K, nt, CMB_TB).transpose(1, 0, 2)
    return pl.pallas_call(
        _combine_kernel,
        grid=(nt,),
        in_specs=[pl.BlockSpec((1, TOP_K, CMB_TB), lambda i: (i, 0, 0), memory_space=pltpu.SMEM),
                  pl.BlockSpec((1, TOP_K, CMB_TB), lambda i: (jnp.minimum(i + 1, nt - 1), 0, 0),
                               memory_space=pltpu.SMEM),
                  pl.BlockSpec(memory_space=pl.ANY),
                  pl.BlockSpec((CMB_TB, D_MODEL), lambda i: (i, 0)),
                  pl.BlockSpec((1, 1, D_MODEL), lambda i: (i // tiles_per_batch, 0, 0)),
                  pl.BlockSpec((CMB_TB, TOP_K), lambda i: (i, 0)),
                  pl.BlockSpec((1, D_MODEL), lambda i: (0, 0))],
        out_specs=pl.BlockSpec((CMB_TB, D_MODEL), lambda i: (i, 0)),
        out_shape=jax.ShapeDtypeStruct((TOKENS, D_MODEL), f32),
        scratch_shapes=[pltpu.VMEM((2, TOP_K, CMB_TB * PACK_ROWS, PACK_LANES), jnp.uint32),
                        pltpu.SemaphoreType.DMA((2,))],
        compiler_params=_cparams(("arbitrary",), 32),
        name="combine",
    )(pos3, pos3, y_packed.reshape(MOE_ROWS, PACK_ROWS, PACK_LANES), x1,
      gate2.reshape(BATCH, 1, D_MODEL), top_w_rows, final_g.reshape(1, D_MODEL))


def _routing_tables(counts, top_i, rank):
    padded = (counts + MOE_TM - 1) // MOE_TM * MOE_TM
    ends = jnp.cumsum(padded)
    starts = ends - padded
    n_valid = ends[-1] // MOE_TM
    tile_ids = jnp.minimum(jnp.arange(MOE_NT, dtype=i32), n_valid - 1)
    tile_expert = jnp.sum(tile_ids[:, None] >= (ends // MOE_TM)[None, :], axis=1).astype(i32)
    experts = jnp.arange(N_EXPERTS, dtype=i32)
    pos = rank + jnp.sum(jnp.where(top_i[..., None] == experts, starts, 0), axis=-1)
    pad_start = (starts + counts).astype(i32)
    pad_len = (padded - counts).astype(i32)
    later = (experts[None, :] > tile_expert[:, None]) & (counts > 0)[None, :]
    next_expert = jnp.min(jnp.where(later, experts[None, :], N_EXPERTS), axis=1)
    next_expert = jnp.where(next_expert == N_EXPERTS, -1, next_expert).astype(i32)
    return (tile_expert, next_expert, n_valid.reshape(1).astype(i32), pos.astype(i32), pad_start,
            pad_len)


def kernel(x, c, w_ada, b_ada, norm1_g, w_in, lambda_q1, lambda_k1, lambda_q2, lambda_k2, subln_g,
           rel_bias, conv_w, conv_b, conv_ln_g, conv_ln_b, w_out, norm2_g, w_router, b_router,
           w_gate_up, b_gate_up, w_down, b_down, final_g):
    assert x.shape == (BATCH, SEQ, D_MODEL) and w_ada.shape[0] == 1
    l = 0
    x2d = x.reshape(TOKENS, D_MODEL)
    mod = _adaln(c, w_ada[l], b_ada[l])
    shift1, scale1, gate1, shift2, scale2, gate2 = [mod[i] for i in range(6)]

    proj = _inproj(x2d, norm1_g[l], scale1, shift1, w_in[l])
    att = _attention(proj, _bias_band(rel_bias), lambda_q1[l], lambda_k1[l], lambda_q2[l],
                     lambda_k2[l], subln_g[l])
    cv = _conv(proj, conv_w[l], conv_b[l], conv_ln_g[l], conv_ln_b[l])

    x1, h2_packed, top_i, top_w, rank, counts = _outproj(
        att, cv, w_out[l].astype(bf16), x2d, gate1, scale2, shift2, norm2_g[l], w_router[l],
        b_router[l])
    tile_expert, next_expert, n_valid, pos, pad_start, pad_len = _routing_tables(
        counts[:, 0], top_i, rank)

    xs = _dispatch(pad_start, pad_len, n_valid, pos, h2_packed)
    act = _gmm1(tile_expert, next_expert, n_valid, xs, w_gate_up[l], b_gate_up[l])
    y = _gmm2(tile_expert, next_expert, n_valid, act, w_down[l], b_down[l])
    out = _combine(pos, y, x1, gate2, top_w.T, final_g)
    return out.reshape(BATCH, SEQ, D_MODEL)
```

```python
import functools
import math

import jax
import jax.numpy as jnp
from jax import lax
from jax.experimental import pallas as pl
from jax.experimental.pallas import tpu as pltpu

f32 = jnp.float32
bf16 = jnp.bfloat16
i32 = jnp.int32

D_MODEL = 2048
BATCH = 4
SEQ = 2048
TOKENS = BATCH * SEQ
ATT_HEADS = 8
ATT_HALF_DIM = 64
ATT_V_DIM = 128
ATT_WIDTH = ATT_HEADS * ATT_V_DIM
CONV_CH = D_MODEL - ATT_WIDTH
CONV_KERNEL = 31
CONV_HALF = CONV_KERNEL // 2
QK_COLS = ATT_HEADS * 2 * ATT_HALF_DIM
IN_COLS = 2 * QK_COLS + ATT_WIDTH + 2 * CONV_CH
N_BUCKETS = 32
MAX_DISTANCE = 128
N_EXPERTS = 32
TOP_K = 4
D_FF = D_MODEL
SWIGLU_LIMIT = 7.0
SWIGLU_ALPHA = 1.702
EPS = 1e-6
LAMBDA_INIT = 0.8 - 0.6 * math.exp(-0.3 * 0)
LOG2E = math.log2(math.e)
Q_SCALE = LOG2E * ATT_HALF_DIM ** -0.5

ADA_TN = 1024
INP_TM = 1024
INP_TN = 1024
ATT_TQ = 2048
ATT_SUB = 256
ATT_KC = 256
ATT_EXT_CHUNKS = 5
CONV_TS = 512
CONV_HALO = 16
CONV_RB = 32
CONV_LN_ROWS = 128
OUT_TM = 512
MOE_TM = 512
MOE_HALF = MOE_TM // 2
MOE_NT = TOKENS * TOP_K // MOE_TM + N_EXPERTS
MOE_ROWS = MOE_NT * MOE_TM
GMM1_TN = 1024
GMM2_TN = 2048
DSP_TB = 256
CMB_TB = 128
ROW_CHUNK = 64
DMA_UNROLL = 8
DMA_PRIORITIES = 2
WEIGHT_DMA_PRIORITY = 1

_MIB = 1024 * 1024


def _cparams(sem, vmem_mib):
    return pltpu.CompilerParams(dimension_semantics=sem, vmem_limit_bytes=vmem_mib * _MIB)


PACK_ROWS = 8
PACK_LANES = 128
_HI_MASK = 0xFFFF0000


def _pack_pair(lo, hi):
    lo_b = lax.shift_right_logical(
        lax.bitcast_convert_type(lo.astype(bf16).astype(f32), jnp.uint32), jnp.uint32(16))
    hi_b = lax.bitcast_convert_type(hi.astype(bf16).astype(f32), jnp.uint32) & jnp.uint32(_HI_MASK)
    return lo_b | hi_b


def _unpack_pair(w):
    lo = lax.bitcast_convert_type(lax.shift_left(w, jnp.uint32(16)), f32)
    hi = lax.bitcast_convert_type(w & jnp.uint32(_HI_MASK), f32)
    return lo, hi


def _store_packed(dst_ref, token0, n_tokens, get_cols):
    for s in range(PACK_ROWS):
        lo = get_cols(slice(2 * s * PACK_LANES, (2 * s + 1) * PACK_LANES))
        hi = get_cols(slice((2 * s + 1) * PACK_LANES, (2 * s + 2) * PACK_LANES))
        dst_ref[pl.ds(token0 * PACK_ROWS + s, n_tokens, stride=PACK_ROWS), :] = _pack_pair(lo, hi)


def _load_packed(src_ref, n_tokens, s):
    return _unpack_pair(src_ref[pl.ds(s, n_tokens, stride=PACK_ROWS), :])


def _row_chunks(n_rows, chunk, body):
    def step(i, carry):
        body(pl.multiple_of(i * chunk, chunk))
        return carry
    lax.fori_loop(0, n_rows // chunk, step, 0)


def _adaln_kernel(c_ref, w_ref, b_ref, o_ref):
    c = c_ref[...]
    cs = c * jax.nn.sigmoid(c)
    o_ref[0] = jnp.dot(cs.astype(bf16), w_ref[...].astype(bf16),
                       preferred_element_type=f32) + b_ref[...]


def _adaln(c, w_ada, b_ada):
    n = w_ada.shape[1]
    per_vec = D_MODEL // ADA_TN
    return pl.pallas_call(
        _adaln_kernel,
        grid=(n // ADA_TN,),
        in_specs=[pl.BlockSpec((BATCH, D_MODEL), lambda j: (0, 0)),
                  pl.BlockSpec((D_MODEL, ADA_TN), lambda j: (0, j)),
                  pl.BlockSpec((1, ADA_TN), lambda j: (0, j))],
        out_specs=pl.BlockSpec((1, BATCH, ADA_TN), lambda j: (j // per_vec, 0, j % per_vec)),
        out_shape=jax.ShapeDtypeStruct((n // D_MODEL, BATCH, D_MODEL), f32),
        compiler_params=_cparams(("arbitrary",), 40),
        name="adaln",
    )(c, w_ada, b_ada.reshape(1, n))


def _inproj_kernel(x_ref, g_ref, sc_ref, sh_ref, w_ref, o_ref, h_scr):
    @pl.when(pl.program_id(1) == 0)
    def _():
        g = g_ref[...]
        one_plus_scale = 1.0 + sc_ref[0]
        shift = sh_ref[0]

        def chunk(r0):
            rows = pl.ds(r0, ROW_CHUNK)
            x = x_ref[rows, :]
            inv = lax.rsqrt(jnp.mean(x * x, axis=-1, keepdims=True) + EPS)
            for c in range(D_MODEL // 128):
                cols = slice(c * 128, (c + 1) * 128)
                y = x_ref[rows, cols] * inv * g[:, cols]
                h_scr[rows, cols] = (y * one_plus_scale[:, cols] + shift[:, cols]).astype(bf16)
        _row_chunks(INP_TM, ROW_CHUNK, chunk)

    out_scale = jnp.where(pl.program_id(1) < QK_COLS // INP_TN, Q_SCALE, 1.0).astype(f32)
    acc = jnp.dot(h_scr[...], w_ref[...].astype(bf16), preferred_element_type=f32)
    o_ref[...] = (acc * out_scale).astype(bf16)


def _inproj(x2d, norm_g, scale, shift, w_in):
    assert QK_COLS % INP_TN == 0
    tiles_per_batch = SEQ // INP_TM
    return pl.pallas_call(
        _inproj_kernel,
        grid=(TOKENS // INP_TM, IN_COLS // INP_TN),
        in_specs=[pl.BlockSpec((INP_TM, D_MODEL), lambda i, j: (i, 0)),
                  pl.BlockSpec((1, D_MODEL), lambda i, j: (0, 0)),
                  pl.BlockSpec((1, 1, D_MODEL), lambda i, j: (i // tiles_per_batch, 0, 0)),
                  pl.BlockSpec((1, 1, D_MODEL), lambda i, j: (i // tiles_per_batch, 0, 0)),
                  pl.BlockSpec((D_MODEL, INP_TN), lambda i, j: (0, j))],
        out_specs=pl.BlockSpec((INP_TM, INP_TN), lambda i, j: (i, j)),
        out_shape=jax.ShapeDtypeStruct((TOKENS, IN_COLS), bf16),
        scratch_shapes=[pltpu.VMEM((INP_TM, D_MODEL), bf16)],
        compiler_params=_cparams(("parallel", "arbitrary"), 48),
        name="inproj",
    )(x2d, norm_g.reshape(1, D_MODEL), scale.reshape(BATCH, 1, D_MODEL),
      shift.reshape(BATCH, 1, D_MODEL), w_in)


def _t5_bucket(rel):
    half = N_BUCKETS // 2
    max_exact = half // 2
    ret = jnp.where(rel > 0, half, 0)
    n = jnp.abs(rel)
    nf = jnp.maximum(n, 1).astype(f32)
    large = max_exact + (jnp.log(nf / max_exact) / math.log(MAX_DISTANCE / max_exact)
                         * (half - max_exact)).astype(i32)
    large = jnp.minimum(large, half - 1)
    return ret + jnp.where(n < max_exact, n, large)


def _bias_band(rel_bias):
    assert ATT_SUB == ATT_KC and ATT_EXT_CHUNKS == 5 and ATT_KC >= MAX_DISTANCE
    r = jnp.arange(ATT_SUB)[:, None]
    j = jnp.arange(ATT_EXT_CHUNKS * ATT_KC)[None, :]
    rel = j - 2 * ATT_KC - r
    onehot = (_t5_bucket(rel)[..., None] == jnp.arange(N_BUCKETS)).astype(f32)
    return jnp.einsum("rjn,nh->hrj", onehot, rel_bias.astype(f32) * LOG2E,
                      precision=lax.Precision.HIGHEST)


def _attn_kernel(q_ref, k_ref, v_ref, band_ref, lq1_ref, lk1_ref, lq2_ref, lk2_ref, sg_ref,
                 o_ref, va_scr):
    t = pl.program_id(2)
    n_chunks = SEQ // ATT_KC

    @pl.when(t == 0)
    def _():
        va_scr[:, :ATT_V_DIM] = v_ref[...]
        va_scr[:, ATT_V_DIM:] = jnp.ones((SEQ, ATT_V_DIM), bf16)

    lam = (jnp.exp(jnp.sum(lq1_ref[...] * lk1_ref[...], axis=-1, keepdims=True))
           - jnp.exp(jnp.sum(lq2_ref[...] * lk2_ref[...], axis=-1, keepdims=True))
           + LAMBDA_INIT)
    def band_offset(rows, c):
        block = t * (ATT_TQ // ATT_SUB) + rows.start // ATT_SUB
        d = jnp.clip(c - block + 2, 0, ATT_EXT_CHUNKS - 1)
        return pl.multiple_of(d * ATT_KC, ATT_KC)

    def scores(qm, rows):
        chunks = []
        m = jnp.full((ATT_SUB, 1), -jnp.inf, f32)
        for c in range(n_chunks):
            s = lax.dot_general(qm, k_ref[c * ATT_KC:(c + 1) * ATT_KC, :],
                                (((1,), (1,)), ((), ())), preferred_element_type=f32)
            s = s + band_ref[0, :, pl.ds(band_offset(rows, c), ATT_KC)]
            chunks.append(s)
            m = jnp.maximum(m, jnp.max(s, axis=-1, keepdims=True))
        return chunks, m

    def weighted_values(chunks, m):
        o = jnp.zeros((ATT_SUB, 2 * ATT_V_DIM), f32)
        for c in range(n_chunks):
            p = jnp.exp2(chunks[c] - m)
            o = o + jnp.dot(p.astype(bf16), va_scr[c * ATT_KC:(c + 1) * ATT_KC, :],
                            preferred_element_type=f32)
        return o[:, :ATT_V_DIM] / o[:, ATT_V_DIM:]

    def finish(rows, o1, o2):
        out = o1 - lam * o2
        ms = jnp.mean(out * out, axis=-1, keepdims=True)
        y = out * lax.rsqrt(ms + EPS) * sg_ref[...]
        o_ref[rows, :] = (y * (1.0 - LAMBDA_INIT)).astype(bf16)

    units = []
    for r in range(ATT_TQ // ATT_SUB):
        rows = slice(r * ATT_SUB, (r + 1) * ATT_SUB)
        q = q_ref[rows, :]
        lane = lax.broadcasted_iota(i32, q.shape, 1)
        zero = jnp.zeros_like(q)
        units.append((rows, jnp.where(lane < ATT_HALF_DIM, q, zero)))
        units.append((rows, jnp.where(lane >= ATT_HALF_DIM, q, zero)))
    first_map = {}
    pending = None
    for unit in units + [None]:
        current = (unit[0],) + scores(unit[1], unit[0]) if unit is not None else None
        if pending is not None:
            p_rows, p_chunks, p_m = pending
            o = weighted_values(p_chunks, p_m)
            if p_rows.start in first_map:
                finish(p_rows, first_map.pop(p_rows.start), o)
            else:
                first_map[p_rows.start] = o
        pending = current


def _attention(proj, band, lq1, lk1, lq2, lk2, subln_g):
    qt = SEQ // ATT_TQ
    k_col0 = QK_COLS // ATT_V_DIM
    v_col0 = 2 * QK_COLS // ATT_V_DIM
    vec = lambda n: pl.BlockSpec((1, n), lambda b, h, t: (0, 0))
    return pl.pallas_call(
        _attn_kernel,
        grid=(BATCH, ATT_HEADS, qt),
        in_specs=[pl.BlockSpec((ATT_TQ, ATT_V_DIM), lambda b, h, t: (b * qt + t, h)),
                  pl.BlockSpec((SEQ, ATT_V_DIM), lambda b, h, t: (b, k_col0 + h)),
                  pl.BlockSpec((SEQ, ATT_V_DIM), lambda b, h, t: (b, v_col0 + h)),
                  pl.BlockSpec((1, ATT_SUB, ATT_EXT_CHUNKS * ATT_KC), lambda b, h, t: (h, 0, 0)),
                  vec(ATT_HALF_DIM), vec(ATT_HALF_DIM), vec(ATT_HALF_DIM), vec(ATT_HALF_DIM),
                  vec(ATT_V_DIM)],
        out_specs=pl.BlockSpec((ATT_TQ, ATT_V_DIM), lambda b, h, t: (b * qt + t, h)),
        out_shape=jax.ShapeDtypeStruct((TOKENS, ATT_WIDTH), bf16),
        scratch_shapes=[pltpu.VMEM((SEQ, 2 * ATT_V_DIM), bf16)],
        compiler_params=_cparams(("parallel", "parallel", "arbitrary"), 40),
        name="attn",
    )(proj, proj, proj, band, lq1.reshape(1, -1), lk1.reshape(1, -1), lq2.reshape(1, -1),
      lk2.reshape(1, -1), subln_g.reshape(1, -1))


def _conv_kernel(ap_ref, ac_ref, an_ref, gp_ref, gc_ref, gn_ref, w_ref, cb_ref, lg_ref, lb_ref,
                 o_ref, z_scr, c_scr):
    s = pl.program_id(1)
    last = pl.num_programs(1) - 1

    lanes = 128
    n_lane_chunks = CONV_CH // lanes

    def store_z(rows, z):
        for lc in range(n_lane_chunks):
            z_scr[lc, rows, :] = z[:, lc * lanes:(lc + 1) * lanes]

    def glu(a, g):
        return a.astype(f32) * jax.nn.sigmoid(g.astype(f32))

    store_z(slice(0, CONV_HALO), jnp.where(s > 0, glu(ap_ref[...], gp_ref[...]), 0.0))
    store_z(slice(CONV_HALO + CONV_TS, CONV_TS + 2 * CONV_HALO),
            jnp.where(s < last, glu(an_ref[...], gn_ref[...]), 0.0))

    def glu_chunk(r0):
        rows = pl.ds(r0, ROW_CHUNK)
        store_z(pl.ds(CONV_HALO + r0, ROW_CHUNK), glu(ac_ref[rows, :], gc_ref[rows, :]))
    _row_chunks(CONV_TS, ROW_CHUNK, glu_chunk)

    tap0 = CONV_HALO - CONV_HALF

    def lane_chunk(lc, carry):
        cols = pl.ds(pl.multiple_of(lc * lanes, lanes), lanes)
        taps = [w_ref[j:j + 1, cols] for j in range(CONV_KERNEL)]
        bias = cb_ref[:, cols]
        for r0 in range(0, CONV_TS, CONV_RB):
            acc = jnp.zeros((CONV_RB, lanes), f32)
            for j in range(CONV_KERNEL):
                acc = acc + z_scr[lc, r0 + tap0 + j:r0 + tap0 + j + CONV_RB, :] * taps[j]
            c_scr[r0:r0 + CONV_RB, cols] = acc + bias
        return carry
    lax.fori_loop(0, n_lane_chunks, lane_chunk, 0)

    lg = lg_ref[...]
    lb = lb_ref[...]

    def ln_chunk(r0):
        c = c_scr[pl.ds(r0, CONV_LN_ROWS), :]
        mu = jnp.mean(c, axis=-1, keepdims=True)
        cc = c - mu
        var = jnp.mean(cc * cc, axis=-1, keepdims=True)
        y = cc * lax.rsqrt(var + EPS) * lg + lb
        o_ref[pl.ds(r0, CONV_LN_ROWS), :] = (y * jax.nn.sigmoid(y)).astype(bf16)
    _row_chunks(CONV_TS, CONV_LN_ROWS, ln_chunk)


def _conv(proj, conv_w, conv_b, ln_g, ln_b):
    st = SEQ // CONV_TS
    halo_per_tile = CONV_TS // CONV_HALO
    halo_per_seq = SEQ // CONV_HALO
    a_col = (2 * QK_COLS + ATT_WIDTH) // CONV_CH
    g_col = a_col + 1

    def cur(col):
        return pl.BlockSpec((CONV_TS, CONV_CH), lambda b, s: (b * st + s, col))

    def prev(col):
        return pl.BlockSpec(
            (CONV_HALO, CONV_CH),
            lambda b, s: (jnp.maximum(b * halo_per_seq + s * halo_per_tile - 1, 0), col))

    def nxt(col):
        return pl.BlockSpec(
            (CONV_HALO, CONV_CH),
            lambda b, s: (jnp.minimum(b * halo_per_seq + (s + 1) * halo_per_tile,
                                      TOKENS // CONV_HALO - 1), col))

    vec = pl.BlockSpec((1, CONV_CH), lambda b, s: (0, 0))
    return pl.pallas_call(
        _conv_kernel,
        grid=(BATCH, st),
        in_specs=[prev(a_col), cur(a_col), nxt(a_col), prev(g_col), cur(g_col), nxt(g_col),
                  pl.BlockSpec((CONV_KERNEL, CONV_CH), lambda b, s: (0, 0)), vec, vec, vec],
        out_specs=pl.BlockSpec((CONV_TS, CONV_CH), lambda b, s: (b * st + s, 0)),
        out_shape=jax.ShapeDtypeStruct((TOKENS, CONV_CH), bf16),
        scratch_shapes=[pltpu.VMEM((CONV_CH // 128, CONV_TS + 2 * CONV_HALO, 128), f32),
                        pltpu.VMEM((CONV_TS, CONV_CH), f32)],
        compiler_params=_cparams(("parallel", "arbitrary"), 32),
        name="conv",
    )(proj, proj, proj, proj, proj, proj, conv_w, conv_b.reshape(1, -1), ln_g.reshape(1, -1),
      ln_b.reshape(1, -1))


def _outproj_kernel(att_ref, cv_ref, wo_ref, x_ref, g1_ref, sc2_ref, sh2_ref, n2_ref, wr_ref,
                    br_ref, x1_ref, h2p_ref, ti_ref, tw_ref, rk_ref, cnt_ref, carry_scr, hi_scr,
                    lo_scr):
    @pl.when(pl.program_id(0) == 0)
    def _():
        carry_scr[...] = jnp.zeros_like(carry_scr)

    mixed = (jnp.dot(att_ref[...], wo_ref[0:ATT_WIDTH, :], preferred_element_type=f32)
             + jnp.dot(cv_ref[...], wo_ref[ATT_WIDTH:, :], preferred_element_type=f32))
    x1_ref[...] = x_ref[...] + g1_ref[0] * mixed

    n2 = n2_ref[...]
    one_plus_scale = 1.0 + sc2_ref[0]
    shift = sh2_ref[0]

    def split(v):
        hi = v.astype(bf16)
        return hi, (v - hi.astype(f32)).astype(bf16)

    def chunk(r0):
        rows = pl.ds(r0, ROW_CHUNK)
        x1 = x1_ref[rows, :]
        inv = lax.rsqrt(jnp.mean(x1 * x1, axis=-1, keepdims=True) + EPS)
        for s in range(PACK_ROWS):
            pair = []
            for c in (2 * s, 2 * s + 1):
                cols = slice(c * PACK_LANES, (c + 1) * PACK_LANES)
                y = x1_ref[rows, cols] * inv * n2[:, cols]
                h2 = y * one_plus_scale[:, cols] + shift[:, cols]
                hi_scr[rows, cols], lo_scr[rows, cols] = split(h2)
                pair.append(h2)
            h2p_ref[pl.ds(r0 * PACK_ROWS + s, ROW_CHUNK, stride=PACK_ROWS), :] = _pack_pair(*pair)
    _row_chunks(OUT_TM, ROW_CHUNK, chunk)

    w_hi, w_lo = split(wr_ref[...])
    h_hi, h_lo = hi_scr[...], lo_scr[...]
    rows_x_rows = (((1,), (1,)), ((), ()))
    logits = (lax.dot_general(w_hi, h_hi, rows_x_rows, preferred_element_type=f32)
              + lax.dot_general(w_hi, h_lo, rows_x_rows, preferred_element_type=f32)
              + lax.dot_general(w_lo, h_hi, rows_x_rows, preferred_element_type=f32)
              + br_ref[...])
    eidx = lax.broadcasted_iota(i32, logits.shape, 0)
    vals = logits
    sels, tops = [], []
    for k in range(TOP_K):
        m = jnp.max(vals, axis=0, keepdims=True)
        idx = jnp.min(jnp.where(vals == m, eidx, N_EXPERTS), axis=0, keepdims=True)
        sel = eidx == idx
        ti_ref[k:k + 1, :] = idx
        sels.append(sel)
        tops.append(m)
        vals = jnp.where(sel, -jnp.inf, vals)
    exps = [jnp.exp(v - tops[0]) for v in tops]
    denom = exps[0] + exps[1] + exps[2] + exps[3]
    for k in range(TOP_K):
        tw_ref[k:k + 1, :] = exps[k] / denom

    onehot = jnp.where(sels[0] | sels[1] | sels[2] | sels[3], 1.0, 0.0)
    r_i = lax.broadcasted_iota(i32, (OUT_TM, OUT_TM), 0)
    c_i = lax.broadcasted_iota(i32, (OUT_TM, OUT_TM), 1)
    earlier = jnp.where(r_i < c_i, 1.0, 0.0).astype(bf16)
    before = jnp.dot(onehot.astype(bf16), earlier, preferred_element_type=f32) + carry_scr[...]
    for k in range(TOP_K):
        rk_ref[k:k + 1, :] = jnp.sum(jnp.where(sels[k], before, 0.0), axis=0,
                                     keepdims=True).astype(i32)
    total = carry_scr[...] + jnp.sum(onehot, axis=1, keepdims=True)
    carry_scr[...] = total
    cnt_ref[...] = jnp.broadcast_to(total, cnt_ref.shape).astype(i32)


def _outproj(att, cv, wo_bf, x2d, gate1, scale2, shift2, norm2_g, w_router, b_router):
    tiles_per_batch = SEQ // OUT_TM
    row = lambda w: pl.BlockSpec((OUT_TM, w), lambda i: (i, 0))
    mod = pl.BlockSpec((1, 1, D_MODEL), lambda i: (i // tiles_per_batch, 0, 0))
    slot = pl.BlockSpec((TOP_K, OUT_TM), lambda i: (0, i))
    return pl.pallas_call(
        _outproj_kernel,
        grid=(TOKENS // OUT_TM,),
        in_specs=[row(ATT_WIDTH), row(CONV_CH),
                  pl.BlockSpec((D_MODEL, D_MODEL), lambda i: (0, 0), pipeline_mode=pl.Buffered(1)),
                  row(D_MODEL), mod, mod, mod,
                  pl.BlockSpec((1, D_MODEL), lambda i: (0, 0)),
                  pl.BlockSpec((N_EXPERTS, D_MODEL), lambda i: (0, 0)),
                  pl.BlockSpec((N_EXPERTS, 1), lambda i: (0, 0))],
        out_specs=[row(D_MODEL),
                   pl.BlockSpec((OUT_TM * PACK_ROWS, PACK_LANES), lambda i: (i, 0)),
                   slot, slot, slot,
                   pl.BlockSpec((N_EXPERTS, 128), lambda i: (0, 0))],
        out_shape=[jax.ShapeDtypeStruct((TOKENS, D_MODEL), f32),
                   jax.ShapeDtypeStruct((TOKENS * PACK_ROWS, PACK_LANES), jnp.uint32),
                   jax.ShapeDtypeStruct((TOP_K, TOKENS), i32),
                   jax.ShapeDtypeStruct((TOP_K, TOKENS), f32),
                   jax.ShapeDtypeStruct((TOP_K, TOKENS), i32),
                   jax.ShapeDtypeStruct((N_EXPERTS, 128), i32)],
        scratch_shapes=[pltpu.VMEM((N_EXPERTS, 1), f32), pltpu.VMEM((OUT_TM, D_MODEL), bf16),
                        pltpu.VMEM((OUT_TM, D_MODEL), bf16)],
        compiler_params=_cparams(("arbitrary",), 52),
        name="outproj",
    )(att, cv, wo_bf, x2d, gate1.reshape(BATCH, 1, D_MODEL), scale2.reshape(BATCH, 1, D_MODEL),
      shift2.reshape(BATCH, 1, D_MODEL), norm2_g.reshape(1, D_MODEL), w_router.T,
      b_router.reshape(N_EXPERTS, 1))


def _token_copy(src_hbm, token, dst_buf, row, sem):
    dst = dst_buf.at[pl.ds(pl.multiple_of(row * PACK_ROWS, PACK_ROWS), PACK_ROWS), :]
    return pltpu.make_async_copy(src_hbm.at[token], dst, sem)


_PAD_BITS = tuple(range(MOE_TM.bit_length() - 2, -1, -1))


def _pad_fill_copies(pad_start, pad_len, zeros, xs_hbm, sem):
    pairs = []
    for b in _PAD_BITS:
        size = 1 << b
        off = pad_start + lax.shift_left(lax.shift_right_logical(pad_len, b + 1), b + 1)
        copy = pltpu.make_async_copy(zeros.at[pl.ds(0, size)], xs_hbm.at[pl.ds(off, size)], sem)
        pairs.append((lax.shift_right_logical(pad_len, b) & 1 == 1, copy))
    return pairs


def _dispatch_kernel(pad_start_ref, pad_len_ref, nv_ref, pos_ref, h2p_ref, xs_hbm, zeros, sem, zsem):
    i = pl.program_id(0)
    last = pl.num_programs(0) - 1

    def for_each_fill(act):
        def per_expert(e, carry):
            for cond, copy in _pad_fill_copies(pad_start_ref[e], pad_len_ref[e], zeros, xs_hbm, zsem):
                @pl.when(cond)
                def _(copy=copy):
                    act(copy)
            return carry
        lax.fori_loop(0, N_EXPERTS, per_expert, 0)

        def per_tile(t, carry):
            row0 = pl.multiple_of(t * MOE_TM, MOE_TM)
            act(pltpu.make_async_copy(zeros, xs_hbm.at[pl.ds(row0, MOE_TM)], zsem))
            return carry
        lax.fori_loop(nv_ref[0], MOE_NT, per_tile, 0)

    @pl.when(i == 0)
    def _():
        zeros[...] = jnp.zeros_like(zeros)
        for_each_fill(lambda copy: copy.start())

    def body(g, carry):
        for u in range(DMA_UNROLL // TOP_K):
            r = g * (DMA_UNROLL // TOP_K) + u
            for k in range(TOP_K):
                pltpu.make_async_copy(h2p_ref.at[r], xs_hbm.at[pos_ref[0, k, r]], sem).start(
                    priority=k % DMA_PRIORITIES)
        return carry
    lax.fori_loop(0, DSP_TB * TOP_K // DMA_UNROLL, body, 0)
    pltpu.make_async_copy(xs_hbm.at[pl.ds(0, DSP_TB * TOP_K)], xs_hbm.at[pl.ds(0, DSP_TB * TOP_K)],
                          sem).wait()

    @pl.when(i == last)
    def _():
        for_each_fill(lambda copy: copy.wait())


def _dispatch(pad_start, pad_len, n_valid, pos, h2_packed):
    nt = TOKENS // DSP_TB
    pos3 = pos.reshape(TOP_K, nt, DSP_TB).transpose(1, 0, 2)
    grid_spec = pltpu.PrefetchScalarGridSpec(
        num_scalar_prefetch=3,
        grid=(nt,),
        in_specs=[pl.BlockSpec((1, TOP_K, DSP_TB), lambda i, *_: (i, 0, 0), memory_space=pltpu.SMEM),
                  pl.BlockSpec((DSP_TB, PACK_ROWS, PACK_LANES), lambda i, *_: (i, 0, 0))],
        out_specs=pl.BlockSpec(memory_space=pl.ANY),
        scratch_shapes=[pltpu.VMEM((MOE_TM, PACK_ROWS, PACK_LANES), jnp.uint32),
                        pltpu.SemaphoreType.DMA(()), pltpu.SemaphoreType.DMA(())])
    return pl.pallas_call(
        _dispatch_kernel,
        grid_spec=grid_spec,
        out_shape=jax.ShapeDtypeStruct((MOE_ROWS, PACK_ROWS, PACK_LANES), jnp.uint32),
        compiler_params=_cparams(("arbitrary",), 24),
        name="dispatch",
    )(pad_start, pad_len, n_valid, pos3, h2_packed.reshape(TOKENS, PACK_ROWS, PACK_LANES))


def _weights_changed(te_ref, m):
    prev = te_ref[jnp.maximum(m - 1, 0)]
    return (m == 0) | (te_ref[m] != prev)


def _expert_tile_step(te_ref, nx_ref, vr_ref, n, m, n_passes, count_ref, copies_for, convert,
                      compute, write_zeros):
    first_expert = te_ref[0]
    first = _weights_changed(te_ref, m)
    rows_valid = vr_ref[m]

    def start(copies):
        for copy in copies:
            copy.start(priority=WEIGHT_DMA_PRIORITY)

    @pl.when((n == 0) & (m == 0))
    def _():
        count_ref[0] = 0
        start(copies_for(0, first_expert, 0))

    def open_item():
        slot = count_ref[0] % 2
        nxt = nx_ref[m]

        @pl.when(nxt >= 0)
        def _():
            start(copies_for(n, nxt, 1 - slot))

        @pl.when((nxt < 0) & (n + 1 < n_passes))
        def _():
            start(copies_for(n + 1, first_expert, 1 - slot))

        for copy in copies_for(0, 0, slot):
            copy.wait()
        convert(slot)
        count_ref[0] = count_ref[0] + 1

    full = rows_valid > MOE_HALF
    half = (rows_valid > 0) & jnp.logical_not(full)
    for is_first in (True, False):
        for cond, n_rows in ((full, MOE_TM), (half, MOE_HALF)):
            @pl.when(cond & (first if is_first else jnp.logical_not(first)))
            def _(is_first=is_first, n_rows=n_rows):
                if is_first:
                    open_item()
                compute(n_rows)

    @pl.when(rows_valid <= 0)
    def _():
        write_zeros()


def _gmm1_kernel(te_ref, nx_ref, nv_ref, vr_ref, xs_ref, w_hbm, bg_ref, bu_ref, o_ref, stage, wg_bf,
                 wu_bf, count_ref, sem):
    n = pl.program_id(0)
    m = pl.program_id(1)

    def copies_for(p, e, slot):
        col = pl.multiple_of(p * GMM1_TN, GMM1_TN)
        return [pltpu.make_async_copy(w_hbm.at[e, :, pl.ds(col, GMM1_TN)], stage.at[slot, 0],
                                      sem.at[slot]),
                pltpu.make_async_copy(w_hbm.at[e, :, pl.ds(D_FF + col, GMM1_TN)],
                                      stage.at[slot, 1], sem.at[slot])]

    def convert(slot):
        wg_bf[...] = stage[slot, 0].astype(bf16)
        wu_bf[...] = stage[slot, 1].astype(bf16)

    def compute(n_rows):
        parts = []
        for s in range(PACK_ROWS):
            lo, hi = _load_packed(xs_ref, n_rows, s)
            parts += [lo.astype(bf16), hi.astype(bf16)]
        x = jnp.concatenate(parts, axis=1)
        gate = jnp.dot(x, wg_bf[...], preferred_element_type=f32) + bg_ref[0]
        up = jnp.dot(x, wu_bf[...], preferred_element_type=f32) + bu_ref[0]
        gate = jnp.minimum(gate, SWIGLU_LIMIT)
        up = jnp.clip(up, -SWIGLU_LIMIT, SWIGLU_LIMIT)
        act = (up + 1.0) * (gate * jax.nn.sigmoid(SWIGLU_ALPHA * gate))
        o_ref[0:n_rows, :] = act.astype(bf16)
        if n_rows < MOE_TM:
            o_ref[n_rows:, :] = jnp.zeros((MOE_TM - n_rows, GMM1_TN), bf16)

    def write_zeros():
        o_ref[...] = jnp.zeros_like(o_ref)

    _expert_tile_step(te_ref, nx_ref, vr_ref, n, m, pl.num_programs(0), count_ref, copies_for,
                      convert, compute, write_zeros)


def _gmm1(tile_expert, next_expert, n_valid, valid_rows, xs_packed, w_gate_up, b_gate_up):
    up0 = D_FF // GMM1_TN
    used = lambda m, nv: jnp.minimum(m, nv[0] - 1)
    grid_spec = pltpu.PrefetchScalarGridSpec(
        num_scalar_prefetch=4,
        grid=(D_FF // GMM1_TN, MOE_NT),
        in_specs=[pl.BlockSpec((MOE_TM * PACK_ROWS, PACK_LANES),
                               lambda n, m, te, nx, nv, vr: (used(m, nv), 0)),
                  pl.BlockSpec(memory_space=pl.ANY),
                  pl.BlockSpec((1, 1, GMM1_TN), lambda n, m, te, nx, nv, vr: (te[m], 0, n)),
                  pl.BlockSpec((1, 1, GMM1_TN), lambda n, m, te, nx, nv, vr: (te[m], 0, up0 + n))],
        out_specs=pl.BlockSpec((MOE_TM, GMM1_TN), lambda n, m, te, nx, nv, vr: (m, n)),
        scratch_shapes=[pltpu.VMEM((2, 2, D_MODEL, GMM1_TN), f32),
                        pltpu.VMEM((D_MODEL, GMM1_TN), bf16), pltpu.VMEM((D_MODEL, GMM1_TN), bf16),
                        pltpu.SMEM((1,), i32), pltpu.SemaphoreType.DMA((2,))])
    b3 = b_gate_up.reshape(N_EXPERTS, 1, 2 * D_FF)
    return pl.pallas_call(
        _gmm1_kernel,
        grid_spec=grid_spec,
        out_shape=jax.ShapeDtypeStruct((MOE_ROWS, D_FF), bf16),
        compiler_params=_cparams(("arbitrary", "arbitrary"), 54),
        name="gmm1",
    )(tile_expert, next_expert, n_valid, valid_rows,
      xs_packed.reshape(MOE_ROWS * PACK_ROWS, PACK_LANES), w_gate_up, b3, b3)


def _gmm2_kernel(te_ref, nx_ref, nv_ref, vr_ref, a_ref, w_hbm, bd_ref, o_ref, stage, wd_bf, y_scr,
                 count_ref, sem):
    def copies_for(p, e, slot):
        return [pltpu.make_async_copy(w_hbm.at[e], stage.at[slot], sem.at[slot])]

    def convert(slot):
        wd_bf[...] = stage[slot].astype(bf16)

    def compute(n_rows):
        y_scr[0:n_rows, :] = (jnp.dot(a_ref[0:n_rows, :], wd_bf[...], preferred_element_type=f32)
                              + bd_ref[0])
        _store_packed(o_ref, 0, n_rows, lambda cols: y_scr[0:n_rows, cols])
        if n_rows < MOE_TM:
            o_ref[n_rows * PACK_ROWS:, :] = jnp.zeros(((MOE_TM - n_rows) * PACK_ROWS, PACK_LANES),
                                                     jnp.uint32)

    def write_zeros():
        o_ref[...] = jnp.zeros_like(o_ref)

    _expert_tile_step(te_ref, nx_ref, vr_ref, pl.program_id(0), pl.program_id(1),
                      pl.num_programs(0), count_ref, copies_for, convert, compute, write_zeros)


def _gmm2(tile_expert, next_expert, n_valid, valid_rows, act, w_down, b_down):
    assert GMM2_TN == D_MODEL
    grid_spec = pltpu.PrefetchScalarGridSpec(
        num_scalar_prefetch=4,
        grid=(1, MOE_NT),
        in_specs=[pl.BlockSpec((MOE_TM, D_FF), lambda n, m, te, nx, nv, vr: (m, 0)),
                  pl.BlockSpec(memory_space=pl.ANY),
                  pl.BlockSpec((1, 1, GMM2_TN), lambda n, m, te, nx, nv, vr: (te[m], 0, 0))],
        out_specs=pl.BlockSpec((MOE_TM * PACK_ROWS, PACK_LANES),
                               lambda n, m, te, nx, nv, vr: (m, 0)),
        scratch_shapes=[pltpu.VMEM((2, D_FF, GMM2_TN), f32), pltpu.VMEM((D_FF, GMM2_TN), bf16),
                        pltpu.VMEM((MOE_TM, D_MODEL), f32), pltpu.SMEM((1,), i32),
                        pltpu.SemaphoreType.DMA((2,))])
    return pl.pallas_call(
        _gmm2_kernel,
        grid_spec=grid_spec,
        out_shape=jax.ShapeDtypeStruct((MOE_ROWS * PACK_ROWS, PACK_LANES), jnp.uint32),
        compiler_params=_cparams(("arbitrary", "arbitrary"), 56),
        name="gmm2",
    )(tile_expert, next_expert, n_valid, valid_rows, act, w_down,
      b_down.reshape(N_EXPERTS, 1, D_MODEL))


def _combine_kernel(pos_cur, pos_nxt, y_hbm, x1_ref, g2_ref, tw_ref, fg_ref, o_ref, buf, sem):
    i = pl.program_id(0)
    n = pl.num_programs(0)
    slot = i % 2

    def start_rows(pos_ref, s, rows):
        for r in rows:
            for k in range(TOP_K):
                _token_copy(y_hbm, pos_ref[0, k, r], buf.at[s, k], r, sem.at[s]).start(
                    priority=k % DMA_PRIORITIES)

    @pl.when(i == 0)
    def _():
        def body(g, carry):
            start_rows(pos_cur, 0, [g * (DMA_UNROLL // TOP_K) + u
                                    for u in range(DMA_UNROLL // TOP_K)])
            return carry
        lax.fori_loop(0, CMB_TB * TOP_K // DMA_UNROLL, body, 0)

    pltpu.make_async_copy(buf.at[slot], buf.at[slot], sem.at[slot]).wait()

    def combine_tile(prefetch_next):
        weights = [jnp.broadcast_to(tw_ref[:, k:k + 1], (CMB_TB, PACK_LANES))
                   for k in range(TOP_K)]
        sumsq = jnp.zeros((CMB_TB, 1), f32)
        rows_per_chunk = CMB_TB // PACK_ROWS
        for s in range(PACK_ROWS):
            if prefetch_next:
                start_rows(pos_nxt, 1 - slot, range(s * rows_per_chunk, (s + 1) * rows_per_chunk))
            lo, hi = _load_packed(buf.at[slot, 0], CMB_TB, s)
            lo, hi = lo * weights[0], hi * weights[0]
            for k in range(1, TOP_K):
                lo_k, hi_k = _load_packed(buf.at[slot, k], CMB_TB, s)
                lo, hi = lo + lo_k * weights[k], hi + hi_k * weights[k]
            for c, moe in ((2 * s, lo), (2 * s + 1, hi)):
                cols = slice(c * PACK_LANES, (c + 1) * PACK_LANES)
                x2 = x1_ref[:, cols] + g2_ref[0, :, cols] * moe
                o_ref[:, cols] = x2
                sumsq = sumsq + jnp.sum(x2 * x2, axis=-1, keepdims=True)
        inv = lax.rsqrt(sumsq * (1.0 / D_MODEL) + EPS)
        o_ref[...] = o_ref[...] * inv * fg_ref[...]

    @pl.when(i + 1 < n)
    def _():
        combine_tile(True)

    @pl.when(i + 1 >= n)
    def _():
        combine_tile(False)


def _combine(pos, y_packed, x1, gate2, top_w_rows, final_g):
    nt = TOKENS // CMB_TB
    tiles_per_batch = SEQ // CMB_TB
    pos3 = pos.reshape(TOP_K, nt, CMB_TB).transpose(1, 0, 2)
    return pl.pallas_call(
        _combine_kernel,
        grid=(nt,),
        in_specs=[pl.BlockSpec((1, TOP_K, CMB_TB), lambda i: (i, 0, 0), memory_space=pltpu.SMEM),
                  pl.BlockSpec((1, TOP_K, CMB_TB), lambda i: (jnp.minimum(i + 1, nt - 1), 0, 0),
                               memory_space=pltpu.SMEM),
                  pl.BlockSpec(memory_space=pl.ANY),
                  pl.BlockSpec((CMB_TB, D_MODEL), lambda i: (i, 0)),
                  pl.BlockSpec((1, 1, D_MODEL), lambda i: (i // tiles_per_batch, 0, 0)),
                  pl.BlockSpec((CMB_TB, TOP_K), lambda i: (i, 0)),
                  pl.BlockSpec((1, D_MODEL), lambda i: (0, 0))],
        out_specs=pl.BlockSpec((CMB_TB, D_MODEL), lambda i: (i, 0)),
        out_shape=jax.ShapeDtypeStruct((TOKENS, D_MODEL), f32),
        scratch_shapes=[pltpu.VMEM((2, TOP_K, CMB_TB * PACK_ROWS, PACK_LANES), jnp.uint32),
                        pltpu.SemaphoreType.DMA((2,))],
        compiler_params=_cparams(("arbitrary",), 32),
        name="combine",
    )(pos3, pos3, y_packed.reshape(MOE_ROWS, PACK_ROWS, PACK_LANES), x1,
      gate2.reshape(BATCH, 1, D_MODEL), top_w_rows, final_g.reshape(1, D_MODEL))


def _routing_tables(counts, top_i, rank):
    padded = (counts + MOE_TM - 1) // MOE_TM * MOE_TM
    ends = jnp.cumsum(padded)
    starts = ends - padded
    n_valid = ends[-1] // MOE_TM
    tile_ids = jnp.minimum(jnp.arange(MOE_NT, dtype=i32), n_valid - 1)
    tile_expert = jnp.sum(tile_ids[:, None] >= (ends // MOE_TM)[None, :], axis=1).astype(i32)
    experts = jnp.arange(N_EXPERTS, dtype=i32)
    pos = rank + jnp.sum(jnp.where(top_i[..., None] == experts, starts, 0), axis=-1)
    pad_start = (starts + counts).astype(i32)
    pad_len = (padded - counts).astype(i32)
    later = (experts[None, :] > tile_expert[:, None]) & (counts > 0)[None, :]
    next_expert = jnp.min(jnp.where(later, experts[None, :], N_EXPERTS), axis=1)
    next_expert = jnp.where(next_expert == N_EXPERTS, -1, next_expert).astype(i32)
    block_ids = jnp.arange(MOE_NT, dtype=i32)
    group_end = jnp.sum(jnp.where(tile_expert[:, None] == experts, pad_start, 0), axis=-1)
    valid_rows = jnp.clip(group_end - block_ids * MOE_TM, 0, MOE_TM)
    valid_rows = jnp.where(block_ids < n_valid, valid_rows, 0).astype(i32)
    return (tile_expert, next_expert, n_valid.reshape(1).astype(i32), valid_rows, pos.astype(i32),
            pad_start, pad_len)


def kernel(x, c, w_ada, b_ada, norm1_g, w_in, lambda_q1, lambda_k1, lambda_q2, lambda_k2, subln_g,
           rel_bias, conv_w, conv_b, conv_ln_g, conv_ln_b, w_out, norm2_g, w_router, b_router,
           w_gate_up, b_gate_up, w_down, b_down, final_g):
    assert x.shape == (BATCH, SEQ, D_MODEL) and w_ada.shape[0] == 1
    l = 0
    x2d = x.reshape(TOKENS, D_MODEL)
    mod = _adaln(c, w_ada[l], b_ada[l])
    shift1, scale1, gate1, shift2, scale2, gate2 = [mod[i] for i in range(6)]

    proj = _inproj(x2d, norm1_g[l], scale1, shift1, w_in[l])
    att = _attention(proj, _bias_band(rel_bias), lambda_q1[l], lambda_k1[l], lambda_q2[l],
                     lambda_k2[l], subln_g[l])
    cv = _conv(proj, conv_w[l], conv_b[l], conv_ln_g[l], conv_ln_b[l])

    x1, h2_packed, top_i, top_w, rank, counts = _outproj(
        att, cv, w_out[l].astype(bf16), x2d, gate1, scale2, shift2, norm2_g[l], w_router[l],
        b_router[l])
    tile_expert, next_expert, n_valid, valid_rows, pos, pad_start, pad_len = _routing_tables(
        counts[:, 0], top_i, rank)

    xs = _dispatch(pad_start, pad_len, n_valid, pos, h2_packed)
    act = _gmm1(tile_expert, next_expert, n_valid, valid_rows, xs, w_gate_up[l], b_gate_up[l])
    y = _gmm2(tile_expert, next_expert, n_valid, valid_rows, act, w_down[l], b_down[l])
    out = _combine(pos, y, x1, gate2, top_w.T, final_g)
    return out.reshape(BATCH, SEQ, D_MODEL)
```

```python
import functools
import math

import jax
import jax.numpy as jnp
from jax import lax
from jax.experimental import pallas as pl
from jax.experimental.pallas import tpu as pltpu

f32 = jnp.float32
bf16 = jnp.bfloat16
i32 = jnp.int32

D_MODEL = 2048
BATCH = 4
SEQ = 2048
TOKENS = BATCH * SEQ
ATT_HEADS = 8
ATT_HALF_DIM = 64
ATT_V_DIM = 128
ATT_WIDTH = ATT_HEADS * ATT_V_DIM
CONV_CH = D_MODEL - ATT_WIDTH
CONV_KERNEL = 31
CONV_HALF = CONV_KERNEL // 2
QK_COLS = ATT_HEADS * 2 * ATT_HALF_DIM
IN_COLS = 2 * QK_COLS + ATT_WIDTH + 2 * CONV_CH
N_BUCKETS = 32
MAX_DISTANCE = 128
N_EXPERTS = 32
TOP_K = 4
D_FF = D_MODEL
SWIGLU_LIMIT = 7.0
SWIGLU_ALPHA = 1.702
EPS = 1e-6
LAMBDA_INIT = 0.8 - 0.6 * math.exp(-0.3 * 0)
LOG2E = math.log2(math.e)
Q_SCALE = LOG2E * ATT_HALF_DIM ** -0.5

ADA_TN = 1024
INP_TM = 1024
INP_TN = 1024
ATT_TQ = 2048
ATT_SUB = 256
ATT_KC = 256
ATT_EXT_CHUNKS = 5
CONV_TS = 512
CONV_HALO = 16
CONV_RB = 32
CONV_LN_ROWS = 128
OUT_TM = 512
MOE_TM = 512
MOE_HALF = MOE_TM // 2
MOE_NT = TOKENS * TOP_K // MOE_TM + N_EXPERTS
MOE_ROWS = MOE_NT * MOE_TM
GMM1_TN = 1024
GMM2_TN = 2048
DSP_TB = 256
CMB_TB = 128
ROW_CHUNK = 64
DMA_UNROLL = 8
DMA_PRIORITIES = 2
WEIGHT_DMA_PRIORITY = 1

_MIB = 1024 * 1024


def _cparams(sem, vmem_mib):
    return pltpu.CompilerParams(dimension_semantics=sem, vmem_limit_bytes=vmem_mib * _MIB)


PACK_ROWS = 8
PACK_LANES = 128
_HI_MASK = 0xFFFF0000


def _pack_pair(lo, hi):
    lo_b = lax.shift_right_logical(
        lax.bitcast_convert_type(lo.astype(bf16).astype(f32), jnp.uint32), jnp.uint32(16))
    hi_b = lax.bitcast_convert_type(hi.astype(bf16).astype(f32), jnp.uint32) & jnp.uint32(_HI_MASK)
    return lo_b | hi_b


def _unpack_pair(w):
    lo = lax.bitcast_convert_type(lax.shift_left(w, jnp.uint32(16)), f32)
    hi = lax.bitcast_convert_type(w & jnp.uint32(_HI_MASK), f32)
    return lo, hi


def _store_packed(dst_ref, token0, n_tokens, get_cols):
    for s in range(PACK_ROWS):
        lo = get_cols(slice(2 * s * PACK_LANES, (2 * s + 1) * PACK_LANES))
        hi = get_cols(slice((2 * s + 1) * PACK_LANES, (2 * s + 2) * PACK_LANES))
        dst_ref[pl.ds(token0 * PACK_ROWS + s, n_tokens, stride=PACK_ROWS), :] = _pack_pair(lo, hi)


def _load_packed(src_ref, n_tokens, s):
    return _unpack_pair(src_ref[pl.ds(s, n_tokens, stride=PACK_ROWS), :])


def _row_chunks(n_rows, chunk, body):
    def step(i, carry):
        body(pl.multiple_of(i * chunk, chunk))
        return carry
    lax.fori_loop(0, n_rows // chunk, step, 0)


def _adaln_kernel(c_ref, w_ref, b_ref, o_ref):
    c = c_ref[...]
    cs = c * jax.nn.sigmoid(c)
    o_ref[0] = jnp.dot(cs.astype(bf16), w_ref[...].astype(bf16),
                       preferred_element_type=f32) + b_ref[...]


def _adaln(c, w_ada, b_ada):
    n = w_ada.shape[1]
    per_vec = D_MODEL // ADA_TN
    return pl.pallas_call(
        _adaln_kernel,
        grid=(n // ADA_TN,),
        in_specs=[pl.BlockSpec((BATCH, D_MODEL), lambda j: (0, 0)),
                  pl.BlockSpec((D_MODEL, ADA_TN), lambda j: (0, j)),
                  pl.BlockSpec((1, ADA_TN), lambda j: (0, j))],
        out_specs=pl.BlockSpec((1, BATCH, ADA_TN), lambda j: (j // per_vec, 0, j % per_vec)),
        out_shape=jax.ShapeDtypeStruct((n // D_MODEL, BATCH, D_MODEL), f32),
        compiler_params=_cparams(("arbitrary",), 40),
        name="adaln",
    )(c, w_ada, b_ada.reshape(1, n))


def _inproj_kernel(x_ref, g_ref, sc_ref, sh_ref, w_ref, o_ref, h_scr):
    j = pl.program_id(1)

    def normalize():
        g = g_ref[...]
        one_plus_scale = 1.0 + sc_ref[0]
        shift = sh_ref[0]
        for r0 in range(0, INP_TM, ROW_CHUNK):
            rows = slice(r0, r0 + ROW_CHUNK)
            x = x_ref[rows, :]
            inv = lax.rsqrt(jnp.mean(x * x, axis=-1, keepdims=True) + EPS)
            for c in range(D_MODEL // 128):
                cols = slice(c * 128, (c + 1) * 128)
                y = x_ref[rows, cols] * inv * g[:, cols]
                h_scr[rows, cols] = (y * one_plus_scale[:, cols] + shift[:, cols]).astype(bf16)

    def project():
        out_scale = jnp.where(j < QK_COLS // INP_TN, Q_SCALE, 1.0).astype(f32)
        acc = jnp.dot(h_scr[...], w_ref[...].astype(bf16), preferred_element_type=f32)
        o_ref[...] = (acc * out_scale).astype(bf16)

    @pl.when(j == 0)
    def _():
        normalize()
        project()

    @pl.when(j != 0)
    def _():
        project()


def _inproj(x2d, norm_g, scale, shift, w_in):
    assert QK_COLS % INP_TN == 0
    tiles_per_batch = SEQ // INP_TM
    return pl.pallas_call(
        _inproj_kernel,
        grid=(TOKENS // INP_TM, IN_COLS // INP_TN),
        in_specs=[pl.BlockSpec((INP_TM, D_MODEL), lambda i, j: (i, 0)),
                  pl.BlockSpec((1, D_MODEL), lambda i, j: (0, 0)),
                  pl.BlockSpec((1, 1, D_MODEL), lambda i, j: (i // tiles_per_batch, 0, 0)),
                  pl.BlockSpec((1, 1, D_MODEL), lambda i, j: (i // tiles_per_batch, 0, 0)),
                  pl.BlockSpec((D_MODEL, INP_TN), lambda i, j: (0, j))],
        out_specs=pl.BlockSpec((INP_TM, INP_TN), lambda i, j: (i, j)),
        out_shape=jax.ShapeDtypeStruct((TOKENS, IN_COLS), bf16),
        scratch_shapes=[pltpu.VMEM((INP_TM, D_MODEL), bf16)],
        compiler_params=_cparams(("parallel", "arbitrary"), 48),
        name="inproj",
    )(x2d, norm_g.reshape(1, D_MODEL), scale.reshape(BATCH, 1, D_MODEL),
      shift.reshape(BATCH, 1, D_MODEL), w_in)


def _t5_bucket(rel):
    half = N_BUCKETS // 2
    max_exact = half // 2
    ret = jnp.where(rel > 0, half, 0)
    n = jnp.abs(rel)
    nf = jnp.maximum(n, 1).astype(f32)
    large = max_exact + (jnp.log(nf / max_exact) / math.log(MAX_DISTANCE / max_exact)
                         * (half - max_exact)).astype(i32)
    large = jnp.minimum(large, half - 1)
    return ret + jnp.where(n < max_exact, n, large)


def _bias_band(rel_bias):
    assert ATT_SUB == ATT_KC and ATT_EXT_CHUNKS == 5 and ATT_KC >= MAX_DISTANCE
    r = jnp.arange(ATT_SUB)[:, None]
    j = jnp.arange(ATT_EXT_CHUNKS * ATT_KC)[None, :]
    rel = j - 2 * ATT_KC - r
    onehot = (_t5_bucket(rel)[..., None] == jnp.arange(N_BUCKETS)).astype(f32)
    return jnp.einsum("rjn,nh->hrj", onehot, rel_bias.astype(f32) * LOG2E,
                      precision=lax.Precision.HIGHEST)


def _attn_kernel(q_ref, k_ref, v_ref, band_ref, lq1_ref, lk1_ref, lq2_ref, lk2_ref, sg_ref,
                 o_ref, va_scr):
    t = pl.program_id(2)
    n_chunks = SEQ // ATT_KC

    @pl.when(t == 0)
    def _():
        va_scr[:, :ATT_V_DIM] = v_ref[...]
        va_scr[:, ATT_V_DIM:] = jnp.ones((SEQ, ATT_V_DIM), bf16)

    lam = (jnp.exp(jnp.sum(lq1_ref[...] * lk1_ref[...], axis=-1, keepdims=True))
           - jnp.exp(jnp.sum(lq2_ref[...] * lk2_ref[...], axis=-1, keepdims=True))
           + LAMBDA_INIT)
    def band_offset(rows, c):
        block = t * (ATT_TQ // ATT_SUB) + rows.start // ATT_SUB
        d = jnp.clip(c - block + 2, 0, ATT_EXT_CHUNKS - 1)
        return pl.multiple_of(d * ATT_KC, ATT_KC)

    def scores(qm, rows):
        chunks = []
        m = jnp.full((ATT_SUB, 1), -jnp.inf, f32)
        for c in range(n_chunks):
            s = lax.dot_general(qm, k_ref[c * ATT_KC:(c + 1) * ATT_KC, :],
                                (((1,), (1,)), ((), ())), preferred_element_type=f32)
            s = s + band_ref[0, :, pl.ds(band_offset(rows, c), ATT_KC)]
            chunks.append(s)
            m = jnp.maximum(m, jnp.max(s, axis=-1, keepdims=True))
        return chunks, m

    def weighted_values(chunks, m):
        o = jnp.zeros((ATT_SUB, 2 * ATT_V_DIM), f32)
        for c in range(n_chunks):
            p = jnp.exp2(chunks[c] - m)
            o = o + jnp.dot(p.astype(bf16), va_scr[c * ATT_KC:(c + 1) * ATT_KC, :],
                            preferred_element_type=f32)
        return o[:, :ATT_V_DIM] / o[:, ATT_V_DIM:]

    def finish(rows, o1, o2):
        out = o1 - lam * o2
        ms = jnp.mean(out * out, axis=-1, keepdims=True)
        y = out * lax.rsqrt(ms + EPS) * sg_ref[...]
        o_ref[rows, :] = (y * (1.0 - LAMBDA_INIT)).astype(bf16)

    units = []
    for r in range(ATT_TQ // ATT_SUB):
        rows = slice(r * ATT_SUB, (r + 1) * ATT_SUB)
        q = q_ref[rows, :]
        lane = lax.broadcasted_iota(i32, q.shape, 1)
        zero = jnp.zeros_like(q)
        units.append((rows, jnp.where(lane < ATT_HALF_DIM, q, zero)))
        units.append((rows, jnp.where(lane >= ATT_HALF_DIM, q, zero)))
    first_map = {}
    pending = None
    for unit in units + [None]:
        current = (unit[0],) + scores(unit[1], unit[0]) if unit is not None else None
        if pending is not None:
            p_rows, p_chunks, p_m = pending
            o = weighted_values(p_chunks, p_m)
            if p_rows.start in first_map:
                finish(p_rows, first_map.pop(p_rows.start), o)
            else:
                first_map[p_rows.start] = o
        pending = current


def _attention(proj, band, lq1, lk1, lq2, lk2, subln_g):
    qt = SEQ // ATT_TQ
    k_col0 = QK_COLS // ATT_V_DIM
    v_col0 = 2 * QK_COLS // ATT_V_DIM
    vec = lambda n: pl.BlockSpec((1, n), lambda b, h, t: (0, 0))
    return pl.pallas_call(
        _attn_kernel,
        grid=(BATCH, ATT_HEADS, qt),
        in_specs=[pl.BlockSpec((ATT_TQ, ATT_V_DIM), lambda b, h, t: (b * qt + t, h)),
                  pl.BlockSpec((SEQ, ATT_V_DIM), lambda b, h, t: (b, k_col0 + h)),
                  pl.BlockSpec((SEQ, ATT_V_DIM), lambda b, h, t: (b, v_col0 + h)),
                  pl.BlockSpec((1, ATT_SUB, ATT_EXT_CHUNKS * ATT_KC), lambda b, h, t: (h, 0, 0)),
                  vec(ATT_HALF_DIM), vec(ATT_HALF_DIM), vec(ATT_HALF_DIM), vec(ATT_HALF_DIM),
                  vec(ATT_V_DIM)],
        out_specs=pl.BlockSpec((ATT_TQ, ATT_V_DIM), lambda b, h, t: (b * qt + t, h)),
        out_shape=jax.ShapeDtypeStruct((TOKENS, ATT_WIDTH), bf16),
        scratch_shapes=[pltpu.VMEM((SEQ, 2 * ATT_V_DIM), bf16)],
        compiler_params=_cparams(("parallel", "parallel", "arbitrary"), 40),
        name="attn",
    )(proj, proj, proj, band, lq1.reshape(1, -1), lk1.reshape(1, -1), lq2.reshape(1, -1),
      lk2.reshape(1, -1), subln_g.reshape(1, -1))


def _conv_kernel(ap_ref, ac_ref, an_ref, gp_ref, gc_ref, gn_ref, w_ref, cb_ref, lg_ref, lb_ref,
                 o_ref, z_scr, c_scr):
    s = pl.program_id(1)
    last = pl.num_programs(1) - 1

    lanes = 128
    n_lane_chunks = CONV_CH // lanes

    def store_z(rows, z):
        for lc in range(n_lane_chunks):
            z_scr[lc, rows, :] = z[:, lc * lanes:(lc + 1) * lanes]

    def glu(a, g):
        return a.astype(f32) * jax.nn.sigmoid(g.astype(f32))

    store_z(slice(0, CONV_HALO), jnp.where(s > 0, glu(ap_ref[...], gp_ref[...]), 0.0))
    store_z(slice(CONV_HALO + CONV_TS, CONV_TS + 2 * CONV_HALO),
            jnp.where(s < last, glu(an_ref[...], gn_ref[...]), 0.0))

    def glu_chunk(r0):
        rows = pl.ds(r0, ROW_CHUNK)
        store_z(pl.ds(CONV_HALO + r0, ROW_CHUNK), glu(ac_ref[rows, :], gc_ref[rows, :]))
    _row_chunks(CONV_TS, ROW_CHUNK, glu_chunk)

    tap0 = CONV_HALO - CONV_HALF

    def lane_chunk(lc, carry):
        cols = pl.ds(pl.multiple_of(lc * lanes, lanes), lanes)
        taps = [w_ref[j:j + 1, cols] for j in range(CONV_KERNEL)]
        bias = cb_ref[:, cols]
        for r0 in range(0, CONV_TS, CONV_RB):
            acc = jnp.zeros((CONV_RB, lanes), f32)
            for j in range(CONV_KERNEL):
                acc = acc + z_scr[lc, r0 + tap0 + j:r0 + tap0 + j + CONV_RB, :] * taps[j]
            c_scr[r0:r0 + CONV_RB, cols] = acc + bias
        return carry
    lax.fori_loop(0, n_lane_chunks, lane_chunk, 0)

    lg = lg_ref[...]
    lb = lb_ref[...]

    def ln_chunk(r0):
        c = c_scr[pl.ds(r0, CONV_LN_ROWS), :]
        mu = jnp.mean(c, axis=-1, keepdims=True)
        cc = c - mu
        var = jnp.mean(cc * cc, axis=-1, keepdims=True)
        y = cc * lax.rsqrt(var + EPS) * lg + lb
        o_ref[pl.ds(r0, CONV_LN_ROWS), :] = (y * jax.nn.sigmoid(y)).astype(bf16)
    _row_chunks(CONV_TS, CONV_LN_ROWS, ln_chunk)


def _conv(proj, conv_w, conv_b, ln_g, ln_b):
    st = SEQ // CONV_TS
    halo_per_tile = CONV_TS // CONV_HALO
    halo_per_seq = SEQ // CONV_HALO
    a_col = (2 * QK_COLS + ATT_WIDTH) // CONV_CH
    g_col = a_col + 1

    def cur(col):
        return pl.BlockSpec((CONV_TS, CONV_CH), lambda b, s: (b * st + s, col))

    def prev(col):
        return pl.BlockSpec(
            (CONV_HALO, CONV_CH),
            lambda b, s: (jnp.maximum(b * halo_per_seq + s * halo_per_tile - 1, 0), col))

    def nxt(col):
        return pl.BlockSpec(
            (CONV_HALO, CONV_CH),
            lambda b, s: (jnp.minimum(b * halo_per_seq + (s + 1) * halo_per_tile,
                                      TOKENS // CONV_HALO - 1), col))

    vec = pl.BlockSpec((1, CONV_CH), lambda b, s: (0, 0))
    return pl.pallas_call(
        _conv_kernel,
        grid=(BATCH, st),
        in_specs=[prev(a_col), cur(a_col), nxt(a_col), prev(g_col), cur(g_col), nxt(g_col),
                  pl.BlockSpec((CONV_KERNEL, CONV_CH), lambda b, s: (0, 0)), vec, vec, vec],
        out_specs=pl.BlockSpec((CONV_TS, CONV_CH), lambda b, s: (b * st + s, 0)),
        out_shape=jax.ShapeDtypeStruct((TOKENS, CONV_CH), bf16),
        scratch_shapes=[pltpu.VMEM((CONV_CH // 128, CONV_TS + 2 * CONV_HALO, 128), f32),
                        pltpu.VMEM((CONV_TS, CONV_CH), f32)],
        compiler_params=_cparams(("parallel", "arbitrary"), 32),
        name="conv",
    )(proj, proj, proj, proj, proj, proj, conv_w, conv_b.reshape(1, -1), ln_g.reshape(1, -1),
      ln_b.reshape(1, -1))


def _outproj_kernel(att_ref, cv_ref, wo_ref, x_ref, g1_ref, sc2_ref, sh2_ref, n2_ref, wr_ref,
                    br_ref, x1_ref, h2p_ref, ti_ref, tw_ref, rk_ref, cnt_ref, carry_scr, hi_scr,
                    lo_scr):
    @pl.when(pl.program_id(0) == 0)
    def _():
        carry_scr[...] = jnp.zeros_like(carry_scr)

    mixed = (jnp.dot(att_ref[...], wo_ref[0:ATT_WIDTH, :], preferred_element_type=f32)
             + jnp.dot(cv_ref[...], wo_ref[ATT_WIDTH:, :], preferred_element_type=f32))
    x1_ref[...] = x_ref[...] + g1_ref[0] * mixed

    n2 = n2_ref[...]
    one_plus_scale = 1.0 + sc2_ref[0]
    shift = sh2_ref[0]

    def split(v):
        hi = v.astype(bf16)
        return hi, (v - hi.astype(f32)).astype(bf16)

    for r0 in range(0, OUT_TM, ROW_CHUNK):
        rows = slice(r0, r0 + ROW_CHUNK)
        x1 = x1_ref[rows, :]
        inv = lax.rsqrt(jnp.mean(x1 * x1, axis=-1, keepdims=True) + EPS)
        for s in range(PACK_ROWS):
            pair = []
            for c in (2 * s, 2 * s + 1):
                cols = slice(c * PACK_LANES, (c + 1) * PACK_LANES)
                y = x1_ref[rows, cols] * inv * n2[:, cols]
                h2 = y * one_plus_scale[:, cols] + shift[:, cols]
                hi_scr[rows, cols], lo_scr[rows, cols] = split(h2)
                pair.append(h2)
            h2p_ref[pl.ds(r0 * PACK_ROWS + s, ROW_CHUNK, stride=PACK_ROWS), :] = _pack_pair(*pair)

    w_hi, w_lo = split(wr_ref[...])
    h_hi, h_lo = hi_scr[...], lo_scr[...]
    rows_x_rows = (((1,), (1,)), ((), ()))
    logits = (lax.dot_general(w_hi, h_hi, rows_x_rows, preferred_element_type=f32)
              + lax.dot_general(w_hi, h_lo, rows_x_rows, preferred_element_type=f32)
              + lax.dot_general(w_lo, h_hi, rows_x_rows, preferred_element_type=f32)
              + br_ref[...])
    eidx = lax.broadcasted_iota(i32, logits.shape, 0)
    vals = logits
    sels, tops = [], []
    for k in range(TOP_K):
        m = jnp.max(vals, axis=0, keepdims=True)
        idx = jnp.min(jnp.where(vals == m, eidx, N_EXPERTS), axis=0, keepdims=True)
        sel = eidx == idx
        ti_ref[k:k + 1, :] = idx
        sels.append(sel)
        tops.append(m)
        vals = jnp.where(sel, -jnp.inf, vals)
    exps = [jnp.exp(v - tops[0]) for v in tops]
    denom = exps[0] + exps[1] + exps[2] + exps[3]
    for k in range(TOP_K):
        tw_ref[k:k + 1, :] = exps[k] / denom

    onehot = jnp.where(sels[0] | sels[1] | sels[2] | sels[3], 1.0, 0.0)
    r_i = lax.broadcasted_iota(i32, (OUT_TM, OUT_TM), 0)
    c_i = lax.broadcasted_iota(i32, (OUT_TM, OUT_TM), 1)
    earlier = jnp.where(r_i < c_i, 1.0, 0.0).astype(bf16)
    before = jnp.dot(onehot.astype(bf16), earlier, preferred_element_type=f32) + carry_scr[...]
    for k in range(TOP_K):
        rk_ref[k:k + 1, :] = jnp.sum(jnp.where(sels[k], before, 0.0), axis=0,
                                     keepdims=True).astype(i32)
    total = carry_scr[...] + jnp.sum(onehot, axis=1, keepdims=True)
    carry_scr[...] = total
    cnt_ref[...] = jnp.broadcast_to(total, cnt_ref.shape).astype(i32)


def _outproj(att, cv, wo_bf, x2d, gate1, scale2, shift2, norm2_g, w_router, b_router):
    tiles_per_batch = SEQ // OUT_TM
    row = lambda w: pl.BlockSpec((OUT_TM, w), lambda i: (i, 0))
    mod = pl.BlockSpec((1, 1, D_MODEL), lambda i: (i // tiles_per_batch, 0, 0))
    slot = pl.BlockSpec((TOP_K, OUT_TM), lambda i: (0, i))
    return pl.pallas_call(
        _outproj_kernel,
        grid=(TOKENS // OUT_TM,),
        in_specs=[row(ATT_WIDTH), row(CONV_CH),
                  pl.BlockSpec((D_MODEL, D_MODEL), lambda i: (0, 0), pipeline_mode=pl.Buffered(1)),
                  row(D_MODEL), mod, mod, mod,
                  pl.BlockSpec((1, D_MODEL), lambda i: (0, 0)),
                  pl.BlockSpec((N_EXPERTS, D_MODEL), lambda i: (0, 0)),
                  pl.BlockSpec((N_EXPERTS, 1), lambda i: (0, 0))],
        out_specs=[row(D_MODEL),
                   pl.BlockSpec((OUT_TM * PACK_ROWS, PACK_LANES), lambda i: (i, 0)),
                   slot, slot, slot,
                   pl.BlockSpec((N_EXPERTS, 128), lambda i: (0, 0))],
        out_shape=[jax.ShapeDtypeStruct((TOKENS, D_MODEL), f32),
                   jax.ShapeDtypeStruct((TOKENS * PACK_ROWS, PACK_LANES), jnp.uint32),
                   jax.ShapeDtypeStruct((TOP_K, TOKENS), i32),
                   jax.ShapeDtypeStruct((TOP_K, TOKENS), f32),
                   jax.ShapeDtypeStruct((TOP_K, TOKENS), i32),
                   jax.ShapeDtypeStruct((N_EXPERTS, 128), i32)],
        scratch_shapes=[pltpu.VMEM((N_EXPERTS, 1), f32), pltpu.VMEM((OUT_TM, D_MODEL), bf16),
                        pltpu.VMEM((OUT_TM, D_MODEL), bf16)],
        compiler_params=_cparams(("arbitrary",), 52),
        name="outproj",
    )(att, cv, wo_bf, x2d, gate1.reshape(BATCH, 1, D_MODEL), scale2.reshape(BATCH, 1, D_MODEL),
      shift2.reshape(BATCH, 1, D_MODEL), norm2_g.reshape(1, D_MODEL), w_router.T,
      b_router.reshape(N_EXPERTS, 1))


def _token_copy(src_hbm, token, dst_buf, row, sem):
    dst = dst_buf.at[pl.ds(pl.multiple_of(row * PACK_ROWS, PACK_ROWS), PACK_ROWS), :]
    return pltpu.make_async_copy(src_hbm.at[token], dst, sem)


_PAD_BITS = tuple(range(MOE_TM.bit_length() - 2, -1, -1))


def _pad_fill_copies(pad_start, pad_len, zeros, xs_hbm, sem):
    pairs = []
    for b in _PAD_BITS:
        size = 1 << b
        off = pad_start + lax.shift_left(lax.shift_right_logical(pad_len, b + 1), b + 1)
        copy = pltpu.make_async_copy(zeros.at[pl.ds(0, size)], xs_hbm.at[pl.ds(off, size)], sem)
        pairs.append((lax.shift_right_logical(pad_len, b) & 1 == 1, copy))
    return pairs


def _dispatch_kernel(pad_start_ref, pad_len_ref, nv_ref, pos_ref, h2p_ref, xs_hbm, zeros, sem, zsem):
    i = pl.program_id(0)
    last = pl.num_programs(0) - 1

    def for_each_fill(act):
        def per_expert(e, carry):
            for cond, copy in _pad_fill_copies(pad_start_ref[e], pad_len_ref[e], zeros, xs_hbm, zsem):
                @pl.when(cond)
                def _(copy=copy):
                    act(copy)
            return carry
        lax.fori_loop(0, N_EXPERTS, per_expert, 0)

        def per_tile(t, carry):
            row0 = pl.multiple_of(t * MOE_TM, MOE_TM)
            act(pltpu.make_async_copy(zeros, xs_hbm.at[pl.ds(row0, MOE_TM)], zsem))
            return carry
        lax.fori_loop(nv_ref[0], MOE_NT, per_tile, 0)

    @pl.when(i == 0)
    def _():
        zeros[...] = jnp.zeros_like(zeros)
        for_each_fill(lambda copy: copy.start())

    def body(g, carry):
        for u in range(DMA_UNROLL // TOP_K):
            r = g * (DMA_UNROLL // TOP_K) + u
            for k in range(TOP_K):
                pltpu.make_async_copy(h2p_ref.at[r], xs_hbm.at[pos_ref[0, k, r]], sem).start(
                    priority=k % DMA_PRIORITIES)
        return carry
    lax.fori_loop(0, DSP_TB * TOP_K // DMA_UNROLL, body, 0)
    pltpu.make_async_copy(xs_hbm.at[pl.ds(0, DSP_TB * TOP_K)], xs_hbm.at[pl.ds(0, DSP_TB * TOP_K)],
                          sem).wait()

    @pl.when(i == last)
    def _():
        for_each_fill(lambda copy: copy.wait())


def _dispatch(pad_start, pad_len, n_valid, pos, h2_packed):
    nt = TOKENS // DSP_TB
    pos3 = pos.reshape(TOP_K, nt, DSP_TB).transpose(1, 0, 2)
    grid_spec = pltpu.PrefetchScalarGridSpec(
        num_scalar_prefetch=3,
        grid=(nt,),
        in_specs=[pl.BlockSpec((1, TOP_K, DSP_TB), lambda i, *_: (i, 0, 0), memory_space=pltpu.SMEM),
                  pl.BlockSpec((DSP_TB, PACK_ROWS, PACK_LANES), lambda i, *_: (i, 0, 0))],
        out_specs=pl.BlockSpec(memory_space=pl.ANY),
        scratch_shapes=[pltpu.VMEM((MOE_TM, PACK_ROWS, PACK_LANES), jnp.uint32),
                        pltpu.SemaphoreType.DMA(()), pltpu.SemaphoreType.DMA(())])
    return pl.pallas_call(
        _dispatch_kernel,
        grid_spec=grid_spec,
        out_shape=jax.ShapeDtypeStruct((MOE_ROWS, PACK_ROWS, PACK_LANES), jnp.uint32),
        compiler_params=_cparams(("arbitrary",), 24),
        name="dispatch",
    )(pad_start, pad_len, n_valid, pos3, h2_packed.reshape(TOKENS, PACK_ROWS, PACK_LANES))


def _weights_changed(te_ref, m):
    prev = te_ref[jnp.maximum(m - 1, 0)]
    return (m == 0) | (te_ref[m] != prev)


def _expert_tile_step(te_ref, nx_ref, vr_ref, n, m, n_passes, count_ref, copies_for, convert,
                      compute, write_zeros):
    first_expert = te_ref[0]
    first = _weights_changed(te_ref, m)
    rows_valid = vr_ref[m]

    def start(copies):
        for copy in copies:
            copy.start(priority=WEIGHT_DMA_PRIORITY)

    @pl.when((n == 0) & (m == 0))
    def _():
        count_ref[0] = 0
        start(copies_for(0, first_expert, 0))

    def open_item():
        slot = count_ref[0] % 2
        nxt = nx_ref[m]

        @pl.when(nxt >= 0)
        def _():
            start(copies_for(n, nxt, 1 - slot))

        @pl.when((nxt < 0) & (n + 1 < n_passes))
        def _():
            start(copies_for(n + 1, first_expert, 1 - slot))

        for copy in copies_for(0, 0, slot):
            copy.wait()
        convert(slot)
        count_ref[0] = count_ref[0] + 1

    full = rows_valid > MOE_HALF
    half = (rows_valid > 0) & jnp.logical_not(full)
    for is_first in (True, False):
        for cond, n_rows in ((full, MOE_TM), (half, MOE_HALF)):
            @pl.when(cond & (first if is_first else jnp.logical_not(first)))
            def _(is_first=is_first, n_rows=n_rows):
                if is_first:
                    open_item()
                compute(n_rows)

    @pl.when(rows_valid <= 0)
    def _():
        write_zeros()


def _gmm1_kernel(te_ref, nx_ref, nv_ref, vr_ref, xs_ref, w_hbm, bg_ref, bu_ref, o_ref, stage, wg_bf,
                 wu_bf, count_ref, sem):
    n = pl.program_id(0)
    m = pl.program_id(1)

    def copies_for(p, e, slot):
        col = pl.multiple_of(p * GMM1_TN, GMM1_TN)
        return [pltpu.make_async_copy(w_hbm.at[e, :, pl.ds(col, GMM1_TN)], stage.at[slot, 0],
                                      sem.at[slot]),
                pltpu.make_async_copy(w_hbm.at[e, :, pl.ds(D_FF + col, GMM1_TN)],
                                      stage.at[slot, 1], sem.at[slot])]

    def convert(slot):
        wg_bf[...] = stage[slot, 0].astype(bf16)
        wu_bf[...] = stage[slot, 1].astype(bf16)

    def compute(n_rows):
        parts = []
        for s in range(PACK_ROWS):
            lo, hi = _load_packed(xs_ref, n_rows, s)
            parts += [lo.astype(bf16), hi.astype(bf16)]
        x = jnp.concatenate(parts, axis=1)
        gate = jnp.dot(x, wg_bf[...], preferred_element_type=f32) + bg_ref[0]
        up = jnp.dot(x, wu_bf[...], preferred_element_type=f32) + bu_ref[0]
        gate = jnp.minimum(gate, SWIGLU_LIMIT)
        up = jnp.clip(up, -SWIGLU_LIMIT, SWIGLU_LIMIT)
        act = (up + 1.0) * (gate * jax.nn.sigmoid(SWIGLU_ALPHA * gate))
        o_ref[0:n_rows, :] = act.astype(bf16)
        if n_rows < MOE_TM:
            o_ref[n_rows:, :] = jnp.zeros((MOE_TM - n_rows, GMM1_TN), bf16)

    def write_zeros():
        o_ref[...] = jnp.zeros_like(o_ref)

    _expert_tile_step(te_ref, nx_ref, vr_ref, n, m, pl.num_programs(0), count_ref, copies_for,
                      convert, compute, write_zeros)


def _gmm1(tile_expert, next_expert, n_valid, valid_rows, xs_packed, w_gate_up, b_gate_up):
    up0 = D_FF // GMM1_TN
    used = lambda m, nv: jnp.minimum(m, nv[0] - 1)
    grid_spec = pltpu.PrefetchScalarGridSpec(
        num_scalar_prefetch=4,
        grid=(D_FF // GMM1_TN, MOE_NT),
        in_specs=[pl.BlockSpec((MOE_TM * PACK_ROWS, PACK_LANES),
                               lambda n, m, te, nx, nv, vr: (used(m, nv), 0)),
                  pl.BlockSpec(memory_space=pl.ANY),
                  pl.BlockSpec((1, 1, GMM1_TN), lambda n, m, te, nx, nv, vr: (te[m], 0, n)),
                  pl.BlockSpec((1, 1, GMM1_TN), lambda n, m, te, nx, nv, vr: (te[m], 0, up0 + n))],
        out_specs=pl.BlockSpec((MOE_TM, GMM1_TN), lambda n, m, te, nx, nv, vr: (m, n)),
        scratch_shapes=[pltpu.VMEM((2, 2, D_MODEL, GMM1_TN), f32),
                        pltpu.VMEM((D_MODEL, GMM1_TN), bf16), pltpu.VMEM((D_MODEL, GMM1_TN), bf16),
                        pltpu.SMEM((1,), i32), pltpu.SemaphoreType.DMA((2,))])
    b3 = b_gate_up.reshape(N_EXPERTS, 1, 2 * D_FF)
    return pl.pallas_call(
        _gmm1_kernel,
        grid_spec=grid_spec,
        out_shape=jax.ShapeDtypeStruct((MOE_ROWS, D_FF), bf16),
        compiler_params=_cparams(("arbitrary", "arbitrary"), 54),
        name="gmm1",
    )(tile_expert, next_expert, n_valid, valid_rows,
      xs_packed.reshape(MOE_ROWS * PACK_ROWS, PACK_LANES), w_gate_up, b3, b3)


def _gmm2_kernel(te_ref, nx_ref, nv_ref, vr_ref, a_ref, w_hbm, bd_ref, o_ref, stage, wd_bf, y_scr,
                 count_ref, sem):
    def copies_for(p, e, slot):
        return [pltpu.make_async_copy(w_hbm.at[e], stage.at[slot], sem.at[slot])]

    def convert(slot):
        wd_bf[...] = stage[slot].astype(bf16)

    def compute(n_rows):
        y_scr[0:n_rows, :] = (jnp.dot(a_ref[0:n_rows, :], wd_bf[...], preferred_element_type=f32)
                              + bd_ref[0])
        _store_packed(o_ref, 0, n_rows, lambda cols: y_scr[0:n_rows, cols])
        if n_rows < MOE_TM:
            o_ref[n_rows * PACK_ROWS:, :] = jnp.zeros(((MOE_TM - n_rows) * PACK_ROWS, PACK_LANES),
                                                     jnp.uint32)

    def write_zeros():
        o_ref[...] = jnp.zeros_like(o_ref)

    _expert_tile_step(te_ref, nx_ref, vr_ref, pl.program_id(0), pl.program_id(1),
                      pl.num_programs(0), count_ref, copies_for, convert, compute, write_zeros)


def _gmm2(tile_expert, next_expert, n_valid, valid_rows, act, w_down, b_down):
    assert GMM2_TN == D_MODEL
    grid_spec = pltpu.PrefetchScalarGridSpec(
        num_scalar_prefetch=4,
        grid=(1, MOE_NT),
        in_specs=[pl.BlockSpec((MOE_TM, D_FF), lambda n, m, te, nx, nv, vr: (m, 0)),
                  pl.BlockSpec(memory_space=pl.ANY),
                  pl.BlockSpec((1, 1, GMM2_TN), lambda n, m, te, nx, nv, vr: (te[m], 0, 0))],
        out_specs=pl.BlockSpec((MOE_TM * PACK_ROWS, PACK_LANES),
                               lambda n, m, te, nx, nv, vr: (m, 0)),
        scratch_shapes=[pltpu.VMEM((2, D_FF, GMM2_TN), f32), pltpu.VMEM((D_FF, GMM2_TN), bf16),
                        pltpu.VMEM((MOE_TM, D_MODEL), f32), pltpu.SMEM((1,), i32),
                        pltpu.SemaphoreType.DMA((2,))])
    return pl.pallas_call(
        _gmm2_kernel,
        grid_spec=grid_spec,
        out_shape=jax.ShapeDtypeStruct((MOE_ROWS * PACK_ROWS, PACK_LANES), jnp.uint32),
        compiler_params=_cparams(("arbitrary", "arbitrary"), 56),
        name="gmm2",
    )(tile_expert, next_expert, n_valid, valid_rows, act, w_down,
      b_down.reshape(N_EXPERTS, 1, D_MODEL))


def _combine_kernel(pos_cur, pos_nxt, y_hbm, x1_ref, g2_ref, tw_ref, fg_ref, o_ref, buf, sem):
    i = pl.program_id(0)
    n = pl.num_programs(0)
    slot = i % 2

    def start_rows(pos_ref, s, rows):
        for r in rows:
            for k in range(TOP_K):
                _token_copy(y_hbm, pos_ref[0, k, r], buf.at[s, k], r, sem.at[s]).start(
                    priority=k % DMA_PRIORITIES)

    @pl.when(i == 0)
    def _():
        def body(g, carry):
            start_rows(pos_cur, 0, [g * (DMA_UNROLL // TOP_K) + u
                                    for u in range(DMA_UNROLL // TOP_K)])
            return carry
        lax.fori_loop(0, CMB_TB * TOP_K // DMA_UNROLL, body, 0)

    pltpu.make_async_copy(buf.at[slot], buf.at[slot], sem.at[slot]).wait()

    def combine_tile(prefetch_next):
        weights = [jnp.broadcast_to(tw_ref[:, k:k + 1], (CMB_TB, PACK_LANES))
                   for k in range(TOP_K)]
        sumsq = jnp.zeros((CMB_TB, 1), f32)
        rows_per_chunk = CMB_TB // PACK_ROWS
        for s in range(PACK_ROWS):
            if prefetch_next:
                start_rows(pos_nxt, 1 - slot, range(s * rows_per_chunk, (s + 1) * rows_per_chunk))
            lo, hi = _load_packed(buf.at[slot, 0], CMB_TB, s)
            lo, hi = lo * weights[0], hi * weights[0]
            for k in range(1, TOP_K):
                lo_k, hi_k = _load_packed(buf.at[slot, k], CMB_TB, s)
                lo, hi = lo + lo_k * weights[k], hi + hi_k * weights[k]
            for c, moe in ((2 * s, lo), (2 * s + 1, hi)):
                cols = slice(c * PACK_LANES, (c + 1) * PACK_LANES)
                x2 = x1_ref[:, cols] + g2_ref[0, :, cols] * moe
                o_ref[:, cols] = x2
                sumsq = sumsq + jnp.sum(x2 * x2, axis=-1, keepdims=True)
        inv = lax.rsqrt(sumsq * (1.0 / D_MODEL) + EPS)
        o_ref[...] = o_ref[...] * inv * fg_ref[...]

    @pl.when(i + 1 < n)
    def _():
        combine_tile(True)

    @pl.when(i + 1 >= n)
    def _():
        combine_tile(False)


def _combine(pos, y_packed, x1, gate2, top_w_rows, final_g):
    nt = TOKENS // CMB_TB
    tiles_per_batch = SEQ // CMB_TB
    pos3 = pos.reshape(TOP_K, nt, CMB_TB).transpose(1, 0, 2)
    return pl.pallas_call(
        _combine_kernel,
        grid=(nt,),
        in_specs=[pl.BlockSpec((1, TOP_K, CMB_TB), lambda i: (i, 0, 0), memory_space=pltpu.SMEM),
                  pl.BlockSpec((1, TOP_K, CMB_TB), lambda i: (jnp.minimum(i + 1, nt - 1), 0, 0),
                               memory_space=pltpu.SMEM),
                  pl.BlockSpec(memory_space=pl.ANY),
                  pl.BlockSpec((CMB_TB, D_MODEL), lambda i: (i, 0)),
                  pl.BlockSpec((1, 1, D_MODEL), lambda i: (i // tiles_per_batch, 0, 0)),
                  pl.BlockSpec((CMB_TB, TOP_K), lambda i: (i, 0)),
                  pl.BlockSpec((1, D_MODEL), lambda i: (0, 0))],
        out_specs=pl.BlockSpec((CMB_TB, D_MODEL), lambda i: (i, 0)),
        out_shape=jax.ShapeDtypeStruct((TOKENS, D_MODEL), f32),
        scratch_shapes=[pltpu.VMEM((2, TOP_K, CMB_TB * PACK_ROWS, PACK_LANES), jnp.uint32),
                        pltpu.SemaphoreType.DMA((2,))],
        compiler_params=_cparams(("arbitrary",), 32),
        name="combine",
    )(pos3, pos3, y_packed.reshape(MOE_ROWS, PACK_ROWS, PACK_LANES), x1,
      gate2.reshape(BATCH, 1, D_MODEL), top_w_rows, final_g.reshape(1, D_MODEL))


def _routing_tables(counts, top_i, rank):
    padded = (counts + MOE_TM - 1) // MOE_TM * MOE_TM
    ends = jnp.cumsum(padded)
    starts = ends - padded
    n_valid = ends[-1] // MOE_TM
    tile_ids = jnp.minimum(jnp.arange(MOE_NT, dtype=i32), n_valid - 1)
    tile_expert = jnp.sum(tile_ids[:, None] >= (ends // MOE_TM)[None, :], axis=1).astype(i32)
    experts = jnp.arange(N_EXPERTS, dtype=i32)
    pos = rank + jnp.sum(jnp.where(top_i[..., None] == experts, starts, 0), axis=-1)
    pad_start = (starts + counts).astype(i32)
    pad_len = (padded - counts).astype(i32)
    later = (experts[None, :] > tile_expert[:, None]) & (counts > 0)[None, :]
    next_expert = jnp.min(jnp.where(later, experts[None, :], N_EXPERTS), axis=1)
    next_expert = jnp.where(next_expert == N_EXPERTS, -1, next_expert).astype(i32)
    block_ids = jnp.arange(MOE_NT, dtype=i32)
    group_end = jnp.sum(jnp.where(tile_expert[:, None] == experts, pad_start, 0), axis=-1)
    valid_rows = jnp.clip(group_end - block_ids * MOE_TM, 0, MOE_TM)
    valid_rows = jnp.where(block_ids < n_valid, valid_rows, 0).astype(i32)
    return (tile_expert, next_expert, n_valid.reshape(1).astype(i32), valid_rows, pos.astype(i32),
            pad_start, pad_len)


def kernel(x, c, w_ada, b_ada, norm1_g, w_in, lambda_q1, lambda_k1, lambda_q2, lambda_k2, subln_g,
           rel_bias, conv_w, conv_b, conv_ln_g, conv_ln_b, w_out, norm2_g, w_router, b_router,
           w_gate_up, b_gate_up, w_down, b_down, final_g):
    assert x.shape == (BATCH, SEQ, D_MODEL) and w_ada.shape[0] == 1
    l = 0
    x2d = x.reshape(TOKENS, D_MODEL)
    mod = _adaln(c, w_ada[l], b_ada[l])
    shift1, scale1, gate1, shift2, scale2, gate2 = [mod[i] for i in range(6)]

    proj = _inproj(x2d, norm1_g[l], scale1, shift1, w_in[l])
    att = _attention(proj, _bias_band(rel_bias), lambda_q1[l], lambda_k1[l], lambda_q2[l],
                     lambda_k2[l], subln_g[l])
    cv = _conv(proj, conv_w[l], conv_b[l], conv_ln_g[l], conv_ln_b[l])

    x1, h2_packed, top_i, top_w, rank, counts = _outproj(
        att, cv, w_out[l].astype(bf16), x2d, gate1, scale2, shift2, norm2_g[l], w_router[l],
        b_router[l])
    tile_expert, next_expert, n_valid, valid_rows, pos, pad_start, pad_len = _routing_tables(
        counts[:, 0], top_i, rank)

    xs = _dispatch(pad_start, pad_len, n_valid, pos, h2_packed)
    act = _gmm1(tile_expert, next_expert, n_valid, valid_rows, xs, w_gate_up[l], b_gate_up[l])
    y = _gmm2(tile_expert, next_expert, n_valid, valid_rows, act, w_down[l], b_down[l])
    out = _combine(pos, y, x1, gate2, top_w.T, final_g)
    return out.reshape(BATCH, SEQ, D_MODEL)
```

```python
import functools
import math

import jax
import jax.numpy as jnp
from jax import lax
from jax.experimental import pallas as pl
from jax.experimental.pallas import tpu as pltpu

f32 = jnp.float32
bf16 = jnp.bfloat16
i32 = jnp.int32

D_MODEL = 2048
BATCH = 4
SEQ = 2048
TOKENS = BATCH * SEQ
ATT_HEADS = 8
ATT_HALF_DIM = 64
ATT_V_DIM = 128
ATT_WIDTH = ATT_HEADS * ATT_V_DIM
CONV_CH = D_MODEL - ATT_WIDTH
CONV_KERNEL = 31
CONV_HALF = CONV_KERNEL // 2
QK_COLS = ATT_HEADS * 2 * ATT_HALF_DIM
IN_COLS = 2 * QK_COLS + ATT_WIDTH + 2 * CONV_CH
N_BUCKETS = 32
MAX_DISTANCE = 128
N_EXPERTS = 32
TOP_K = 4
D_FF = D_MODEL
SWIGLU_LIMIT = 7.0
SWIGLU_ALPHA = 1.702
EPS = 1e-6
LAMBDA_INIT = 0.8 - 0.6 * math.exp(-0.3 * 0)
LOG2E = math.log2(math.e)
Q_SCALE = LOG2E * ATT_HALF_DIM ** -0.5

ADA_TN = 1024
INP_TM = 1024
INP_TN = 1024
ATT_TQ = 2048
ATT_SUB = 256
ATT_KC = 256
ATT_EXT_CHUNKS = 5
CONV_TS = 512
CONV_HALO = 16
CONV_RB = 32
CONV_LN_ROWS = 128
OUT_TM = 512
MOE_TM = 512
MOE_ROW_STEP = 128
MOE_NT = TOKENS * TOP_K // MOE_TM + N_EXPERTS
MOE_ROWS = MOE_NT * MOE_TM
GMM1_TN = 1024
GMM2_TN = 2048
DSP_TB = 512
CMB_TB = 128
ROW_CHUNK = 64
DMA_UNROLL = 8
DMA_PRIORITIES = 2
WEIGHT_DMA_PRIORITY = 1

_MIB = 1024 * 1024


def _cparams(sem, vmem_mib):
    return pltpu.CompilerParams(dimension_semantics=sem, vmem_limit_bytes=vmem_mib * _MIB)


PACK_ROWS = 8
PACK_LANES = 128
_HI_MASK = 0xFFFF0000


def _pack_pair(lo, hi):
    lo_b = lax.shift_right_logical(
        lax.bitcast_convert_type(lo.astype(bf16).astype(f32), jnp.uint32), jnp.uint32(16))
    hi_b = lax.bitcast_convert_type(hi.astype(bf16).astype(f32), jnp.uint32) & jnp.uint32(_HI_MASK)
    return lo_b | hi_b


def _unpack_pair(w):
    lo = lax.bitcast_convert_type(lax.shift_left(w, jnp.uint32(16)), f32)
    hi = lax.bitcast_convert_type(w & jnp.uint32(_HI_MASK), f32)
    return lo, hi


def _store_packed(dst_ref, token0, n_tokens, get_cols):
    for s in range(PACK_ROWS):
        lo = get_cols(slice(2 * s * PACK_LANES, (2 * s + 1) * PACK_LANES))
        hi = get_cols(slice((2 * s + 1) * PACK_LANES, (2 * s + 2) * PACK_LANES))
        dst_ref[pl.ds(token0 * PACK_ROWS + s, n_tokens, stride=PACK_ROWS), :] = _pack_pair(lo, hi)


def _load_packed(src_ref, n_tokens, s):
    return _unpack_pair(src_ref[pl.ds(s, n_tokens, stride=PACK_ROWS), :])


def _row_chunks(n_rows, chunk, body):
    def step(i, carry):
        body(pl.multiple_of(i * chunk, chunk))
        return carry
    lax.fori_loop(0, n_rows // chunk, step, 0)


def _adaln_kernel(c_ref, w_ref, b_ref, o_ref):
    c = c_ref[...]
    cs = c * jax.nn.sigmoid(c)
    o_ref[0] = jnp.dot(cs.astype(bf16), w_ref[...].astype(bf16),
                       preferred_element_type=f32) + b_ref[...]


def _adaln(c, w_ada, b_ada):
    n = w_ada.shape[1]
    per_vec = D_MODEL // ADA_TN
    return pl.pallas_call(
        _adaln_kernel,
        grid=(n // ADA_TN,),
        in_specs=[pl.BlockSpec((BATCH, D_MODEL), lambda j: (0, 0)),
                  pl.BlockSpec((D_MODEL, ADA_TN), lambda j: (0, j)),
                  pl.BlockSpec((1, ADA_TN), lambda j: (0, j))],
        out_specs=pl.BlockSpec((1, BATCH, ADA_TN), lambda j: (j // per_vec, 0, j % per_vec)),
        out_shape=jax.ShapeDtypeStruct((n // D_MODEL, BATCH, D_MODEL), f32),
        compiler_params=_cparams(("arbitrary",), 40),
        name="adaln",
    )(c, w_ada, b_ada.reshape(1, n))


def _inproj_kernel(x_ref, g_ref, sc_ref, sh_ref, w_ref, o_ref, h_scr):
    j = pl.program_id(1)

    def normalize():
        g = g_ref[...]
        one_plus_scale = 1.0 + sc_ref[0]
        shift = sh_ref[0]
        for r0 in range(0, INP_TM, ROW_CHUNK):
            rows = slice(r0, r0 + ROW_CHUNK)
            x = x_ref[rows, :]
            inv = lax.rsqrt(jnp.mean(x * x, axis=-1, keepdims=True) + EPS)
            for c in range(D_MODEL // 128):
                cols = slice(c * 128, (c + 1) * 128)
                y = x_ref[rows, cols] * inv * g[:, cols]
                h_scr[rows, cols] = (y * one_plus_scale[:, cols] + shift[:, cols]).astype(bf16)

    def project():
        out_scale = jnp.where(j < QK_COLS // INP_TN, Q_SCALE, 1.0).astype(f32)
        acc = jnp.dot(h_scr[...], w_ref[...].astype(bf16), preferred_element_type=f32)
        o_ref[...] = (acc * out_scale).astype(bf16)

    @pl.when(j == 0)
    def _():
        normalize()
        project()

    @pl.when(j != 0)
    def _():
        project()


def _inproj(x2d, norm_g, scale, shift, w_in):
    assert QK_COLS % INP_TN == 0
    tiles_per_batch = SEQ // INP_TM
    return pl.pallas_call(
        _inproj_kernel,
        grid=(TOKENS // INP_TM, IN_COLS // INP_TN),
        in_specs=[pl.BlockSpec((INP_TM, D_MODEL), lambda i, j: (i, 0)),
                  pl.BlockSpec((1, D_MODEL), lambda i, j: (0, 0)),
                  pl.BlockSpec((1, 1, D_MODEL), lambda i, j: (i // tiles_per_batch, 0, 0)),
                  pl.BlockSpec((1, 1, D_MODEL), lambda i, j: (i // tiles_per_batch, 0, 0)),
                  pl.BlockSpec((D_MODEL, INP_TN), lambda i, j: (0, j))],
        out_specs=pl.BlockSpec((INP_TM, INP_TN), lambda i, j: (i, j)),
        out_shape=jax.ShapeDtypeStruct((TOKENS, IN_COLS), bf16),
        scratch_shapes=[pltpu.VMEM((INP_TM, D_MODEL), bf16)],
        compiler_params=_cparams(("parallel", "arbitrary"), 48),
        name="inproj",
    )(x2d, norm_g.reshape(1, D_MODEL), scale.reshape(BATCH, 1, D_MODEL),
      shift.reshape(BATCH, 1, D_MODEL), w_in)


def _t5_bucket(rel):
    half = N_BUCKETS // 2
    max_exact = half // 2
    ret = jnp.where(rel > 0, half, 0)
    n = jnp.abs(rel)
    nf = jnp.maximum(n, 1).astype(f32)
    large = max_exact + (jnp.log(nf / max_exact) / math.log(MAX_DISTANCE / max_exact)
                         * (half - max_exact)).astype(i32)
    large = jnp.minimum(large, half - 1)
    return ret + jnp.where(n < max_exact, n, large)


def _bias_band(rel_bias):
    assert ATT_SUB == ATT_KC and ATT_EXT_CHUNKS == 5 and ATT_KC >= MAX_DISTANCE
    r = jnp.arange(ATT_SUB)[:, None]
    j = jnp.arange(ATT_EXT_CHUNKS * ATT_KC)[None, :]
    rel = j - 2 * ATT_KC - r
    onehot = (_t5_bucket(rel)[..., None] == jnp.arange(N_BUCKETS)).astype(f32)
    return jnp.einsum("rjn,nh->hrj", onehot, rel_bias.astype(f32) * LOG2E,
                      precision=lax.Precision.HIGHEST)


def _attn_kernel(q_ref, k_ref, v_ref, band_ref, lq1_ref, lk1_ref, lq2_ref, lk2_ref, sg_ref,
                 o_ref, va_scr):
    t = pl.program_id(2)
    n_chunks = SEQ // ATT_KC

    @pl.when(t == 0)
    def _():
        va_scr[:, :ATT_V_DIM] = v_ref[...]
        va_scr[:, ATT_V_DIM:] = jnp.ones((SEQ, ATT_V_DIM), bf16)

    lam = (jnp.exp(jnp.sum(lq1_ref[...] * lk1_ref[...], axis=-1, keepdims=True))
           - jnp.exp(jnp.sum(lq2_ref[...] * lk2_ref[...], axis=-1, keepdims=True))
           + LAMBDA_INIT)
    def band_offset(rows, c):
        block = t * (ATT_TQ // ATT_SUB) + rows.start // ATT_SUB
        d = jnp.clip(c - block + 2, 0, ATT_EXT_CHUNKS - 1)
        return pl.multiple_of(d * ATT_KC, ATT_KC)

    def scores(qm, rows):
        chunks = []
        m = jnp.full((ATT_SUB, 1), -jnp.inf, f32)
        for c in range(n_chunks):
            s = lax.dot_general(qm, k_ref[c * ATT_KC:(c + 1) * ATT_KC, :],
                                (((1,), (1,)), ((), ())), preferred_element_type=f32)
            s = s + band_ref[0, :, pl.ds(band_offset(rows, c), ATT_KC)]
            chunks.append(s)
            m = jnp.maximum(m, jnp.max(s, axis=-1, keepdims=True))
        return chunks, m

    def weighted_values(chunks, m):
        o = jnp.zeros((ATT_SUB, 2 * ATT_V_DIM), f32)
        for c in range(n_chunks):
            p = jnp.exp2(chunks[c] - m)
            o = o + jnp.dot(p.astype(bf16), va_scr[c * ATT_KC:(c + 1) * ATT_KC, :],
                            preferred_element_type=f32)
        return o[:, :ATT_V_DIM] / o[:, ATT_V_DIM:]

    def finish(rows, o1, o2):
        out = o1 - lam * o2
        ms = jnp.mean(out * out, axis=-1, keepdims=True)
        y = out * lax.rsqrt(ms + EPS) * sg_ref[...]
        o_ref[rows, :] = (y * (1.0 - LAMBDA_INIT)).astype(bf16)

    units = []
    for r in range(ATT_TQ // ATT_SUB):
        rows = slice(r * ATT_SUB, (r + 1) * ATT_SUB)
        q = q_ref[rows, :]
        lane = lax.broadcasted_iota(i32, q.shape, 1)
        zero = jnp.zeros_like(q)
        units.append((rows, jnp.where(lane < ATT_HALF_DIM, q, zero)))
        units.append((rows, jnp.where(lane >= ATT_HALF_DIM, q, zero)))
    first_map = {}
    pending = None
    for unit in units + [None]:
        current = (unit[0],) + scores(unit[1], unit[0]) if unit is not None else None
        if pending is not None:
            p_rows, p_chunks, p_m = pending
            o = weighted_values(p_chunks, p_m)
            if p_rows.start in first_map:
                finish(p_rows, first_map.pop(p_rows.start), o)
            else:
                first_map[p_rows.start] = o
        pending = current


def _attention(proj, band, lq1, lk1, lq2, lk2, subln_g):
    qt = SEQ // ATT_TQ
    k_col0 = QK_COLS // ATT_V_DIM
    v_col0 = 2 * QK_COLS // ATT_V_DIM
    vec = lambda n: pl.BlockSpec((1, n), lambda b, h, t: (0, 0))
    return pl.pallas_call(
        _attn_kernel,
        grid=(BATCH, ATT_HEADS, qt),
        in_specs=[pl.BlockSpec((ATT_TQ, ATT_V_DIM), lambda b, h, t: (b * qt + t, h)),
                  pl.BlockSpec((SEQ, ATT_V_DIM), lambda b, h, t: (b, k_col0 + h)),
                  pl.BlockSpec((SEQ, ATT_V_DIM), lambda b, h, t: (b, v_col0 + h)),
                  pl.BlockSpec((1, ATT_SUB, ATT_EXT_CHUNKS * ATT_KC), lambda b, h, t: (h, 0, 0)),
                  vec(ATT_HALF_DIM), vec(ATT_HALF_DIM), vec(ATT_HALF_DIM), vec(ATT_HALF_DIM),
                  vec(ATT_V_DIM)],
        out_specs=pl.BlockSpec((ATT_TQ, ATT_V_DIM), lambda b, h, t: (b * qt + t, h)),
        out_shape=jax.ShapeDtypeStruct((TOKENS, ATT_WIDTH), bf16),
        scratch_shapes=[pltpu.VMEM((SEQ, 2 * ATT_V_DIM), bf16)],
        compiler_params=_cparams(("parallel", "parallel", "arbitrary"), 40),
        name="attn",
    )(proj, proj, proj, band, lq1.reshape(1, -1), lk1.reshape(1, -1), lq2.reshape(1, -1),
      lk2.reshape(1, -1), subln_g.reshape(1, -1))


def _conv_kernel(ap_ref, ac_ref, an_ref, gp_ref, gc_ref, gn_ref, w_ref, cb_ref, lg_ref, lb_ref,
                 o_ref, z_scr, c_scr):
    s = pl.program_id(1)
    last = pl.num_programs(1) - 1

    lanes = 128
    n_lane_chunks = CONV_CH // lanes

    def store_z(rows, z):
        for lc in range(n_lane_chunks):
            z_scr[lc, rows, :] = z[:, lc * lanes:(lc + 1) * lanes]

    def glu(a, g):
        return a.astype(f32) * jax.nn.sigmoid(g.astype(f32))

    store_z(slice(0, CONV_HALO), jnp.where(s > 0, glu(ap_ref[...], gp_ref[...]), 0.0))
    store_z(slice(CONV_HALO + CONV_TS, CONV_TS + 2 * CONV_HALO),
            jnp.where(s < last, glu(an_ref[...], gn_ref[...]), 0.0))

    def glu_chunk(r0):
        rows = pl.ds(r0, ROW_CHUNK)
        store_z(pl.ds(CONV_HALO + r0, ROW_CHUNK), glu(ac_ref[rows, :], gc_ref[rows, :]))
    _row_chunks(CONV_TS, ROW_CHUNK, glu_chunk)

    tap0 = CONV_HALO - CONV_HALF

    def lane_chunk(lc, carry):
        cols = pl.ds(pl.multiple_of(lc * lanes, lanes), lanes)
        taps = [w_ref[j:j + 1, cols] for j in range(CONV_KERNEL)]
        bias = cb_ref[:, cols]
        for r0 in range(0, CONV_TS, CONV_RB):
            acc = jnp.zeros((CONV_RB, lanes), f32)
            for j in range(CONV_KERNEL):
                acc = acc + z_scr[lc, r0 + tap0 + j:r0 + tap0 + j + CONV_RB, :] * taps[j]
            c_scr[r0:r0 + CONV_RB, cols] = acc + bias
        return carry
    lax.fori_loop(0, n_lane_chunks, lane_chunk, 0)

    lg = lg_ref[...]
    lb = lb_ref[...]

    def ln_chunk(r0):
        c = c_scr[pl.ds(r0, CONV_LN_ROWS), :]
        mu = jnp.mean(c, axis=-1, keepdims=True)
        cc = c - mu
        var = jnp.mean(cc * cc, axis=-1, keepdims=True)
        y = cc * lax.rsqrt(var + EPS) * lg + lb
        o_ref[pl.ds(r0, CONV_LN_ROWS), :] = (y * jax.nn.sigmoid(y)).astype(bf16)
    _row_chunks(CONV_TS, CONV_LN_ROWS, ln_chunk)


def _conv(proj, conv_w, conv_b, ln_g, ln_b):
    st = SEQ // CONV_TS
    halo_per_tile = CONV_TS // CONV_HALO
    halo_per_seq = SEQ // CONV_HALO
    a_col = (2 * QK_COLS + ATT_WIDTH) // CONV_CH
    g_col = a_col + 1

    def cur(col):
        return pl.BlockSpec((CONV_TS, CONV_CH), lambda b, s: (b * st + s, col))

    def prev(col):
        return pl.BlockSpec(
            (CONV_HALO, CONV_CH),
            lambda b, s: (jnp.maximum(b * halo_per_seq + s * halo_per_tile - 1, 0), col))

    def nxt(col):
        return pl.BlockSpec(
            (CONV_HALO, CONV_CH),
            lambda b, s: (jnp.minimum(b * halo_per_seq + (s + 1) * halo_per_tile,
                                      TOKENS // CONV_HALO - 1), col))

    vec = pl.BlockSpec((1, CONV_CH), lambda b, s: (0, 0))
    return pl.pallas_call(
        _conv_kernel,
        grid=(BATCH, st),
        in_specs=[prev(a_col), cur(a_col), nxt(a_col), prev(g_col), cur(g_col), nxt(g_col),
                  pl.BlockSpec((CONV_KERNEL, CONV_CH), lambda b, s: (0, 0)), vec, vec, vec],
        out_specs=pl.BlockSpec((CONV_TS, CONV_CH), lambda b, s: (b * st + s, 0)),
        out_shape=jax.ShapeDtypeStruct((TOKENS, CONV_CH), bf16),
        scratch_shapes=[pltpu.VMEM((CONV_CH // 128, CONV_TS + 2 * CONV_HALO, 128), f32),
                        pltpu.VMEM((CONV_TS, CONV_CH), f32)],
        compiler_params=_cparams(("parallel", "arbitrary"), 32),
        name="conv",
    )(proj, proj, proj, proj, proj, proj, conv_w, conv_b.reshape(1, -1), ln_g.reshape(1, -1),
      ln_b.reshape(1, -1))


def _outproj_kernel(att_ref, cv_ref, wo_ref, x_ref, g1_ref, sc2_ref, sh2_ref, n2_ref, wr_ref,
                    br_ref, x1_ref, h2p_ref, ti_ref, tw_ref, rk_ref, cnt_ref, carry_scr, hi_scr,
                    lo_scr):
    @pl.when(pl.program_id(0) == 0)
    def _():
        carry_scr[...] = jnp.zeros_like(carry_scr)

    mixed = (jnp.dot(att_ref[...], wo_ref[0:ATT_WIDTH, :], preferred_element_type=f32)
             + jnp.dot(cv_ref[...], wo_ref[ATT_WIDTH:, :], preferred_element_type=f32))
    x1_ref[...] = x_ref[...] + g1_ref[0] * mixed

    n2 = n2_ref[...]
    one_plus_scale = 1.0 + sc2_ref[0]
    shift = sh2_ref[0]

    def split(v):
        hi = v.astype(bf16)
        return hi, (v - hi.astype(f32)).astype(bf16)

    for r0 in range(0, OUT_TM, ROW_CHUNK):
        rows = slice(r0, r0 + ROW_CHUNK)
        x1 = x1_ref[rows, :]
        inv = lax.rsqrt(jnp.mean(x1 * x1, axis=-1, keepdims=True) + EPS)
        for s in range(PACK_ROWS):
            pair = []
            for c in (2 * s, 2 * s + 1):
                cols = slice(c * PACK_LANES, (c + 1) * PACK_LANES)
                y = x1_ref[rows, cols] * inv * n2[:, cols]
                h2 = y * one_plus_scale[:, cols] + shift[:, cols]
                hi_scr[rows, cols], lo_scr[rows, cols] = split(h2)
                pair.append(h2)
            h2p_ref[pl.ds(r0 * PACK_ROWS + s, ROW_CHUNK, stride=PACK_ROWS), :] = _pack_pair(*pair)

    w_hi, w_lo = split(wr_ref[...])
    h_hi, h_lo = hi_scr[...], lo_scr[...]
    rows_x_rows = (((1,), (1,)), ((), ()))
    logits = (lax.dot_general(w_hi, h_hi, rows_x_rows, preferred_element_type=f32)
              + lax.dot_general(w_hi, h_lo, rows_x_rows, preferred_element_type=f32)
              + lax.dot_general(w_lo, h_hi, rows_x_rows, preferred_element_type=f32)
              + br_ref[...])
    eidx = lax.broadcasted_iota(i32, logits.shape, 0)
    vals = logits
    sels, tops = [], []
    for k in range(TOP_K):
        m = jnp.max(vals, axis=0, keepdims=True)
        idx = jnp.min(jnp.where(vals == m, eidx, N_EXPERTS), axis=0, keepdims=True)
        sel = eidx == idx
        ti_ref[k:k + 1, :] = idx
        sels.append(sel)
        tops.append(m)
        vals = jnp.where(sel, -jnp.inf, vals)
    exps = [jnp.exp(v - tops[0]) for v in tops]
    denom = exps[0] + exps[1] + exps[2] + exps[3]
    for k in range(TOP_K):
        tw_ref[k:k + 1, :] = exps[k] / denom

    onehot = jnp.where(sels[0] | sels[1] | sels[2] | sels[3], 1.0, 0.0)
    r_i = lax.broadcasted_iota(i32, (OUT_TM, OUT_TM), 0)
    c_i = lax.broadcasted_iota(i32, (OUT_TM, OUT_TM), 1)
    earlier = jnp.where(r_i < c_i, 1.0, 0.0).astype(bf16)
    before = jnp.dot(onehot.astype(bf16), earlier, preferred_element_type=f32) + carry_scr[...]
    for k in range(TOP_K):
        rk_ref[k:k + 1, :] = jnp.sum(jnp.where(sels[k], before, 0.0), axis=0,
                                     keepdims=True).astype(i32)
    total = carry_scr[...] + jnp.sum(onehot, axis=1, keepdims=True)
    carry_scr[...] = total
    cnt_ref[...] = jnp.broadcast_to(total, cnt_ref.shape).astype(i32)


def _outproj(att, cv, wo_bf, x2d, gate1, scale2, shift2, norm2_g, w_router, b_router):
    tiles_per_batch = SEQ // OUT_TM
    row = lambda w: pl.BlockSpec((OUT_TM, w), lambda i: (i, 0))
    mod = pl.BlockSpec((1, 1, D_MODEL), lambda i: (i // tiles_per_batch, 0, 0))
    slot = pl.BlockSpec((TOP_K, OUT_TM), lambda i: (0, i))
    return pl.pallas_call(
        _outproj_kernel,
        grid=(TOKENS // OUT_TM,),
        in_specs=[row(ATT_WIDTH), row(CONV_CH),
                  pl.BlockSpec((D_MODEL, D_MODEL), lambda i: (0, 0), pipeline_mode=pl.Buffered(1)),
                  row(D_MODEL), mod, mod, mod,
                  pl.BlockSpec((1, D_MODEL), lambda i: (0, 0)),
                  pl.BlockSpec((N_EXPERTS, D_MODEL), lambda i: (0, 0)),
                  pl.BlockSpec((N_EXPERTS, 1), lambda i: (0, 0))],
        out_specs=[row(D_MODEL),
                   pl.BlockSpec((OUT_TM * PACK_ROWS, PACK_LANES), lambda i: (i, 0)),
                   slot, slot, slot,
                   pl.BlockSpec((N_EXPERTS, 128), lambda i: (0, 0))],
        out_shape=[jax.ShapeDtypeStruct((TOKENS, D_MODEL), f32),
                   jax.ShapeDtypeStruct((TOKENS * PACK_ROWS, PACK_LANES), jnp.uint32),
                   jax.ShapeDtypeStruct((TOP_K, TOKENS), i32),
                   jax.ShapeDtypeStruct((TOP_K, TOKENS), f32),
                   jax.ShapeDtypeStruct((TOP_K, TOKENS), i32),
                   jax.ShapeDtypeStruct((N_EXPERTS, 128), i32)],
        scratch_shapes=[pltpu.VMEM((N_EXPERTS, 1), f32), pltpu.VMEM((OUT_TM, D_MODEL), bf16),
                        pltpu.VMEM((OUT_TM, D_MODEL), bf16)],
        compiler_params=_cparams(("arbitrary",), 52),
        name="outproj",
    )(att, cv, wo_bf, x2d, gate1.reshape(BATCH, 1, D_MODEL), scale2.reshape(BATCH, 1, D_MODEL),
      shift2.reshape(BATCH, 1, D_MODEL), norm2_g.reshape(1, D_MODEL), w_router.T,
      b_router.reshape(N_EXPERTS, 1))


def _token_copy(src_hbm, token, dst_buf, row, sem):
    dst = dst_buf.at[pl.ds(pl.multiple_of(row * PACK_ROWS, PACK_ROWS), PACK_ROWS), :]
    return pltpu.make_async_copy(src_hbm.at[token], dst, sem)


_PAD_BITS = tuple(range(MOE_TM.bit_length() - 2, -1, -1))


def _pad_fill_copies(pad_start, pad_len, zeros, xs_hbm, sem):
    pairs = []
    for b in _PAD_BITS:
        size = 1 << b
        off = pad_start + lax.shift_left(lax.shift_right_logical(pad_len, b + 1), b + 1)
        copy = pltpu.make_async_copy(zeros.at[pl.ds(0, size)], xs_hbm.at[pl.ds(off, size)], sem)
        pairs.append((lax.shift_right_logical(pad_len, b) & 1 == 1, copy))
    return pairs


def _dispatch_kernel(pad_start_ref, pad_len_ref, nv_ref, pos_ref, h2p_ref, xs_hbm, zeros, sem, zsem):
    i = pl.program_id(0)
    last = pl.num_programs(0) - 1

    def for_each_fill(act):
        def per_expert(e, carry):
            for cond, copy in _pad_fill_copies(pad_start_ref[e], pad_len_ref[e], zeros, xs_hbm, zsem):
                @pl.when(cond)
                def _(copy=copy):
                    act(copy)
            return carry
        lax.fori_loop(0, N_EXPERTS, per_expert, 0)

        def per_tile(t, carry):
            row0 = pl.multiple_of(t * MOE_TM, MOE_TM)
            act(pltpu.make_async_copy(zeros, xs_hbm.at[pl.ds(row0, MOE_TM)], zsem))
            return carry
        lax.fori_loop(nv_ref[0], MOE_NT, per_tile, 0)

    @pl.when(i == 0)
    def _():
        zeros[...] = jnp.zeros_like(zeros)
        for_each_fill(lambda copy: copy.start())

    def body(g, carry):
        for u in range(DMA_UNROLL // TOP_K):
            r = g * (DMA_UNROLL // TOP_K) + u
            for k in range(TOP_K):
                pltpu.make_async_copy(h2p_ref.at[r], xs_hbm.at[pos_ref[0, k, r]], sem).start(
                    priority=k % DMA_PRIORITIES)
        return carry
    lax.fori_loop(0, DSP_TB * TOP_K // DMA_UNROLL, body, 0)
    pltpu.make_async_copy(xs_hbm.at[pl.ds(0, DSP_TB * TOP_K)], xs_hbm.at[pl.ds(0, DSP_TB * TOP_K)],
                          sem).wait()

    @pl.when(i == last)
    def _():
        for_each_fill(lambda copy: copy.wait())


def _dispatch(pad_start, pad_len, n_valid, pos, h2_packed):
    nt = TOKENS // DSP_TB
    pos3 = pos.reshape(TOP_K, nt, DSP_TB).transpose(1, 0, 2)
    grid_spec = pltpu.PrefetchScalarGridSpec(
        num_scalar_prefetch=3,
        grid=(nt,),
        in_specs=[pl.BlockSpec((1, TOP_K, DSP_TB), lambda i, *_: (i, 0, 0), memory_space=pltpu.SMEM),
                  pl.BlockSpec((DSP_TB, PACK_ROWS, PACK_LANES), lambda i, *_: (i, 0, 0))],
        out_specs=pl.BlockSpec(memory_space=pl.ANY),
        scratch_shapes=[pltpu.VMEM((MOE_TM, PACK_ROWS, PACK_LANES), jnp.uint32),
                        pltpu.SemaphoreType.DMA(()), pltpu.SemaphoreType.DMA(())])
    return pl.pallas_call(
        _dispatch_kernel,
        grid_spec=grid_spec,
        out_shape=jax.ShapeDtypeStruct((MOE_ROWS, PACK_ROWS, PACK_LANES), jnp.uint32),
        compiler_params=_cparams(("arbitrary",), 24),
        name="dispatch",
    )(pad_start, pad_len, n_valid, pos3, h2_packed.reshape(TOKENS, PACK_ROWS, PACK_LANES))


def _weights_changed(te_ref, m):
    prev = te_ref[jnp.maximum(m - 1, 0)]
    return (m == 0) | (te_ref[m] != prev)


def _expert_tile_step(te_ref, nx_ref, vr_ref, n, m, n_passes, count_ref, copies_for, convert,
                      compute, write_zeros):
    first_expert = te_ref[0]
    first = _weights_changed(te_ref, m)
    rows_valid = vr_ref[m]

    def start(copies):
        for copy in copies:
            copy.start(priority=WEIGHT_DMA_PRIORITY)

    @pl.when((n == 0) & (m == 0))
    def _():
        count_ref[0] = 0
        start(copies_for(0, first_expert, 0))

    def open_item():
        slot = count_ref[0] % 2
        nxt = nx_ref[m]

        @pl.when(nxt >= 0)
        def _():
            start(copies_for(n, nxt, 1 - slot))

        @pl.when((nxt < 0) & (n + 1 < n_passes))
        def _():
            start(copies_for(n + 1, first_expert, 1 - slot))

        for copy in copies_for(0, 0, slot):
            copy.wait()
        convert(slot)
        count_ref[0] = count_ref[0] + 1

    full = rows_valid > MOE_TM - MOE_ROW_STEP
    fused = first & full

    @pl.when(fused)
    def _():
        open_item()
        compute(MOE_TM)

    @pl.when(first & jnp.logical_not(full))
    def _():
        open_item()

    for n_rows in range(MOE_ROW_STEP, MOE_TM + 1, MOE_ROW_STEP):
        in_range = (rows_valid > n_rows - MOE_ROW_STEP) & (rows_valid <= n_rows)
        if n_rows == MOE_TM:
            in_range = in_range & jnp.logical_not(fused)

        @pl.when(in_range)
        def _(n_rows=n_rows):
            compute(n_rows)

    @pl.when(rows_valid <= 0)
    def _():
        write_zeros()


def _gmm1_kernel(te_ref, nx_ref, nv_ref, vr_ref, xs_ref, w_hbm, bg_ref, bu_ref, o_ref, stage, wg_bf,
                 wu_bf, count_ref, sem):
    n = pl.program_id(0)
    m = pl.program_id(1)

    def copies_for(p, e, slot):
        col = pl.multiple_of(p * GMM1_TN, GMM1_TN)
        return [pltpu.make_async_copy(w_hbm.at[e, :, pl.ds(col, GMM1_TN)], stage.at[slot, 0],
                                      sem.at[slot]),
                pltpu.make_async_copy(w_hbm.at[e, :, pl.ds(D_FF + col, GMM1_TN)],
                                      stage.at[slot, 1], sem.at[slot])]

    def convert(slot):
        wg_bf[...] = stage[slot, 0].astype(bf16)
        wu_bf[...] = stage[slot, 1].astype(bf16)

    def compute(n_rows):
        parts = []
        for s in range(PACK_ROWS):
            lo, hi = _load_packed(xs_ref, n_rows, s)
            parts += [lo.astype(bf16), hi.astype(bf16)]
        x = jnp.concatenate(parts, axis=1)
        gate = jnp.dot(x, wg_bf[...], preferred_element_type=f32) + bg_ref[0]
        up = jnp.dot(x, wu_bf[...], preferred_element_type=f32) + bu_ref[0]
        gate = jnp.minimum(gate, SWIGLU_LIMIT)
        up = jnp.clip(up, -SWIGLU_LIMIT, SWIGLU_LIMIT)
        act = (up + 1.0) * (gate * jax.nn.sigmoid(SWIGLU_ALPHA * gate))
        o_ref[0:n_rows, :] = act.astype(bf16)
        if n_rows < MOE_TM:
            o_ref[n_rows:, :] = jnp.zeros((MOE_TM - n_rows, GMM1_TN), bf16)

    def write_zeros():
        o_ref[...] = jnp.zeros_like(o_ref)

    _expert_tile_step(te_ref, nx_ref, vr_ref, n, m, pl.num_programs(0), count_ref, copies_for,
                      convert, compute, write_zeros)


def _gmm1(tile_expert, next_expert, n_valid, valid_rows, xs_packed, w_gate_up, b_gate_up):
    up0 = D_FF // GMM1_TN
    used = lambda m, nv: jnp.minimum(m, nv[0] - 1)
    grid_spec = pltpu.PrefetchScalarGridSpec(
        num_scalar_prefetch=4,
        grid=(D_FF // GMM1_TN, MOE_NT),
        in_specs=[pl.BlockSpec((MOE_TM * PACK_ROWS, PACK_LANES),
                               lambda n, m, te, nx, nv, vr: (used(m, nv), 0)),
                  pl.BlockSpec(memory_space=pl.ANY),
                  pl.BlockSpec((1, 1, GMM1_TN), lambda n, m, te, nx, nv, vr: (te[m], 0, n)),
                  pl.BlockSpec((1, 1, GMM1_TN), lambda n, m, te, nx, nv, vr: (te[m], 0, up0 + n))],
        out_specs=pl.BlockSpec((MOE_TM, GMM1_TN), lambda n, m, te, nx, nv, vr: (m, n)),
        scratch_shapes=[pltpu.VMEM((2, 2, D_MODEL, GMM1_TN), f32),
                        pltpu.VMEM((D_MODEL, GMM1_TN), bf16), pltpu.VMEM((D_MODEL, GMM1_TN), bf16),
                        pltpu.SMEM((1,), i32), pltpu.SemaphoreType.DMA((2,))])
    b3 = b_gate_up.reshape(N_EXPERTS, 1, 2 * D_FF)
    return pl.pallas_call(
        _gmm1_kernel,
        grid_spec=grid_spec,
        out_shape=jax.ShapeDtypeStruct((MOE_ROWS, D_FF), bf16),
        compiler_params=_cparams(("arbitrary", "arbitrary"), 54),
        name="gmm1",
    )(tile_expert, next_expert, n_valid, valid_rows,
      xs_packed.reshape(MOE_ROWS * PACK_ROWS, PACK_LANES), w_gate_up, b3, b3)


def _gmm2_kernel(te_ref, nx_ref, nv_ref, vr_ref, a_ref, w_hbm, bd_ref, o_ref, stage, wd_bf, y_scr,
                 count_ref, sem):
    def copies_for(p, e, slot):
        return [pltpu.make_async_copy(w_hbm.at[e], stage.at[slot], sem.at[slot])]

    def convert(slot):
        wd_bf[...] = stage[slot].astype(bf16)

    def compute(n_rows):
        y_scr[0:n_rows, :] = (jnp.dot(a_ref[0:n_rows, :], wd_bf[...], preferred_element_type=f32)
                              + bd_ref[0])
        _store_packed(o_ref, 0, n_rows, lambda cols: y_scr[0:n_rows, cols])
        if n_rows < MOE_TM:
            o_ref[n_rows * PACK_ROWS:, :] = jnp.zeros(((MOE_TM - n_rows) * PACK_ROWS, PACK_LANES),
                                                     jnp.uint32)

    def write_zeros():
        o_ref[...] = jnp.zeros_like(o_ref)

    _expert_tile_step(te_ref, nx_ref, vr_ref, pl.program_id(0), pl.program_id(1),
                      pl.num_programs(0), count_ref, copies_for, convert, compute, write_zeros)


def _gmm2(tile_expert, next_expert, n_valid, valid_rows, act, w_down, b_down):
    assert GMM2_TN == D_MODEL
    grid_spec = pltpu.PrefetchScalarGridSpec(
        num_scalar_prefetch=4,
        grid=(1, MOE_NT),
        in_specs=[pl.BlockSpec((MOE_TM, D_FF), lambda n, m, te, nx, nv, vr: (m, 0)),
                  pl.BlockSpec(memory_space=pl.ANY),
                  pl.BlockSpec((1, 1, GMM2_TN), lambda n, m, te, nx, nv, vr: (te[m], 0, 0))],
        out_specs=pl.BlockSpec((MOE_TM * PACK_ROWS, PACK_LANES),
                               lambda n, m, te, nx, nv, vr: (m, 0)),
        scratch_shapes=[pltpu.VMEM((2, D_FF, GMM2_TN), f32), pltpu.VMEM((D_FF, GMM2_TN), bf16),
                        pltpu.VMEM((MOE_TM, D_MODEL), f32), pltpu.SMEM((1,), i32),
                        pltpu.SemaphoreType.DMA((2,))])
    return pl.pallas_call(
        _gmm2_kernel,
        grid_spec=grid_spec,
        out_shape=jax.ShapeDtypeStruct((MOE_ROWS * PACK_ROWS, PACK_LANES), jnp.uint32),
        compiler_params=_cparams(("arbitrary", "arbitrary"), 56),
        name="gmm2",
    )(tile_expert, next_expert, n_valid, valid_rows, act, w_down,
      b_down.reshape(N_EXPERTS, 1, D_MODEL))


def _combine_kernel(pos_cur, pos_nxt, y_hbm, x1_ref, g2_ref, tw_ref, fg_ref, o_ref, buf, sem):
    i = pl.program_id(0)
    n = pl.num_programs(0)
    slot = i % 2

    def start_rows(pos_ref, s, rows):
        for r in rows:
            for k in range(TOP_K):
                _token_copy(y_hbm, pos_ref[0, k, r], buf.at[s, k], r, sem.at[s]).start(
                    priority=k % DMA_PRIORITIES)

    @pl.when(i == 0)
    def _():
        def body(g, carry):
            start_rows(pos_cur, 0, [g * (DMA_UNROLL // TOP_K) + u
                                    for u in range(DMA_UNROLL // TOP_K)])
            return carry
        lax.fori_loop(0, CMB_TB * TOP_K // DMA_UNROLL, body, 0)

    pltpu.make_async_copy(buf.at[slot], buf.at[slot], sem.at[slot]).wait()

    def combine_tile(prefetch_next):
        weights = [jnp.broadcast_to(tw_ref[:, k:k + 1], (CMB_TB, PACK_LANES))
                   for k in range(TOP_K)]
        sumsq = jnp.zeros((CMB_TB, 1), f32)
        rows_per_chunk = CMB_TB // PACK_ROWS
        for s in range(PACK_ROWS):
            if prefetch_next:
                start_rows(pos_nxt, 1 - slot, range(s * rows_per_chunk, (s + 1) * rows_per_chunk))
            lo, hi = _load_packed(buf.at[slot, 0], CMB_TB, s)
            lo, hi = lo * weights[0], hi * weights[0]
            for k in range(1, TOP_K):
                lo_k, hi_k = _load_packed(buf.at[slot, k], CMB_TB, s)
                lo, hi = lo + lo_k * weights[k], hi + hi_k * weights[k]
            for c, moe in ((2 * s, lo), (2 * s + 1, hi)):
                cols = slice(c * PACK_LANES, (c + 1) * PACK_LANES)
                x2 = x1_ref[:, cols] + g2_ref[0, :, cols] * moe
                o_ref[:, cols] = x2
                sumsq = sumsq + jnp.sum(x2 * x2, axis=-1, keepdims=True)
        inv = lax.rsqrt(sumsq * (1.0 / D_MODEL) + EPS)
        o_ref[...] = o_ref[...] * inv * fg_ref[...]

    @pl.when(i + 1 < n)
    def _():
        combine_tile(True)

    @pl.when(i + 1 >= n)
    def _():
        combine_tile(False)


def _combine(pos, y_packed, x1, gate2, top_w_rows, final_g):
    nt = TOKENS // CMB_TB
    tiles_per_batch = SEQ // CMB_TB
    pos3 = pos.reshape(TOP_K, nt, CMB_TB).transpose(1, 0, 2)
    return pl.pallas_call(
        _combine_kernel,
        grid=(nt,),
        in_specs=[pl.BlockSpec((1, TOP_K, CMB_TB), lambda i: (i, 0, 0), memory_space=pltpu.SMEM),
                  pl.BlockSpec((1, TOP_K, CMB_TB), lambda i: (jnp.minimum(i + 1, nt - 1), 0, 0),
                               memory_space=pltpu.SMEM),
                  pl.BlockSpec(memory_space=pl.ANY),
                  pl.BlockSpec((CMB_TB, D_MODEL), lambda i: (i, 0)),
                  pl.BlockSpec((1, 1, D_MODEL), lambda i: (i // tiles_per_batch, 0, 0)),
                  pl.BlockSpec((CMB_TB, TOP_K), lambda i: (i, 0)),
                  pl.BlockSpec((1, D_MODEL), lambda i: (0, 0))],
        out_specs=pl.BlockSpec((CMB_TB, D_MODEL), lambda i: (i, 0)),
        out_shape=jax.ShapeDtypeStruct((TOKENS, D_MODEL), f32),
        scratch_shapes=[pltpu.VMEM((2, TOP_K, CMB_TB * PACK_ROWS, PACK_LANES), jnp.uint32),
                        pltpu.SemaphoreType.DMA((2,))],
        compiler_params=_cparams(("arbitrary",), 32),
        name="combine",
    )(pos3, pos3, y_packed.reshape(MOE_ROWS, PACK_ROWS, PACK_LANES), x1,
      gate2.reshape(BATCH, 1, D_MODEL), top_w_rows, final_g.reshape(1, D_MODEL))


def _routing_tables(counts, top_i, rank):
    padded = (counts + MOE_TM - 1) // MOE_TM * MOE_TM
    ends = jnp.cumsum(padded)
    starts = ends - padded
    n_valid = ends[-1] // MOE_TM
    tile_ids = jnp.minimum(jnp.arange(MOE_NT, dtype=i32), n_valid - 1)
    tile_expert = jnp.sum(tile_ids[:, None] >= (ends // MOE_TM)[None, :], axis=1).astype(i32)
    experts = jnp.arange(N_EXPERTS, dtype=i32)
    pos = rank + jnp.sum(jnp.where(top_i[..., None] == experts, starts, 0), axis=-1)
    pad_start = (starts + counts).astype(i32)
    pad_len = (padded - counts).astype(i32)
    later = (experts[None, :] > tile_expert[:, None]) & (counts > 0)[None, :]
    next_expert = jnp.min(jnp.where(later, experts[None, :], N_EXPERTS), axis=1)
    next_expert = jnp.where(next_expert == N_EXPERTS, -1, next_expert).astype(i32)
    block_ids = jnp.arange(MOE_NT, dtype=i32)
    group_end = jnp.sum(jnp.where(tile_expert[:, None] == experts, pad_start, 0), axis=-1)
    valid_rows = jnp.clip(group_end - block_ids * MOE_TM, 0, MOE_TM)
    valid_rows = jnp.where(block_ids < n_valid, valid_rows, 0).astype(i32)
    return (tile_expert, next_expert, n_valid.reshape(1).astype(i32), valid_rows, pos.astype(i32),
            pad_start, pad_len)


def kernel(x, c, w_ada, b_ada, norm1_g, w_in, lambda_q1, lambda_k1, lambda_q2, lambda_k2, subln_g,
           rel_bias, conv_w, conv_b, conv_ln_g, conv_ln_b, w_out, norm2_g, w_router, b_router,
           w_gate_up, b_gate_up, w_down, b_down, final_g):
    assert x.shape == (BATCH, SEQ, D_MODEL) and w_ada.shape[0] == 1
    l = 0
    x2d = x.reshape(TOKENS, D_MODEL)
    mod = _adaln(c, w_ada[l], b_ada[l])
    shift1, scale1, gate1, shift2, scale2, gate2 = [mod[i] for i in range(6)]

    proj = _inproj(x2d, norm1_g[l], scale1, shift1, w_in[l])
    att = _attention(proj, _bias_band(rel_bias), lambda_q1[l], lambda_k1[l], lambda_q2[l],
                     lambda_k2[l], subln_g[l])
    cv = _conv(proj, conv_w[l], conv_b[l], conv_ln_g[l], conv_ln_b[l])

    x1, h2_packed, top_i, top_w, rank, counts = _outproj(
        att, cv, w_out[l].astype(bf16), x2d, gate1, scale2, shift2, norm2_g[l], w_router[l],
        b_router[l])
    tile_expert, next_expert, n_valid, valid_rows, pos, pad_start, pad_len = _routing_tables(
        counts[:, 0], top_i, rank)

    xs = _dispatch(pad_start, pad_len, n_valid, pos, h2_packed)
    act = _gmm1(tile_expert, next_expert, n_valid, valid_rows, xs, w_gate_up[l], b_gate_up[l])
    y = _gmm2(tile_expert, next_expert, n_valid, valid_rows, act, w_down[l], b_down[l])
    out = _combine(pos, y, x1, gate2, top_w.T, final_g)
    return out.reshape(BATCH, SEQ, D_MODEL)
```

```python
import math

import jax
import jax.numpy as jnp
from jax import lax
from jax.experimental import pallas as pl
from jax.experimental.pallas import tpu as pltpu

f32 = jnp.float32
bf16 = jnp.bfloat16
i32 = jnp.int32

D_MODEL = 2048
BATCH = 4
SEQ = 2048
TOKENS = BATCH * SEQ
ATT_HEADS = 8
ATT_HALF_DIM = 64
ATT_V_DIM = 128
ATT_WIDTH = ATT_HEADS * ATT_V_DIM
CONV_CH = D_MODEL - ATT_WIDTH
CONV_KERNEL = 31
CONV_HALF = CONV_KERNEL // 2
QK_COLS = ATT_HEADS * 2 * ATT_HALF_DIM
IN_COLS = 2 * QK_COLS + ATT_WIDTH + 2 * CONV_CH
N_BUCKETS = 32
MAX_DISTANCE = 128
N_EXPERTS = 32
TOP_K = 4
D_FF = D_MODEL
SWIGLU_LIMIT = 7.0
SWIGLU_ALPHA = 1.702
EPS = 1e-6
LAMBDA_INIT = 0.8 - 0.6 * math.exp(-0.3 * 0)
LOG2E = math.log2(math.e)
Q_SCALE = LOG2E * ATT_HALF_DIM ** -0.5

ADA_TN = 1024
INP_TM = 1024
INP_TN = 1024
ATT_TQ = 2048
ATT_SUB = 256
ATT_KC = 256
ATT_EXT_CHUNKS = 5
CONV_TS = 1024
CONV_HALO = 16
CONV_RB = 32
CONV_LN_ROWS = 128
OUT_TM = 512
MOE_TM = 512
MOE_ROW_STEP = 128
MOE_NT = TOKENS * TOP_K // MOE_TM + N_EXPERTS
MOE_ROWS = MOE_NT * MOE_TM
GMM1_TN = 1024
GMM2_TN = 2048
DSP_TB = 512
CMB_TB = 256
ROW_CHUNK = 64
DMA_UNROLL = 8
DMA_PRIORITIES = 2
WEIGHT_DMA_PRIORITY = 1

_MIB = 1024 * 1024
LANES = 128


def _cparams(sem, vmem_mib):
    return pltpu.CompilerParams(dimension_semantics=sem, vmem_limit_bytes=vmem_mib * _MIB)


PACK_ROWS = 8
PACK_LANES = LANES
_HI_MASK = 0xFFFF0000


def _pack_pair(lo, hi):
    lo_b = lax.shift_right_logical(
        lax.bitcast_convert_type(lo.astype(bf16).astype(f32), jnp.uint32), jnp.uint32(16))
    hi_b = lax.bitcast_convert_type(hi.astype(bf16).astype(f32), jnp.uint32) & jnp.uint32(_HI_MASK)
    return lo_b | hi_b


def _unpack_pair(w):
    lo = lax.bitcast_convert_type(lax.shift_left(w, jnp.uint32(16)), f32)
    hi = lax.bitcast_convert_type(w & jnp.uint32(_HI_MASK), f32)
    return lo, hi


def _store_packed(dst_ref, token0, n_tokens, get_cols):
    for s in range(PACK_ROWS):
        lo = get_cols(slice(2 * s * PACK_LANES, (2 * s + 1) * PACK_LANES))
        hi = get_cols(slice((2 * s + 1) * PACK_LANES, (2 * s + 2) * PACK_LANES))
        dst_ref[pl.ds(token0 * PACK_ROWS + s, n_tokens, stride=PACK_ROWS), :] = _pack_pair(lo, hi)


def _load_packed(src_ref, n_tokens, s):
    return _unpack_pair(src_ref[pl.ds(s, n_tokens, stride=PACK_ROWS), :])


def _row_chunks(n_rows, chunk, body):
    def step(i, carry):
        body(pl.multiple_of(i * chunk, chunk))
        return carry
    lax.fori_loop(0, n_rows // chunk, step, 0)


def _adaln_kernel(c_ref, w_ref, b_ref, o_ref):
    c = c_ref[...]
    cs = c * jax.nn.sigmoid(c)
    o_ref[0] = jnp.dot(cs.astype(bf16), w_ref[...].astype(bf16),
                       preferred_element_type=f32) + b_ref[...]


def _adaln(c, w_ada, b_ada):
    n = w_ada.shape[1]
    per_vec = D_MODEL // ADA_TN
    return pl.pallas_call(
        _adaln_kernel,
        grid=(n // ADA_TN,),
        in_specs=[pl.BlockSpec((BATCH, D_MODEL), lambda j: (0, 0)),
                  pl.BlockSpec((D_MODEL, ADA_TN), lambda j: (0, j)),
                  pl.BlockSpec((1, ADA_TN), lambda j: (0, j))],
        out_specs=pl.BlockSpec((1, BATCH, ADA_TN), lambda j: (j // per_vec, 0, j % per_vec)),
        out_shape=jax.ShapeDtypeStruct((n // D_MODEL, BATCH, D_MODEL), f32),
        compiler_params=_cparams(("arbitrary",), 40),
        name="adaln",
    )(c, w_ada, b_ada.reshape(1, n))


def _inproj_kernel(x_ref, g_ref, sc_ref, sh_ref, w_ref, o_ref, h_scr):
    j = pl.program_id(1)

    def normalize():
        g = g_ref[...]
        one_plus_scale = 1.0 + sc_ref[0]
        shift = sh_ref[0]
        for r0 in range(0, INP_TM, ROW_CHUNK):
            rows = slice(r0, r0 + ROW_CHUNK)
            x = x_ref[rows, :]
            inv = lax.rsqrt(jnp.mean(x * x, axis=-1, keepdims=True) + EPS)
            for c in range(D_MODEL // LANES):
                cols = slice(c * LANES, (c + 1) * LANES)
                y = x_ref[rows, cols] * inv * g[:, cols]
                h_scr[rows, cols] = (y * one_plus_scale[:, cols] + shift[:, cols]).astype(bf16)

    def project():
        out_scale = jnp.where(j < QK_COLS // INP_TN, Q_SCALE, 1.0).astype(f32)
        acc = jnp.dot(h_scr[...], w_ref[...].astype(bf16), preferred_element_type=f32)
        o_ref[...] = (acc * out_scale).astype(bf16)

    @pl.when(j == 0)
    def _():
        normalize()
        project()

    @pl.when(j != 0)
    def _():
        project()


def _inproj(x2d, norm_g, scale, shift, w_in):
    assert QK_COLS % INP_TN == 0
    tiles_per_batch = SEQ // INP_TM
    return pl.pallas_call(
        _inproj_kernel,
        grid=(TOKENS // INP_TM, IN_COLS // INP_TN),
        in_specs=[pl.BlockSpec((INP_TM, D_MODEL), lambda i, j: (i, 0)),
                  pl.BlockSpec((1, D_MODEL), lambda i, j: (0, 0)),
                  pl.BlockSpec((1, 1, D_MODEL), lambda i, j: (i // tiles_per_batch, 0, 0)),
                  pl.BlockSpec((1, 1, D_MODEL), lambda i, j: (i // tiles_per_batch, 0, 0)),
                  pl.BlockSpec((D_MODEL, INP_TN), lambda i, j: (0, j))],
        out_specs=pl.BlockSpec((INP_TM, INP_TN), lambda i, j: (i, j)),
        out_shape=jax.ShapeDtypeStruct((TOKENS, IN_COLS), bf16),
        scratch_shapes=[pltpu.VMEM((INP_TM, D_MODEL), bf16)],
        compiler_params=_cparams(("parallel", "arbitrary"), 48),
        name="inproj",
    )(x2d, norm_g.reshape(1, D_MODEL), scale.reshape(BATCH, 1, D_MODEL),
      shift.reshape(BATCH, 1, D_MODEL), w_in)


def _t5_bucket(rel):
    half = N_BUCKETS // 2
    max_exact = half // 2
    ret = jnp.where(rel > 0, half, 0)
    n = jnp.abs(rel)
    nf = jnp.maximum(n, 1).astype(f32)
    large = max_exact + (jnp.log(nf / max_exact) / math.log(MAX_DISTANCE / max_exact)
                         * (half - max_exact)).astype(i32)
    large = jnp.minimum(large, half - 1)
    return ret + jnp.where(n < max_exact, n, large)


def _bias_band(rel_bias):
    assert ATT_SUB == ATT_KC and ATT_EXT_CHUNKS == 5 and ATT_KC >= MAX_DISTANCE
    r = jnp.arange(ATT_SUB)[:, None]
    j = jnp.arange(ATT_EXT_CHUNKS * ATT_KC)[None, :]
    rel = j - 2 * ATT_KC - r
    onehot = (_t5_bucket(rel)[..., None] == jnp.arange(N_BUCKETS)).astype(f32)
    return jnp.einsum("rjn,nh->hrj", onehot, rel_bias.astype(f32) * LOG2E,
                      precision=lax.Precision.HIGHEST)


def _attn_kernel(q_ref, k_ref, v_ref, band_ref, lq1_ref, lk1_ref, lq2_ref, lk2_ref, sg_ref,
                 o_ref, va_scr):
    t = pl.program_id(2)
    n_chunks = SEQ // ATT_KC

    @pl.when(t == 0)
    def _():
        va_scr[:, :ATT_V_DIM] = v_ref[...]
        va_scr[:, ATT_V_DIM:] = jnp.ones((SEQ, ATT_V_DIM), bf16)

    lam = (jnp.exp(jnp.sum(lq1_ref[...] * lk1_ref[...], axis=-1, keepdims=True))
           - jnp.exp(jnp.sum(lq2_ref[...] * lk2_ref[...], axis=-1, keepdims=True))
           + LAMBDA_INIT)
    def band_offset(rows, c):
        block = t * (ATT_TQ // ATT_SUB) + rows.start // ATT_SUB
        d = jnp.clip(c - block + 2, 0, ATT_EXT_CHUNKS - 1)
        return pl.multiple_of(d * ATT_KC, ATT_KC)

    def scores(qm, rows):
        chunks = []
        m = jnp.full((ATT_SUB, 1), -jnp.inf, f32)
        for c in range(n_chunks):
            s = lax.dot_general(qm, k_ref[c * ATT_KC:(c + 1) * ATT_KC, :],
                                (((1,), (1,)), ((), ())), preferred_element_type=f32)
            s = s + band_ref[0, :, pl.ds(band_offset(rows, c), ATT_KC)]
            chunks.append(s)
            m = jnp.maximum(m, jnp.max(s, axis=-1, keepdims=True))
        return chunks, m

    def weighted_values(chunks, m):
        o = jnp.zeros((ATT_SUB, 2 * ATT_V_DIM), f32)
        for c in range(n_chunks):
            p = jnp.exp2(chunks[c] - m)
            o = o + jnp.dot(p.astype(bf16), va_scr[c * ATT_KC:(c + 1) * ATT_KC, :],
                            preferred_element_type=f32)
        return o[:, :ATT_V_DIM] / o[:, ATT_V_DIM:]

    def finish(rows, o1, o2):
        out = o1 - lam * o2
        ms = jnp.mean(out * out, axis=-1, keepdims=True)
        y = out * lax.rsqrt(ms + EPS) * sg_ref[...]
        o_ref[rows, :] = (y * (1.0 - LAMBDA_INIT)).astype(bf16)

    units = []
    for r in range(ATT_TQ // ATT_SUB):
        rows = slice(r * ATT_SUB, (r + 1) * ATT_SUB)
        q = q_ref[rows, :]
        lane = lax.broadcasted_iota(i32, q.shape, 1)
        zero = jnp.zeros_like(q)
        units.append((rows, jnp.where(lane < ATT_HALF_DIM, q, zero)))
        units.append((rows, jnp.where(lane >= ATT_HALF_DIM, q, zero)))
    first_map = {}
    pending = None
    for unit in units + [None]:
        current = (unit[0],) + scores(unit[1], unit[0]) if unit is not None else None
        if pending is not None:
            p_rows, p_chunks, p_m = pending
            o = weighted_values(p_chunks, p_m)
            if p_rows.start in first_map:
                finish(p_rows, first_map.pop(p_rows.start), o)
            else:
                first_map[p_rows.start] = o
        pending = current


def _attention(proj, band, lq1, lk1, lq2, lk2, subln_g):
    qt = SEQ // ATT_TQ
    k_col0 = QK_COLS // ATT_V_DIM
    v_col0 = 2 * QK_COLS // ATT_V_DIM
    vec = lambda n: pl.BlockSpec((1, n), lambda b, h, t: (0, 0))
    return pl.pallas_call(
        _attn_kernel,
        grid=(BATCH, ATT_HEADS, qt),
        in_specs=[pl.BlockSpec((ATT_TQ, ATT_V_DIM), lambda b, h, t: (b * qt + t, h)),
                  pl.BlockSpec((SEQ, ATT_V_DIM), lambda b, h, t: (b, k_col0 + h)),
                  pl.BlockSpec((SEQ, ATT_V_DIM), lambda b, h, t: (b, v_col0 + h)),
                  pl.BlockSpec((1, ATT_SUB, ATT_EXT_CHUNKS * ATT_KC), lambda b, h, t: (h, 0, 0)),
                  vec(ATT_HALF_DIM), vec(ATT_HALF_DIM), vec(ATT_HALF_DIM), vec(ATT_HALF_DIM),
                  vec(ATT_V_DIM)],
        out_specs=pl.BlockSpec((ATT_TQ, ATT_V_DIM), lambda b, h, t: (b * qt + t, h)),
        out_shape=jax.ShapeDtypeStruct((TOKENS, ATT_WIDTH), bf16),
        scratch_shapes=[pltpu.VMEM((SEQ, 2 * ATT_V_DIM), bf16)],
        compiler_params=_cparams(("parallel", "parallel", "arbitrary"), 40),
        name="attn",
    )(proj, proj, proj, band, lq1.reshape(1, -1), lk1.reshape(1, -1), lq2.reshape(1, -1),
      lk2.reshape(1, -1), subln_g.reshape(1, -1))


def _conv_kernel(ap_ref, ac_ref, an_ref, gp_ref, gc_ref, gn_ref, w_ref, cb_ref, lg_ref, lb_ref,
                 o_ref, z_scr, c_scr):
    s = pl.program_id(1)
    last = pl.num_programs(1) - 1

    lanes = LANES
    n_lane_chunks = CONV_CH // lanes

    def store_z(rows, z):
        for lc in range(n_lane_chunks):
            z_scr[lc, rows, :] = z[:, lc * lanes:(lc + 1) * lanes]

    def glu(a, g):
        return a.astype(f32) * jax.nn.sigmoid(g.astype(f32))

    store_z(slice(0, CONV_HALO), jnp.where(s > 0, glu(ap_ref[...], gp_ref[...]), 0.0))
    store_z(slice(CONV_HALO + CONV_TS, CONV_TS + 2 * CONV_HALO),
            jnp.where(s < last, glu(an_ref[...], gn_ref[...]), 0.0))

    def glu_chunk(r0):
        rows = pl.ds(r0, ROW_CHUNK)
        store_z(pl.ds(CONV_HALO + r0, ROW_CHUNK), glu(ac_ref[rows, :], gc_ref[rows, :]))
    _row_chunks(CONV_TS, ROW_CHUNK, glu_chunk)

    tap0 = CONV_HALO - CONV_HALF

    def lane_chunk(lc, carry):
        cols = pl.ds(pl.multiple_of(lc * lanes, lanes), lanes)
        taps = [w_ref[j:j + 1, cols] for j in range(CONV_KERNEL)]
        bias = cb_ref[:, cols]
        for r0 in range(0, CONV_TS, CONV_RB):
            acc = jnp.zeros((CONV_RB, lanes), f32)
            for j in range(CONV_KERNEL):
                acc = acc + z_scr[lc, r0 + tap0 + j:r0 + tap0 + j + CONV_RB, :] * taps[j]
            c_scr[r0:r0 + CONV_RB, cols] = acc + bias
        return carry
    lax.fori_loop(0, n_lane_chunks, lane_chunk, 0)

    lg = lg_ref[...]
    lb = lb_ref[...]

    def ln_chunk(r0):
        c = c_scr[pl.ds(r0, CONV_LN_ROWS), :]
        mu = jnp.mean(c, axis=-1, keepdims=True)
        cc = c - mu
        var = jnp.mean(cc * cc, axis=-1, keepdims=True)
        y = cc * lax.rsqrt(var + EPS) * lg + lb
        o_ref[pl.ds(r0, CONV_LN_ROWS), :] = (y * jax.nn.sigmoid(y)).astype(bf16)
    _row_chunks(CONV_TS, CONV_LN_ROWS, ln_chunk)


def _conv(proj, conv_w, conv_b, ln_g, ln_b):
    st = SEQ // CONV_TS
    halo_per_tile = CONV_TS // CONV_HALO
    halo_per_seq = SEQ // CONV_HALO
    a_col = (2 * QK_COLS + ATT_WIDTH) // CONV_CH
    g_col = a_col + 1

    def cur(col):
        return pl.BlockSpec((CONV_TS, CONV_CH), lambda b, s: (b * st + s, col))

    def prev(col):
        return pl.BlockSpec(
            (CONV_HALO, CONV_CH),
            lambda b, s: (jnp.maximum(b * halo_per_seq + s * halo_per_tile - 1, 0), col))

    def nxt(col):
        return pl.BlockSpec(
            (CONV_HALO, CONV_CH),
            lambda b, s: (jnp.minimum(b * halo_per_seq + (s + 1) * halo_per_tile,
                                      TOKENS // CONV_HALO - 1), col))

    vec = pl.BlockSpec((1, CONV_CH), lambda b, s: (0, 0))
    return pl.pallas_call(
        _conv_kernel,
        grid=(BATCH, st),
        in_specs=[prev(a_col), cur(a_col), nxt(a_col), prev(g_col), cur(g_col), nxt(g_col),
                  pl.BlockSpec((CONV_KERNEL, CONV_CH), lambda b, s: (0, 0)), vec, vec, vec],
        out_specs=pl.BlockSpec((CONV_TS, CONV_CH), lambda b, s: (b * st + s, 0)),
        out_shape=jax.ShapeDtypeStruct((TOKENS, CONV_CH), bf16),
        scratch_shapes=[pltpu.VMEM((CONV_CH // LANES, CONV_TS + 2 * CONV_HALO, LANES), f32),
                        pltpu.VMEM((CONV_TS, CONV_CH), f32)],
        compiler_params=_cparams(("parallel", "arbitrary"), 32),
        name="conv",
    )(proj, proj, proj, proj, proj, proj, conv_w, conv_b.reshape(1, -1), ln_g.reshape(1, -1),
      ln_b.reshape(1, -1))


def _outproj_kernel(att_ref, cv_ref, wo_ref, x_ref, g1_ref, sc2_ref, sh2_ref, n2_ref, wr_ref,
                    br_ref, x1_ref, h2p_ref, ti_ref, tw_ref, rk_ref, cnt_ref, carry_scr, hi_scr,
                    lo_scr):
    @pl.when(pl.program_id(0) == 0)
    def _():
        carry_scr[...] = jnp.zeros_like(carry_scr)

    mixed = (jnp.dot(att_ref[...], wo_ref[0:ATT_WIDTH, :], preferred_element_type=f32)
             + jnp.dot(cv_ref[...], wo_ref[ATT_WIDTH:, :], preferred_element_type=f32))
    x1_ref[...] = x_ref[...] + g1_ref[0] * mixed

    n2 = n2_ref[...]
    one_plus_scale = 1.0 + sc2_ref[0]
    shift = sh2_ref[0]

    def split(v):
        hi = v.astype(bf16)
        return hi, (v - hi.astype(f32)).astype(bf16)

    for r0 in range(0, OUT_TM, ROW_CHUNK):
        rows = slice(r0, r0 + ROW_CHUNK)
        x1 = x1_ref[rows, :]
        inv = lax.rsqrt(jnp.mean(x1 * x1, axis=-1, keepdims=True) + EPS)
        for s in range(PACK_ROWS):
            pair = []
            for c in (2 * s, 2 * s + 1):
                cols = slice(c * PACK_LANES, (c + 1) * PACK_LANES)
                y = x1_ref[rows, cols] * inv * n2[:, cols]
                h2 = y * one_plus_scale[:, cols] + shift[:, cols]
                hi_scr[rows, cols], lo_scr[rows, cols] = split(h2)
                pair.append(h2)
            h2p_ref[pl.ds(r0 * PACK_ROWS + s, ROW_CHUNK, stride=PACK_ROWS), :] = _pack_pair(*pair)

    w_hi, w_lo = split(wr_ref[...])
    h_hi, h_lo = hi_scr[...], lo_scr[...]
    rows_x_rows = (((1,), (1,)), ((), ()))
    logits = (lax.dot_general(w_hi, h_hi, rows_x_rows, preferred_element_type=f32)
              + lax.dot_general(w_hi, h_lo, rows_x_rows, preferred_element_type=f32)
              + lax.dot_general(w_lo, h_hi, rows_x_rows, preferred_element_type=f32)
              + br_ref[...])
    eidx = lax.broadcasted_iota(i32, logits.shape, 0)
    vals = logits
    sels, tops = [], []
    for k in range(TOP_K):
        m = jnp.max(vals, axis=0, keepdims=True)
        idx = jnp.min(jnp.where(vals == m, eidx, N_EXPERTS), axis=0, keepdims=True)
        sel = eidx == idx
        ti_ref[k:k + 1, :] = idx
        sels.append(sel)
        tops.append(m)
        vals = jnp.where(sel, -jnp.inf, vals)
    exps = [jnp.exp(v - tops[0]) for v in tops]
    denom = exps[0] + exps[1] + exps[2] + exps[3]
    for k in range(TOP_K):
        tw_ref[k:k + 1, :] = exps[k] / denom

    onehot = jnp.where(sels[0] | sels[1] | sels[2] | sels[3], 1.0, 0.0)
    r_i = lax.broadcasted_iota(i32, (OUT_TM, OUT_TM), 0)
    c_i = lax.broadcasted_iota(i32, (OUT_TM, OUT_TM), 1)
    earlier = jnp.where(r_i < c_i, 1.0, 0.0).astype(bf16)
    before = jnp.dot(onehot.astype(bf16), earlier, preferred_element_type=f32) + carry_scr[...]
    for k in range(TOP_K):
        rk_ref[k:k + 1, :] = jnp.sum(jnp.where(sels[k], before, 0.0), axis=0,
                                     keepdims=True).astype(i32)
    total = carry_scr[...] + jnp.sum(onehot, axis=1, keepdims=True)
    carry_scr[...] = total
    cnt_ref[...] = jnp.broadcast_to(total, cnt_ref.shape).astype(i32)


def _outproj(att, cv, wo_bf, x2d, gate1, scale2, shift2, norm2_g, w_router, b_router):
    tiles_per_batch = SEQ // OUT_TM
    row = lambda w: pl.BlockSpec((OUT_TM, w), lambda i: (i, 0))
    mod = pl.BlockSpec((1, 1, D_MODEL), lambda i: (i // tiles_per_batch, 0, 0))
    slot = pl.BlockSpec((TOP_K, OUT_TM), lambda i: (0, i))
    return pl.pallas_call(
        _outproj_kernel,
        grid=(TOKENS // OUT_TM,),
        in_specs=[row(ATT_WIDTH), row(CONV_CH),
                  pl.BlockSpec((D_MODEL, D_MODEL), lambda i: (0, 0), pipeline_mode=pl.Buffered(1)),
                  row(D_MODEL), mod, mod, mod,
                  pl.BlockSpec((1, D_MODEL), lambda i: (0, 0)),
                  pl.BlockSpec((N_EXPERTS, D_MODEL), lambda i: (0, 0)),
                  pl.BlockSpec((N_EXPERTS, 1), lambda i: (0, 0))],
        out_specs=[row(D_MODEL),
                   pl.BlockSpec((OUT_TM * PACK_ROWS, PACK_LANES), lambda i: (i, 0)),
                   slot, slot, slot,
                   pl.BlockSpec((N_EXPERTS, LANES), lambda i: (0, 0))],
        out_shape=[jax.ShapeDtypeStruct((TOKENS, D_MODEL), f32),
                   jax.ShapeDtypeStruct((TOKENS * PACK_ROWS, PACK_LANES), jnp.uint32),
                   jax.ShapeDtypeStruct((TOP_K, TOKENS), i32),
                   jax.ShapeDtypeStruct((TOP_K, TOKENS), f32),
                   jax.ShapeDtypeStruct((TOP_K, TOKENS), i32),
                   jax.ShapeDtypeStruct((N_EXPERTS, LANES), i32)],
        scratch_shapes=[pltpu.VMEM((N_EXPERTS, 1), f32), pltpu.VMEM((OUT_TM, D_MODEL), bf16),
                        pltpu.VMEM((OUT_TM, D_MODEL), bf16)],
        compiler_params=_cparams(("arbitrary",), 52),
        name="outproj",
    )(att, cv, wo_bf, x2d, gate1.reshape(BATCH, 1, D_MODEL), scale2.reshape(BATCH, 1, D_MODEL),
      shift2.reshape(BATCH, 1, D_MODEL), norm2_g.reshape(1, D_MODEL), w_router.T,
      b_router.reshape(N_EXPERTS, 1))


def _token_copy(src_hbm, token, dst_buf, row, sem):
    dst = dst_buf.at[pl.ds(pl.multiple_of(row * PACK_ROWS, PACK_ROWS), PACK_ROWS), :]
    return pltpu.make_async_copy(src_hbm.at[token], dst, sem)


_PAD_BITS = tuple(range(MOE_TM.bit_length() - 2, -1, -1))


def _pad_fill_copies(pad_start, pad_len, zeros, xs_hbm, sem):
    pairs = []
    for b in _PAD_BITS:
        size = 1 << b
        off = pad_start + lax.shift_left(lax.shift_right_logical(pad_len, b + 1), b + 1)
        copy = pltpu.make_async_copy(zeros.at[pl.ds(0, size)], xs_hbm.at[pl.ds(off, size)], sem)
        pairs.append((lax.shift_right_logical(pad_len, b) & 1 == 1, copy))
    return pairs


def _dispatch_kernel(pad_start_ref, pad_len_ref, nv_ref, pos_ref, h2p_ref, xs_hbm, zeros, sem, zsem):
    i = pl.program_id(0)
    last = pl.num_programs(0) - 1

    def for_each_fill(act):
        def per_expert(e, carry):
            for cond, copy in _pad_fill_copies(pad_start_ref[e], pad_len_ref[e], zeros, xs_hbm, zsem):
                @pl.when(cond)
                def _(copy=copy):
                    act(copy)
            return carry
        lax.fori_loop(0, N_EXPERTS, per_expert, 0)

        def per_tile(t, carry):
            row0 = pl.multiple_of(t * MOE_TM, MOE_TM)
            act(pltpu.make_async_copy(zeros, xs_hbm.at[pl.ds(row0, MOE_TM)], zsem))
            return carry
        lax.fori_loop(nv_ref[0], MOE_NT, per_tile, 0)

    @pl.when(i == 0)
    def _():
        zeros[...] = jnp.zeros_like(zeros)
        for_each_fill(lambda copy: copy.start())

    def body(g, carry):
        for u in range(DMA_UNROLL // TOP_K):
            r = g * (DMA_UNROLL // TOP_K) + u
            for k in range(TOP_K):
                pltpu.make_async_copy(h2p_ref.at[r], xs_hbm.at[pos_ref[0, k, r]], sem).start(
                    priority=k % DMA_PRIORITIES)
        return carry
    lax.fori_loop(0, DSP_TB * TOP_K // DMA_UNROLL, body, 0)
    pltpu.make_async_copy(xs_hbm.at[pl.ds(0, DSP_TB * TOP_K)], xs_hbm.at[pl.ds(0, DSP_TB * TOP_K)],
                          sem).wait()

    @pl.when(i == last)
    def _():
        for_each_fill(lambda copy: copy.wait())


def _dispatch(pad_start, pad_len, n_valid, pos, h2_packed):
    nt = TOKENS // DSP_TB
    pos3 = pos.reshape(TOP_K, nt, DSP_TB).transpose(1, 0, 2)
    grid_spec = pltpu.PrefetchScalarGridSpec(
        num_scalar_prefetch=3,
        grid=(nt,),
        in_specs=[pl.BlockSpec((1, TOP_K, DSP_TB), lambda i, *_: (i, 0, 0), memory_space=pltpu.SMEM),
                  pl.BlockSpec((DSP_TB, PACK_ROWS, PACK_LANES), lambda i, *_: (i, 0, 0))],
        out_specs=pl.BlockSpec(memory_space=pl.ANY),
        scratch_shapes=[pltpu.VMEM((MOE_TM, PACK_ROWS, PACK_LANES), jnp.uint32),
                        pltpu.SemaphoreType.DMA(()), pltpu.SemaphoreType.DMA(())])
    return pl.pallas_call(
        _dispatch_kernel,
        grid_spec=grid_spec,
        out_shape=jax.ShapeDtypeStruct((MOE_ROWS, PACK_ROWS, PACK_LANES), jnp.uint32),
        compiler_params=_cparams(("arbitrary",), 24),
        name="dispatch",
    )(pad_start, pad_len, n_valid, pos3, h2_packed.reshape(TOKENS, PACK_ROWS, PACK_LANES))


def _weights_changed(te_ref, m):
    prev = te_ref[jnp.maximum(m - 1, 0)]
    return (m == 0) | (te_ref[m] != prev)


def _expert_tile_step(te_ref, nx_ref, vr_ref, n, m, n_passes, count_ref, copies_for, convert,
                      compute, write_zeros):
    first_expert = te_ref[0]
    first = _weights_changed(te_ref, m)
    rows_valid = vr_ref[m]

    def start(copies):
        for copy in copies:
            copy.start(priority=WEIGHT_DMA_PRIORITY)

    @pl.when((n == 0) & (m == 0))
    def _():
        count_ref[0] = 0
        start(copies_for(0, first_expert, 0))

    def open_item():
        slot = count_ref[0] % 2
        nxt = nx_ref[m]

        @pl.when(nxt >= 0)
        def _():
            start(copies_for(n, nxt, 1 - slot))

        @pl.when((nxt < 0) & (n + 1 < n_passes))
        def _():
            start(copies_for(n + 1, first_expert, 1 - slot))

        for copy in copies_for(0, 0, slot):
            copy.wait()
        convert(slot)
        count_ref[0] = count_ref[0] + 1

    full = rows_valid > MOE_TM - MOE_ROW_STEP
    fused = first & full

    @pl.when(fused)
    def _():
        open_item()
        compute(MOE_TM)

    @pl.when(first & jnp.logical_not(full))
    def _():
        open_item()

    for n_rows in range(MOE_ROW_STEP, MOE_TM + 1, MOE_ROW_STEP):
        in_range = (rows_valid > n_rows - MOE_ROW_STEP) & (rows_valid <= n_rows)
        if n_rows == MOE_TM:
            in_range = in_range & jnp.logical_not(fused)

        @pl.when(in_range)
        def _(n_rows=n_rows):
            compute(n_rows)

    @pl.when(rows_valid <= 0)
    def _():
        write_zeros()


def _gmm1_kernel(te_ref, nx_ref, nv_ref, vr_ref, xs_ref, w_hbm, bg_ref, bu_ref, o_ref, stage, wg_bf,
                 wu_bf, count_ref, sem):
    n = pl.program_id(0)
    m = pl.program_id(1)

    def copies_for(p, e, slot):
        col = pl.multiple_of(p * GMM1_TN, GMM1_TN)
        return [pltpu.make_async_copy(w_hbm.at[e, :, pl.ds(col, GMM1_TN)], stage.at[slot, 0],
                                      sem.at[slot]),
                pltpu.make_async_copy(w_hbm.at[e, :, pl.ds(D_FF + col, GMM1_TN)],
                                      stage.at[slot, 1], sem.at[slot])]

    def convert(slot):
        wg_bf[...] = stage[slot, 0].astype(bf16)
        wu_bf[...] = stage[slot, 1].astype(bf16)

    def compute(n_rows):
        parts = []
        for s in range(PACK_ROWS):
            lo, hi = _load_packed(xs_ref, n_rows, s)
            parts += [lo.astype(bf16), hi.astype(bf16)]
        x = jnp.concatenate(parts, axis=1)
        gate = jnp.dot(x, wg_bf[...], preferred_element_type=f32) + bg_ref[0]
        up = jnp.dot(x, wu_bf[...], preferred_element_type=f32) + bu_ref[0]
        gate = jnp.minimum(gate, SWIGLU_LIMIT)
        up = jnp.clip(up, -SWIGLU_LIMIT, SWIGLU_LIMIT)
        act = (up + 1.0) * (gate * jax.nn.sigmoid(SWIGLU_ALPHA * gate))
        o_ref[0:n_rows, :] = act.astype(bf16)
        if n_rows < MOE_TM:
            o_ref[n_rows:, :] = jnp.zeros((MOE_TM - n_rows, GMM1_TN), bf16)

    def write_zeros():
        o_ref[...] = jnp.zeros_like(o_ref)

    _expert_tile_step(te_ref, nx_ref, vr_ref, n, m, pl.num_programs(0), count_ref, copies_for,
                      convert, compute, write_zeros)


def _gmm1(tile_expert, next_expert, n_valid, valid_rows, xs_packed, w_gate_up, b_gate_up):
    up0 = D_FF // GMM1_TN
    used = lambda m, nv: jnp.minimum(m, nv[0] - 1)
    grid_spec = pltpu.PrefetchScalarGridSpec(
        num_scalar_prefetch=4,
        grid=(D_FF // GMM1_TN, MOE_NT),
        in_specs=[pl.BlockSpec((MOE_TM * PACK_ROWS, PACK_LANES),
                               lambda n, m, te, nx, nv, vr: (used(m, nv), 0)),
                  pl.BlockSpec(memory_space=pl.ANY),
                  pl.BlockSpec((1, 1, GMM1_TN), lambda n, m, te, nx, nv, vr: (te[m], 0, n)),
                  pl.BlockSpec((1, 1, GMM1_TN), lambda n, m, te, nx, nv, vr: (te[m], 0, up0 + n))],
        out_specs=pl.BlockSpec((MOE_TM, GMM1_TN), lambda n, m, te, nx, nv, vr: (m, n)),
        scratch_shapes=[pltpu.VMEM((2, 2, D_MODEL, GMM1_TN), f32),
                        pltpu.VMEM((D_MODEL, GMM1_TN), bf16), pltpu.VMEM((D_MODEL, GMM1_TN), bf16),
                        pltpu.SMEM((1,), i32), pltpu.SemaphoreType.DMA((2,))])
    b3 = b_gate_up.reshape(N_EXPERTS, 1, 2 * D_FF)
    return pl.pallas_call(
        _gmm1_kernel,
        grid_spec=grid_spec,
        out_shape=jax.ShapeDtypeStruct((MOE_ROWS, D_FF), bf16),
        compiler_params=_cparams(("arbitrary", "arbitrary"), 54),
        name="gmm1",
    )(tile_expert, next_expert, n_valid, valid_rows,
      xs_packed.reshape(MOE_ROWS * PACK_ROWS, PACK_LANES), w_gate_up, b3, b3)


def _gmm2_kernel(te_ref, nx_ref, nv_ref, vr_ref, a_ref, w_hbm, bd_ref, o_ref, stage, wd_bf, y_scr,
                 count_ref, sem):
    def copies_for(p, e, slot):
        return [pltpu.make_async_copy(w_hbm.at[e], stage.at[slot], sem.at[slot])]

    def convert(slot):
        wd_bf[...] = stage[slot].astype(bf16)

    def compute(n_rows):
        y_scr[0:n_rows, :] = (jnp.dot(a_ref[0:n_rows, :], wd_bf[...], preferred_element_type=f32)
                              + bd_ref[0])
        _store_packed(o_ref, 0, n_rows, lambda cols: y_scr[0:n_rows, cols])
        if n_rows < MOE_TM:
            o_ref[n_rows * PACK_ROWS:, :] = jnp.zeros(((MOE_TM - n_rows) * PACK_ROWS, PACK_LANES),
                                                     jnp.uint32)

    def write_zeros():
        o_ref[...] = jnp.zeros_like(o_ref)

    _expert_tile_step(te_ref, nx_ref, vr_ref, pl.program_id(0), pl.program_id(1),
                      pl.num_programs(0), count_ref, copies_for, convert, compute, write_zeros)


def _gmm2(tile_expert, next_expert, n_valid, valid_rows, act, w_down, b_down):
    assert GMM2_TN == D_MODEL
    grid_spec = pltpu.PrefetchScalarGridSpec(
        num_scalar_prefetch=4,
        grid=(1, MOE_NT),
        in_specs=[pl.BlockSpec((MOE_TM, D_FF), lambda n, m, te, nx, nv, vr: (m, 0)),
                  pl.BlockSpec(memory_space=pl.ANY),
                  pl.BlockSpec((1, 1, GMM2_TN), lambda n, m, te, nx, nv, vr: (te[m], 0, 0))],
        out_specs=pl.BlockSpec((MOE_TM * PACK_ROWS, PACK_LANES),
                               lambda n, m, te, nx, nv, vr: (m, 0)),
        scratch_shapes=[pltpu.VMEM((2, D_FF, GMM2_TN), f32), pltpu.VMEM((D_FF, GMM2_TN), bf16),
                        pltpu.VMEM((MOE_TM, D_MODEL), f32), pltpu.SMEM((1,), i32),
                        pltpu.SemaphoreType.DMA((2,))])
    return pl.pallas_call(
        _gmm2_kernel,
        grid_spec=grid_spec,
        out_shape=jax.ShapeDtypeStruct((MOE_ROWS * PACK_ROWS, PACK_LANES), jnp.uint32),
        compiler_params=_cparams(("arbitrary", "arbitrary"), 56),
        name="gmm2",
    )(tile_expert, next_expert, n_valid, valid_rows, act, w_down,
      b_down.reshape(N_EXPERTS, 1, D_MODEL))


def _combine_kernel(pos_cur, pos_nxt, y_hbm, x1_ref, g2_ref, tw_ref, fg_ref, o_ref, buf, sem):
    i = pl.program_id(0)
    n = pl.num_programs(0)
    slot = i % 2

    def start_rows(pos_ref, s, rows):
        for r in rows:
            for k in range(TOP_K):
                _token_copy(y_hbm, pos_ref[0, k, r], buf.at[s, k], r, sem.at[s]).start(
                    priority=k % DMA_PRIORITIES)

    @pl.when(i == 0)
    def _():
        def body(g, carry):
            start_rows(pos_cur, 0, [g * (DMA_UNROLL // TOP_K) + u
                                    for u in range(DMA_UNROLL // TOP_K)])
            return carry
        lax.fori_loop(0, CMB_TB * TOP_K // DMA_UNROLL, body, 0)

    pltpu.make_async_copy(buf.at[slot], buf.at[slot], sem.at[slot]).wait()

    def combine_tile(prefetch_next):
        weights = [jnp.broadcast_to(tw_ref[:, k:k + 1], (CMB_TB, PACK_LANES))
                   for k in range(TOP_K)]
        sumsq = jnp.zeros((CMB_TB, 1), f32)
        rows_per_chunk = CMB_TB // PACK_ROWS
        for s in range(PACK_ROWS):
            if prefetch_next:
                start_rows(pos_nxt, 1 - slot, range(s * rows_per_chunk, (s + 1) * rows_per_chunk))
            lo, hi = _load_packed(buf.at[slot, 0], CMB_TB, s)
            lo, hi = lo * weights[0], hi * weights[0]
            for k in range(1, TOP_K):
                lo_k, hi_k = _load_packed(buf.at[slot, k], CMB_TB, s)
                lo, hi = lo + lo_k * weights[k], hi + hi_k * weights[k]
            for c, moe in ((2 * s, lo), (2 * s + 1, hi)):
                cols = slice(c * PACK_LANES, (c + 1) * PACK_LANES)
                x2 = x1_ref[:, cols] + g2_ref[0, :, cols] * moe
                o_ref[:, cols] = x2
                sumsq = sumsq + jnp.sum(x2 * x2, axis=-1, keepdims=True)
        inv = lax.rsqrt(sumsq * (1.0 / D_MODEL) + EPS)
        o_ref[...] = o_ref[...] * inv * fg_ref[...]

    @pl.when(i + 1 < n)
    def _():
        combine_tile(True)

    @pl.when(i + 1 >= n)
    def _():
        combine_tile(False)


def _combine(pos, y_packed, x1, gate2, top_w_rows, final_g):
    nt = TOKENS // CMB_TB
    tiles_per_batch = SEQ // CMB_TB
    pos3 = pos.reshape(TOP_K, nt, CMB_TB).transpose(1, 0, 2)
    return pl.pallas_call(
        _combine_kernel,
        grid=(nt,),
        in_specs=[pl.BlockSpec((1, TOP_K, CMB_TB), lambda i: (i, 0, 0), memory_space=pltpu.SMEM),
                  pl.BlockSpec((1, TOP_K, CMB_TB), lambda i: (jnp.minimum(i + 1, nt - 1), 0, 0),
                               memory_space=pltpu.SMEM),
                  pl.BlockSpec(memory_space=pl.ANY),
                  pl.BlockSpec((CMB_TB, D_MODEL), lambda i: (i, 0)),
                  pl.BlockSpec((1, 1, D_MODEL), lambda i: (i // tiles_per_batch, 0, 0)),
                  pl.BlockSpec((CMB_TB, TOP_K), lambda i: (i, 0)),
                  pl.BlockSpec((1, D_MODEL), lambda i: (0, 0))],
        out_specs=pl.BlockSpec((CMB_TB, D_MODEL), lambda i: (i, 0)),
        out_shape=jax.ShapeDtypeStruct((TOKENS, D_MODEL), f32),
        scratch_shapes=[pltpu.VMEM((2, TOP_K, CMB_TB * PACK_ROWS, PACK_LANES), jnp.uint32),
                        pltpu.SemaphoreType.DMA((2,))],
        compiler_params=_cparams(("arbitrary",), 32),
        name="combine",
    )(pos3, pos3, y_packed.reshape(MOE_ROWS, PACK_ROWS, PACK_LANES), x1,
      gate2.reshape(BATCH, 1, D_MODEL), top_w_rows, final_g.reshape(1, D_MODEL))


def _routing_tables(counts, top_i, rank):
    padded = (counts + MOE_TM - 1) // MOE_TM * MOE_TM
    ends = jnp.cumsum(padded)
    starts = ends - padded
    n_valid = ends[-1] // MOE_TM
    tile_ids = jnp.minimum(jnp.arange(MOE_NT, dtype=i32), n_valid - 1)
    tile_expert = jnp.sum(tile_ids[:, None] >= (ends // MOE_TM)[None, :], axis=1).astype(i32)
    experts = jnp.arange(N_EXPERTS, dtype=i32)
    pos = rank + jnp.sum(jnp.where(top_i[..., None] == experts, starts, 0), axis=-1)
    pad_start = (starts + counts).astype(i32)
    pad_len = (padded - counts).astype(i32)
    later = (experts[None, :] > tile_expert[:, None]) & (counts > 0)[None, :]
    next_expert = jnp.min(jnp.where(later, experts[None, :], N_EXPERTS), axis=1)
    next_expert = jnp.where(next_expert == N_EXPERTS, -1, next_expert).astype(i32)
    block_ids = jnp.arange(MOE_NT, dtype=i32)
    group_end = jnp.sum(jnp.where(tile_expert[:, None] == experts, pad_start, 0), axis=-1)
    valid_rows = jnp.clip(group_end - block_ids * MOE_TM, 0, MOE_TM)
    valid_rows = jnp.where(block_ids < n_valid, valid_rows, 0).astype(i32)
    return (tile_expert, next_expert, n_valid.reshape(1).astype(i32), valid_rows, pos.astype(i32),
            pad_start, pad_len)


def kernel(x, c, w_ada, b_ada, norm1_g, w_in, lambda_q1, lambda_k1, lambda_q2, lambda_k2, subln_g,
           rel_bias, conv_w, conv_b, conv_ln_g, conv_ln_b, w_out, norm2_g, w_router, b_router,
           w_gate_up, b_gate_up, w_down, b_down, final_g):
    assert x.shape == (BATCH, SEQ, D_MODEL) and w_ada.shape[0] == 1
    l = 0
    x2d = x.reshape(TOKENS, D_MODEL)
    mod = _adaln(c, w_ada[l], b_ada[l])
    shift1, scale1, gate1, shift2, scale2, gate2 = [mod[i] for i in range(6)]

    proj = _inproj(x2d, norm1_g[l], scale1, shift1, w_in[l])
    att = _attention(proj, _bias_band(rel_bias), lambda_q1[l], lambda_k1[l], lambda_q2[l],
                     lambda_k2[l], subln_g[l])
    cv = _conv(proj, conv_w[l], conv_b[l], conv_ln_g[l], conv_ln_b[l])

    x1, h2_packed, top_i, top_w, rank, counts = _outproj(
        att, cv, w_out[l].astype(bf16), x2d, gate1, scale2, shift2, norm2_g[l], w_router[l],
        b_router[l])
    tile_expert, next_expert, n_valid, valid_rows, pos, pad_start, pad_len = _routing_tables(
        counts[:, 0], top_i, rank)

    xs = _dispatch(pad_start, pad_len, n_valid, pos, h2_packed)
    act = _gmm1(tile_expert, next_expert, n_valid, valid_rows, xs, w_gate_up[l], b_gate_up[l])
    y = _gmm2(tile_expert, next_expert, n_valid, valid_rows, act, w_down[l], b_down[l])
    out = _combine(pos, y, x1, gate2, top_w.T, final_g)
    return out.reshape(BATCH, SEQ, D_MODEL)
```

```python
import math

import jax
import jax.numpy as jnp
from jax import lax
from jax.experimental import pallas as pl
from jax.experimental.pallas import tpu as pltpu

f32 = jnp.float32
bf16 = jnp.bfloat16
i32 = jnp.int32

D_MODEL = 2048
BATCH = 4
SEQ = 2048
TOKENS = BATCH * SEQ
ATT_HEADS = 8
ATT_HALF_DIM = 64
ATT_V_DIM = 128
ATT_WIDTH = ATT_HEADS * ATT_V_DIM
CONV_CH = D_MODEL - ATT_WIDTH
CONV_KERNEL = 31
CONV_HALF = CONV_KERNEL // 2
QK_COLS = ATT_HEADS * 2 * ATT_HALF_DIM
IN_COLS = 2 * QK_COLS + ATT_WIDTH + 2 * CONV_CH
N_BUCKETS = 32
MAX_DISTANCE = 128
N_EXPERTS = 32
TOP_K = 4
D_FF = D_MODEL
SWIGLU_LIMIT = 7.0
SWIGLU_ALPHA = 1.702
EPS = 1e-6
LAMBDA_INIT = 0.8 - 0.6 * math.exp(-0.3 * 0)
LOG2E = math.log2(math.e)
Q_SCALE = LOG2E * ATT_HALF_DIM ** -0.5

ADA_TN = 1024
INP_TM = 1024
INP_TN = 1024
ATT_TQ = 2048
ATT_SUB = 256
ATT_KC = 256
ATT_EXT_CHUNKS = 5
CONV_TS = 1024
CONV_HALO = 16
CONV_RB = 32
CONV_LN_ROWS = 128
OUT_TM = 512
MOE_TM = 512
MOE_ROW_STEP = 128
MOE_NT = TOKENS * TOP_K // MOE_TM + N_EXPERTS
MOE_ROWS = MOE_NT * MOE_TM
GMM1_TN = 1024
GMM2_TN = 2048
DSP_TB = 1024
CMB_TB = 512
ROW_CHUNK = 64
DMA_UNROLL = 8
DMA_PRIORITIES = 2
WEIGHT_DMA_PRIORITY = 1

_MIB = 1024 * 1024
LANES = 128


def _cparams(sem, vmem_mib):
    return pltpu.CompilerParams(dimension_semantics=sem, vmem_limit_bytes=vmem_mib * _MIB)


PACK_ROWS = 8
PACK_LANES = LANES
_HI_MASK = 0xFFFF0000


def _pack_pair(lo, hi):
    lo_b = lax.shift_right_logical(
        lax.bitcast_convert_type(lo.astype(bf16).astype(f32), jnp.uint32), jnp.uint32(16))
    hi_b = lax.bitcast_convert_type(hi.astype(bf16).astype(f32), jnp.uint32) & jnp.uint32(_HI_MASK)
    return lo_b | hi_b


def _unpack_pair(w):
    lo = lax.bitcast_convert_type(lax.shift_left(w, jnp.uint32(16)), f32)
    hi = lax.bitcast_convert_type(w & jnp.uint32(_HI_MASK), f32)
    return lo, hi


def _store_packed(dst_ref, token0, n_tokens, get_cols):
    for s in range(PACK_ROWS):
        lo = get_cols(slice(2 * s * PACK_LANES, (2 * s + 1) * PACK_LANES))
        hi = get_cols(slice((2 * s + 1) * PACK_LANES, (2 * s + 2) * PACK_LANES))
        dst_ref[pl.ds(token0 * PACK_ROWS + s, n_tokens, stride=PACK_ROWS), :] = _pack_pair(lo, hi)


def _load_packed(src_ref, n_tokens, s):
    return _unpack_pair(src_ref[pl.ds(s, n_tokens, stride=PACK_ROWS), :])


def _row_chunks(n_rows, chunk, body):
    def step(i, carry):
        body(pl.multiple_of(i * chunk, chunk))
        return carry
    lax.fori_loop(0, n_rows // chunk, step, 0)


def _adaln_kernel(c_ref, w_ref, b_ref, o_ref):
    c = c_ref[...]
    cs = c * jax.nn.sigmoid(c)
    o_ref[0] = jnp.dot(cs.astype(bf16), w_ref[...].astype(bf16),
                       preferred_element_type=f32) + b_ref[...]


def _adaln(c, w_ada, b_ada):
    n = w_ada.shape[1]
    per_vec = D_MODEL // ADA_TN
    return pl.pallas_call(
        _adaln_kernel,
        grid=(n // ADA_TN,),
        in_specs=[pl.BlockSpec((BATCH, D_MODEL), lambda j: (0, 0)),
                  pl.BlockSpec((D_MODEL, ADA_TN), lambda j: (0, j)),
                  pl.BlockSpec((1, ADA_TN), lambda j: (0, j))],
        out_specs=pl.BlockSpec((1, BATCH, ADA_TN), lambda j: (j // per_vec, 0, j % per_vec)),
        out_shape=jax.ShapeDtypeStruct((n // D_MODEL, BATCH, D_MODEL), f32),
        compiler_params=_cparams(("arbitrary",), 40),
        name="adaln",
    )(c, w_ada, b_ada.reshape(1, n))


def _inproj_kernel(x_ref, g_ref, sc_ref, sh_ref, w_ref, o_ref, h_scr):
    j = pl.program_id(1)

    def normalize():
        g = g_ref[...]
        one_plus_scale = 1.0 + sc_ref[0]
        shift = sh_ref[0]
        for r0 in range(0, INP_TM, ROW_CHUNK):
            rows = slice(r0, r0 + ROW_CHUNK)
            x = x_ref[rows, :]
            inv = lax.rsqrt(jnp.mean(x * x, axis=-1, keepdims=True) + EPS)
            for c in range(D_MODEL // LANES):
                cols = slice(c * LANES, (c + 1) * LANES)
                y = x_ref[rows, cols] * inv * g[:, cols]
                h_scr[rows, cols] = (y * one_plus_scale[:, cols] + shift[:, cols]).astype(bf16)

    def project():
        out_scale = jnp.where(j < QK_COLS // INP_TN, Q_SCALE, 1.0).astype(f32)
        acc = jnp.dot(h_scr[...], w_ref[...].astype(bf16), preferred_element_type=f32)
        o_ref[...] = (acc * out_scale).astype(bf16)

    @pl.when(j == 0)
    def _():
        normalize()
        project()

    @pl.when(j != 0)
    def _():
        project()


def _inproj(x2d, norm_g, scale, shift, w_in):
    assert QK_COLS % INP_TN == 0
    tiles_per_batch = SEQ // INP_TM
    return pl.pallas_call(
        _inproj_kernel,
        grid=(TOKENS // INP_TM, IN_COLS // INP_TN),
        in_specs=[pl.BlockSpec((INP_TM, D_MODEL), lambda i, j: (i, 0)),
                  pl.BlockSpec((1, D_MODEL), lambda i, j: (0, 0)),
                  pl.BlockSpec((1, 1, D_MODEL), lambda i, j: (i // tiles_per_batch, 0, 0)),
                  pl.BlockSpec((1, 1, D_MODEL), lambda i, j: (i // tiles_per_batch, 0, 0)),
                  pl.BlockSpec((D_MODEL, INP_TN), lambda i, j: (0, j))],
        out_specs=pl.BlockSpec((INP_TM, INP_TN), lambda i, j: (i, j)),
        out_shape=jax.ShapeDtypeStruct((TOKENS, IN_COLS), bf16),
        scratch_shapes=[pltpu.VMEM((INP_TM, D_MODEL), bf16)],
        compiler_params=_cparams(("parallel", "arbitrary"), 48),
        name="inproj",
    )(x2d, norm_g.reshape(1, D_MODEL), scale.reshape(BATCH, 1, D_MODEL),
      shift.reshape(BATCH, 1, D_MODEL), w_in)


def _t5_bucket(rel):
    half = N_BUCKETS // 2
    max_exact = half // 2
    ret = jnp.where(rel > 0, half, 0)
    n = jnp.abs(rel)
    nf = jnp.maximum(n, 1).astype(f32)
    large = max_exact + (jnp.log(nf / max_exact) / math.log(MAX_DISTANCE / max_exact)
                         * (half - max_exact)).astype(i32)
    large = jnp.minimum(large, half - 1)
    return ret + jnp.where(n < max_exact, n, large)


def _bias_band(rel_bias):
    assert ATT_SUB == ATT_KC and ATT_EXT_CHUNKS == 5 and ATT_KC >= MAX_DISTANCE
    r = jnp.arange(ATT_SUB)[:, None]
    j = jnp.arange(ATT_EXT_CHUNKS * ATT_KC)[None, :]
    rel = j - 2 * ATT_KC - r
    onehot = (_t5_bucket(rel)[..., None] == jnp.arange(N_BUCKETS)).astype(f32)
    return jnp.einsum("rjn,nh->hrj", onehot, rel_bias.astype(f32) * LOG2E,
                      precision=lax.Precision.HIGHEST)


def _attn_kernel(q_ref, k_ref, v_ref, band_ref, lq1_ref, lk1_ref, lq2_ref, lk2_ref, sg_ref,
                 o_ref, va_scr):
    t = pl.program_id(2)
    n_chunks = SEQ // ATT_KC

    @pl.when(t == 0)
    def _():
        va_scr[:, :ATT_V_DIM] = v_ref[...]
        va_scr[:, ATT_V_DIM:] = jnp.ones((SEQ, ATT_V_DIM), bf16)

    lam = (jnp.exp(jnp.sum(lq1_ref[...] * lk1_ref[...], axis=-1, keepdims=True))
           - jnp.exp(jnp.sum(lq2_ref[...] * lk2_ref[...], axis=-1, keepdims=True))
           + LAMBDA_INIT)
    def band_offset(rows, c):
        block = t * (ATT_TQ // ATT_SUB) + rows.start // ATT_SUB
        d = jnp.clip(c - block + 2, 0, ATT_EXT_CHUNKS - 1)
        return pl.multiple_of(d * ATT_KC, ATT_KC)

    def scores(qm, rows):
        chunks = []
        m = jnp.full((ATT_SUB, 1), -jnp.inf, f32)
        for c in range(n_chunks):
            s = lax.dot_general(qm, k_ref[c * ATT_KC:(c + 1) * ATT_KC, :],
                                (((1,), (1,)), ((), ())), preferred_element_type=f32)
            s = s + band_ref[0, :, pl.ds(band_offset(rows, c), ATT_KC)]
            chunks.append(s)
            m = jnp.maximum(m, jnp.max(s, axis=-1, keepdims=True))
        return chunks, m

    def weighted_values(chunks, m):
        o = jnp.zeros((ATT_SUB, 2 * ATT_V_DIM), f32)
        for c in range(n_chunks):
            p = jnp.exp2(chunks[c] - m)
            o = o + jnp.dot(p.astype(bf16), va_scr[c * ATT_KC:(c + 1) * ATT_KC, :],
                            preferred_element_type=f32)
        return o[:, :ATT_V_DIM] / o[:, ATT_V_DIM:]

    def finish(rows, o1, o2):
        out = o1 - lam * o2
        ms = jnp.mean(out * out, axis=-1, keepdims=True)
        y = out * lax.rsqrt(ms + EPS) * sg_ref[...]
        o_ref[rows, :] = (y * (1.0 - LAMBDA_INIT)).astype(bf16)

    units = []
    for r in range(ATT_TQ // ATT_SUB):
        rows = slice(r * ATT_SUB, (r + 1) * ATT_SUB)
        q = q_ref[rows, :]
        lane = lax.broadcasted_iota(i32, q.shape, 1)
        zero = jnp.zeros_like(q)
        units.append((rows, jnp.where(lane < ATT_HALF_DIM, q, zero)))
        units.append((rows, jnp.where(lane >= ATT_HALF_DIM, q, zero)))
    first_map = {}
    pending = None
    for unit in units + [None]:
        current = (unit[0],) + scores(unit[1], unit[0]) if unit is not None else None
        if pending is not None:
            p_rows, p_chunks, p_m = pending
            o = weighted_values(p_chunks, p_m)
            if p_rows.start in first_map:
                finish(p_rows, first_map.pop(p_rows.start), o)
            else:
                first_map[p_rows.start] = o
        pending = current


def _attention(proj, band, lq1, lk1, lq2, lk2, subln_g):
    qt = SEQ // ATT_TQ
    k_col0 = QK_COLS // ATT_V_DIM
    v_col0 = 2 * QK_COLS // ATT_V_DIM
    vec = lambda n: pl.BlockSpec((1, n), lambda b, h, t: (0, 0))
    return pl.pallas_call(
        _attn_kernel,
        grid=(BATCH, ATT_HEADS, qt),
        in_specs=[pl.BlockSpec((ATT_TQ, ATT_V_DIM), lambda b, h, t: (b * qt + t, h)),
                  pl.BlockSpec((SEQ, ATT_V_DIM), lambda b, h, t: (b, k_col0 + h)),
                  pl.BlockSpec((SEQ, ATT_V_DIM), lambda b, h, t: (b, v_col0 + h)),
                  pl.BlockSpec((1, ATT_SUB, ATT_EXT_CHUNKS * ATT_KC), lambda b, h, t: (h, 0, 0)),
                  vec(ATT_HALF_DIM), vec(ATT_HALF_DIM), vec(ATT_HALF_DIM), vec(ATT_HALF_DIM),
                  vec(ATT_V_DIM)],
        out_specs=pl.BlockSpec((ATT_TQ, ATT_V_DIM), lambda b, h, t: (b * qt + t, h)),
        out_shape=jax.ShapeDtypeStruct((TOKENS, ATT_WIDTH), bf16),
        scratch_shapes=[pltpu.VMEM((SEQ, 2 * ATT_V_DIM), bf16)],
        compiler_params=_cparams(("parallel", "parallel", "arbitrary"), 40),
        name="attn",
    )(proj, proj, proj, band, lq1.reshape(1, -1), lk1.reshape(1, -1), lq2.reshape(1, -1),
      lk2.reshape(1, -1), subln_g.reshape(1, -1))


def _conv_kernel(ap_ref, ac_ref, an_ref, gp_ref, gc_ref, gn_ref, w_ref, cb_ref, lg_ref, lb_ref,
                 o_ref, z_scr, c_scr):
    s = pl.program_id(1)
    last = pl.num_programs(1) - 1

    lanes = LANES
    n_lane_chunks = CONV_CH // lanes

    def store_z(rows, z):
        for lc in range(n_lane_chunks):
            z_scr[lc, rows, :] = z[:, lc * lanes:(lc + 1) * lanes]

    def glu(a, g):
        return a.astype(f32) * jax.nn.sigmoid(g.astype(f32))

    store_z(slice(0, CONV_HALO), jnp.where(s > 0, glu(ap_ref[...], gp_ref[...]), 0.0))
    store_z(slice(CONV_HALO + CONV_TS, CONV_TS + 2 * CONV_HALO),
            jnp.where(s < last, glu(an_ref[...], gn_ref[...]), 0.0))

    def glu_chunk(r0):
        rows = pl.ds(r0, ROW_CHUNK)
        store_z(pl.ds(CONV_HALO + r0, ROW_CHUNK), glu(ac_ref[rows, :], gc_ref[rows, :]))
    _row_chunks(CONV_TS, ROW_CHUNK, glu_chunk)

    tap0 = CONV_HALO - CONV_HALF

    def lane_chunk(lc, carry):
        cols = pl.ds(pl.multiple_of(lc * lanes, lanes), lanes)
        taps = [w_ref[j:j + 1, cols] for j in range(CONV_KERNEL)]
        bias = cb_ref[:, cols]
        for r0 in range(0, CONV_TS, CONV_RB):
            acc = jnp.zeros((CONV_RB, lanes), f32)
            for j in range(CONV_KERNEL):
                acc = acc + z_scr[lc, r0 + tap0 + j:r0 + tap0 + j + CONV_RB, :] * taps[j]
            c_scr[r0:r0 + CONV_RB, cols] = acc + bias
        return carry
    lax.fori_loop(0, n_lane_chunks, lane_chunk, 0)

    lg = lg_ref[...]
    lb = lb_ref[...]

    def ln_chunk(r0):
        c = c_scr[pl.ds(r0, CONV_LN_ROWS), :]
        mu = jnp.mean(c, axis=-1, keepdims=True)
        cc = c - mu
        var = jnp.mean(cc * cc, axis=-1, keepdims=True)
        y = cc * lax.rsqrt(var + EPS) * lg + lb
        o_ref[pl.ds(r0, CONV_LN_ROWS), :] = (y * jax.nn.sigmoid(y)).astype(bf16)
    _row_chunks(CONV_TS, CONV_LN_ROWS, ln_chunk)


def _conv(proj, conv_w, conv_b, ln_g, ln_b):
    st = SEQ // CONV_TS
    halo_per_tile = CONV_TS // CONV_HALO
    halo_per_seq = SEQ // CONV_HALO
    a_col = (2 * QK_COLS + ATT_WIDTH) // CONV_CH
    g_col = a_col + 1

    def cur(col):
        return pl.BlockSpec((CONV_TS, CONV_CH), lambda b, s: (b * st + s, col))

    def prev(col):
        return pl.BlockSpec(
            (CONV_HALO, CONV_CH),
            lambda b, s: (jnp.maximum(b * halo_per_seq + s * halo_per_tile - 1, 0), col))

    def nxt(col):
        return pl.BlockSpec(
            (CONV_HALO, CONV_CH),
            lambda b, s: (jnp.minimum(b * halo_per_seq + (s + 1) * halo_per_tile,
                                      TOKENS // CONV_HALO - 1), col))

    vec = pl.BlockSpec((1, CONV_CH), lambda b, s: (0, 0))
    return pl.pallas_call(
        _conv_kernel,
        grid=(BATCH, st),
        in_specs=[prev(a_col), cur(a_col), nxt(a_col), prev(g_col), cur(g_col), nxt(g_col),
                  pl.BlockSpec((CONV_KERNEL, CONV_CH), lambda b, s: (0, 0)), vec, vec, vec],
        out_specs=pl.BlockSpec((CONV_TS, CONV_CH), lambda b, s: (b * st + s, 0)),
        out_shape=jax.ShapeDtypeStruct((TOKENS, CONV_CH), bf16),
        scratch_shapes=[pltpu.VMEM((CONV_CH // LANES, CONV_TS + 2 * CONV_HALO, LANES), f32),
                        pltpu.VMEM((CONV_TS, CONV_CH), f32)],
        compiler_params=_cparams(("parallel", "arbitrary"), 32),
        name="conv",
    )(proj, proj, proj, proj, proj, proj, conv_w, conv_b.reshape(1, -1), ln_g.reshape(1, -1),
      ln_b.reshape(1, -1))


def _outproj_kernel(att_ref, cv_ref, wo_ref, x_ref, g1_ref, sc2_ref, sh2_ref, n2_ref, wr_ref,
                    br_ref, x1_ref, h2p_ref, ti_ref, tw_ref, rk_ref, cnt_ref, carry_scr, hi_scr,
                    lo_scr):
    @pl.when(pl.program_id(0) == 0)
    def _():
        carry_scr[...] = jnp.zeros_like(carry_scr)

    mixed = (jnp.dot(att_ref[...], wo_ref[0:ATT_WIDTH, :], preferred_element_type=f32)
             + jnp.dot(cv_ref[...], wo_ref[ATT_WIDTH:, :], preferred_element_type=f32))
    x1_ref[...] = x_ref[...] + g1_ref[0] * mixed

    n2 = n2_ref[...]
    one_plus_scale = 1.0 + sc2_ref[0]
    shift = sh2_ref[0]

    def split(v):
        hi = v.astype(bf16)
        return hi, (v - hi.astype(f32)).astype(bf16)

    for r0 in range(0, OUT_TM, ROW_CHUNK):
        rows = slice(r0, r0 + ROW_CHUNK)
        x1 = x1_ref[rows, :]
        inv = lax.rsqrt(jnp.mean(x1 * x1, axis=-1, keepdims=True) + EPS)
        for s in range(PACK_ROWS):
            pair = []
            for c in (2 * s, 2 * s + 1):
                cols = slice(c * PACK_LANES, (c + 1) * PACK_LANES)
                y = x1_ref[rows, cols] * inv * n2[:, cols]
                h2 = y * one_plus_scale[:, cols] + shift[:, cols]
                hi_scr[rows, cols], lo_scr[rows, cols] = split(h2)
                pair.append(h2)
            h2p_ref[pl.ds(r0 * PACK_ROWS + s, ROW_CHUNK, stride=PACK_ROWS), :] = _pack_pair(*pair)

    w_hi, w_lo = split(wr_ref[...])
    h_hi, h_lo = hi_scr[...], lo_scr[...]
    rows_x_rows = (((1,), (1,)), ((), ()))
    logits = (lax.dot_general(w_hi, h_hi, rows_x_rows, preferred_element_type=f32)
              + lax.dot_general(w_hi, h_lo, rows_x_rows, preferred_element_type=f32)
              + lax.dot_general(w_lo, h_hi, rows_x_rows, preferred_element_type=f32)
              + br_ref[...])
    eidx = lax.broadcasted_iota(i32, logits.shape, 0)
    vals = logits
    sels, tops = [], []
    for k in range(TOP_K):
        m = jnp.max(vals, axis=0, keepdims=True)
        idx = jnp.min(jnp.where(vals == m, eidx, N_EXPERTS), axis=0, keepdims=True)
        sel = eidx == idx
        ti_ref[k:k + 1, :] = idx
        sels.append(sel)
        tops.append(m)
        vals = jnp.where(sel, -jnp.inf, vals)
    exps = [jnp.exp(v - tops[0]) for v in tops]
    denom = exps[0] + exps[1] + exps[2] + exps[3]
    for k in range(TOP_K):
        tw_ref[k:k + 1, :] = exps[k] / denom

    onehot = jnp.where(sels[0] | sels[1] | sels[2] | sels[3], 1.0, 0.0)
    r_i = lax.broadcasted_iota(i32, (OUT_TM, OUT_TM), 0)
    c_i = lax.broadcasted_iota(i32, (OUT_TM, OUT_TM), 1)
    earlier = jnp.where(r_i < c_i, 1.0, 0.0).astype(bf16)
    before = jnp.dot(onehot.astype(bf16), earlier, preferred_element_type=f32) + carry_scr[...]
    for k in range(TOP_K):
        rk_ref[k:k + 1, :] = jnp.sum(jnp.where(sels[k], before, 0.0), axis=0,
                                     keepdims=True).astype(i32)
    total = carry_scr[...] + jnp.sum(onehot, axis=1, keepdims=True)
    carry_scr[...] = total
    cnt_ref[...] = jnp.broadcast_to(total, cnt_ref.shape).astype(i32)


def _outproj(att, cv, wo_bf, x2d, gate1, scale2, shift2, norm2_g, w_router, b_router):
    tiles_per_batch = SEQ // OUT_TM
    row = lambda w: pl.BlockSpec((OUT_TM, w), lambda i: (i, 0))
    mod = pl.BlockSpec((1, 1, D_MODEL), lambda i: (i // tiles_per_batch, 0, 0))
    slot = pl.BlockSpec((TOP_K, OUT_TM), lambda i: (0, i))
    return pl.pallas_call(
        _outproj_kernel,
        grid=(TOKENS // OUT_TM,),
        in_specs=[row(ATT_WIDTH), row(CONV_CH),
                  pl.BlockSpec((D_MODEL, D_MODEL), lambda i: (0, 0), pipeline_mode=pl.Buffered(1)),
                  row(D_MODEL), mod, mod, mod,
                  pl.BlockSpec((1, D_MODEL), lambda i: (0, 0)),
                  pl.BlockSpec((N_EXPERTS, D_MODEL), lambda i: (0, 0)),
                  pl.BlockSpec((N_EXPERTS, 1), lambda i: (0, 0))],
        out_specs=[row(D_MODEL),
                   pl.BlockSpec((OUT_TM * PACK_ROWS, PACK_LANES), lambda i: (i, 0)),
                   slot, slot, slot,
                   pl.BlockSpec((N_EXPERTS, LANES), lambda i: (0, 0))],
        out_shape=[jax.ShapeDtypeStruct((TOKENS, D_MODEL), f32),
                   jax.ShapeDtypeStruct((TOKENS * PACK_ROWS, PACK_LANES), jnp.uint32),
                   jax.ShapeDtypeStruct((TOP_K, TOKENS), i32),
                   jax.ShapeDtypeStruct((TOP_K, TOKENS), f32),
                   jax.ShapeDtypeStruct((TOP_K, TOKENS), i32),
                   jax.ShapeDtypeStruct((N_EXPERTS, LANES), i32)],
        scratch_shapes=[pltpu.VMEM((N_EXPERTS, 1), f32), pltpu.VMEM((OUT_TM, D_MODEL), bf16),
                        pltpu.VMEM((OUT_TM, D_MODEL), bf16)],
        compiler_params=_cparams(("arbitrary",), 52),
        name="outproj",
    )(att, cv, wo_bf, x2d, gate1.reshape(BATCH, 1, D_MODEL), scale2.reshape(BATCH, 1, D_MODEL),
      shift2.reshape(BATCH, 1, D_MODEL), norm2_g.reshape(1, D_MODEL), w_router.T,
      b_router.reshape(N_EXPERTS, 1))


def _token_copy(src_hbm, token, dst_buf, row, sem):
    dst = dst_buf.at[pl.ds(pl.multiple_of(row * PACK_ROWS, PACK_ROWS), PACK_ROWS), :]
    return pltpu.make_async_copy(src_hbm.at[token], dst, sem)


_PAD_BITS = tuple(range(MOE_TM.bit_length() - 2, -1, -1))


def _pad_fill_copies(pad_start, pad_len, zeros, xs_hbm, sem):
    pairs = []
    for b in _PAD_BITS:
        size = 1 << b
        off = pad_start + lax.shift_left(lax.shift_right_logical(pad_len, b + 1), b + 1)
        copy = pltpu.make_async_copy(zeros.at[pl.ds(0, size)], xs_hbm.at[pl.ds(off, size)], sem)
        pairs.append((lax.shift_right_logical(pad_len, b) & 1 == 1, copy))
    return pairs


def _dispatch_kernel(pad_start_ref, pad_len_ref, nv_ref, pos_ref, h2p_ref, xs_hbm, zeros, sem, zsem):
    i = pl.program_id(0)
    last = pl.num_programs(0) - 1

    def for_each_fill(act):
        def per_expert(e, carry):
            for cond, copy in _pad_fill_copies(pad_start_ref[e], pad_len_ref[e], zeros, xs_hbm, zsem):
                @pl.when(cond)
                def _(copy=copy):
                    act(copy)
            return carry
        lax.fori_loop(0, N_EXPERTS, per_expert, 0)

        def per_tile(t, carry):
            row0 = pl.multiple_of(t * MOE_TM, MOE_TM)
            act(pltpu.make_async_copy(zeros, xs_hbm.at[pl.ds(row0, MOE_TM)], zsem))
            return carry
        lax.fori_loop(nv_ref[0], MOE_NT, per_tile, 0)

    @pl.when(i == 0)
    def _():
        zeros[...] = jnp.zeros_like(zeros)
        for_each_fill(lambda copy: copy.start())

    def body(g, carry):
        for u in range(DMA_UNROLL // TOP_K):
            r = g * (DMA_UNROLL // TOP_K) + u
            for k in range(TOP_K):
                pltpu.make_async_copy(h2p_ref.at[r], xs_hbm.at[pos_ref[0, k, r]], sem).start(
                    priority=k % DMA_PRIORITIES)
        return carry
    lax.fori_loop(0, DSP_TB * TOP_K // DMA_UNROLL, body, 0)
    pltpu.make_async_copy(xs_hbm.at[pl.ds(0, DSP_TB * TOP_K)], xs_hbm.at[pl.ds(0, DSP_TB * TOP_K)],
                          sem).wait()

    @pl.when(i == last)
    def _():
        for_each_fill(lambda copy: copy.wait())


def _dispatch(pad_start, pad_len, n_valid, pos, h2_packed):
    nt = TOKENS // DSP_TB
    pos3 = pos.reshape(TOP_K, nt, DSP_TB).transpose(1, 0, 2)
    grid_spec = pltpu.PrefetchScalarGridSpec(
        num_scalar_prefetch=3,
        grid=(nt,),
        in_specs=[pl.BlockSpec((1, TOP_K, DSP_TB), lambda i, *_: (i, 0, 0), memory_space=pltpu.SMEM),
                  pl.BlockSpec((DSP_TB, PACK_ROWS, PACK_LANES), lambda i, *_: (i, 0, 0))],
        out_specs=pl.BlockSpec(memory_space=pl.ANY),
        scratch_shapes=[pltpu.VMEM((MOE_TM, PACK_ROWS, PACK_LANES), jnp.uint32),
                        pltpu.SemaphoreType.DMA(()), pltpu.SemaphoreType.DMA(())])
    return pl.pallas_call(
        _dispatch_kernel,
        grid_spec=grid_spec,
        out_shape=jax.ShapeDtypeStruct((MOE_ROWS, PACK_ROWS, PACK_LANES), jnp.uint32),
        compiler_params=_cparams(("arbitrary",), 24),
        name="dispatch",
    )(pad_start, pad_len, n_valid, pos3, h2_packed.reshape(TOKENS, PACK_ROWS, PACK_LANES))


def _weights_changed(te_ref, m):
    prev = te_ref[jnp.maximum(m - 1, 0)]
    return (m == 0) | (te_ref[m] != prev)


def _expert_tile_step(te_ref, nx_ref, vr_ref, n, m, n_passes, count_ref, copies_for, convert,
                      compute, write_zeros):
    first_expert = te_ref[0]
    first = _weights_changed(te_ref, m)
    rows_valid = vr_ref[m]

    def start(copies):
        for copy in copies:
            copy.start(priority=WEIGHT_DMA_PRIORITY)

    @pl.when((n == 0) & (m == 0))
    def _():
        count_ref[0] = 0
        start(copies_for(0, first_expert, 0))

    def open_item():
        slot = count_ref[0] % 2
        nxt = nx_ref[m]

        @pl.when(nxt >= 0)
        def _():
            start(copies_for(n, nxt, 1 - slot))

        @pl.when((nxt < 0) & (n + 1 < n_passes))
        def _():
            start(copies_for(n + 1, first_expert, 1 - slot))

        for copy in copies_for(0, 0, slot):
            copy.wait()
        convert(slot)
        count_ref[0] = count_ref[0] + 1

    full = rows_valid > MOE_TM - MOE_ROW_STEP
    fused = first & full

    @pl.when(fused)
    def _():
        open_item()
        compute(MOE_TM)

    @pl.when(first & jnp.logical_not(full))
    def _():
        open_item()

    for n_rows in range(MOE_ROW_STEP, MOE_TM + 1, MOE_ROW_STEP):
        in_range = (rows_valid > n_rows - MOE_ROW_STEP) & (rows_valid <= n_rows)
        if n_rows == MOE_TM:
            in_range = in_range & jnp.logical_not(fused)

        @pl.when(in_range)
        def _(n_rows=n_rows):
            compute(n_rows)

    @pl.when(rows_valid <= 0)
    def _():
        write_zeros()


def _gmm1_kernel(te_ref, nx_ref, nv_ref, vr_ref, xs_ref, w_hbm, bg_ref, bu_ref, o_ref, stage, wg_bf,
                 wu_bf, count_ref, sem):
    n = pl.program_id(0)
    m = pl.program_id(1)

    def copies_for(p, e, slot):
        col = pl.multiple_of(p * GMM1_TN, GMM1_TN)
        return [pltpu.make_async_copy(w_hbm.at[e, :, pl.ds(col, GMM1_TN)], stage.at[slot, 0],
                                      sem.at[slot]),
                pltpu.make_async_copy(w_hbm.at[e, :, pl.ds(D_FF + col, GMM1_TN)],
                                      stage.at[slot, 1], sem.at[slot])]

    def convert(slot):
        wg_bf[...] = stage[slot, 0].astype(bf16)
        wu_bf[...] = stage[slot, 1].astype(bf16)

    def compute(n_rows):
        parts = []
        for s in range(PACK_ROWS):
            lo, hi = _load_packed(xs_ref, n_rows, s)
            parts += [lo.astype(bf16), hi.astype(bf16)]
        x = jnp.concatenate(parts, axis=1)
        gate = jnp.dot(x, wg_bf[...], preferred_element_type=f32) + bg_ref[0]
        up = jnp.dot(x, wu_bf[...], preferred_element_type=f32) + bu_ref[0]
        gate = jnp.minimum(gate, SWIGLU_LIMIT)
        up = jnp.clip(up, -SWIGLU_LIMIT, SWIGLU_LIMIT)
        act = (up + 1.0) * (gate * jax.nn.sigmoid(SWIGLU_ALPHA * gate))
        o_ref[0:n_rows, :] = act.astype(bf16)
        if n_rows < MOE_TM:
            o_ref[n_rows:, :] = jnp.zeros((MOE_TM - n_rows, GMM1_TN), bf16)

    def write_zeros():
        o_ref[...] = jnp.zeros_like(o_ref)

    _expert_tile_step(te_ref, nx_ref, vr_ref, n, m, pl.num_programs(0), count_ref, copies_for,
                      convert, compute, write_zeros)


def _gmm1(tile_expert, next_expert, n_valid, valid_rows, xs_packed, w_gate_up, b_gate_up):
    up0 = D_FF // GMM1_TN
    used = lambda m, nv: jnp.minimum(m, nv[0] - 1)
    grid_spec = pltpu.PrefetchScalarGridSpec(
        num_scalar_prefetch=4,
        grid=(D_FF // GMM1_TN, MOE_NT),
        in_specs=[pl.BlockSpec((MOE_TM * PACK_ROWS, PACK_LANES),
                               lambda n, m, te, nx, nv, vr: (used(m, nv), 0)),
                  pl.BlockSpec(memory_space=pl.ANY),
                  pl.BlockSpec((1, 1, GMM1_TN), lambda n, m, te, nx, nv, vr: (te[m], 0, n)),
                  pl.BlockSpec((1, 1, GMM1_TN), lambda n, m, te, nx, nv, vr: (te[m], 0, up0 + n))],
        out_specs=pl.BlockSpec((MOE_TM, GMM1_TN), lambda n, m, te, nx, nv, vr: (m, n)),
        scratch_shapes=[pltpu.VMEM((2, 2, D_MODEL, GMM1_TN), f32),
                        pltpu.VMEM((D_MODEL, GMM1_TN), bf16), pltpu.VMEM((D_MODEL, GMM1_TN), bf16),
                        pltpu.SMEM((1,), i32), pltpu.SemaphoreType.DMA((2,))])
    b3 = b_gate_up.reshape(N_EXPERTS, 1, 2 * D_FF)
    return pl.pallas_call(
        _gmm1_kernel,
        grid_spec=grid_spec,
        out_shape=jax.ShapeDtypeStruct((MOE_ROWS, D_FF), bf16),
        compiler_params=_cparams(("arbitrary", "arbitrary"), 54),
        name="gmm1",
    )(tile_expert, next_expert, n_valid, valid_rows,
      xs_packed.reshape(MOE_ROWS * PACK_ROWS, PACK_LANES), w_gate_up, b3, b3)


def _gmm2_kernel(te_ref, nx_ref, nv_ref, vr_ref, a_ref, w_hbm, bd_ref, o_ref, stage, wd_bf, y_scr,
                 count_ref, sem):
    def copies_for(p, e, slot):
        return [pltpu.make_async_copy(w_hbm.at[e], stage.at[slot], sem.at[slot])]

    def convert(slot):
        wd_bf[...] = stage[slot].astype(bf16)

    def compute(n_rows):
        y_scr[0:n_rows, :] = (jnp.dot(a_ref[0:n_rows, :], wd_bf[...], preferred_element_type=f32)
                              + bd_ref[0])
        _store_packed(o_ref, 0, n_rows, lambda cols: y_scr[0:n_rows, cols])
        if n_rows < MOE_TM:
            o_ref[n_rows * PACK_ROWS:, :] = jnp.zeros(((MOE_TM - n_rows) * PACK_ROWS, PACK_LANES),
                                                     jnp.uint32)

    def write_zeros():
        o_ref[...] = jnp.zeros_like(o_ref)

    _expert_tile_step(te_ref, nx_ref, vr_ref, pl.program_id(0), pl.program_id(1),
                      pl.num_programs(0), count_ref, copies_for, convert, compute, write_zeros)


def _gmm2(tile_expert, next_expert, n_valid, valid_rows, act, w_down, b_down):
    assert GMM2_TN == D_MODEL
    grid_spec = pltpu.PrefetchScalarGridSpec(
        num_scalar_prefetch=4,
        grid=(1, MOE_NT),
        in_specs=[pl.BlockSpec((MOE_TM, D_FF), lambda n, m, te, nx, nv, vr: (m, 0)),
                  pl.BlockSpec(memory_space=pl.ANY),
                  pl.BlockSpec((1, 1, GMM2_TN), lambda n, m, te, nx, nv, vr: (te[m], 0, 0))],
        out_specs=pl.BlockSpec((MOE_TM * PACK_ROWS, PACK_LANES),
                               lambda n, m, te, nx, nv, vr: (m, 0)),
        scratch_shapes=[pltpu.VMEM((2, D_FF, GMM2_TN), f32), pltpu.VMEM((D_FF, GMM2_TN), bf16),
                        pltpu.VMEM((MOE_TM, D_MODEL), f32), pltpu.SMEM((1,), i32),
                        pltpu.SemaphoreType.DMA((2,))])
    return pl.pallas_call(
        _gmm2_kernel,
        grid_spec=grid_spec,
        out_shape=jax.ShapeDtypeStruct((MOE_ROWS * PACK_ROWS, PACK_LANES), jnp.uint32),
        compiler_params=_cparams(("arbitrary", "arbitrary"), 56),
        name="gmm2",
    )(tile_expert, next_expert, n_valid, valid_rows, act, w_down,
      b_down.reshape(N_EXPERTS, 1, D_MODEL))


def _combine_kernel(pos_cur, pos_nxt, y_hbm, x1_ref, g2_ref, tw_ref, fg_ref, o_ref, buf, sem):
    i = pl.program_id(0)
    n = pl.num_programs(0)
    slot = i % 2

    def start_rows(pos_ref, s, rows):
        for r in rows:
            for k in range(TOP_K):
                _token_copy(y_hbm, pos_ref[0, k, r], buf.at[s, k], r, sem.at[s]).start(
                    priority=k % DMA_PRIORITIES)

    @pl.when(i == 0)
    def _():
        def body(g, carry):
            start_rows(pos_cur, 0, [g * (DMA_UNROLL // TOP_K) + u
                                    for u in range(DMA_UNROLL // TOP_K)])
            return carry
        lax.fori_loop(0, CMB_TB * TOP_K // DMA_UNROLL, body, 0)

    pltpu.make_async_copy(buf.at[slot], buf.at[slot], sem.at[slot]).wait()

    def combine_tile(prefetch_next):
        weights = [jnp.broadcast_to(tw_ref[:, k:k + 1], (CMB_TB, PACK_LANES))
                   for k in range(TOP_K)]
        sumsq = jnp.zeros((CMB_TB, 1), f32)
        rows_per_chunk = CMB_TB // PACK_ROWS
        for s in range(PACK_ROWS):
            if prefetch_next:
                start_rows(pos_nxt, 1 - slot, range(s * rows_per_chunk, (s + 1) * rows_per_chunk))
            lo, hi = _load_packed(buf.at[slot, 0], CMB_TB, s)
            lo, hi = lo * weights[0], hi * weights[0]
            for k in range(1, TOP_K):
                lo_k, hi_k = _load_packed(buf.at[slot, k], CMB_TB, s)
                lo, hi = lo + lo_k * weights[k], hi + hi_k * weights[k]
            for c, moe in ((2 * s, lo), (2 * s + 1, hi)):
                cols = slice(c * PACK_LANES, (c + 1) * PACK_LANES)
                x2 = x1_ref[:, cols] + g2_ref[0, :, cols] * moe
                o_ref[:, cols] = x2
                sumsq = sumsq + jnp.sum(x2 * x2, axis=-1, keepdims=True)
        inv = lax.rsqrt(sumsq * (1.0 / D_MODEL) + EPS)
        o_ref[...] = o_ref[...] * inv * fg_ref[...]

    @pl.when(i + 1 < n)
    def _():
        combine_tile(True)

    @pl.when(i + 1 >= n)
    def _():
        combine_tile(False)


def _combine(pos, y_packed, x1, gate2, top_w_rows, final_g):
    nt = TOKENS // CMB_TB
    tiles_per_batch = SEQ // CMB_TB
    pos3 = pos.reshape(TOP_K, nt, CMB_TB).transpose(1, 0, 2)
    return pl.pallas_call(
        _combine_kernel,
        grid=(nt,),
        in_specs=[pl.BlockSpec((1, TOP_K, CMB_TB), lambda i: (i, 0, 0), memory_space=pltpu.SMEM),
                  pl.BlockSpec((1, TOP_K, CMB_TB), lambda i: (jnp.minimum(i + 1, nt - 1), 0, 0),
                               memory_space=pltpu.SMEM),
                  pl.BlockSpec(memory_space=pl.ANY),
                  pl.BlockSpec((CMB_TB, D_MODEL), lambda i: (i, 0)),
                  pl.BlockSpec((1, 1, D_MODEL), lambda i: (i // tiles_per_batch, 0, 0)),
                  pl.BlockSpec((CMB_TB, TOP_K), lambda i: (i, 0)),
                  pl.BlockSpec((1, D_MODEL), lambda i: (0, 0))],
        out_specs=pl.BlockSpec((CMB_TB, D_MODEL), lambda i: (i, 0)),
        out_shape=jax.ShapeDtypeStruct((TOKENS, D_MODEL), f32),
        scratch_shapes=[pltpu.VMEM((2, TOP_K, CMB_TB * PACK_ROWS, PACK_LANES), jnp.uint32),
                        pltpu.SemaphoreType.DMA((2,))],
        compiler_params=_cparams(("arbitrary",), 44),
        name="combine",
    )(pos3, pos3, y_packed.reshape(MOE_ROWS, PACK_ROWS, PACK_LANES), x1,
      gate2.reshape(BATCH, 1, D_MODEL), top_w_rows, final_g.reshape(1, D_MODEL))


def _routing_tables(counts, top_i, rank):
    padded = (counts + MOE_TM - 1) // MOE_TM * MOE_TM
    ends = jnp.cumsum(padded)
    starts = ends - padded
    n_valid = ends[-1] // MOE_TM
    tile_ids = jnp.minimum(jnp.arange(MOE_NT, dtype=i32), n_valid - 1)
    tile_expert = jnp.sum(tile_ids[:, None] >= (ends // MOE_TM)[None, :], axis=1).astype(i32)
    experts = jnp.arange(N_EXPERTS, dtype=i32)
    pos = rank + jnp.sum(jnp.where(top_i[..., None] == experts, starts, 0), axis=-1)
    pad_start = (starts + counts).astype(i32)
    pad_len = (padded - counts).astype(i32)
    later = (experts[None, :] > tile_expert[:, None]) & (counts > 0)[None, :]
    next_expert = jnp.min(jnp.where(later, experts[None, :], N_EXPERTS), axis=1)
    next_expert = jnp.where(next_expert == N_EXPERTS, -1, next_expert).astype(i32)
    block_ids = jnp.arange(MOE_NT, dtype=i32)
    group_end = jnp.sum(jnp.where(tile_expert[:, None] == experts, pad_start, 0), axis=-1)
    valid_rows = jnp.clip(group_end - block_ids * MOE_TM, 0, MOE_TM)
    valid_rows = jnp.where(block_ids < n_valid, valid_rows, 0).astype(i32)
    return (tile_expert, next_expert, n_valid.reshape(1).astype(i32), valid_rows, pos.astype(i32),
            pad_start, pad_len)


def kernel(x, c, w_ada, b_ada, norm1_g, w_in, lambda_q1, lambda_k1, lambda_q2, lambda_k2, subln_g,
           rel_bias, conv_w, conv_b, conv_ln_g, conv_ln_b, w_out, norm2_g, w_router, b_router,
           w_gate_up, b_gate_up, w_down, b_down, final_g):
    assert x.shape == (BATCH, SEQ, D_MODEL) and w_ada.shape[0] == 1
    l = 0
    x2d = x.reshape(TOKENS, D_MODEL)
    mod = _adaln(c, w_ada[l], b_ada[l])
    shift1, scale1, gate1, shift2, scale2, gate2 = [mod[i] for i in range(6)]

    proj = _inproj(x2d, norm1_g[l], scale1, shift1, w_in[l])
    att = _attention(proj, _bias_band(rel_bias), lambda_q1[l], lambda_k1[l], lambda_q2[l],
                     lambda_k2[l], subln_g[l])
    cv = _conv(proj, conv_w[l], conv_b[l], conv_ln_g[l], conv_ln_b[l])

    x1, h2_packed, top_i, top_w, rank, counts = _outproj(
        att, cv, w_out[l].astype(bf16), x2d, gate1, scale2, shift2, norm2_g[l], w_router[l],
        b_router[l])
    tile_expert, next_expert, n_valid, valid_rows, pos, pad_start, pad_len = _routing_tables(
        counts[:, 0], top_i, rank)

    xs = _dispatch(pad_start, pad_len, n_valid, pos, h2_packed)
    act = _gmm1(tile_expert, next_expert, n_valid, valid_rows, xs, w_gate_up[l], b_gate_up[l])
    y = _gmm2(tile_expert, next_expert, n_valid, valid_rows, act, w_down[l], b_down[l])
    out = _combine(pos, y, x1, gate2, top_w.T, final_g)
    return out.reshape(BATCH, SEQ, D_MODEL)
```

```python
import math

import jax
import jax.numpy as jnp
from jax import lax
from jax.experimental import pallas as pl
from jax.experimental.pallas import tpu as pltpu

f32 = jnp.float32
bf16 = jnp.bfloat16
i32 = jnp.int32

D_MODEL = 2048
BATCH = 4
SEQ = 2048
TOKENS = BATCH * SEQ
ATT_HEADS = 8
ATT_HALF_DIM = 64
ATT_V_DIM = 128
ATT_WIDTH = ATT_HEADS * ATT_V_DIM
CONV_CH = D_MODEL - ATT_WIDTH
CONV_KERNEL = 31
CONV_HALF = CONV_KERNEL // 2
QK_COLS = ATT_HEADS * 2 * ATT_HALF_DIM
IN_COLS = 2 * QK_COLS + ATT_WIDTH + 2 * CONV_CH
N_BUCKETS = 32
MAX_DISTANCE = 128
N_EXPERTS = 32
TOP_K = 4
D_FF = D_MODEL
SWIGLU_LIMIT = 7.0
SWIGLU_ALPHA = 1.702
EPS = 1e-6
LAMBDA_INIT = 0.8 - 0.6 * math.exp(-0.3 * 0)
LOG2E = math.log2(math.e)
Q_SCALE = LOG2E * ATT_HALF_DIM ** -0.5

ADA_TN = 1024
INP_TM = 1024
INP_TN = 1024
ATT_TQ = 2048
ATT_SUB = 256
ATT_KC = 256
ATT_EXT_CHUNKS = 5
CONV_TS = 1024
CONV_HALO = 16
CONV_RB = 32
CONV_LN_ROWS = 128
OUT_TM = 512
MOE_TM = 512
MOE_ROW_STEP = 128
MOE_NT = TOKENS * TOP_K // MOE_TM + N_EXPERTS
MOE_ROWS = MOE_NT * MOE_TM
GMM1_TN = 1024
GMM2_TN = 2048
DSP_TB = 1024
CMB_TB = 256
ROW_CHUNK = 64
DMA_UNROLL = 8
DMA_PRIORITIES = 2
WEIGHT_DMA_PRIORITY = 1

_MIB = 1024 * 1024
LANES = 128


def _cparams(sem, vmem_mib):
    return pltpu.CompilerParams(dimension_semantics=sem, vmem_limit_bytes=vmem_mib * _MIB)


PACK_ROWS = 8
PACK_LANES = LANES
_HI_MASK = 0xFFFF0000


def _pack_pair(lo, hi):
    lo_b = lax.shift_right_logical(
        lax.bitcast_convert_type(lo.astype(bf16).astype(f32), jnp.uint32), jnp.uint32(16))
    hi_b = lax.bitcast_convert_type(hi.astype(bf16).astype(f32), jnp.uint32) & jnp.uint32(_HI_MASK)
    return lo_b | hi_b


def _unpack_pair(w):
    lo = lax.bitcast_convert_type(lax.shift_left(w, jnp.uint32(16)), f32)
    hi = lax.bitcast_convert_type(w & jnp.uint32(_HI_MASK), f32)
    return lo, hi


def _store_packed(dst_ref, token0, n_tokens, get_cols):
    for s in range(PACK_ROWS):
        lo = get_cols(slice(2 * s * PACK_LANES, (2 * s + 1) * PACK_LANES))
        hi = get_cols(slice((2 * s + 1) * PACK_LANES, (2 * s + 2) * PACK_LANES))
        dst_ref[pl.ds(token0 * PACK_ROWS + s, n_tokens, stride=PACK_ROWS), :] = _pack_pair(lo, hi)


def _load_packed(src_ref, n_tokens, s):
    return _unpack_pair(src_ref[pl.ds(s, n_tokens, stride=PACK_ROWS), :])


def _row_chunks(n_rows, chunk, body):
    def step(i, carry):
        body(pl.multiple_of(i * chunk, chunk))
        return carry
    lax.fori_loop(0, n_rows // chunk, step, 0)


def _adaln_kernel(c_ref, w_ref, b_ref, o_ref):
    c = c_ref[...]
    cs = c * jax.nn.sigmoid(c)
    o_ref[0] = jnp.dot(cs.astype(bf16), w_ref[...].astype(bf16),
                       preferred_element_type=f32) + b_ref[...]


def _adaln(c, w_ada, b_ada):
    n = w_ada.shape[1]
    per_vec = D_MODEL // ADA_TN
    return pl.pallas_call(
        _adaln_kernel,
        grid=(n // ADA_TN,),
        in_specs=[pl.BlockSpec((BATCH, D_MODEL), lambda j: (0, 0)),
                  pl.BlockSpec((D_MODEL, ADA_TN), lambda j: (0, j)),
                  pl.BlockSpec((1, ADA_TN), lambda j: (0, j))],
        out_specs=pl.BlockSpec((1, BATCH, ADA_TN), lambda j: (j // per_vec, 0, j % per_vec)),
        out_shape=jax.ShapeDtypeStruct((n // D_MODEL, BATCH, D_MODEL), f32),
        compiler_params=_cparams(("arbitrary",), 40),
        name="adaln",
    )(c, w_ada, b_ada.reshape(1, n))


def _inproj_kernel(x_ref, g_ref, sc_ref, sh_ref, w_ref, o_ref, h_scr):
    j = pl.program_id(1)

    def normalize():
        g = g_ref[...]
        one_plus_scale = 1.0 + sc_ref[0]
        shift = sh_ref[0]
        for r0 in range(0, INP_TM, ROW_CHUNK):
            rows = slice(r0, r0 + ROW_CHUNK)
            x = x_ref[rows, :]
            inv = lax.rsqrt(jnp.mean(x * x, axis=-1, keepdims=True) + EPS)
            for c in range(D_MODEL // LANES):
                cols = slice(c * LANES, (c + 1) * LANES)
                y = x_ref[rows, cols] * inv * g[:, cols]
                h_scr[rows, cols] = (y * one_plus_scale[:, cols] + shift[:, cols]).astype(bf16)

    def project():
        out_scale = jnp.where(j < QK_COLS // INP_TN, Q_SCALE, 1.0).astype(f32)
        acc = jnp.dot(h_scr[...], w_ref[...].astype(bf16), preferred_element_type=f32)
        o_ref[...] = (acc * out_scale).astype(bf16)

    @pl.when(j == 0)
    def _():
        normalize()
        project()

    @pl.when(j != 0)
    def _():
        project()


def _inproj(x2d, norm_g, scale, shift, w_in):
    assert QK_COLS % INP_TN == 0
    tiles_per_batch = SEQ // INP_TM
    return pl.pallas_call(
        _inproj_kernel,
        grid=(TOKENS // INP_TM, IN_COLS // INP_TN),
        in_specs=[pl.BlockSpec((INP_TM, D_MODEL), lambda i, j: (i, 0)),
                  pl.BlockSpec((1, D_MODEL), lambda i, j: (0, 0)),
                  pl.BlockSpec((1, 1, D_MODEL), lambda i, j: (i // tiles_per_batch, 0, 0)),
                  pl.BlockSpec((1, 1, D_MODEL), lambda i, j: (i // tiles_per_batch, 0, 0)),
                  pl.BlockSpec((D_MODEL, INP_TN), lambda i, j: (0, j))],
        out_specs=pl.BlockSpec((INP_TM, INP_TN), lambda i, j: (i, j)),
        out_shape=jax.ShapeDtypeStruct((TOKENS, IN_COLS), bf16),
        scratch_shapes=[pltpu.VMEM((INP_TM, D_MODEL), bf16)],
        compiler_params=_cparams(("parallel", "arbitrary"), 48),
        name="inproj",
    )(x2d, norm_g.reshape(1, D_MODEL), scale.reshape(BATCH, 1, D_MODEL),
      shift.reshape(BATCH, 1, D_MODEL), w_in)


def _t5_bucket(rel):
    half = N_BUCKETS // 2
    max_exact = half // 2
    ret = jnp.where(rel > 0, half, 0)
    n = jnp.abs(rel)
    nf = jnp.maximum(n, 1).astype(f32)
    large = max_exact + (jnp.log(nf / max_exact) / math.log(MAX_DISTANCE / max_exact)
                         * (half - max_exact)).astype(i32)
    large = jnp.minimum(large, half - 1)
    return ret + jnp.where(n < max_exact, n, large)


def _bias_band(rel_bias):
    assert ATT_SUB == ATT_KC and ATT_EXT_CHUNKS == 5 and ATT_KC >= MAX_DISTANCE
    r = jnp.arange(ATT_SUB)[:, None]
    j = jnp.arange(ATT_EXT_CHUNKS * ATT_KC)[None, :]
    rel = j - 2 * ATT_KC - r
    onehot = (_t5_bucket(rel)[..., None] == jnp.arange(N_BUCKETS)).astype(f32)
    return jnp.einsum("rjn,nh->hrj", onehot, rel_bias.astype(f32) * LOG2E,
                      precision=lax.Precision.HIGHEST)


def _attn_kernel(q_ref, k_ref, v_ref, band_ref, lq1_ref, lk1_ref, lq2_ref, lk2_ref, sg_ref,
                 o_ref, va_scr):
    t = pl.program_id(2)
    n_chunks = SEQ // ATT_KC

    @pl.when(t == 0)
    def _():
        va_scr[:, :ATT_V_DIM] = v_ref[...]
        va_scr[:, ATT_V_DIM:] = jnp.ones((SEQ, ATT_V_DIM), bf16)

    lam = (jnp.exp(jnp.sum(lq1_ref[...] * lk1_ref[...], axis=-1, keepdims=True))
           - jnp.exp(jnp.sum(lq2_ref[...] * lk2_ref[...], axis=-1, keepdims=True))
           + LAMBDA_INIT)
    def band_offset(rows, c):
        block = t * (ATT_TQ // ATT_SUB) + rows.start // ATT_SUB
        d = jnp.clip(c - block + 2, 0, ATT_EXT_CHUNKS - 1)
        return pl.multiple_of(d * ATT_KC, ATT_KC)

    def scores(qm, rows):
        chunks = []
        m = jnp.full((ATT_SUB, 1), -jnp.inf, f32)
        for c in range(n_chunks):
            s = lax.dot_general(qm, k_ref[c * ATT_KC:(c + 1) * ATT_KC, :],
                                (((1,), (1,)), ((), ())), preferred_element_type=f32)
            s = s + band_ref[0, :, pl.ds(band_offset(rows, c), ATT_KC)]
            chunks.append(s)
            m = jnp.maximum(m, jnp.max(s, axis=-1, keepdims=True))
        return chunks, m

    def weighted_values(chunks, m):
        o = jnp.zeros((ATT_SUB, 2 * ATT_V_DIM), f32)
        for c in range(n_chunks):
            p = jnp.exp2(chunks[c] - m)
            o = o + jnp.dot(p.astype(bf16), va_scr[c * ATT_KC:(c + 1) * ATT_KC, :],
                            preferred_element_type=f32)
        return o[:, :ATT_V_DIM] / o[:, ATT_V_DIM:]

    def finish(rows, o1, o2):
        out = o1 - lam * o2
        ms = jnp.mean(out * out, axis=-1, keepdims=True)
        y = out * lax.rsqrt(ms + EPS) * sg_ref[...]
        o_ref[rows, :] = (y * (1.0 - LAMBDA_INIT)).astype(bf16)

    units = []
    for r in range(ATT_TQ // ATT_SUB):
        rows = slice(r * ATT_SUB, (r + 1) * ATT_SUB)
        q = q_ref[rows, :]
        lane = lax.broadcasted_iota(i32, q.shape, 1)
        zero = jnp.zeros_like(q)
        units.append((rows, jnp.where(lane < ATT_HALF_DIM, q, zero)))
        units.append((rows, jnp.where(lane >= ATT_HALF_DIM, q, zero)))
    first_map = {}
    pending = None
    for unit in units + [None]:
        current = (unit[0],) + scores(unit[1], unit[0]) if unit is not None else None
        if pending is not None:
            p_rows, p_chunks, p_m = pending
            o = weighted_values(p_chunks, p_m)
            if p_rows.start in first_map:
                finish(p_rows, first_map.pop(p_rows.start), o)
            else:
                first_map[p_rows.start] = o
        pending = current


def _attention(proj, band, lq1, lk1, lq2, lk2, subln_g):
    qt = SEQ // ATT_TQ
    k_col0 = QK_COLS // ATT_V_DIM
    v_col0 = 2 * QK_COLS // ATT_V_DIM
    vec = lambda n: pl.BlockSpec((1, n), lambda b, h, t: (0, 0))
    return pl.pallas_call(
        _attn_kernel,
        grid=(BATCH, ATT_HEADS, qt),
        in_specs=[pl.BlockSpec((ATT_TQ, ATT_V_DIM), lambda b, h, t: (b * qt + t, h)),
                  pl.BlockSpec((SEQ, ATT_V_DIM), lambda b, h, t: (b, k_col0 + h)),
                  pl.BlockSpec((SEQ, ATT_V_DIM), lambda b, h, t: (b, v_col0 + h)),
                  pl.BlockSpec((1, ATT_SUB, ATT_EXT_CHUNKS * ATT_KC), lambda b, h, t: (h, 0, 0)),
                  vec(ATT_HALF_DIM), vec(ATT_HALF_DIM), vec(ATT_HALF_DIM), vec(ATT_HALF_DIM),
                  vec(ATT_V_DIM)],
        out_specs=pl.BlockSpec((ATT_TQ, ATT_V_DIM), lambda b, h, t: (b * qt + t, h)),
        out_shape=jax.ShapeDtypeStruct((TOKENS, ATT_WIDTH), bf16),
        scratch_shapes=[pltpu.VMEM((SEQ, 2 * ATT_V_DIM), bf16)],
        compiler_params=_cparams(("parallel", "parallel", "arbitrary"), 40),
        name="attn",
    )(proj, proj, proj, band, lq1.reshape(1, -1), lk1.reshape(1, -1), lq2.reshape(1, -1),
      lk2.reshape(1, -1), subln_g.reshape(1, -1))


def _conv_kernel(ap_ref, ac_ref, an_ref, gp_ref, gc_ref, gn_ref, w_ref, cb_ref, lg_ref, lb_ref,
                 o_ref, z_scr, c_scr):
    s = pl.program_id(1)
    last = pl.num_programs(1) - 1

    lanes = LANES
    n_lane_chunks = CONV_CH // lanes

    def store_z(rows, z):
        for lc in range(n_lane_chunks):
            z_scr[lc, rows, :] = z[:, lc * lanes:(lc + 1) * lanes]

    def glu(a, g):
        return a.astype(f32) * jax.nn.sigmoid(g.astype(f32))

    store_z(slice(0, CONV_HALO), jnp.where(s > 0, glu(ap_ref[...], gp_ref[...]), 0.0))
    store_z(slice(CONV_HALO + CONV_TS, CONV_TS + 2 * CONV_HALO),
            jnp.where(s < last, glu(an_ref[...], gn_ref[...]), 0.0))

    def glu_chunk(r0):
        rows = pl.ds(r0, ROW_CHUNK)
        store_z(pl.ds(CONV_HALO + r0, ROW_CHUNK), glu(ac_ref[rows, :], gc_ref[rows, :]))
    _row_chunks(CONV_TS, ROW_CHUNK, glu_chunk)

    tap0 = CONV_HALO - CONV_HALF

    def lane_chunk(lc, carry):
        cols = pl.ds(pl.multiple_of(lc * lanes, lanes), lanes)
        taps = [w_ref[j:j + 1, cols] for j in range(CONV_KERNEL)]
        bias = cb_ref[:, cols]
        for r0 in range(0, CONV_TS, CONV_RB):
            acc = jnp.zeros((CONV_RB, lanes), f32)
            for j in range(CONV_KERNEL):
                acc = acc + z_scr[lc, r0 + tap0 + j:r0 + tap0 + j + CONV_RB, :] * taps[j]
            c_scr[r0:r0 + CONV_RB, cols] = acc + bias
        return carry
    lax.fori_loop(0, n_lane_chunks, lane_chunk, 0)

    lg = lg_ref[...]
    lb = lb_ref[...]

    def ln_chunk(r0):
        c = c_scr[pl.ds(r0, CONV_LN_ROWS), :]
        mu = jnp.mean(c, axis=-1, keepdims=True)
        cc = c - mu
        var = jnp.mean(cc * cc, axis=-1, keepdims=True)
        y = cc * lax.rsqrt(var + EPS) * lg + lb
        o_ref[pl.ds(r0, CONV_LN_ROWS), :] = (y * jax.nn.sigmoid(y)).astype(bf16)
    _row_chunks(CONV_TS, CONV_LN_ROWS, ln_chunk)


def _conv(proj, conv_w, conv_b, ln_g, ln_b):
    st = SEQ // CONV_TS
    halo_per_tile = CONV_TS // CONV_HALO
    halo_per_seq = SEQ // CONV_HALO
    a_col = (2 * QK_COLS + ATT_WIDTH) // CONV_CH
    g_col = a_col + 1

    def cur(col):
        return pl.BlockSpec((CONV_TS, CONV_CH), lambda b, s: (b * st + s, col))

    def prev(col):
        return pl.BlockSpec(
            (CONV_HALO, CONV_CH),
            lambda b, s: (jnp.maximum(b * halo_per_seq + s * halo_per_tile - 1, 0), col))

    def nxt(col):
        return pl.BlockSpec(
            (CONV_HALO, CONV_CH),
            lambda b, s: (jnp.minimum(b * halo_per_seq + (s + 1) * halo_per_tile,
                                      TOKENS // CONV_HALO - 1), col))

    vec = pl.BlockSpec((1, CONV_CH), lambda b, s: (0, 0))
    return pl.pallas_call(
        _conv_kernel,
        grid=(BATCH, st),
        in_specs=[prev(a_col), cur(a_col), nxt(a_col), prev(g_col), cur(g_col), nxt(g_col),
                  pl.BlockSpec((CONV_KERNEL, CONV_CH), lambda b, s: (0, 0)), vec, vec, vec],
        out_specs=pl.BlockSpec((CONV_TS, CONV_CH), lambda b, s: (b * st + s, 0)),
        out_shape=jax.ShapeDtypeStruct((TOKENS, CONV_CH), bf16),
        scratch_shapes=[pltpu.VMEM((CONV_CH // LANES, CONV_TS + 2 * CONV_HALO, LANES), f32),
                        pltpu.VMEM((CONV_TS, CONV_CH), f32)],
        compiler_params=_cparams(("parallel", "arbitrary"), 32),
        name="conv",
    )(proj, proj, proj, proj, proj, proj, conv_w, conv_b.reshape(1, -1), ln_g.reshape(1, -1),
      ln_b.reshape(1, -1))


def _outproj_kernel(att_ref, cv_ref, wo_ref, x_ref, g1_ref, sc2_ref, sh2_ref, n2_ref, wr_ref,
                    br_ref, x1_ref, h2p_ref, ti_ref, tw_ref, rk_ref, cnt_ref, carry_scr, hi_scr,
                    lo_scr, wo_bf):
    @pl.when(pl.program_id(0) == 0)
    def _():
        carry_scr[...] = jnp.zeros_like(carry_scr)
        wo_bf[...] = wo_ref[...].astype(bf16)

    mixed = (jnp.dot(att_ref[...], wo_bf[0:ATT_WIDTH, :], preferred_element_type=f32)
             + jnp.dot(cv_ref[...], wo_bf[ATT_WIDTH:, :], preferred_element_type=f32))
    x1_ref[...] = x_ref[...] + g1_ref[0] * mixed

    n2 = n2_ref[...]
    one_plus_scale = 1.0 + sc2_ref[0]
    shift = sh2_ref[0]

    def split(v):
        hi = v.astype(bf16)
        return hi, (v - hi.astype(f32)).astype(bf16)

    for r0 in range(0, OUT_TM, ROW_CHUNK):
        rows = slice(r0, r0 + ROW_CHUNK)
        x1 = x1_ref[rows, :]
        inv = lax.rsqrt(jnp.mean(x1 * x1, axis=-1, keepdims=True) + EPS)
        for s in range(PACK_ROWS):
            pair = []
            for c in (2 * s, 2 * s + 1):
                cols = slice(c * PACK_LANES, (c + 1) * PACK_LANES)
                y = x1_ref[rows, cols] * inv * n2[:, cols]
                h2 = y * one_plus_scale[:, cols] + shift[:, cols]
                hi_scr[rows, cols], lo_scr[rows, cols] = split(h2)
                pair.append(h2)
            h2p_ref[pl.ds(r0 * PACK_ROWS + s, ROW_CHUNK, stride=PACK_ROWS), :] = _pack_pair(*pair)

    w_hi, w_lo = split(wr_ref[...])
    h_hi, h_lo = hi_scr[...], lo_scr[...]
    rows_x_rows = (((1,), (1,)), ((), ()))
    logits = (lax.dot_general(w_hi, h_hi, rows_x_rows, preferred_element_type=f32)
              + lax.dot_general(w_hi, h_lo, rows_x_rows, preferred_element_type=f32)
              + lax.dot_general(w_lo, h_hi, rows_x_rows, preferred_element_type=f32)
              + br_ref[...])
    eidx = lax.broadcasted_iota(i32, logits.shape, 0)
    vals = logits
    sels, tops = [], []
    for k in range(TOP_K):
        m = jnp.max(vals, axis=0, keepdims=True)
        idx = jnp.min(jnp.where(vals == m, eidx, N_EXPERTS), axis=0, keepdims=True)
        sel = eidx == idx
        ti_ref[k:k + 1, :] = idx
        sels.append(sel)
        tops.append(m)
        vals = jnp.where(sel, -jnp.inf, vals)
    exps = [jnp.exp(v - tops[0]) for v in tops]
    denom = exps[0] + exps[1] + exps[2] + exps[3]
    for k in range(TOP_K):
        tw_ref[k:k + 1, :] = exps[k] / denom

    onehot = jnp.where(sels[0] | sels[1] | sels[2] | sels[3], 1.0, 0.0)
    r_i = lax.broadcasted_iota(i32, (OUT_TM, OUT_TM), 0)
    c_i = lax.broadcasted_iota(i32, (OUT_TM, OUT_TM), 1)
    earlier = jnp.where(r_i < c_i, 1.0, 0.0).astype(bf16)
    before = jnp.dot(onehot.astype(bf16), earlier, preferred_element_type=f32) + carry_scr[...]
    for k in range(TOP_K):
        rk_ref[k:k + 1, :] = jnp.sum(jnp.where(sels[k], before, 0.0), axis=0,
                                     keepdims=True).astype(i32)
    total = carry_scr[...] + jnp.sum(onehot, axis=1, keepdims=True)
    carry_scr[...] = total
    cnt_ref[...] = jnp.broadcast_to(total, cnt_ref.shape).astype(i32)


def _outproj(att, cv, w_out, x2d, gate1, scale2, shift2, norm2_g, w_router, b_router):
    tiles_per_batch = SEQ // OUT_TM
    row = lambda w: pl.BlockSpec((OUT_TM, w), lambda i: (i, 0))
    mod = pl.BlockSpec((1, 1, D_MODEL), lambda i: (i // tiles_per_batch, 0, 0))
    slot = pl.BlockSpec((TOP_K, OUT_TM), lambda i: (0, i))
    return pl.pallas_call(
        _outproj_kernel,
        grid=(TOKENS // OUT_TM,),
        in_specs=[row(ATT_WIDTH), row(CONV_CH),
                  pl.BlockSpec((D_MODEL, D_MODEL), lambda i: (0, 0), pipeline_mode=pl.Buffered(1)),
                  row(D_MODEL), mod, mod, mod,
                  pl.BlockSpec((1, D_MODEL), lambda i: (0, 0)),
                  pl.BlockSpec((N_EXPERTS, D_MODEL), lambda i: (0, 0)),
                  pl.BlockSpec((N_EXPERTS, 1), lambda i: (0, 0))],
        out_specs=[row(D_MODEL),
                   pl.BlockSpec((OUT_TM * PACK_ROWS, PACK_LANES), lambda i: (i, 0)),
                   slot, slot, slot,
                   pl.BlockSpec((N_EXPERTS, LANES), lambda i: (0, 0))],
        out_shape=[jax.ShapeDtypeStruct((TOKENS, D_MODEL), f32),
                   jax.ShapeDtypeStruct((TOKENS * PACK_ROWS, PACK_LANES), jnp.uint32),
                   jax.ShapeDtypeStruct((TOP_K, TOKENS), i32),
                   jax.ShapeDtypeStruct((TOP_K, TOKENS), f32),
                   jax.ShapeDtypeStruct((TOP_K, TOKENS), i32),
                   jax.ShapeDtypeStruct((N_EXPERTS, LANES), i32)],
        scratch_shapes=[pltpu.VMEM((N_EXPERTS, 1), f32), pltpu.VMEM((OUT_TM, D_MODEL), bf16),
                        pltpu.VMEM((OUT_TM, D_MODEL), bf16), pltpu.VMEM((D_MODEL, D_MODEL), bf16)],
        compiler_params=_cparams(("arbitrary",), 58),
        name="outproj",
    )(att, cv, w_out, x2d, gate1.reshape(BATCH, 1, D_MODEL), scale2.reshape(BATCH, 1, D_MODEL),
      shift2.reshape(BATCH, 1, D_MODEL), norm2_g.reshape(1, D_MODEL), w_router.T,
      b_router.reshape(N_EXPERTS, 1))


def _token_copy(src_hbm, token, dst_buf, row, sem):
    dst = dst_buf.at[pl.ds(pl.multiple_of(row * PACK_ROWS, PACK_ROWS), PACK_ROWS), :]
    return pltpu.make_async_copy(src_hbm.at[token], dst, sem)


_PAD_BITS = tuple(range(MOE_TM.bit_length() - 2, -1, -1))


def _pad_fill_copies(pad_start, pad_len, zeros, xs_hbm, sem):
    pairs = []
    for b in _PAD_BITS:
        size = 1 << b
        off = pad_start + lax.shift_left(lax.shift_right_logical(pad_len, b + 1), b + 1)
        copy = pltpu.make_async_copy(zeros.at[pl.ds(0, size)], xs_hbm.at[pl.ds(off, size)], sem)
        pairs.append((lax.shift_right_logical(pad_len, b) & 1 == 1, copy))
    return pairs


def _dispatch_kernel(pad_start_ref, pad_len_ref, nv_ref, pos_ref, h2p_ref, xs_hbm, zeros, sem, zsem):
    i = pl.program_id(0)
    last = pl.num_programs(0) - 1

    def for_each_fill(act):
        def per_expert(e, carry):
            for cond, copy in _pad_fill_copies(pad_start_ref[e], pad_len_ref[e], zeros, xs_hbm, zsem):
                @pl.when(cond)
                def _(copy=copy):
                    act(copy)
            return carry
        lax.fori_loop(0, N_EXPERTS, per_expert, 0)

        def per_tile(t, carry):
            row0 = pl.multiple_of(t * MOE_TM, MOE_TM)
            act(pltpu.make_async_copy(zeros, xs_hbm.at[pl.ds(row0, MOE_TM)], zsem))
            return carry
        lax.fori_loop(nv_ref[0], MOE_NT, per_tile, 0)

    @pl.when(i == 0)
    def _():
        zeros[...] = jnp.zeros_like(zeros)
        for_each_fill(lambda copy: copy.start())

    def body(g, carry):
        for u in range(DMA_UNROLL // TOP_K):
            r = g * (DMA_UNROLL // TOP_K) + u
            for k in range(TOP_K):
                pltpu.make_async_copy(h2p_ref.at[r], xs_hbm.at[pos_ref[0, k, r]], sem).start(
                    priority=k % DMA_PRIORITIES)
        return carry
    lax.fori_loop(0, DSP_TB * TOP_K // DMA_UNROLL, body, 0)
    pltpu.make_async_copy(xs_hbm.at[pl.ds(0, DSP_TB * TOP_K)], xs_hbm.at[pl.ds(0, DSP_TB * TOP_K)],
                          sem).wait()

    @pl.when(i == last)
    def _():
        for_each_fill(lambda copy: copy.wait())


def _dispatch(pad_start, pad_len, n_valid, pos, h2_packed):
    nt = TOKENS // DSP_TB
    pos3 = pos.reshape(TOP_K, nt, DSP_TB).transpose(1, 0, 2)
    grid_spec = pltpu.PrefetchScalarGridSpec(
        num_scalar_prefetch=3,
        grid=(nt,),
        in_specs=[pl.BlockSpec((1, TOP_K, DSP_TB), lambda i, *_: (i, 0, 0), memory_space=pltpu.SMEM),
                  pl.BlockSpec((DSP_TB, PACK_ROWS, PACK_LANES), lambda i, *_: (i, 0, 0))],
        out_specs=pl.BlockSpec(memory_space=pl.ANY),
        scratch_shapes=[pltpu.VMEM((MOE_TM, PACK_ROWS, PACK_LANES), jnp.uint32),
                        pltpu.SemaphoreType.DMA(()), pltpu.SemaphoreType.DMA(())])
    return pl.pallas_call(
        _dispatch_kernel,
        grid_spec=grid_spec,
        out_shape=jax.ShapeDtypeStruct((MOE_ROWS, PACK_ROWS, PACK_LANES), jnp.uint32),
        compiler_params=_cparams(("arbitrary",), 24),
        name="dispatch",
    )(pad_start, pad_len, n_valid, pos3, h2_packed.reshape(TOKENS, PACK_ROWS, PACK_LANES))


def _weights_changed(te_ref, m):
    prev = te_ref[jnp.maximum(m - 1, 0)]
    return (m == 0) | (te_ref[m] != prev)


def _expert_tile_step(te_ref, nx_ref, vr_ref, n, m, n_passes, count_ref, copies_for, convert,
                      compute, write_zeros):
    first_expert = te_ref[0]
    first = _weights_changed(te_ref, m)
    rows_valid = vr_ref[m]

    def start(copies):
        for copy in copies:
            copy.start(priority=WEIGHT_DMA_PRIORITY)

    @pl.when((n == 0) & (m == 0))
    def _():
        count_ref[0] = 0
        start(copies_for(0, first_expert, 0))

    def open_item():
        slot = count_ref[0] % 2
        nxt = nx_ref[m]

        @pl.when(nxt >= 0)
        def _():
            start(copies_for(n, nxt, 1 - slot))

        @pl.when((nxt < 0) & (n + 1 < n_passes))
        def _():
            start(copies_for(n + 1, first_expert, 1 - slot))

        for copy in copies_for(0, 0, slot):
            copy.wait()
        convert(slot)
        count_ref[0] = count_ref[0] + 1

    full = rows_valid > MOE_TM - MOE_ROW_STEP
    fused = first & full

    @pl.when(fused)
    def _():
        open_item()
        compute(MOE_TM)

    @pl.when(first & jnp.logical_not(full))
    def _():
        open_item()

    for n_rows in range(MOE_ROW_STEP, MOE_TM + 1, MOE_ROW_STEP):
        in_range = (rows_valid > n_rows - MOE_ROW_STEP) & (rows_valid <= n_rows)
        if n_rows == MOE_TM:
            in_range = in_range & jnp.logical_not(fused)

        @pl.when(in_range)
        def _(n_rows=n_rows):
            compute(n_rows)

    @pl.when(rows_valid <= 0)
    def _():
        write_zeros()


def _gmm1_kernel(te_ref, nx_ref, nv_ref, vr_ref, xs_ref, w_hbm, bg_ref, bu_ref, o_ref, stage, wg_bf,
                 wu_bf, count_ref, sem):
    n = pl.program_id(0)
    m = pl.program_id(1)

    def copies_for(p, e, slot):
        col = pl.multiple_of(p * GMM1_TN, GMM1_TN)
        return [pltpu.make_async_copy(w_hbm.at[e, :, pl.ds(col, GMM1_TN)], stage.at[slot, 0],
                                      sem.at[slot]),
                pltpu.make_async_copy(w_hbm.at[e, :, pl.ds(D_FF + col, GMM1_TN)],
                                      stage.at[slot, 1], sem.at[slot])]

    def convert(slot):
        wg_bf[...] = stage[slot, 0].astype(bf16)
        wu_bf[...] = stage[slot, 1].astype(bf16)

    def compute(n_rows):
        parts = []
        for s in range(PACK_ROWS):
            lo, hi = _load_packed(xs_ref, n_rows, s)
            parts += [lo.astype(bf16), hi.astype(bf16)]
        x = jnp.concatenate(parts, axis=1)
        gate = jnp.dot(x, wg_bf[...], preferred_element_type=f32) + bg_ref[0]
        up = jnp.dot(x, wu_bf[...], preferred_element_type=f32) + bu_ref[0]
        gate = jnp.minimum(gate, SWIGLU_LIMIT)
        up = jnp.clip(up, -SWIGLU_LIMIT, SWIGLU_LIMIT)
        act = (up + 1.0) * (gate * jax.nn.sigmoid(SWIGLU_ALPHA * gate))
        o_ref[0:n_rows, :] = act.astype(bf16)
        if n_rows < MOE_TM:
            o_ref[n_rows:, :] = jnp.zeros((MOE_TM - n_rows, GMM1_TN), bf16)

    def write_zeros():
        o_ref[...] = jnp.zeros_like(o_ref)

    _expert_tile_step(te_ref, nx_ref, vr_ref, n, m, pl.num_programs(0), count_ref, copies_for,
                      convert, compute, write_zeros)


def _gmm1(tile_expert, next_expert, n_valid, valid_rows, xs_packed, w_gate_up, b_gate_up):
    up0 = D_FF // GMM1_TN
    used = lambda m, nv: jnp.minimum(m, nv[0] - 1)
    grid_spec = pltpu.PrefetchScalarGridSpec(
        num_scalar_prefetch=4,
        grid=(D_FF // GMM1_TN, MOE_NT),
        in_specs=[pl.BlockSpec((MOE_TM * PACK_ROWS, PACK_LANES),
                               lambda n, m, te, nx, nv, vr: (used(m, nv), 0)),
                  pl.BlockSpec(memory_space=pl.ANY),
                  pl.BlockSpec((1, 1, GMM1_TN), lambda n, m, te, nx, nv, vr: (te[m], 0, n)),
                  pl.BlockSpec((1, 1, GMM1_TN), lambda n, m, te, nx, nv, vr: (te[m], 0, up0 + n))],
        out_specs=pl.BlockSpec((MOE_TM, GMM1_TN), lambda n, m, te, nx, nv, vr: (m, n)),
        scratch_shapes=[pltpu.VMEM((2, 2, D_MODEL, GMM1_TN), f32),
                        pltpu.VMEM((D_MODEL, GMM1_TN), bf16), pltpu.VMEM((D_MODEL, GMM1_TN), bf16),
                        pltpu.SMEM((1,), i32), pltpu.SemaphoreType.DMA((2,))])
    b3 = b_gate_up.reshape(N_EXPERTS, 1, 2 * D_FF)
    return pl.pallas_call(
        _gmm1_kernel,
        grid_spec=grid_spec,
        out_shape=jax.ShapeDtypeStruct((MOE_ROWS, D_FF), bf16),
        compiler_params=_cparams(("arbitrary", "arbitrary"), 54),
        name="gmm1",
    )(tile_expert, next_expert, n_valid, valid_rows,
      xs_packed.reshape(MOE_ROWS * PACK_ROWS, PACK_LANES), w_gate_up, b3, b3)


def _gmm2_kernel(te_ref, nx_ref, nv_ref, vr_ref, a_ref, w_hbm, bd_ref, o_ref, stage, wd_bf, y_scr,
                 count_ref, sem):
    def copies_for(p, e, slot):
        return [pltpu.make_async_copy(w_hbm.at[e], stage.at[slot], sem.at[slot])]

    def convert(slot):
        wd_bf[...] = stage[slot].astype(bf16)

    def compute(n_rows):
        y_scr[0:n_rows, :] = (jnp.dot(a_ref[0:n_rows, :], wd_bf[...], preferred_element_type=f32)
                              + bd_ref[0])
        _store_packed(o_ref, 0, n_rows, lambda cols: y_scr[0:n_rows, cols])
        if n_rows < MOE_TM:
            o_ref[n_rows * PACK_ROWS:, :] = jnp.zeros(((MOE_TM - n_rows) * PACK_ROWS, PACK_LANES),
                                                     jnp.uint32)

    def write_zeros():
        o_ref[...] = jnp.zeros_like(o_ref)

    _expert_tile_step(te_ref, nx_ref, vr_ref, pl.program_id(0), pl.program_id(1),
                      pl.num_programs(0), count_ref, copies_for, convert, compute, write_zeros)


def _gmm2(tile_expert, next_expert, n_valid, valid_rows, act, w_down, b_down):
    assert GMM2_TN == D_MODEL
    grid_spec = pltpu.PrefetchScalarGridSpec(
        num_scalar_prefetch=4,
        grid=(1, MOE_NT),
        in_specs=[pl.BlockSpec((MOE_TM, D_FF), lambda n, m, te, nx, nv, vr: (m, 0)),
                  pl.BlockSpec(memory_space=pl.ANY),
                  pl.BlockSpec((1, 1, GMM2_TN), lambda n, m, te, nx, nv, vr: (te[m], 0, 0))],
        out_specs=pl.BlockSpec((MOE_TM * PACK_ROWS, PACK_LANES),
                               lambda n, m, te, nx, nv, vr: (m, 0)),
        scratch_shapes=[pltpu.VMEM((2, D_FF, GMM2_TN), f32), pltpu.VMEM((D_FF, GMM2_TN), bf16),
                        pltpu.VMEM((MOE_TM, D_MODEL), f32), pltpu.SMEM((1,), i32),
                        pltpu.SemaphoreType.DMA((2,))])
    return pl.pallas_call(
        _gmm2_kernel,
        grid_spec=grid_spec,
        out_shape=jax.ShapeDtypeStruct((MOE_ROWS * PACK_ROWS, PACK_LANES), jnp.uint32),
        compiler_params=_cparams(("arbitrary", "arbitrary"), 56),
        name="gmm2",
    )(tile_expert, next_expert, n_valid, valid_rows, act, w_down,
      b_down.reshape(N_EXPERTS, 1, D_MODEL))


def _combine_kernel(pos_cur, pos_nxt, y_hbm, x1_ref, g2_ref, tw_ref, fg_ref, o_ref, buf, sem):
    i = pl.program_id(0)
    n = pl.num_programs(0)
    slot = i % 2

    def start_rows(pos_ref, s, rows):
        for r in rows:
            for k in range(TOP_K):
                _token_copy(y_hbm, pos_ref[0, k, r], buf.at[s, k], r, sem.at[s]).start(
                    priority=k % DMA_PRIORITIES)

    @pl.when(i == 0)
    def _():
        def body(g, carry):
            start_rows(pos_cur, 0, [g * (DMA_UNROLL // TOP_K) + u
                                    for u in range(DMA_UNROLL // TOP_K)])
            return carry
        lax.fori_loop(0, CMB_TB * TOP_K // DMA_UNROLL, body, 0)

    pltpu.make_async_copy(buf.at[slot], buf.at[slot], sem.at[slot]).wait()

    def combine_tile(prefetch_next):
        weights = [jnp.broadcast_to(tw_ref[:, k:k + 1], (CMB_TB, PACK_LANES))
                   for k in range(TOP_K)]
        sumsq = jnp.zeros((CMB_TB, 1), f32)
        rows_per_chunk = CMB_TB // PACK_ROWS
        for s in range(PACK_ROWS):
            if prefetch_next:
                start_rows(pos_nxt, 1 - slot, range(s * rows_per_chunk, (s + 1) * rows_per_chunk))
            lo, hi = _load_packed(buf.at[slot, 0], CMB_TB, s)
            lo, hi = lo * weights[0], hi * weights[0]
            for k in range(1, TOP_K):
                lo_k, hi_k = _load_packed(buf.at[slot, k], CMB_TB, s)
                lo, hi = lo + lo_k * weights[k], hi + hi_k * weights[k]
            for c, moe in ((2 * s, lo), (2 * s + 1, hi)):
                cols = slice(c * PACK_LANES, (c + 1) * PACK_LANES)
                x2 = x1_ref[:, cols] + g2_ref[0, :, cols] * moe
                o_ref[:, cols] = x2
                sumsq = sumsq + jnp.sum(x2 * x2, axis=-1, keepdims=True)
        inv = lax.rsqrt(sumsq * (1.0 / D_MODEL) + EPS)
        o_ref[...] = o_ref[...] * inv * fg_ref[...]

    @pl.when(i + 1 < n)
    def _():
        combine_tile(True)

    @pl.when(i + 1 >= n)
    def _():
        combine_tile(False)


def _combine(pos, y_packed, x1, gate2, top_w_rows, final_g):
    nt = TOKENS // CMB_TB
    tiles_per_batch = SEQ // CMB_TB
    pos3 = pos.reshape(TOP_K, nt, CMB_TB).transpose(1, 0, 2)
    return pl.pallas_call(
        _combine_kernel,
        grid=(nt,),
        in_specs=[pl.BlockSpec((1, TOP_K, CMB_TB), lambda i: (i, 0, 0), memory_space=pltpu.SMEM),
                  pl.BlockSpec((1, TOP_K, CMB_TB), lambda i: (jnp.minimum(i + 1, nt - 1), 0, 0),
                               memory_space=pltpu.SMEM),
                  pl.BlockSpec(memory_space=pl.ANY),
                  pl.BlockSpec((CMB_TB, D_MODEL), lambda i: (i, 0)),
                  pl.BlockSpec((1, 1, D_MODEL), lambda i: (i // tiles_per_batch, 0, 0)),
                  pl.BlockSpec((CMB_TB, TOP_K), lambda i: (i, 0)),
                  pl.BlockSpec((1, D_MODEL), lambda i: (0, 0))],
        out_specs=pl.BlockSpec((CMB_TB, D_MODEL), lambda i: (i, 0)),
        out_shape=jax.ShapeDtypeStruct((TOKENS, D_MODEL), f32),
        scratch_shapes=[pltpu.VMEM((2, TOP_K, CMB_TB * PACK_ROWS, PACK_LANES), jnp.uint32),
                        pltpu.SemaphoreType.DMA((2,))],
        compiler_params=_cparams(("arbitrary",), 44),
        name="combine",
    )(pos3, pos3, y_packed.reshape(MOE_ROWS, PACK_ROWS, PACK_LANES), x1,
      gate2.reshape(BATCH, 1, D_MODEL), top_w_rows, final_g.reshape(1, D_MODEL))


def _routing_tables(counts, top_i, rank):
    padded = (counts + MOE_TM - 1) // MOE_TM * MOE_TM
    ends = jnp.cumsum(padded)
    starts = ends - padded
    n_valid = ends[-1] // MOE_TM
    tile_ids = jnp.minimum(jnp.arange(MOE_NT, dtype=i32), n_valid - 1)
    tile_expert = jnp.sum(tile_ids[:, None] >= (ends // MOE_TM)[None, :], axis=1).astype(i32)
    experts = jnp.arange(N_EXPERTS, dtype=i32)
    pos = rank + jnp.sum(jnp.where(top_i[..., None] == experts, starts, 0), axis=-1)
    pad_start = (starts + counts).astype(i32)
    pad_len = (padded - counts).astype(i32)
    later = (experts[None, :] > tile_expert[:, None]) & (counts > 0)[None, :]
    next_expert = jnp.min(jnp.where(later, experts[None, :], N_EXPERTS), axis=1)
    next_expert = jnp.where(next_expert == N_EXPERTS, -1, next_expert).astype(i32)
    block_ids = jnp.arange(MOE_NT, dtype=i32)
    group_end = jnp.sum(jnp.where(tile_expert[:, None] == experts, pad_start, 0), axis=-1)
    valid_rows = jnp.clip(group_end - block_ids * MOE_TM, 0, MOE_TM)
    valid_rows = jnp.where(block_ids < n_valid, valid_rows, 0).astype(i32)
    return (tile_expert, next_expert, n_valid.reshape(1).astype(i32), valid_rows, pos.astype(i32),
            pad_start, pad_len)


def kernel(x, c, w_ada, b_ada, norm1_g, w_in, lambda_q1, lambda_k1, lambda_q2, lambda_k2, subln_g,
           rel_bias, conv_w, conv_b, conv_ln_g, conv_ln_b, w_out, norm2_g, w_router, b_router,
           w_gate_up, b_gate_up, w_down, b_down, final_g):
    assert x.shape == (BATCH, SEQ, D_MODEL) and w_ada.shape[0] == 1
    l = 0
    x2d = x.reshape(TOKENS, D_MODEL)
    mod = _adaln(c, w_ada[l], b_ada[l])
    shift1, scale1, gate1, shift2, scale2, gate2 = [mod[i] for i in range(6)]

    proj = _inproj(x2d, norm1_g[l], scale1, shift1, w_in[l])
    att = _attention(proj, _bias_band(rel_bias), lambda_q1[l], lambda_k1[l], lambda_q2[l],
                     lambda_k2[l], subln_g[l])
    cv = _conv(proj, conv_w[l], conv_b[l], conv_ln_g[l], conv_ln_b[l])

    x1, h2_packed, top_i, top_w, rank, counts = _outproj(
        att, cv, w_out[l], x2d, gate1, scale2, shift2, norm2_g[l], w_router[l],
        b_router[l])
    tile_expert, next_expert, n_valid, valid_rows, pos, pad_start, pad_len = _routing_tables(
        counts[:, 0], top_i, rank)

    xs = _dispatch(pad_start, pad_len, n_valid, pos, h2_packed)
    act = _gmm1(tile_expert, next_expert, n_valid, valid_rows, xs, w_gate_up[l], b_gate_up[l])
    y = _gmm2(tile_expert, next_expert, n_valid, valid_rows, act, w_down[l], b_down[l])
    out = _combine(pos, y, x1, gate2, top_w.T, final_g)
    return out.reshape(BATCH, SEQ, D_MODEL)
```

```python
import math

import jax
import jax.numpy as jnp
from jax import lax
from jax.experimental import pallas as pl
from jax.experimental.pallas import tpu as pltpu

f32 = jnp.float32
bf16 = jnp.bfloat16
i32 = jnp.int32

D_MODEL = 2048
BATCH = 4
SEQ = 2048
TOKENS = BATCH * SEQ
ATT_HEADS = 8
ATT_HALF_DIM = 64
ATT_V_DIM = 128
ATT_WIDTH = ATT_HEADS * ATT_V_DIM
CONV_CH = D_MODEL - ATT_WIDTH
CONV_KERNEL = 31
CONV_HALF = CONV_KERNEL // 2
QK_COLS = ATT_HEADS * 2 * ATT_HALF_DIM
IN_COLS = 2 * QK_COLS + ATT_WIDTH + 2 * CONV_CH
N_BUCKETS = 32
MAX_DISTANCE = 128
N_EXPERTS = 32
TOP_K = 4
D_FF = D_MODEL
SWIGLU_LIMIT = 7.0
SWIGLU_ALPHA = 1.702
EPS = 1e-6
LAMBDA_INIT = 0.8 - 0.6 * math.exp(-0.3 * 0)
LOG2E = math.log2(math.e)
Q_SCALE = LOG2E * ATT_HALF_DIM ** -0.5

ADA_TN = 1024
INP_TM = 1024
INP_TN = 1024
ATT_TQ = 2048
ATT_SUB = 256
ATT_KC = 256
ATT_EXT_CHUNKS = 5
CONV_TS = 1024
CONV_HALO = 16
CONV_RB = 32
CONV_LN_ROWS = 128
OUT_TM = 512
MOE_TM = 512
MOE_ROW_STEP = 128
MOE_NT = TOKENS * TOP_K // MOE_TM + N_EXPERTS
MOE_ROWS = MOE_NT * MOE_TM
GMM1_TN = 1024
GMM2_TN = 2048
DSP_TB = 1024
CMB_TB = 256
ROW_CHUNK = 64
DMA_UNROLL = 8
DMA_PRIORITIES = 2
WEIGHT_DMA_PRIORITY = 1

_MIB = 1024 * 1024
LANES = 128


def _cparams(sem, vmem_mib):
    return pltpu.CompilerParams(dimension_semantics=sem, vmem_limit_bytes=vmem_mib * _MIB)


PACK_ROWS = 8
PACK_LANES = LANES
_HI_MASK = 0xFFFF0000


def _pack_pair(lo, hi):
    lo_b = lax.shift_right_logical(
        lax.bitcast_convert_type(lo.astype(bf16).astype(f32), jnp.uint32), jnp.uint32(16))
    hi_b = lax.bitcast_convert_type(hi.astype(bf16).astype(f32), jnp.uint32) & jnp.uint32(_HI_MASK)
    return lo_b | hi_b


def _unpack_pair(w):
    lo = lax.bitcast_convert_type(lax.shift_left(w, jnp.uint32(16)), f32)
    hi = lax.bitcast_convert_type(w & jnp.uint32(_HI_MASK), f32)
    return lo, hi


def _store_packed(dst_ref, token0, n_tokens, get_cols):
    for s in range(PACK_ROWS):
        lo = get_cols(slice(2 * s * PACK_LANES, (2 * s + 1) * PACK_LANES))
        hi = get_cols(slice((2 * s + 1) * PACK_LANES, (2 * s + 2) * PACK_LANES))
        dst_ref[pl.ds(token0 * PACK_ROWS + s, n_tokens, stride=PACK_ROWS), :] = _pack_pair(lo, hi)


def _load_packed(src_ref, n_tokens, s):
    return _unpack_pair(src_ref[pl.ds(s, n_tokens, stride=PACK_ROWS), :])


def _row_chunks(n_rows, chunk, body):
    def step(i, carry):
        body(pl.multiple_of(i * chunk, chunk))
        return carry
    lax.fori_loop(0, n_rows // chunk, step, 0)


def _adaln_kernel(c_ref, w_ref, b_ref, o_ref):
    c = c_ref[...]
    cs = c * jax.nn.sigmoid(c)
    o_ref[0] = jnp.dot(cs.astype(bf16), w_ref[...].astype(bf16),
                       preferred_element_type=f32) + b_ref[...]


def _adaln(c, w_ada, b_ada):
    n = w_ada.shape[1]
    per_vec = D_MODEL // ADA_TN
    return pl.pallas_call(
        _adaln_kernel,
        grid=(n // ADA_TN,),
        in_specs=[pl.BlockSpec((BATCH, D_MODEL), lambda j: (0, 0)),
                  pl.BlockSpec((D_MODEL, ADA_TN), lambda j: (0, j)),
                  pl.BlockSpec((1, ADA_TN), lambda j: (0, j))],
        out_specs=pl.BlockSpec((1, BATCH, ADA_TN), lambda j: (j // per_vec, 0, j % per_vec)),
        out_shape=jax.ShapeDtypeStruct((n // D_MODEL, BATCH, D_MODEL), f32),
        compiler_params=_cparams(("arbitrary",), 40),
        name="adaln",
    )(c, w_ada, b_ada.reshape(1, n))


def _inproj_kernel(x_ref, g_ref, sc_ref, sh_ref, w_ref, o_ref, h_scr):
    j = pl.program_id(1)

    def normalize():
        g = g_ref[...]
        one_plus_scale = 1.0 + sc_ref[0]
        shift = sh_ref[0]
        for r0 in range(0, INP_TM, ROW_CHUNK):
            rows = slice(r0, r0 + ROW_CHUNK)
            x = x_ref[rows, :]
            inv = lax.rsqrt(jnp.mean(x * x, axis=-1, keepdims=True) + EPS)
            for c in range(D_MODEL // LANES):
                cols = slice(c * LANES, (c + 1) * LANES)
                y = x_ref[rows, cols] * inv * g[:, cols]
                h_scr[rows, cols] = (y * one_plus_scale[:, cols] + shift[:, cols]).astype(bf16)

    def project():
        out_scale = jnp.where(j < QK_COLS // INP_TN, Q_SCALE, 1.0).astype(f32)
        acc = jnp.dot(h_scr[...], w_ref[...].astype(bf16), preferred_element_type=f32)
        o_ref[...] = (acc * out_scale).astype(bf16)

    @pl.when(j == 0)
    def _():
        normalize()
        project()

    @pl.when(j != 0)
    def _():
        project()


def _inproj(x2d, norm_g, scale, shift, w_in):
    assert QK_COLS % INP_TN == 0
    tiles_per_batch = SEQ // INP_TM
    return pl.pallas_call(
        _inproj_kernel,
        grid=(TOKENS // INP_TM, IN_COLS // INP_TN),
        in_specs=[pl.BlockSpec((INP_TM, D_MODEL), lambda i, j: (i, 0)),
                  pl.BlockSpec((1, D_MODEL), lambda i, j: (0, 0)),
                  pl.BlockSpec((1, 1, D_MODEL), lambda i, j: (i // tiles_per_batch, 0, 0)),
                  pl.BlockSpec((1, 1, D_MODEL), lambda i, j: (i // tiles_per_batch, 0, 0)),
                  pl.BlockSpec((D_MODEL, INP_TN), lambda i, j: (0, j))],
        out_specs=pl.BlockSpec((INP_TM, INP_TN), lambda i, j: (i, j)),
        out_shape=jax.ShapeDtypeStruct((TOKENS, IN_COLS), bf16),
        scratch_shapes=[pltpu.VMEM((INP_TM, D_MODEL), bf16)],
        compiler_params=_cparams(("parallel", "arbitrary"), 48),
        name="inproj",
    )(x2d, norm_g.reshape(1, D_MODEL), scale.reshape(BATCH, 1, D_MODEL),
      shift.reshape(BATCH, 1, D_MODEL), w_in)


def _t5_bucket(rel):
    half = N_BUCKETS // 2
    max_exact = half // 2
    ret = jnp.where(rel > 0, half, 0)
    n = jnp.abs(rel)
    nf = jnp.maximum(n, 1).astype(f32)
    large = max_exact + (jnp.log(nf / max_exact) / math.log(MAX_DISTANCE / max_exact)
                         * (half - max_exact)).astype(i32)
    large = jnp.minimum(large, half - 1)
    return ret + jnp.where(n < max_exact, n, large)


def _bias_band(rel_bias):
    assert ATT_SUB == ATT_KC and ATT_EXT_CHUNKS == 5 and ATT_KC >= MAX_DISTANCE
    r = jnp.arange(ATT_SUB)[:, None]
    j = jnp.arange(ATT_EXT_CHUNKS * ATT_KC)[None, :]
    rel = j - 2 * ATT_KC - r
    onehot = (_t5_bucket(rel)[..., None] == jnp.arange(N_BUCKETS)).astype(f32)
    return jnp.einsum("rjn,nh->hrj", onehot, rel_bias.astype(f32) * LOG2E,
                      precision=lax.Precision.HIGHEST)


def _attn_kernel(q_ref, k_ref, v_ref, band_ref, lq1_ref, lk1_ref, lq2_ref, lk2_ref, sg_ref,
                 o_ref, va_scr, kt_scr):
    t = pl.program_id(2)
    n_chunks = SEQ // ATT_KC

    @pl.when(t == 0)
    def _():
        va_scr[:, :ATT_V_DIM] = v_ref[...]
        va_scr[:, ATT_V_DIM:] = jnp.ones((SEQ, ATT_V_DIM), bf16)
        for c in range(n_chunks):
            keys = slice(c * ATT_KC, (c + 1) * ATT_KC)
            kt_scr[:, keys] = k_ref[keys, :].T

    lam = (jnp.exp(jnp.sum(lq1_ref[...] * lk1_ref[...], axis=-1, keepdims=True))
           - jnp.exp(jnp.sum(lq2_ref[...] * lk2_ref[...], axis=-1, keepdims=True))
           + LAMBDA_INIT)
    def band_offset(rows, c):
        block = t * (ATT_TQ // ATT_SUB) + rows.start // ATT_SUB
        d = jnp.clip(c - block + 2, 0, ATT_EXT_CHUNKS - 1)
        return pl.multiple_of(d * ATT_KC, ATT_KC)

    def scores(qm, rows):
        chunks = []
        m = jnp.full((ATT_SUB, 1), -jnp.inf, f32)
        for c in range(n_chunks):
            s = jnp.dot(qm, kt_scr[:, c * ATT_KC:(c + 1) * ATT_KC], preferred_element_type=f32)
            s = s + band_ref[0, :, pl.ds(band_offset(rows, c), ATT_KC)]
            chunks.append(s)
            m = jnp.maximum(m, jnp.max(s, axis=-1, keepdims=True))
        return chunks, m

    def weighted_values(chunks, m):
        o = jnp.zeros((ATT_SUB, 2 * ATT_V_DIM), f32)
        for c in range(n_chunks):
            p = jnp.exp2(chunks[c] - m)
            o = o + jnp.dot(p.astype(bf16), va_scr[c * ATT_KC:(c + 1) * ATT_KC, :],
                            preferred_element_type=f32)
        return o[:, :ATT_V_DIM] / o[:, ATT_V_DIM:]

    def finish(rows, o1, o2):
        out = o1 - lam * o2
        ms = jnp.mean(out * out, axis=-1, keepdims=True)
        y = out * lax.rsqrt(ms + EPS) * sg_ref[...]
        o_ref[rows, :] = (y * (1.0 - LAMBDA_INIT)).astype(bf16)

    units = []
    for r in range(ATT_TQ // ATT_SUB):
        rows = slice(r * ATT_SUB, (r + 1) * ATT_SUB)
        q = q_ref[rows, :]
        lane = lax.broadcasted_iota(i32, q.shape, 1)
        zero = jnp.zeros_like(q)
        units.append((rows, jnp.where(lane < ATT_HALF_DIM, q, zero)))
        units.append((rows, jnp.where(lane >= ATT_HALF_DIM, q, zero)))
    first_map = {}
    pending = None
    for unit in units + [None]:
        current = (unit[0],) + scores(unit[1], unit[0]) if unit is not None else None
        if pending is not None:
            p_rows, p_chunks, p_m = pending
            o = weighted_values(p_chunks, p_m)
            if p_rows.start in first_map:
                finish(p_rows, first_map.pop(p_rows.start), o)
            else:
                first_map[p_rows.start] = o
        pending = current


def _attention(proj, band, lq1, lk1, lq2, lk2, subln_g):
    qt = SEQ // ATT_TQ
    k_col0 = QK_COLS // ATT_V_DIM
    v_col0 = 2 * QK_COLS // ATT_V_DIM
    vec = lambda n: pl.BlockSpec((1, n), lambda b, h, t: (0, 0))
    return pl.pallas_call(
        _attn_kernel,
        grid=(BATCH, ATT_HEADS, qt),
        in_specs=[pl.BlockSpec((ATT_TQ, ATT_V_DIM), lambda b, h, t: (b * qt + t, h)),
                  pl.BlockSpec((SEQ, ATT_V_DIM), lambda b, h, t: (b, k_col0 + h)),
                  pl.BlockSpec((SEQ, ATT_V_DIM), lambda b, h, t: (b, v_col0 + h)),
                  pl.BlockSpec((1, ATT_SUB, ATT_EXT_CHUNKS * ATT_KC), lambda b, h, t: (h, 0, 0)),
                  vec(ATT_HALF_DIM), vec(ATT_HALF_DIM), vec(ATT_HALF_DIM), vec(ATT_HALF_DIM),
                  vec(ATT_V_DIM)],
        out_specs=pl.BlockSpec((ATT_TQ, ATT_V_DIM), lambda b, h, t: (b * qt + t, h)),
        out_shape=jax.ShapeDtypeStruct((TOKENS, ATT_WIDTH), bf16),
        scratch_shapes=[pltpu.VMEM((SEQ, 2 * ATT_V_DIM), bf16),
                        pltpu.VMEM((ATT_V_DIM, SEQ), bf16)],
        compiler_params=_cparams(("parallel", "parallel", "arbitrary"), 40),
        name="attn",
    )(proj, proj, proj, band, lq1.reshape(1, -1), lk1.reshape(1, -1), lq2.reshape(1, -1),
      lk2.reshape(1, -1), subln_g.reshape(1, -1))


def _conv_kernel(ap_ref, ac_ref, an_ref, gp_ref, gc_ref, gn_ref, w_ref, cb_ref, lg_ref, lb_ref,
                 o_ref, z_scr, c_scr):
    s = pl.program_id(1)
    last = pl.num_programs(1) - 1

    lanes = LANES
    n_lane_chunks = CONV_CH // lanes

    def store_z(rows, z):
        for lc in range(n_lane_chunks):
            z_scr[lc, rows, :] = z[:, lc * lanes:(lc + 1) * lanes]

    def glu(a, g):
        return a.astype(f32) * jax.nn.sigmoid(g.astype(f32))

    store_z(slice(0, CONV_HALO), jnp.where(s > 0, glu(ap_ref[...], gp_ref[...]), 0.0))
    store_z(slice(CONV_HALO + CONV_TS, CONV_TS + 2 * CONV_HALO),
            jnp.where(s < last, glu(an_ref[...], gn_ref[...]), 0.0))

    def glu_chunk(r0):
        rows = pl.ds(r0, ROW_CHUNK)
        store_z(pl.ds(CONV_HALO + r0, ROW_CHUNK), glu(ac_ref[rows, :], gc_ref[rows, :]))
    _row_chunks(CONV_TS, ROW_CHUNK, glu_chunk)

    tap0 = CONV_HALO - CONV_HALF

    def lane_chunk(lc, carry):
        cols = pl.ds(pl.multiple_of(lc * lanes, lanes), lanes)
        taps = [w_ref[j:j + 1, cols] for j in range(CONV_KERNEL)]
        bias = cb_ref[:, cols]
        for r0 in range(0, CONV_TS, CONV_RB):
            acc = jnp.zeros((CONV_RB, lanes), f32)
            for j in range(CONV_KERNEL):
                acc = acc + z_scr[lc, r0 + tap0 + j:r0 + tap0 + j + CONV_RB, :] * taps[j]
            c_scr[r0:r0 + CONV_RB, cols] = acc + bias
        return carry
    lax.fori_loop(0, n_lane_chunks, lane_chunk, 0)

    lg = lg_ref[...]
    lb = lb_ref[...]

    def ln_chunk(r0):
        c = c_scr[pl.ds(r0, CONV_LN_ROWS), :]
        mu = jnp.mean(c, axis=-1, keepdims=True)
        cc = c - mu
        var = jnp.mean(cc * cc, axis=-1, keepdims=True)
        y = cc * lax.rsqrt(var + EPS) * lg + lb
        o_ref[pl.ds(r0, CONV_LN_ROWS), :] = (y * jax.nn.sigmoid(y)).astype(bf16)
    _row_chunks(CONV_TS, CONV_LN_ROWS, ln_chunk)


def _conv(proj, conv_w, conv_b, ln_g, ln_b):
    st = SEQ // CONV_TS
    halo_per_tile = CONV_TS // CONV_HALO
    halo_per_seq = SEQ // CONV_HALO
    a_col = (2 * QK_COLS + ATT_WIDTH) // CONV_CH
    g_col = a_col + 1

    def cur(col):
        return pl.BlockSpec((CONV_TS, CONV_CH), lambda b, s: (b * st + s, col))

    def prev(col):
        return pl.BlockSpec(
            (CONV_HALO, CONV_CH),
            lambda b, s: (jnp.maximum(b * halo_per_seq + s * halo_per_tile - 1, 0), col))

    def nxt(col):
        return pl.BlockSpec(
            (CONV_HALO, CONV_CH),
            lambda b, s: (jnp.minimum(b * halo_per_seq + (s + 1) * halo_per_tile,
                                      TOKENS // CONV_HALO - 1), col))

    vec = pl.BlockSpec((1, CONV_CH), lambda b, s: (0, 0))
    return pl.pallas_call(
        _conv_kernel,
        grid=(BATCH, st),
        in_specs=[prev(a_col), cur(a_col), nxt(a_col), prev(g_col), cur(g_col), nxt(g_col),
                  pl.BlockSpec((CONV_KERNEL, CONV_CH), lambda b, s: (0, 0)), vec, vec, vec],
        out_specs=pl.BlockSpec((CONV_TS, CONV_CH), lambda b, s: (b * st + s, 0)),
        out_shape=jax.ShapeDtypeStruct((TOKENS, CONV_CH), bf16),
        scratch_shapes=[pltpu.VMEM((CONV_CH // LANES, CONV_TS + 2 * CONV_HALO, LANES), f32),
                        pltpu.VMEM((CONV_TS, CONV_CH), f32)],
        compiler_params=_cparams(("parallel", "arbitrary"), 32),
        name="conv",
    )(proj, proj, proj, proj, proj, proj, conv_w, conv_b.reshape(1, -1), ln_g.reshape(1, -1),
      ln_b.reshape(1, -1))


def _outproj_kernel(att_ref, cv_ref, wo_ref, x_ref, g1_ref, sc2_ref, sh2_ref, n2_ref, wr_ref,
                    br_ref, x1_ref, h2p_ref, ti_ref, tw_ref, rk_ref, cnt_ref, carry_scr, hi_scr,
                    lo_scr, wo_bf):
    @pl.when(pl.program_id(0) == 0)
    def _():
        carry_scr[...] = jnp.zeros_like(carry_scr)
        wo_bf[...] = wo_ref[...].astype(bf16)

    mixed = (jnp.dot(att_ref[...], wo_bf[0:ATT_WIDTH, :], preferred_element_type=f32)
             + jnp.dot(cv_ref[...], wo_bf[ATT_WIDTH:, :], preferred_element_type=f32))
    x1_ref[...] = x_ref[...] + g1_ref[0] * mixed

    n2 = n2_ref[...]
    one_plus_scale = 1.0 + sc2_ref[0]
    shift = sh2_ref[0]

    def split(v):
        hi = v.astype(bf16)
        return hi, (v - hi.astype(f32)).astype(bf16)

    for r0 in range(0, OUT_TM, ROW_CHUNK):
        rows = slice(r0, r0 + ROW_CHUNK)
        x1 = x1_ref[rows, :]
        inv = lax.rsqrt(jnp.mean(x1 * x1, axis=-1, keepdims=True) + EPS)
        for s in range(PACK_ROWS):
            pair = []
            for c in (2 * s, 2 * s + 1):
                cols = slice(c * PACK_LANES, (c + 1) * PACK_LANES)
                y = x1_ref[rows, cols] * inv * n2[:, cols]
                h2 = y * one_plus_scale[:, cols] + shift[:, cols]
                hi_scr[rows, cols], lo_scr[rows, cols] = split(h2)
                pair.append(h2)
            h2p_ref[pl.ds(r0 * PACK_ROWS + s, ROW_CHUNK, stride=PACK_ROWS), :] = _pack_pair(*pair)

    w_hi, w_lo = split(wr_ref[...])
    h_hi, h_lo = hi_scr[...], lo_scr[...]
    rows_x_rows = (((1,), (1,)), ((), ()))
    logits = (lax.dot_general(w_hi, h_hi, rows_x_rows, preferred_element_type=f32)
              + lax.dot_general(w_hi, h_lo, rows_x_rows, preferred_element_type=f32)
              + lax.dot_general(w_lo, h_hi, rows_x_rows, preferred_element_type=f32)
              + br_ref[...])
    eidx = lax.broadcasted_iota(i32, logits.shape, 0)
    vals = logits
    sels, tops = [], []
    for k in range(TOP_K):
        m = jnp.max(vals, axis=0, keepdims=True)
        idx = jnp.min(jnp.where(vals == m, eidx, N_EXPERTS), axis=0, keepdims=True)
        sel = eidx == idx
        ti_ref[k:k + 1, :] = idx
        sels.append(sel)
        tops.append(m)
        vals = jnp.where(sel, -jnp.inf, vals)
    exps = [jnp.exp(v - tops[0]) for v in tops]
    denom = exps[0] + exps[1] + exps[2] + exps[3]
    for k in range(TOP_K):
        tw_ref[k:k + 1, :] = exps[k] / denom

    onehot = jnp.where(sels[0] | sels[1] | sels[2] | sels[3], 1.0, 0.0)
    r_i = lax.broadcasted_iota(i32, (OUT_TM, OUT_TM), 0)
    c_i = lax.broadcasted_iota(i32, (OUT_TM, OUT_TM), 1)
    earlier = jnp.where(r_i < c_i, 1.0, 0.0).astype(bf16)
    before = jnp.dot(onehot.astype(bf16), earlier, preferred_element_type=f32) + carry_scr[...]
    for k in range(TOP_K):
        rk_ref[k:k + 1, :] = jnp.sum(jnp.where(sels[k], before, 0.0), axis=0,
                                     keepdims=True).astype(i32)
    total = carry_scr[...] + jnp.sum(onehot, axis=1, keepdims=True)
    carry_scr[...] = total
    cnt_ref[...] = jnp.broadcast_to(total, cnt_ref.shape).astype(i32)


def _outproj(att, cv, w_out, x2d, gate1, scale2, shift2, norm2_g, w_router, b_router):
    tiles_per_batch = SEQ // OUT_TM
    row = lambda w: pl.BlockSpec((OUT_TM, w), lambda i: (i, 0))
    mod = pl.BlockSpec((1, 1, D_MODEL), lambda i: (i // tiles_per_batch, 0, 0))
    slot = pl.BlockSpec((TOP_K, OUT_TM), lambda i: (0, i))
    return pl.pallas_call(
        _outproj_kernel,
        grid=(TOKENS // OUT_TM,),
        in_specs=[row(ATT_WIDTH), row(CONV_CH),
                  pl.BlockSpec((D_MODEL, D_MODEL), lambda i: (0, 0), pipeline_mode=pl.Buffered(1)),
                  row(D_MODEL), mod, mod, mod,
                  pl.BlockSpec((1, D_MODEL), lambda i: (0, 0)),
                  pl.BlockSpec((N_EXPERTS, D_MODEL), lambda i: (0, 0)),
                  pl.BlockSpec((N_EXPERTS, 1), lambda i: (0, 0))],
        out_specs=[row(D_MODEL),
                   pl.BlockSpec((OUT_TM * PACK_ROWS, PACK_LANES), lambda i: (i, 0)),
                   slot, slot, slot,
                   pl.BlockSpec((N_EXPERTS, LANES), lambda i: (0, 0))],
        out_shape=[jax.ShapeDtypeStruct((TOKENS, D_MODEL), f32),
                   jax.ShapeDtypeStruct((TOKENS * PACK_ROWS, PACK_LANES), jnp.uint32),
                   jax.ShapeDtypeStruct((TOP_K, TOKENS), i32),
                   jax.ShapeDtypeStruct((TOP_K, TOKENS), f32),
                   jax.ShapeDtypeStruct((TOP_K, TOKENS), i32),
                   jax.ShapeDtypeStruct((N_EXPERTS, LANES), i32)],
        scratch_shapes=[pltpu.VMEM((N_EXPERTS, 1), f32), pltpu.VMEM((OUT_TM, D_MODEL), bf16),
                        pltpu.VMEM((OUT_TM, D_MODEL), bf16), pltpu.VMEM((D_MODEL, D_MODEL), bf16)],
        compiler_params=_cparams(("arbitrary",), 58),
        name="outproj",
    )(att, cv, w_out, x2d, gate1.reshape(BATCH, 1, D_MODEL), scale2.reshape(BATCH, 1, D_MODEL),
      shift2.reshape(BATCH, 1, D_MODEL), norm2_g.reshape(1, D_MODEL), w_router.T,
      b_router.reshape(N_EXPERTS, 1))


def _token_copy(src_hbm, token, dst_buf, row, sem):
    dst = dst_buf.at[pl.ds(pl.multiple_of(row * PACK_ROWS, PACK_ROWS), PACK_ROWS), :]
    return pltpu.make_async_copy(src_hbm.at[token], dst, sem)


_PAD_BITS = tuple(range(MOE_TM.bit_length() - 2, -1, -1))


def _pad_fill_copies(pad_start, pad_len, zeros, xs_hbm, sem):
    pairs = []
    for b in _PAD_BITS:
        size = 1 << b
        off = pad_start + lax.shift_left(lax.shift_right_logical(pad_len, b + 1), b + 1)
        copy = pltpu.make_async_copy(zeros.at[pl.ds(0, size)], xs_hbm.at[pl.ds(off, size)], sem)
        pairs.append((lax.shift_right_logical(pad_len, b) & 1 == 1, copy))
    return pairs


def _dispatch_kernel(pad_start_ref, pad_len_ref, nv_ref, pos_ref, h2p_ref, xs_hbm, zeros, sem, zsem):
    i = pl.program_id(0)
    last = pl.num_programs(0) - 1

    def for_each_fill(act):
        def per_expert(e, carry):
            for cond, copy in _pad_fill_copies(pad_start_ref[e], pad_len_ref[e], zeros, xs_hbm, zsem):
                @pl.when(cond)
                def _(copy=copy):
                    act(copy)
            return carry
        lax.fori_loop(0, N_EXPERTS, per_expert, 0)

        def per_tile(t, carry):
            row0 = pl.multiple_of(t * MOE_TM, MOE_TM)
            act(pltpu.make_async_copy(zeros, xs_hbm.at[pl.ds(row0, MOE_TM)], zsem))
            return carry
        lax.fori_loop(nv_ref[0], MOE_NT, per_tile, 0)

    @pl.when(i == 0)
    def _():
        zeros[...] = jnp.zeros_like(zeros)
        for_each_fill(lambda copy: copy.start())

    def body(g, carry):
        for u in range(DMA_UNROLL // TOP_K):
            r = g * (DMA_UNROLL // TOP_K) + u
            for k in range(TOP_K):
                pltpu.make_async_copy(h2p_ref.at[r], xs_hbm.at[pos_ref[0, k, r]], sem).start(
                    priority=k % DMA_PRIORITIES)
        return carry
    lax.fori_loop(0, DSP_TB * TOP_K // DMA_UNROLL, body, 0)
    pltpu.make_async_copy(xs_hbm.at[pl.ds(0, DSP_TB * TOP_K)], xs_hbm.at[pl.ds(0, DSP_TB * TOP_K)],
                          sem).wait()

    @pl.when(i == last)
    def _():
        for_each_fill(lambda copy: copy.wait())


def _dispatch(pad_start, pad_len, n_valid, pos, h2_packed):
    nt = TOKENS // DSP_TB
    pos3 = pos.reshape(TOP_K, nt, DSP_TB).transpose(1, 0, 2)
    grid_spec = pltpu.PrefetchScalarGridSpec(
        num_scalar_prefetch=3,
        grid=(nt,),
        in_specs=[pl.BlockSpec((1, TOP_K, DSP_TB), lambda i, *_: (i, 0, 0), memory_space=pltpu.SMEM),
                  pl.BlockSpec((DSP_TB, PACK_ROWS, PACK_LANES), lambda i, *_: (i, 0, 0))],
        out_specs=pl.BlockSpec(memory_space=pl.ANY),
        scratch_shapes=[pltpu.VMEM((MOE_TM, PACK_ROWS, PACK_LANES), jnp.uint32),
                        pltpu.SemaphoreType.DMA(()), pltpu.SemaphoreType.DMA(())])
    return pl.pallas_call(
        _dispatch_kernel,
        grid_spec=grid_spec,
        out_shape=jax.ShapeDtypeStruct((MOE_ROWS, PACK_ROWS, PACK_LANES), jnp.uint32),
        compiler_params=_cparams(("arbitrary",), 24),
        name="dispatch",
    )(pad_start, pad_len, n_valid, pos3, h2_packed.reshape(TOKENS, PACK_ROWS, PACK_LANES))


def _weights_changed(te_ref, m):
    prev = te_ref[jnp.maximum(m - 1, 0)]
    return (m == 0) | (te_ref[m] != prev)


def _expert_tile_step(te_ref, nx_ref, vr_ref, n, m, n_passes, count_ref, copies_for, convert,
                      compute, write_zeros):
    first_expert = te_ref[0]
    first = _weights_changed(te_ref, m)
    rows_valid = vr_ref[m]

    def start(copies):
        for copy in copies:
            copy.start(priority=WEIGHT_DMA_PRIORITY)

    @pl.when((n == 0) & (m == 0))
    def _():
        count_ref[0] = 0
        start(copies_for(0, first_expert, 0))

    def open_item():
        slot = count_ref[0] % 2
        nxt = nx_ref[m]

        @pl.when(nxt >= 0)
        def _():
            start(copies_for(n, nxt, 1 - slot))

        @pl.when((nxt < 0) & (n + 1 < n_passes))
        def _():
            start(copies_for(n + 1, first_expert, 1 - slot))

        for copy in copies_for(0, 0, slot):
            copy.wait()
        convert(slot)
        count_ref[0] = count_ref[0] + 1

    full = rows_valid > MOE_TM - MOE_ROW_STEP
    fused = first & full

    @pl.when(fused)
    def _():
        open_item()
        compute(MOE_TM)

    @pl.when(first & jnp.logical_not(full))
    def _():
        open_item()

    for n_rows in range(MOE_ROW_STEP, MOE_TM + 1, MOE_ROW_STEP):
        in_range = (rows_valid > n_rows - MOE_ROW_STEP) & (rows_valid <= n_rows)
        if n_rows == MOE_TM:
            in_range = in_range & jnp.logical_not(fused)

        @pl.when(in_range)
        def _(n_rows=n_rows):
            compute(n_rows)

    @pl.when(rows_valid <= 0)
    def _():
        write_zeros()


def _gmm1_kernel(te_ref, nx_ref, nv_ref, vr_ref, xs_ref, w_hbm, bg_ref, bu_ref, o_ref, stage, wg_bf,
                 wu_bf, count_ref, sem):
    n = pl.program_id(0)
    m = pl.program_id(1)

    def copies_for(p, e, slot):
        col = pl.multiple_of(p * GMM1_TN, GMM1_TN)
        return [pltpu.make_async_copy(w_hbm.at[e, :, pl.ds(col, GMM1_TN)], stage.at[slot, 0],
                                      sem.at[slot]),
                pltpu.make_async_copy(w_hbm.at[e, :, pl.ds(D_FF + col, GMM1_TN)],
                                      stage.at[slot, 1], sem.at[slot])]

    def convert(slot):
        wg_bf[...] = stage[slot, 0].astype(bf16)
        wu_bf[...] = stage[slot, 1].astype(bf16)

    def compute(n_rows):
        parts = []
        for s in range(PACK_ROWS):
            lo, hi = _load_packed(xs_ref, n_rows, s)
            parts += [lo.astype(bf16), hi.astype(bf16)]
        x = jnp.concatenate(parts, axis=1)
        gate = jnp.dot(x, wg_bf[...], preferred_element_type=f32) + bg_ref[0]
        up = jnp.dot(x, wu_bf[...], preferred_element_type=f32) + bu_ref[0]
        gate = jnp.minimum(gate, SWIGLU_LIMIT)
        up = jnp.clip(up, -SWIGLU_LIMIT, SWIGLU_LIMIT)
        act = (up + 1.0) * (gate * jax.nn.sigmoid(SWIGLU_ALPHA * gate))
        o_ref[0:n_rows, :] = act.astype(bf16)
        if n_rows < MOE_TM:
            o_ref[n_rows:, :] = jnp.zeros((MOE_TM - n_rows, GMM1_TN), bf16)

    def write_zeros():
        o_ref[...] = jnp.zeros_like(o_ref)

    _expert_tile_step(te_ref, nx_ref, vr_ref, n, m, pl.num_programs(0), count_ref, copies_for,
                      convert, compute, write_zeros)


def _gmm1(tile_expert, next_expert, n_valid, valid_rows, xs_packed, w_gate_up, b_gate_up):
    up0 = D_FF // GMM1_TN
    used = lambda m, nv: jnp.minimum(m, nv[0] - 1)
    grid_spec = pltpu.PrefetchScalarGridSpec(
        num_scalar_prefetch=4,
        grid=(D_FF // GMM1_TN, MOE_NT),
        in_specs=[pl.BlockSpec((MOE_TM * PACK_ROWS, PACK_LANES),
                               lambda n, m, te, nx, nv, vr: (used(m, nv), 0)),
                  pl.BlockSpec(memory_space=pl.ANY),
                  pl.BlockSpec((1, 1, GMM1_TN), lambda n, m, te, nx, nv, vr: (te[m], 0, n)),
                  pl.BlockSpec((1, 1, GMM1_TN), lambda n, m, te, nx, nv, vr: (te[m], 0, up0 + n))],
        out_specs=pl.BlockSpec((MOE_TM, GMM1_TN), lambda n, m, te, nx, nv, vr: (m, n)),
        scratch_shapes=[pltpu.VMEM((2, 2, D_MODEL, GMM1_TN), f32),
                        pltpu.VMEM((D_MODEL, GMM1_TN), bf16), pltpu.VMEM((D_MODEL, GMM1_TN), bf16),
                        pltpu.SMEM((1,), i32), pltpu.SemaphoreType.DMA((2,))])
    b3 = b_gate_up.reshape(N_EXPERTS, 1, 2 * D_FF)
    return pl.pallas_call(
        _gmm1_kernel,
        grid_spec=grid_spec,
        out_shape=jax.ShapeDtypeStruct((MOE_ROWS, D_FF), bf16),
        compiler_params=_cparams(("arbitrary", "arbitrary"), 54),
        name="gmm1",
    )(tile_expert, next_expert, n_valid, valid_rows,
      xs_packed.reshape(MOE_ROWS * PACK_ROWS, PACK_LANES), w_gate_up, b3, b3)


def _gmm2_kernel(te_ref, nx_ref, nv_ref, vr_ref, a_ref, w_hbm, bd_ref, o_ref, stage, wd_bf, y_scr,
                 count_ref, sem):
    def copies_for(p, e, slot):
        return [pltpu.make_async_copy(w_hbm.at[e], stage.at[slot], sem.at[slot])]

    def convert(slot):
        wd_bf[...] = stage[slot].astype(bf16)

    def compute(n_rows):
        y_scr[0:n_rows, :] = (jnp.dot(a_ref[0:n_rows, :], wd_bf[...], preferred_element_type=f32)
                              + bd_ref[0])
        _store_packed(o_ref, 0, n_rows, lambda cols: y_scr[0:n_rows, cols])
        if n_rows < MOE_TM:
            o_ref[n_rows * PACK_ROWS:, :] = jnp.zeros(((MOE_TM - n_rows) * PACK_ROWS, PACK_LANES),
                                                     jnp.uint32)

    def write_zeros():
        o_ref[...] = jnp.zeros_like(o_ref)

    _expert_tile_step(te_ref, nx_ref, vr_ref, pl.program_id(0), pl.program_id(1),
                      pl.num_programs(0), count_ref, copies_for, convert, compute, write_zeros)


def _gmm2(tile_expert, next_expert, n_valid, valid_rows, act, w_down, b_down):
    assert GMM2_TN == D_MODEL
    grid_spec = pltpu.PrefetchScalarGridSpec(
        num_scalar_prefetch=4,
        grid=(1, MOE_NT),
        in_specs=[pl.BlockSpec((MOE_TM, D_FF), lambda n, m, te, nx, nv, vr: (m, 0)),
                  pl.BlockSpec(memory_space=pl.ANY),
                  pl.BlockSpec((1, 1, GMM2_TN), lambda n, m, te, nx, nv, vr: (te[m], 0, 0))],
        out_specs=pl.BlockSpec((MOE_TM * PACK_ROWS, PACK_LANES),
                               lambda n, m, te, nx, nv, vr: (m, 0)),
        scratch_shapes=[pltpu.VMEM((2, D_FF, GMM2_TN), f32), pltpu.VMEM((D_FF, GMM2_TN), bf16),
                        pltpu.VMEM((MOE_TM, D_MODEL), f32), pltpu.SMEM((1,), i32),
                        pltpu.SemaphoreType.DMA((2,))])
    return pl.pallas_call(
        _gmm2_kernel,
        grid_spec=grid_spec,
        out_shape=jax.ShapeDtypeStruct((MOE_ROWS * PACK_ROWS, PACK_LANES), jnp.uint32),
        compiler_params=_cparams(("arbitrary", "arbitrary"), 56),
        name="gmm2",
    )(tile_expert, next_expert, n_valid, valid_rows, act, w_down,
      b_down.reshape(N_EXPERTS, 1, D_MODEL))


def _combine_kernel(pos_cur, pos_nxt, y_hbm, x1_ref, g2_ref, tw_ref, fg_ref, o_ref, buf, sem):
    i = pl.program_id(0)
    n = pl.num_programs(0)
    slot = i % 2

    def start_rows(pos_ref, s, rows):
        for r in rows:
            for k in range(TOP_K):
                _token_copy(y_hbm, pos_ref[0, k, r], buf.at[s, k], r, sem.at[s]).start(
                    priority=k % DMA_PRIORITIES)

    @pl.when(i == 0)
    def _():
        def body(g, carry):
            start_rows(pos_cur, 0, [g * (DMA_UNROLL // TOP_K) + u
                                    for u in range(DMA_UNROLL // TOP_K)])
            return carry
        lax.fori_loop(0, CMB_TB * TOP_K // DMA_UNROLL, body, 0)

    pltpu.make_async_copy(buf.at[slot], buf.at[slot], sem.at[slot]).wait()

    def combine_tile(prefetch_next):
        weights = [jnp.broadcast_to(tw_ref[:, k:k + 1], (CMB_TB, PACK_LANES))
                   for k in range(TOP_K)]
        sumsq = jnp.zeros((CMB_TB, 1), f32)
        rows_per_chunk = CMB_TB // PACK_ROWS
        for s in range(PACK_ROWS):
            if prefetch_next:
                start_rows(pos_nxt, 1 - slot, range(s * rows_per_chunk, (s + 1) * rows_per_chunk))
            lo, hi = _load_packed(buf.at[slot, 0], CMB_TB, s)
            lo, hi = lo * weights[0], hi * weights[0]
            for k in range(1, TOP_K):
                lo_k, hi_k = _load_packed(buf.at[slot, k], CMB_TB, s)
                lo, hi = lo + lo_k * weights[k], hi + hi_k * weights[k]
            for c, moe in ((2 * s, lo), (2 * s + 1, hi)):
                cols = slice(c * PACK_LANES, (c + 1) * PACK_LANES)
                x2 = x1_ref[:, cols] + g2_ref[0, :, cols] * moe
                o_ref[:, cols] = x2
                sumsq = sumsq + jnp.sum(x2 * x2, axis=-1, keepdims=True)
        inv = lax.rsqrt(sumsq * (1.0 / D_MODEL) + EPS)
        o_ref[...] = o_ref[...] * inv * fg_ref[...]

    @pl.when(i + 1 < n)
    def _():
        combine_tile(True)

    @pl.when(i + 1 >= n)
    def _():
        combine_tile(False)


def _combine(pos, y_packed, x1, gate2, top_w_rows, final_g):
    nt = TOKENS // CMB_TB
    tiles_per_batch = SEQ // CMB_TB
    pos3 = pos.reshape(TOP_K, nt, CMB_TB).transpose(1, 0, 2)
    return pl.pallas_call(
        _combine_kernel,
        grid=(nt,),
        in_specs=[pl.BlockSpec((1, TOP_K, CMB_TB), lambda i: (i, 0, 0), memory_space=pltpu.SMEM),
                  pl.BlockSpec((1, TOP_K, CMB_TB), lambda i: (jnp.minimum(i + 1, nt - 1), 0, 0),
                               memory_space=pltpu.SMEM),
                  pl.BlockSpec(memory_space=pl.ANY),
                  pl.BlockSpec((CMB_TB, D_MODEL), lambda i: (i, 0)),
                  pl.BlockSpec((1, 1, D_MODEL), lambda i: (i // tiles_per_batch, 0, 0)),
                  pl.BlockSpec((CMB_TB, TOP_K), lambda i: (i, 0)),
                  pl.BlockSpec((1, D_MODEL), lambda i: (0, 0))],
        out_specs=pl.BlockSpec((CMB_TB, D_MODEL), lambda i: (i, 0)),
        out_shape=jax.ShapeDtypeStruct((TOKENS, D_MODEL), f32),
        scratch_shapes=[pltpu.VMEM((2, TOP_K, CMB_TB * PACK_ROWS, PACK_LANES), jnp.uint32),
                        pltpu.SemaphoreType.DMA((2,))],
        compiler_params=_cparams(("arbitrary",), 44),
        name="combine",
    )(pos3, pos3, y_packed.reshape(MOE_ROWS, PACK_ROWS, PACK_LANES), x1,
      gate2.reshape(BATCH, 1, D_MODEL), top_w_rows, final_g.reshape(1, D_MODEL))


def _routing_tables(counts, top_i, rank):
    padded = (counts + MOE_TM - 1) // MOE_TM * MOE_TM
    ends = jnp.cumsum(padded)
    starts = ends - padded
    n_valid = ends[-1] // MOE_TM
    tile_ids = jnp.minimum(jnp.arange(MOE_NT, dtype=i32), n_valid - 1)
    tile_expert = jnp.sum(tile_ids[:, None] >= (ends // MOE_TM)[None, :], axis=1).astype(i32)
    experts = jnp.arange(N_EXPERTS, dtype=i32)
    pos = rank + jnp.sum(jnp.where(top_i[..., None] == experts, starts, 0), axis=-1)
    pad_start = (starts + counts).astype(i32)
    pad_len = (padded - counts).astype(i32)
    later = (experts[None, :] > tile_expert[:, None]) & (counts > 0)[None, :]
    next_expert = jnp.min(jnp.where(later, experts[None, :], N_EXPERTS), axis=1)
    next_expert = jnp.where(next_expert == N_EXPERTS, -1, next_expert).astype(i32)
    block_ids = jnp.arange(MOE_NT, dtype=i32)
    group_end = jnp.sum(jnp.where(tile_expert[:, None] == experts, pad_start, 0), axis=-1)
    valid_rows = jnp.clip(group_end - block_ids * MOE_TM, 0, MOE_TM)
    valid_rows = jnp.where(block_ids < n_valid, valid_rows, 0).astype(i32)
    return (tile_expert, next_expert, n_valid.reshape(1).astype(i32), valid_rows, pos.astype(i32),
            pad_start, pad_len)


def kernel(x, c, w_ada, b_ada, norm1_g, w_in, lambda_q1, lambda_k1, lambda_q2, lambda_k2, subln_g,
           rel_bias, conv_w, conv_b, conv_ln_g, conv_ln_b, w_out, norm2_g, w_router, b_router,
           w_gate_up, b_gate_up, w_down, b_down, final_g):
    assert x.shape == (BATCH, SEQ, D_MODEL) and w_ada.shape[0] == 1
    l = 0
    x2d = x.reshape(TOKENS, D_MODEL)
    mod = _adaln(c, w_ada[l], b_ada[l])
    shift1, scale1, gate1, shift2, scale2, gate2 = [mod[i] for i in range(6)]

    proj = _inproj(x2d, norm1_g[l], scale1, shift1, w_in[l])
    att = _attention(proj, _bias_band(rel_bias), lambda_q1[l], lambda_k1[l], lambda_q2[l],
                     lambda_k2[l], subln_g[l])
    cv = _conv(proj, conv_w[l], conv_b[l], conv_ln_g[l], conv_ln_b[l])

    x1, h2_packed, top_i, top_w, rank, counts = _outproj(
        att, cv, w_out[l], x2d, gate1, scale2, shift2, norm2_g[l], w_router[l],
        b_router[l])
    tile_expert, next_expert, n_valid, valid_rows, pos, pad_start, pad_len = _routing_tables(
        counts[:, 0], top_i, rank)

    xs = _dispatch(pad_start, pad_len, n_valid, pos, h2_packed)
    act = _gmm1(tile_expert, next_expert, n_valid, valid_rows, xs, w_gate_up[l], b_gate_up[l])
    y = _gmm2(tile_expert, next_expert, n_valid, valid_rows, act, w_down[l], b_down[l])
    out = _combine(pos, y, x1, gate2, top_w.T, final_g)
    return out.reshape(BATCH, SEQ, D_MODEL)
```
